```python
import math
import jax
import jax.numpy as jnp
from jax import lax
import numpy as np

D_MODEL = 1024
BATCH = 8
SEQ = 2048
DEPTH = 2

A_HEADS = 4
A_NOPE = 64
A_ROPE = 32
A_V = 64
Q_LORA = 192
KV_LORA = 128
ROPE_THETA = 10000.0
B_HEADS = 4
B_DIM = 64
DILATED_PATTERNS = ((128, 1), (512, 4), (2048, 16))
DIL_BLOCK = 128
C_HEADS = 4
C_DIM = 64
IDX_HEADS = 8
IDX_DIM = 32
IDX_TOPK_MAX = 256
D_HEADS = 4
D_QK = 32
D_V = 64
BRANCH_WIDTH = 256
MIX_WIDTH = 4 * BRANCH_WIDTH
NUM_BUCKETS = 32
MAX_DISTANCE = 2048
BIAS_HEADS = B_HEADS + C_HEADS + D_HEADS
Q_BLOCK = 128
NEG_INF = -1e30
EPS = 1e-6
SPLIT_SIZES = (
    Q_LORA, KV_LORA, A_ROPE, BRANCH_WIDTH,
    BRANCH_WIDTH, BRANCH_WIDTH, BRANCH_WIDTH, BRANCH_WIDTH,
    BRANCH_WIDTH, BRANCH_WIDTH, BRANCH_WIDTH, IDX_HEADS * IDX_DIM, IDX_DIM, IDX_HEADS, BRANCH_WIDTH,
    2 * D_HEADS * D_QK, 2 * D_HEADS * D_QK, D_HEADS * D_V, BRANCH_WIDTH,
)
IN_COLS = sum(SPLIT_SIZES)

kernel_name = "hybrid_mla_dilated_dsa_diff_parallel_heads"


def rms_norm(x, g):
    xf = x.astype(jnp.float32)
    y = xf * lax.rsqrt(jnp.mean(xf * xf, axis=-1, keepdims=True) + EPS)
    return (y * g.astype(jnp.float32)).astype(x.dtype)


def t5_bucket(rel):
    n = jnp.maximum(rel, 0)
    max_exact = NUM_BUCKETS // 2
    nf = jnp.maximum(n, max_exact).astype(jnp.float32)
    large = max_exact + (jnp.log(nf / max_exact) / math.log(MAX_DISTANCE / max_exact)
                         * (NUM_BUCKETS - max_exact)).astype(jnp.int32)
    large = jnp.minimum(large, NUM_BUCKETS - 1)
    return jnp.where(n < max_exact, n, large)


def rope_tables(seq):
    inv = ROPE_THETA ** (-jnp.arange(0, A_ROPE, 2, dtype=jnp.float32) / A_ROPE)
    ang = jnp.arange(seq, dtype=jnp.float32)[:, None] * inv[None, :]
    return jnp.cos(ang), jnp.sin(ang)


def apply_rope(x, cos, sin):
    half = x.shape[-1] // 2
    x1, x2 = x[..., :half], x[..., half:]
    return jnp.concatenate([x1 * cos - x2 * sin, x1 * sin + x2 * cos], axis=-1).astype(x.dtype)


def to_blocks(a, nb):
    return jnp.moveaxis(a.reshape((a.shape[0], nb, Q_BLOCK) + a.shape[2:]), 1, 0)


def from_blocks(a):
    a = jnp.moveaxis(a, 0, 1)
    return a.reshape((a.shape[0], a.shape[1] * a.shape[2]) + a.shape[3:])


def mla_branch(c_q, c_kv, k_rope, q_norm, kv_norm, w_uq, w_ukv, cos, sin):
    bsz, seq, _ = c_q.shape
    q = (rms_norm(c_q, q_norm) @ w_uq).reshape(bsz, seq, A_HEADS, A_NOPE + A_ROPE)
    q_nope = q[..., :A_NOPE]
    q_rope = apply_rope(q[..., A_NOPE:], cos[:, None, :], sin[:, None, :])
    kv = (rms_norm(c_kv, kv_norm) @ w_ukv).reshape(bsz, seq, A_HEADS, A_NOPE + A_V)
    k_nope, v = kv[..., :A_NOPE], kv[..., A_NOPE:]
    k_rope = apply_rope(k_rope, cos, sin)
    scale = (A_NOPE + A_ROPE) ** -0.5
    keys = jnp.arange(seq)

    def block(args):
        qn, qr, t0 = args
        t = t0 + jnp.arange(Q_BLOCK)
        s = (jnp.einsum('bqhd,bshd->bhqs', qn, k_nope)
             + jnp.einsum('bqhd,bsd->bhqs', qr, k_rope)).astype(jnp.float32) * scale
        s = jnp.where(keys[None, :] <= t[:, None], s, NEG_INF)
        p = jax.nn.softmax(s, axis=-1).astype(v.dtype)
        return jnp.einsum('bhqs,bshd->bqhd', p, v)

    nb = seq // Q_BLOCK
    o = lax.map(block, (to_blocks(q_nope, nb), to_blocks(q_rope, nb), jnp.arange(nb) * Q_BLOCK))
    return from_blocks(o).reshape(bsz, seq, A_HEADS * A_V)


def dilated_pattern(q, k, v, table, window, dilation):
    bsz, seq, heads, dim = q.shape
    n = seq // dilation
    span = window // dilation
    blk = min(DIL_BLOCK, n)
    nb = -(-n // blk)
    n_pad = nb * blk

    def to_classes(a):
        a = jnp.swapaxes(a.reshape(bsz, n, dilation, heads, dim), 1, 2)
        a = jnp.pad(a, ((0, 0), (0, 0), (0, n_pad - n), (0, 0), (0, 0)))
        return a.reshape(bsz, dilation, nb, blk, heads, dim)

    def with_prev(a):
        prev = jnp.pad(a, ((0, 0), (0, 0), (1, 0), (0, 0), (0, 0), (0, 0)))[:, :, :-1]
        return jnp.concatenate([prev, a], axis=3)

    qc = to_classes(q)
    kc = with_prev(to_classes(k))
    vc = with_prev(to_classes(v))
    rel = jnp.arange(blk)[:, None] + blk - jnp.arange(2 * blk)[None, :]
    bias = jnp.transpose(table[t5_bucket(rel * dilation)], (0, 2, 1))
    in_band = ((rel >= 0) & (rel <= span))[None, :, None, :]
    has_prev = (jnp.arange(nb)[:, None, None, None] > 0) | (jnp.arange(2 * blk)[None, None, None, :] >= blk)
    mask = in_band & has_prev
    s = jnp.einsum('brnqhd,brnkhd->brnqhk', qc, kc).astype(jnp.float32) * (dim ** -0.5)
    s = jnp.where(mask, s + bias, NEG_INF)
    m = jnp.max(s, axis=-1)
    p = jnp.exp(s - m[..., None])
    l = jnp.sum(p, axis=-1)
    o = jnp.einsum('brnqhk,brnkhd->brnqhd', p, vc.astype(jnp.float32)) / l[..., None]

    def from_classes(a):
        a = a.reshape((bsz, dilation, n_pad) + a.shape[4:])[:, :, :n]
        return jnp.swapaxes(a, 1, 2).reshape((bsz, seq) + a.shape[3:])

    return from_classes(o), from_classes(m), from_classes(l)


def dilated_branch(q, k, v, table):
    bsz, seq, _ = q.shape
    q, k, v = (a.reshape(bsz, seq, B_HEADS, B_DIM) for a in (q, k, v))
    outs = [dilated_pattern(q, k, v, table, w, d) for (w, d) in DILATED_PATTERNS]
    o = jnp.stack([r[0] for r in outs])
    m = jnp.stack([r[1] for r in outs])
    l = jnp.stack([r[2] for r in outs])
    wts = l * jnp.exp(m - jnp.max(m, axis=0, keepdims=True))
    y = jnp.sum(wts[..., None] * o, axis=0) / jnp.sum(wts, axis=0)[..., None]
    return y.reshape(bsz, seq, BRANCH_WIDTH).astype(q.dtype)


def sparse_branch(q, k, v, q_idx, k_idx, w_idx, table, k_top):
    bsz, seq, _ = q.shape
    q = q.reshape(bsz, seq, C_HEADS, C_DIM)
    k = k.reshape(bsz, seq, C_HEADS, C_DIM)
    v = v.reshape(bsz, seq, C_HEADS, C_DIM)
    q_idx = q_idx.reshape(bsz, seq, IDX_HEADS, IDX_DIM)
    keys = jnp.arange(seq)
    gather = jax.vmap(lambda kb, ib: kb[ib])

    def block(args):
        qb, qib, wb, t0 = args
        t = t0 + jnp.arange(Q_BLOCK)
        logit = jnp.einsum('bqhd,bsd->bqhs', qib, k_idx).astype(jnp.float32) * (IDX_DIM ** -0.5)
        score = jnp.einsum('bqhs,bqh->bqs', jax.nn.relu(logit), wb.astype(jnp.float32)) * (IDX_HEADS ** -0.5)
        score = jnp.where(keys[None, None, :] <= t[None, :, None], score, NEG_INF)
        _, idx = lax.top_k(score, k_top)
        k_sel = gather(k, idx)
        v_sel = gather(v, idx)
        s = jnp.einsum('bqhd,bqkhd->bhqk', qb, k_sel).astype(jnp.float32) * (C_DIM ** -0.5)
        rel = t[None, :, None] - idx
        bias = jnp.transpose(table[t5_bucket(rel)], (0, 3, 1, 2))
        s = jnp.where((rel >= 0)[:, None], s + bias, NEG_INF)
        p = jax.nn.softmax(s, axis=-1).astype(v.dtype)
        return jnp.einsum('bhqk,bqkhd->bqhd', p, v_sel)

    nb = seq // Q_BLOCK
    o = lax.map(block, (to_blocks(q, nb), to_blocks(q_idx, nb), to_blocks(w_idx, nb),
                        jnp.arange(nb) * Q_BLOCK))
    return from_blocks(o).reshape(bsz, seq, BRANCH_WIDTH)


def diff_branch(q, k, v, lam_params, subln, table, lambda_init):
    bsz, seq, _ = q.shape
    q = q.reshape(bsz, seq, D_HEADS, 2, D_QK)
    k = k.reshape(bsz, seq, D_HEADS, 2, D_QK)
    v = v.reshape(bsz, seq, D_HEADS, D_V)
    lp = lam_params.astype(jnp.float32)
    lam = jnp.exp(jnp.sum(lp[0] * lp[1])) - jnp.exp(jnp.sum(lp[2] * lp[3])) + lambda_init
    keys = jnp.arange(seq)

    def block(args):
        qb, t0 = args
        t = t0 + jnp.arange(Q_BLOCK)
        s = jnp.einsum('bqhmd,bshmd->bmhqs', qb, k).astype(jnp.float32) * (D_QK ** -0.5)
        rel = t[:, None] - keys[None, :]
        bias = jnp.transpose(table[t5_bucket(rel)], (2, 0, 1))
        s = jnp.where(rel >= 0, s + bias, NEG_INF)
        p = jax.nn.softmax(s, axis=-1)
        a = p[:, 0] - lam * p[:, 1]
        return jnp.einsum('bhqs,bshd->bqhd', a.astype(v.dtype), v)

    nb = seq // Q_BLOCK
    o = from_blocks(lax.map(block, (to_blocks(q, nb), jnp.arange(nb) * Q_BLOCK)))
    o = rms_norm(o, subln) * (1.0 - lambda_init)
    return o.reshape(bsz, seq, BRANCH_WIDTH)


def setup_inputs(seed: int = 0) -> dict:
    key = jax.random.key(seed)
    ks = jax.random.split(key, 12)
    nrm = jax.random.normal
    return {
        "x": nrm(ks[0], (BATCH, SEQ, D_MODEL), jnp.float32),
        "w_in": nrm(ks[1], (DEPTH, D_MODEL, IN_COLS), jnp.float32) * D_MODEL ** -0.5,
        "w_out": nrm(ks[2], (DEPTH, MIX_WIDTH, D_MODEL), jnp.float32) * MIX_WIDTH ** -0.5,
        "norm_pre": 1.0 + 0.05 * nrm(ks[3], (DEPTH, D_MODEL), jnp.float32),
        "norm_post": 1.0 + 0.05 * nrm(ks[4], (DEPTH, D_MODEL), jnp.float32),
        "mla_q_norm": 1.0 + 0.05 * nrm(ks[5], (DEPTH, Q_LORA), jnp.float32),
        "mla_kv_norm": 1.0 + 0.05 * nrm(ks[6], (DEPTH, KV_LORA), jnp.float32),
        "mla_w_uq": nrm(ks[7], (DEPTH, Q_LORA, A_HEADS * (A_NOPE + A_ROPE)), jnp.float32) * Q_LORA ** -0.5,
        "mla_w_ukv": nrm(ks[8], (DEPTH, KV_LORA, A_HEADS * (A_NOPE + A_V)), jnp.float32) * KV_LORA ** -0.5,
        "diff_lambda": 0.1 * nrm(ks[9], (DEPTH, 4, D_QK), jnp.float32),
        "diff_subln": 1.0 + 0.05 * nrm(ks[10], (DEPTH, D_V), jnp.float32),
        "rel_bias": 0.5 * nrm(ks[11], (NUM_BUCKETS, BIAS_HEADS), jnp.float32),
    }


def reference(x, w_in, w_out, norm_pre, norm_post, mla_q_norm, mla_kv_norm, mla_w_uq, mla_w_ukv,
              diff_lambda, diff_subln, rel_bias):
    seq = x.shape[1]
    cos, sin = rope_tables(seq)
    k_top = min(IDX_TOPK_MAX, seq // 4)
    bias_b = rel_bias[:, :B_HEADS]
    bias_c = rel_bias[:, B_HEADS:B_HEADS + C_HEADS]
    bias_d = rel_bias[:, B_HEADS + C_HEADS:]
    split_at = np.cumsum(SPLIT_SIZES)[:-1].tolist()
    h = x
    for layer in range(DEPTH):
        u = rms_norm(h, norm_pre[layer])
        proj = u @ w_in[layer]
        (a_cq, a_ckv, a_kr, a_g,
         b_q, b_k, b_v, b_g,
         c_q, c_k, c_v, c_qi, c_ki, c_wi, c_g,
         d_q, d_k, d_v, d_g) = jnp.split(proj, split_at, axis=-1)
        y_a = mla_branch(a_cq, a_ckv, a_kr, mla_q_norm[layer], mla_kv_norm[layer],
                         mla_w_uq[layer], mla_w_ukv[layer], cos, sin) * jax.nn.silu(a_g)
        y_b = dilated_branch(b_q, b_k, b_v, bias_b) * jax.nn.silu(b_g)
        y_c = sparse_branch(c_q, c_k, c_v, c_qi, c_ki, c_wi, bias_c, k_top) * jax.nn.silu(c_g)
        lambda_init = 0.8 - 0.6 * math.exp(-0.3 * layer)
        y_d = diff_branch(d_q, d_k, d_v, diff_lambda[layer], diff_subln[layer], bias_d,
                          lambda_init) * jax.nn.silu(d_g)
        y = jnp.concatenate([y_a, y_b, y_c, y_d], axis=-1) @ w_out[layer]
        h = h + rms_norm(y, norm_post[layer])
    return h
```

```python
import functools
import math

import jax
import jax.numpy as jnp
import numpy as np
from jax import lax
from jax.experimental import pallas as pl
from jax.experimental.pallas import tpu as pltpu

F32 = jnp.float32
BF16 = jnp.bfloat16

D_MODEL = 1024
A_HEADS, A_NOPE, A_ROPE, A_V = 4, 64, 32, 64
Q_LORA, KV_LORA = 192, 128
ROPE_THETA = 10000.0
HEAD_DIM = 64
DILATED_PATTERNS = ((128, 1), (512, 4), (2048, 16))
IDX_HEADS, IDX_DIM, IDX_TOPK_MAX = 8, 32, 256
D_QK = 32
BRANCH_WIDTH = 256
NUM_BUCKETS, MAX_DISTANCE = 32, 2048
NEG_INF = -1e30
EPS = 1e-6
LANES = 128
QB = 128
VMEM_LIMIT = 56 * 1024 * 1024

_SPLIT = (Q_LORA, KV_LORA, A_ROPE, 256, 256, 256, 256, 256, 256, 256, 256, IDX_HEADS * IDX_DIM, IDX_DIM,
          IDX_HEADS, 256, 256, 256, 256, 256)
_OFF = np.concatenate([[0], np.cumsum(_SPLIT)]).tolist()
(_A_CQ, _A_CKV, _A_KR, _A_G, _B_Q, _B_K, _B_V, _B_G, _C_Q, _C_K, _C_V, _C_QI, _C_KI, _C_WI, _C_G,
 _D_Q, _D_K, _D_V, _D_G) = range(19)

P_CQ, P_AG, P_CKV, P_KR, P_KRROT, P_CKI = 0, 256, 512, 640, 768, 896
P_BQ, P_BK, P_BV, P_BG = 1024, 1280, 1536, 1792
P_CQ2, P_CK, P_CV, P_CQI, P_CG = 2048, 2304, 2560, 2816, 3072
P_DQ, P_DK, P_DV, P_DG = 3328, 3584, 3840, 4096
P_CWI = 4352
NCOL = 4480


def _dot(a, b):
    return jnp.dot(a, b, preferred_element_type=F32)


def _dot_nt(a, b):
    return lax.dot_general(a, b, (((1,), (1,)), ((), ())), preferred_element_type=F32)


def _params(sem):
    return pltpu.CompilerParams(dimension_semantics=sem, vmem_limit_bytes=VMEM_LIMIT)


def _rot_cols(w):
    half = w.shape[-1] // 2
    return jnp.concatenate([-w[..., half:], w[..., :half]], axis=-1)


def _arrange_w_in(w):
    def seg(i):
        return w[:, _OFF[i]:_OFF[i + 1]]

    def z(n):
        return jnp.zeros((w.shape[0], n), w.dtype)

    kr = seg(_A_KR)
    cols = [
        seg(_A_CQ), z(64),
        seg(_A_G),
        seg(_A_CKV),
        z(64), kr, z(32),
        z(64), _rot_cols(kr), z(32),
        seg(_C_KI), seg(_C_KI), seg(_C_KI), seg(_C_KI),
        seg(_B_Q), seg(_B_K), seg(_B_V), seg(_B_G),
        seg(_C_Q), seg(_C_K), seg(_C_V), seg(_C_QI), seg(_C_G),
        seg(_D_Q), seg(_D_K), seg(_D_V), seg(_D_G),
        seg(_C_WI), z(LANES - IDX_HEADS),
    ]
    out = jnp.concatenate(cols, axis=1)
    assert out.shape[1] == NCOL
    return out


def _arrange_mla(w_uq, w_ukv):
    wq = w_uq.reshape(Q_LORA, A_HEADS, A_NOPE + A_ROPE)
    nope, rope = wq[..., :A_NOPE], wq[..., A_NOPE:]
    zq = jnp.zeros((Q_LORA, A_HEADS, LANES - A_NOPE - A_ROPE), w_uq.dtype)
    wq_main = jnp.concatenate([nope, rope, zq], axis=-1).reshape(Q_LORA, A_HEADS * LANES)
    wq_rot = jnp.concatenate([jnp.zeros_like(nope), _rot_cols(rope), zq], axis=-1).reshape(Q_LORA, A_HEADS * LANES)
    pad = jnp.zeros((256 - Q_LORA, A_HEADS * LANES), w_uq.dtype)
    wq_main = jnp.concatenate([wq_main, pad], axis=0)
    wq_rot = jnp.concatenate([wq_rot, pad], axis=0)
    wkv = w_ukv.reshape(KV_LORA, A_HEADS, A_NOPE + A_V)
    knope, v = wkv[..., :A_NOPE], wkv[..., A_NOPE:]
    wk = jnp.concatenate([knope, jnp.zeros_like(knope)], axis=-1).reshape(KV_LORA, A_HEADS * LANES)
    zv = jnp.zeros_like(v)
    wv = jnp.stack([jnp.concatenate([v[:, h], zv[:, h]] if h % 2 == 0 else [zv[:, h], v[:, h]], axis=-1)
                    for h in range(A_HEADS)], axis=1).reshape(KV_LORA, A_HEADS * LANES)
    return wq_main.astype(BF16), wq_rot.astype(BF16), wk.astype(BF16), wv.astype(BF16)


def _rope_tables(seq):
    inv = ROPE_THETA ** (-jnp.arange(0, A_ROPE, 2, dtype=F32) / A_ROPE)
    ang = jnp.arange(seq, dtype=F32)[:, None] * inv[None, :]
    cos, sin = jnp.cos(ang), jnp.sin(ang)
    one = jnp.ones((seq, A_NOPE), F32)
    zero = jnp.zeros((seq, LANES - A_NOPE - A_ROPE), F32)
    cos_t = jnp.concatenate([one, cos, cos, zero], axis=1)
    sin_t = jnp.concatenate([jnp.zeros_like(one), sin, sin, zero], axis=1)
    return cos_t, sin_t


def _t5_bucket_np(rel):
    n = np.maximum(rel, 0)
    max_exact = NUM_BUCKETS // 2
    nf = np.maximum(n, max_exact).astype(np.float64)
    large = max_exact + (np.log(nf / max_exact) / math.log(MAX_DISTANCE / max_exact)
                         * (NUM_BUCKETS - max_exact)).astype(np.int32)
    large = np.minimum(large, NUM_BUCKETS - 1)
    return np.where(n < max_exact, n, large).astype(np.int32)


def _bucket_tiles_causal(nq):
    q = np.arange(QB)[:, None]
    k = np.arange(QB)[None, :]
    return np.stack([_t5_bucket_np(QB * d + q - k) for d in range(nq)])


def _bucket_tiles_dilated():
    q = np.arange(QB)[:, None]
    k = np.arange(2 * QB)[None, :]
    rel = q + QB - k
    return np.stack([_t5_bucket_np(rel * d) for (_, d) in DILATED_PATTERNS])


def _bias_expand_kernel(table_ref, bucket_ref, out_ref, *, head0):
    h = pl.program_id(0) + head0
    bk = bucket_ref[...]
    acc = jnp.zeros(bk.shape, F32)
    for b in range(NUM_BUCKETS):
        acc = jnp.where(bk == b, table_ref[b, h], acc)
    out_ref[...] = acc


def _bias_expand(table, buckets, head0, nheads):
    n, r, c = buckets.shape
    return pl.pallas_call(
        functools.partial(_bias_expand_kernel, head0=head0),
        grid=(nheads,),
        in_specs=[pl.BlockSpec(memory_space=pltpu.SMEM),
                  pl.BlockSpec((n, r, c), lambda h: (0, 0, 0))],
        out_specs=pl.BlockSpec((None, n, r, c), lambda h: (h, 0, 0, 0)),
        out_shape=jax.ShapeDtypeStruct((nheads, n, r, c), F32),
        compiler_params=_params(("arbitrary",)),
        name="bias_expand",
    )(table, buckets)


def _in_proj_kernel(x_ref, g_ref, w_ref, o32_ref, o16_ref):
    x = x_ref[...]
    ms = jnp.mean(x * x, axis=-1, keepdims=True)
    xn = x * lax.rsqrt(ms + EPS) * g_ref[...]
    p = _dot(xn.astype(BF16), w_ref[...])
    o32_ref[...] = p
    o16_ref[...] = p.astype(BF16)


def _in_proj(h, g, w, tm=256):
    bsz, seq, _ = h.shape
    return pl.pallas_call(
        _in_proj_kernel,
        grid=(bsz, seq // tm),
        in_specs=[pl.BlockSpec((None, tm, D_MODEL), lambda b, i: (b, i, 0)),
                  pl.BlockSpec((1, D_MODEL), lambda b, i: (0, 0)),
                  pl.BlockSpec((D_MODEL, NCOL), lambda b, i: (0, 0))],
        out_specs=[pl.BlockSpec((None, tm, NCOL), lambda b, i: (b, i, 0)),
                   pl.BlockSpec((None, tm, NCOL), lambda b, i: (b, i, 0))],
        out_shape=[jax.ShapeDtypeStruct((bsz, seq, NCOL), F32),
                   jax.ShapeDtypeStruct((bsz, seq, NCOL), BF16)],
        compiler_params=_params(("arbitrary", "arbitrary")),
        name="in_proj",
    )(h, g, w)


def _mla_prep_kernel(cq_ref, ckv_ref, kr_ref, krrot_ref, cos_ref, sin_ref, gq_ref, gkv_ref,
                     wq_ref, wqrot_ref, wk_ref, wv_ref, q_ref, k_ref, v_ref):
    cos = cos_ref[...]
    sin = sin_ref[...]
    cos4 = jnp.concatenate([cos] * A_HEADS, axis=1)
    sin4 = jnp.concatenate([sin] * A_HEADS, axis=1)
    cq = cq_ref[...]
    ms = jnp.sum(cq * cq, axis=-1, keepdims=True) * (1.0 / Q_LORA)
    nq = (cq * lax.rsqrt(ms + EPS) * gq_ref[...]).astype(BF16)
    q = _dot(nq, wq_ref[...]) * cos4 + _dot(nq, wqrot_ref[...]) * sin4
    q_ref[...] = q.astype(BF16)
    ckv = ckv_ref[...]
    ms = jnp.mean(ckv * ckv, axis=-1, keepdims=True)
    nkv = (ckv * lax.rsqrt(ms + EPS) * gkv_ref[...]).astype(BF16)
    kr = kr_ref[...] * cos + krrot_ref[...] * sin
    k = _dot(nkv, wk_ref[...]) + jnp.concatenate([kr] * A_HEADS, axis=1)
    k_ref[...] = k.astype(BF16)
    v_ref[...] = _dot(nkv, wv_ref[...]).astype(BF16)


def _mla_prep(p32, cos_t, sin_t, gq, gkv, wq, wqrot, wk, wv, tm=512):
    bsz, seq, _ = p32.shape
    w4 = A_HEADS * LANES
    row = lambda b, i: (b, i, 0)
    const = lambda b, i: (0, 0)
    return pl.pallas_call(
        _mla_prep_kernel,
        grid=(bsz, seq // tm),
        in_specs=[pl.BlockSpec((None, tm, 256), lambda b, i: (b, i, P_CQ // 256)),
                  pl.BlockSpec((None, tm, LANES), lambda b, i: (b, i, P_CKV // LANES)),
                  pl.BlockSpec((None, tm, LANES), lambda b, i: (b, i, P_KR // LANES)),
                  pl.BlockSpec((None, tm, LANES), lambda b, i: (b, i, P_KRROT // LANES)),
                  pl.BlockSpec((tm, LANES), lambda b, i: (i, 0)),
                  pl.BlockSpec((tm, LANES), lambda b, i: (i, 0)),
                  pl.BlockSpec((1, 256), const),
                  pl.BlockSpec((1, KV_LORA), const),
                  pl.BlockSpec((256, w4), const),
                  pl.BlockSpec((256, w4), const),
                  pl.BlockSpec((KV_LORA, w4), const),
                  pl.BlockSpec((KV_LORA, w4), const)],
        out_specs=[pl.BlockSpec((None, tm, w4), row)] * 3,
        out_shape=[jax.ShapeDtypeStruct((bsz, seq, w4), BF16)] * 3,
        compiler_params=_params(("arbitrary", "arbitrary")),
        name="mla_prep",
    )(p32, p32, p32, p32, cos_t, sin_t, gq, gkv, wq, wqrot, wk, wv)


def _silu(g):
    return g * (1.0 / (1.0 + jnp.exp(-g)))


def _causal_mask(i, seq):
    t = i * QB + lax.broadcasted_iota(jnp.int32, (QB, seq), 0)
    s = lax.broadcasted_iota(jnp.int32, (QB, seq), 1)
    return s <= t


def _lane_mask(width, seg, dtype):
    lane = lax.broadcasted_iota(jnp.int32, (1, LANES), 1)
    return jnp.where((lane >= seg * width) & (lane < (seg + 1) * width), 1.0, 0.0).astype(dtype)


def _mla_attn_kernel(q_ref, k_ref, v_ref, g_ref, o_ref):
    i = pl.program_id(1)
    seq = k_ref.shape[0]
    causal = _causal_mask(i, seq)
    scale = (A_NOPE + A_ROPE) ** -0.5
    pairs = []
    for g in range(A_HEADS // 2):
        acc = None
        for h in (2 * g, 2 * g + 1):
            sl = slice(h * LANES, (h + 1) * LANES)
            s = _dot_nt(q_ref[:, sl], k_ref[:, sl]) * scale
            s = jnp.where(causal, s, NEG_INF)
            m = jnp.max(s, axis=-1, keepdims=True)
            e = jnp.exp(s - m)
            l = jnp.sum(e, axis=-1, keepdims=True)
            o = _dot(e.astype(BF16), v_ref[:, sl]) / l
            acc = o if acc is None else acc + o
        pairs.append(acc)
    y = jnp.concatenate(pairs, axis=1)
    o_ref[...] = (y * _silu(g_ref[...])).astype(BF16)


def _mla_attn(q, k, v, p32):
    bsz, seq, w4 = q.shape
    return pl.pallas_call(
        _mla_attn_kernel,
        grid=(bsz, seq // QB),
        in_specs=[pl.BlockSpec((None, QB, w4), lambda b, i: (b, i, 0)),
                  pl.BlockSpec((None, seq, w4), lambda b, i: (b, 0, 0)),
                  pl.BlockSpec((None, seq, w4), lambda b, i: (b, 0, 0)),
                  pl.BlockSpec((None, QB, 256), lambda b, i: (b, i, P_AG // 256))],
        out_specs=pl.BlockSpec((None, QB, BRANCH_WIDTH), lambda b, i: (b, i, 0)),
        out_shape=jax.ShapeDtypeStruct((bsz, seq, BRANCH_WIDTH), BF16),
        compiler_params=_params(("arbitrary", "arbitrary")),
        name="mla_attn",
    )(q, k, v, p32)


def _bias_row(bias_ref, h, i, nq):
    return jnp.concatenate([bias_ref[h, jnp.maximum(i - j, 0)] for j in range(nq)], axis=1)


def _diff_attn_kernel(q_ref, k_ref, v_ref, g_ref, bias_ref, lam_ref, subln_ref, o_ref, *, lambda_init):
    i = pl.program_id(1)
    seq = k_ref.shape[0]
    nq = seq // QB
    causal = _causal_mask(i, seq)
    scale = D_QK ** -0.5
    lp = lam_ref[...]
    lam = (jnp.exp(jnp.sum(lp[0:1] * lp[1:2], axis=-1, keepdims=True))
           - jnp.exp(jnp.sum(lp[2:3] * lp[3:4], axis=-1, keepdims=True)) + lambda_init)
    subln = subln_ref[...]
    pairs = []
    for g in range(2):
        sl = slice(g * LANES, (g + 1) * LANES)
        qg = q_ref[:, sl]
        kg = k_ref[:, sl]
        vg = v_ref[:, sl]
        acc = None
        for hh in range(2):
            h = 2 * g + hh
            bias = _bias_row(bias_ref, h, i, nq)
            outs = []
            for mm in range(2):
                qm = qg * _lane_mask(D_QK, 2 * hh + mm, BF16)
                s = _dot_nt(qm, kg) * scale
                s = jnp.where(causal, s + bias, NEG_INF)
                mx = jnp.max(s, axis=-1, keepdims=True)
                e = jnp.exp(s - mx)
                l = jnp.sum(e, axis=-1, keepdims=True)
                outs.append(_dot(e.astype(BF16), vg) / l)
            hm = _lane_mask(HEAD_DIM, hh, F32)
            a = (outs[0] - lam * outs[1]) * hm
            ms = jnp.sum(a * a, axis=-1, keepdims=True) * (1.0 / HEAD_DIM)
            a = a * lax.rsqrt(ms + EPS)
            acc = a if acc is None else acc + a
        pairs.append(acc * subln * (1.0 - lambda_init))
    y = jnp.concatenate(pairs, axis=1)
    o_ref[...] = (y * _silu(g_ref[...])).astype(BF16)


def _diff_attn(p32, p16, bias, lam_params, subln, lambda_init):
    bsz, seq, _ = p32.shape
    nq = seq // QB
    return pl.pallas_call(
        functools.partial(_diff_attn_kernel, lambda_init=lambda_init),
        grid=(bsz, nq),
        in_specs=[pl.BlockSpec((None, QB, 256), lambda b, i: (b, i, P_DQ // 256)),
                  pl.BlockSpec((None, seq, 256), lambda b, i: (b, 0, P_DK // 256)),
                  pl.BlockSpec((None, seq, 256), lambda b, i: (b, 0, P_DV // 256)),
                  pl.BlockSpec((None, QB, 256), lambda b, i: (b, i, P_DG // 256)),
                  pl.BlockSpec((4, nq, QB, QB), lambda b, i: (0, 0, 0, 0)),
                  pl.BlockSpec((4, D_QK), lambda b, i: (0, 0)),
                  pl.BlockSpec((1, LANES), lambda b, i: (0, 0))],
        out_specs=pl.BlockSpec((None, QB, BRANCH_WIDTH), lambda b, i: (b, i, 0)),
        out_shape=jax.ShapeDtypeStruct((bsz, seq, BRANCH_WIDTH), BF16),
        compiler_params=_params(("arbitrary", "arbitrary")),
        name="diff_attn",
    )(p16, p16, p16, p32, bias, lam_params, subln)


def _sortable_to_float(key):
    return pltpu.bitcast(jnp.where(key < 0, key ^ jnp.int32(0x7FFFFFFF), key), F32)


def _kth_largest(score, k_top):
    rows = score.shape[0]

    def count_ge(key):
        thr = _sortable_to_float(key)
        return jnp.sum(jnp.where(score >= thr, 1.0, 0.0), axis=-1, keepdims=True)

    int_min = jnp.full((rows, 1), -2 ** 31, jnp.int32)
    zero = jnp.zeros((rows, 1), jnp.int32)
    t = jnp.where(count_ge(zero) >= k_top, zero, int_min)

    def body(it, t):
        cand = t + (jnp.int32(1) << (30 - it))
        return jnp.where(count_ge(cand) >= k_top, cand, t)

    t = lax.fori_loop(0, 31, body, t)
    return _sortable_to_float(t)


def _first_ties(eq, need, col):
    rows = eq.shape[0]
    eqf = jnp.where(eq, 1.0, 0.0)
    nbits = int(eq.shape[1]).bit_length()

    def body(it, j):
        cand = j + (jnp.int32(1) << (nbits - 1 - it))
        cnt = jnp.sum(jnp.where(col < cand, eqf, 0.0), axis=-1, keepdims=True)
        return jnp.where(cnt <= need, cand, j)

    j = lax.fori_loop(0, nbits, body, jnp.zeros((rows, 1), jnp.int32))
    return eq & (col < j)


def _sparse_attn_kernel(q_ref, k_ref, v_ref, qi_ref, ki_ref, wi_ref, g_ref, bias_ref, o_ref, *, k_top):
    i = pl.program_id(1)
    seq = k_ref.shape[0]
    nq = seq // QB
    causal = _causal_mask(i, seq)
    col = lax.broadcasted_iota(jnp.int32, (QB, seq), 1)
    ki = ki_ref[...]
    wi = wi_ref[...]
    score = None
    for h in range(IDX_HEADS):
        qm = qi_ref[:, (h // 4) * LANES:(h // 4 + 1) * LANES] * _lane_mask(IDX_DIM, h % 4, BF16)
        logit = _dot_nt(qm, ki) * (IDX_DIM ** -0.5)
        term = jnp.maximum(logit, 0.0) * wi[:, h:h + 1]
        score = term if score is None else score + term
    score = score * (IDX_HEADS ** -0.5)
    score = jnp.where(causal, score, NEG_INF)
    thr = _kth_largest(score, k_top)
    gt = score > thr
    eq = score == thr
    need = k_top - jnp.sum(jnp.where(gt, 1.0, 0.0), axis=-1, keepdims=True)
    sel = (gt | _first_ties(eq, need, col)) & causal
    scale = HEAD_DIM ** -0.5
    pairs = []
    for g in range(2):
        sl = slice(g * LANES, (g + 1) * LANES)
        qg = q_ref[:, sl]
        kg = k_ref[:, sl]
        vg = v_ref[:, sl]
        acc = None
        for hh in range(2):
            h = 2 * g + hh
            hm16 = _lane_mask(HEAD_DIM, hh, BF16)
            s = _dot_nt(qg * hm16, kg) * scale
            s = jnp.where(sel, s + _bias_row(bias_ref, h, i, nq), NEG_INF)
            mx = jnp.max(s, axis=-1, keepdims=True)
            e = jnp.exp(s - mx)
            l = jnp.sum(e, axis=-1, keepdims=True)
            o = _dot(e.astype(BF16), vg) / l * _lane_mask(HEAD_DIM, hh, F32)
            acc = o if acc is None else acc + o
        pairs.append(acc)
    y = jnp.concatenate(pairs, axis=1)
    o_ref[...] = (y * _silu(g_ref[...])).astype(BF16)


def _sparse_attn(p32, p16, bias, k_top):
    bsz, seq, _ = p32.shape
    nq = seq // QB
    return pl.pallas_call(
        functools.partial(_sparse_attn_kernel, k_top=k_top),
        grid=(bsz, nq),
        in_specs=[pl.BlockSpec((None, QB, 256), lambda b, i: (b, i, P_CQ2 // 256)),
                  pl.BlockSpec((None, seq, 256), lambda b, i: (b, 0, P_CK // 256)),
                  pl.BlockSpec((None, seq, 256), lambda b, i: (b, 0, P_CV // 256)),
                  pl.BlockSpec((None, QB, 256), lambda b, i: (b, i, P_CQI // 256)),
                  pl.BlockSpec((None, seq, LANES), lambda b, i: (b, 0, P_CKI // LANES)),
                  pl.BlockSpec((None, QB, LANES), lambda b, i: (b, i, P_CWI // LANES)),
                  pl.BlockSpec((None, QB, 256), lambda b, i: (b, i, P_CG // 256)),
                  pl.BlockSpec((4, nq, QB, QB), lambda b, i: (0, 0, 0, 0))],
        out_specs=pl.BlockSpec((None, QB, BRANCH_WIDTH), lambda b, i: (b, i, 0)),
        out_shape=jax.ShapeDtypeStruct((bsz, seq, BRANCH_WIDTH), BF16),
        compiler_params=_params(("arbitrary", "arbitrary")),
        name="sparse_attn",
    )(p16, p16, p16, p16, p16, p32, p32, bias)


def _dilated_kernel(q0_ref, q1_ref, k0_ref, k1_ref, v0_ref, v1_ref, g_ref, bias_ref, o_ref,
                    m_ref, l_ref, acc_ref):
    seq = q0_ref.shape[0]
    q_refs, k_refs, v_refs = (q0_ref, q1_ref), (k0_ref, k1_ref), (v0_ref, v1_ref)
    scale = HEAD_DIM ** -0.5
    rel = (lax.broadcasted_iota(jnp.int32, (QB, 2 * QB), 0) + QB
           - lax.broadcasted_iota(jnp.int32, (QB, 2 * QB), 1))
    kcol = lax.broadcasted_iota(jnp.int32, (QB, 2 * QB), 1)
    lane = lax.broadcasted_iota(jnp.int32, (QB, LANES), 1)
    masks16 = [_lane_mask(HEAD_DIM, hh, BF16) for hh in range(2)]
    for p, (window, d) in enumerate(DILATED_PATTERNS):
        span = window // d
        n_cls = seq // d
        nb = n_cls // QB
        in_band = (rel >= 0) & (rel <= span)

        def tile(n, carry, p=p, d=d, nb=nb, in_band=in_band):
            r = n % d
            j = n // d
            start = r + d * QB * j
            prev = jnp.maximum(start - d * QB, r)
            mask = in_band & ((j > 0) | (kcol >= QB))
            cur_rows = pl.ds(start, QB, stride=d)
            prev_rows = pl.ds(prev, QB, stride=d)
            for g in range(2):
                q = q_refs[g][cur_rows, :].astype(BF16)
                kcat = jnp.concatenate([k_refs[g][prev_rows, :], k_refs[g][cur_rows, :]], axis=0).astype(BF16)
                vcat = jnp.concatenate([v_refs[g][prev_rows, :], v_refs[g][cur_rows, :]], axis=0).astype(BF16)
                ms, ls, accs = [], [], []
                for hh in range(2):
                    s = _dot_nt(q * masks16[hh], kcat) * scale
                    s = jnp.where(mask, s + bias_ref[p * 4 + 2 * g + hh], NEG_INF)
                    m = jnp.max(s, axis=-1, keepdims=True)
                    e = jnp.exp(s - m)
                    ms.append(m)
                    ls.append(jnp.sum(e, axis=-1, keepdims=True))
                    accs.append(_dot(e.astype(BF16), vcat))
                first = lane < HEAD_DIM
                m_new = jnp.where(first, ms[0], ms[1])
                l_new = jnp.where(first, ls[0], ls[1])
                a_new = jnp.where(first, accs[0], accs[1])
                if p == 0:
                    m_ref[g, cur_rows, :] = m_new
                    l_ref[g, cur_rows, :] = l_new
                    acc_ref[g, cur_rows, :] = a_new
                else:
                    m_old = m_ref[g, cur_rows, :]
                    m_tot = jnp.maximum(m_old, m_new)
                    w_old = jnp.exp(m_old - m_tot)
                    w_new = jnp.exp(m_new - m_tot)
                    m_ref[g, cur_rows, :] = m_tot
                    l_ref[g, cur_rows, :] = w_old * l_ref[g, cur_rows, :] + w_new * l_new
                    acc_ref[g, cur_rows, :] = w_old * acc_ref[g, cur_rows, :] + w_new * a_new
            return carry

        lax.fori_loop(0, d * nb, tile, 0)
    gate = _silu(g_ref[...])
    for g in range(2):
        sl = slice(g * LANES, (g + 1) * LANES)
        o_ref[:, sl] = (acc_ref[g] / l_ref[g] * gate[:, sl]).astype(BF16)


def _dilated_attn(p32, bias):
    bsz, seq, _ = p32.shape

    def slab(off):
        return pl.BlockSpec((None, seq, LANES), lambda b, off=off: (b, 0, off // LANES))

    return pl.pallas_call(
        _dilated_kernel,
        grid=(bsz,),
        in_specs=[slab(P_BQ), slab(P_BQ + LANES), slab(P_BK), slab(P_BK + LANES),
                  slab(P_BV), slab(P_BV + LANES),
                  pl.BlockSpec((None, seq, 256), lambda b: (b, 0, P_BG // 256)),
                  pl.BlockSpec((12, QB, 2 * QB), lambda b: (0, 0, 0))],
        out_specs=pl.BlockSpec((None, seq, BRANCH_WIDTH), lambda b: (b, 0, 0)),
        out_shape=jax.ShapeDtypeStruct((bsz, seq, BRANCH_WIDTH), BF16),
        scratch_shapes=[pltpu.VMEM((2, seq, LANES), F32)] * 3,
        compiler_params=_params(("arbitrary",)),
        name="dilated_attn",
    )(p32, p32, p32, p32, p32, p32, p32, bias)


def _out_proj_kernel(ya_ref, yb_ref, yc_ref, yd_ref, w_ref, h_ref, g_ref, o_ref):
    y = None
    for n, ref in enumerate((ya_ref, yb_ref, yc_ref, yd_ref)):
        t = _dot(ref[...], w_ref[n * BRANCH_WIDTH:(n + 1) * BRANCH_WIDTH, :])
        y = t if y is None else y + t
    ms = jnp.mean(y * y, axis=-1, keepdims=True)
    o_ref[...] = h_ref[...] + y * lax.rsqrt(ms + EPS) * g_ref[...]


def _out_proj(ya, yb, yc, yd, w, h, g, tm=512):
    bsz, seq, _ = h.shape
    yspec = pl.BlockSpec((None, tm, BRANCH_WIDTH), lambda b, i: (b, i, 0))
    return pl.pallas_call(
        _out_proj_kernel,
        grid=(bsz, seq // tm),
        in_specs=[yspec, yspec, yspec, yspec,
                  pl.BlockSpec((4 * BRANCH_WIDTH, D_MODEL), lambda b, i: (0, 0)),
                  pl.BlockSpec((None, tm, D_MODEL), lambda b, i: (b, i, 0)),
                  pl.BlockSpec((1, D_MODEL), lambda b, i: (0, 0))],
        out_specs=pl.BlockSpec((None, tm, D_MODEL), lambda b, i: (b, i, 0)),
        out_shape=jax.ShapeDtypeStruct(h.shape, F32),
        compiler_params=_params(("arbitrary", "arbitrary")),
        name="out_proj",
    )(ya, yb, yc, yd, w, h, g)


def kernel(x, w_in, w_out, norm_pre, norm_post, mla_q_norm, mla_kv_norm, mla_w_uq, mla_w_ukv,
           diff_lambda, diff_subln, rel_bias):
    bsz, seq, _ = x.shape
    depth = w_in.shape[0]
    nq = seq // QB
    k_top = min(IDX_TOPK_MAX, seq // 4)
    cos_t, sin_t = _rope_tables(seq)
    causal_buckets = jnp.asarray(_bucket_tiles_causal(nq))
    bias_cd = _bias_expand(rel_bias, causal_buckets, 4, 8)
    bias_b = _bias_expand(rel_bias, jnp.asarray(_bucket_tiles_dilated()), 0, 4)
    bias_b = jnp.swapaxes(bias_b, 0, 1).reshape(12, QB, 2 * QB)
    bias_c, bias_d = bias_cd[:4], bias_cd[4:]
    h = x
    for layer in range(depth):
        w_arr = _arrange_w_in(w_in[layer]).astype(BF16)
        wq, wqrot, wk, wv = _arrange_mla(mla_w_uq[layer], mla_w_ukv[layer])
        gq = jnp.concatenate([mla_q_norm[layer], jnp.ones((256 - Q_LORA,), F32)])[None, :]
        gkv = mla_kv_norm[layer][None, :]
        p32, p16 = _in_proj(h, norm_pre[layer][None, :], w_arr)
        qa, ka, va = _mla_prep(p32, cos_t, sin_t, gq, gkv, wq, wqrot, wk, wv)
        y_a = _mla_attn(qa, ka, va, p32)
        y_b = _dilated_attn(p32, bias_b)
        y_c = _sparse_attn(p32, p16, bias_c, k_top)
        lambda_init = 0.8 - 0.6 * math.exp(-0.3 * layer)
        subln = jnp.concatenate([diff_subln[layer], diff_subln[layer]])[None, :]
        y_d = _diff_attn(p32, p16, bias_d, diff_lambda[layer], subln, lambda_init)
        h = _out_proj(y_a, y_b, y_c, y_d, w_out[layer].astype(BF16), h, norm_post[layer][None, :])
    return h
```

```python
import functools
import math

import jax
import jax.numpy as jnp
import numpy as np
from jax import lax
from jax.experimental import pallas as pl
from jax.experimental.pallas import tpu as pltpu

F32 = jnp.float32
BF16 = jnp.bfloat16

D_MODEL = 1024
A_HEADS, A_NOPE, A_ROPE, A_V = 4, 64, 32, 64
Q_LORA, KV_LORA = 192, 128
ROPE_THETA = 10000.0
HEAD_DIM = 64
DILATED_PATTERNS = ((128, 1), (512, 4), (2048, 16))
IDX_HEADS, IDX_DIM, IDX_TOPK_MAX = 8, 32, 256
D_QK = 32
BRANCH_WIDTH = 256
NUM_BUCKETS, MAX_DISTANCE = 32, 2048
NEG_INF = -1e30
EPS = 1e-6
LOG2E = math.log2(math.e)
KV_CHUNK = 512
BISECT_GROUPS = 4
LANES = 128
QB = 128
VMEM_LIMIT = 56 * 1024 * 1024

_SPLIT = (Q_LORA, KV_LORA, A_ROPE, 256, 256, 256, 256, 256, 256, 256, 256, IDX_HEADS * IDX_DIM, IDX_DIM,
          IDX_HEADS, 256, 256, 256, 256, 256)
_OFF = np.concatenate([[0], np.cumsum(_SPLIT)]).tolist()
(_A_CQ, _A_CKV, _A_KR, _A_G, _B_Q, _B_K, _B_V, _B_G, _C_Q, _C_K, _C_V, _C_QI, _C_KI, _C_WI, _C_G,
 _D_Q, _D_K, _D_V, _D_G) = range(19)

P_CQ, P_AG, P_CKV, P_KR, P_KRROT, P_CKI = 0, 256, 512, 640, 768, 896
P_BQ, P_BK, P_BV, P_BG = 1024, 1280, 1536, 1792
P_CQ2, P_CK, P_CV, P_CQI, P_CG = 2048, 2304, 2560, 2816, 3072
P_DQ, P_DK, P_DV, P_DG = 3328, 3584, 3840, 4096
P_CWI = 4352
NCOL = 4480


def _dot(a, b):
    return jnp.dot(a, b, preferred_element_type=F32)


def _dot_nt(a, b):
    return lax.dot_general(a, b, (((1,), (1,)), ((), ())), preferred_element_type=F32)


def _params(sem):
    return pltpu.CompilerParams(dimension_semantics=sem, vmem_limit_bytes=VMEM_LIMIT)


def _rot_cols(w):
    half = w.shape[-1] // 2
    return jnp.concatenate([-w[..., half:], w[..., :half]], axis=-1)


def _arrange_w_in(w):
    def seg(i):
        return w[:, _OFF[i]:_OFF[i + 1]]

    def z(n):
        return jnp.zeros((w.shape[0], n), w.dtype)

    kr = seg(_A_KR)
    cols = [
        seg(_A_CQ), z(64),
        seg(_A_G),
        seg(_A_CKV),
        z(64), kr, z(32),
        z(64), _rot_cols(kr), z(32),
        seg(_C_KI), seg(_C_KI), seg(_C_KI), seg(_C_KI),
        seg(_B_Q), seg(_B_K), seg(_B_V), seg(_B_G),
        seg(_C_Q), seg(_C_K), seg(_C_V), seg(_C_QI), seg(_C_G),
        seg(_D_Q), seg(_D_K), seg(_D_V), seg(_D_G),
        seg(_C_WI), z(LANES - IDX_HEADS),
    ]
    out = jnp.concatenate(cols, axis=1)
    assert out.shape[1] == NCOL
    return out


def _arrange_mla(w_uq, w_ukv):
    wq = w_uq.reshape(Q_LORA, A_HEADS, A_NOPE + A_ROPE)
    nope, rope = wq[..., :A_NOPE], wq[..., A_NOPE:]
    zq = jnp.zeros((Q_LORA, A_HEADS, LANES - A_NOPE - A_ROPE), w_uq.dtype)
    wq_main = jnp.concatenate([nope, rope, zq], axis=-1).reshape(Q_LORA, A_HEADS * LANES)
    wq_rot = jnp.concatenate([jnp.zeros_like(nope), _rot_cols(rope), zq], axis=-1).reshape(Q_LORA, A_HEADS * LANES)
    pad = jnp.zeros((256 - Q_LORA, A_HEADS * LANES), w_uq.dtype)
    wq_main = jnp.concatenate([wq_main, pad], axis=0)
    wq_rot = jnp.concatenate([wq_rot, pad], axis=0)
    wkv = w_ukv.reshape(KV_LORA, A_HEADS, A_NOPE + A_V)
    knope, v = wkv[..., :A_NOPE], wkv[..., A_NOPE:]
    wk = jnp.concatenate([knope, jnp.zeros_like(knope)], axis=-1).reshape(KV_LORA, A_HEADS * LANES)
    zv = jnp.zeros_like(v)
    wv = jnp.stack([jnp.concatenate([v[:, h], zv[:, h]] if h % 2 == 0 else [zv[:, h], v[:, h]], axis=-1)
                    for h in range(A_HEADS)], axis=1).reshape(KV_LORA, A_HEADS * LANES)
    return wq_main.astype(BF16), wq_rot.astype(BF16), wk.astype(BF16), wv.astype(BF16)


def _rope_tables(seq):
    inv = ROPE_THETA ** (-jnp.arange(0, A_ROPE, 2, dtype=F32) / A_ROPE)
    ang = jnp.arange(seq, dtype=F32)[:, None] * inv[None, :]
    cos, sin = jnp.cos(ang), jnp.sin(ang)
    one = jnp.ones((seq, A_NOPE), F32)
    zero = jnp.zeros((seq, LANES - A_NOPE - A_ROPE), F32)
    cos_t = jnp.concatenate([one, cos, cos, zero], axis=1)
    sin_t = jnp.concatenate([jnp.zeros_like(one), sin, sin, zero], axis=1)
    return cos_t, sin_t


def _t5_bucket_np(rel):
    n = np.maximum(rel, 0)
    max_exact = NUM_BUCKETS // 2
    nf = np.maximum(n, max_exact).astype(np.float64)
    large = max_exact + (np.log(nf / max_exact) / math.log(MAX_DISTANCE / max_exact)
                         * (NUM_BUCKETS - max_exact)).astype(np.int32)
    large = np.minimum(large, NUM_BUCKETS - 1)
    return np.where(n < max_exact, n, large).astype(np.int32)


MASKED_BUCKET = NUM_BUCKETS


def _bucket_tiles_causal(nq):
    q = np.arange(QB)[:, None]
    k = np.arange(QB)[None, :]
    tiles = [np.full((QB, QB), MASKED_BUCKET, np.int32)]
    for d in range(nq):
        rel = QB * d + q - k
        tiles.append(np.where(rel >= 0, _t5_bucket_np(rel), MASKED_BUCKET).astype(np.int32))
    return np.stack(tiles)


def _mask_tiles():
    q = np.arange(QB)[:, None]
    k = np.arange(QB)[None, :]
    diag = np.where(k <= q, 0.0, NEG_INF)
    return np.stack([np.full((QB, QB), NEG_INF), diag, np.zeros((QB, QB))]).astype(np.float32)


def _bucket_tiles_dilated():
    q = np.arange(QB)[:, None]
    k = np.arange(2 * QB)[None, :]
    rel = q + QB - k
    tiles = []
    for (window, d) in DILATED_PATTERNS:
        in_band = (rel >= 0) & (rel <= window // d)
        for has_prev in (False, True):
            ok = in_band & (has_prev | (k >= QB))
            tiles.append(np.where(ok, _t5_bucket_np(rel * d), MASKED_BUCKET).astype(np.int32))
    return np.stack(tiles)


def _bias_expand_kernel(table_ref, bucket_ref, out_ref, *, head0, scale):
    h = pl.program_id(0) + head0
    bk = bucket_ref[...]
    acc = jnp.where(bk == MASKED_BUCKET, NEG_INF, 0.0)
    for b in range(NUM_BUCKETS):
        acc = jnp.where(bk == b, table_ref[b, h] * scale, acc)
    out_ref[...] = acc


def _bias_expand(table, buckets, head0, nheads, scale=1.0):
    n, r, c = buckets.shape
    return pl.pallas_call(
        functools.partial(_bias_expand_kernel, head0=head0, scale=scale),
        grid=(nheads,),
        in_specs=[pl.BlockSpec(memory_space=pltpu.SMEM),
                  pl.BlockSpec((n, r, c), lambda h: (0, 0, 0))],
        out_specs=pl.BlockSpec((None, n, r, c), lambda h: (h, 0, 0, 0)),
        out_shape=jax.ShapeDtypeStruct((nheads, n, r, c), F32),
        compiler_params=_params(("arbitrary",)),
        name="bias_expand",
    )(table, buckets)


def _in_proj_kernel(x_ref, g_ref, w_ref, o32_ref, o16_ref):
    x = x_ref[...]
    ms = jnp.mean(x * x, axis=-1, keepdims=True)
    xn = x * lax.rsqrt(ms + EPS) * g_ref[...]
    p = _dot(xn.astype(BF16), w_ref[...])
    o32_ref[...] = p
    o16_ref[...] = p.astype(BF16)


def _in_proj(h, g, w, tm=256):
    bsz, seq, _ = h.shape
    return pl.pallas_call(
        _in_proj_kernel,
        grid=(bsz, seq // tm),
        in_specs=[pl.BlockSpec((None, tm, D_MODEL), lambda b, i: (b, i, 0)),
                  pl.BlockSpec((1, D_MODEL), lambda b, i: (0, 0)),
                  pl.BlockSpec((D_MODEL, NCOL), lambda b, i: (0, 0))],
        out_specs=[pl.BlockSpec((None, tm, NCOL), lambda b, i: (b, i, 0)),
                   pl.BlockSpec((None, tm, NCOL), lambda b, i: (b, i, 0))],
        out_shape=[jax.ShapeDtypeStruct((bsz, seq, NCOL), F32),
                   jax.ShapeDtypeStruct((bsz, seq, NCOL), BF16)],
        compiler_params=_params(("arbitrary", "arbitrary")),
        name="in_proj",
    )(h, g, w)


def _mla_prep_kernel(cq_ref, ckv_ref, kr_ref, krrot_ref, cos_ref, sin_ref, gq_ref, gkv_ref,
                     wq_ref, wqrot_ref, wk_ref, wv_ref, q_ref, k_ref, v_ref):
    cos = cos_ref[...]
    sin = sin_ref[...]
    cos4 = jnp.concatenate([cos] * A_HEADS, axis=1)
    sin4 = jnp.concatenate([sin] * A_HEADS, axis=1)
    cq = cq_ref[...]
    ms = jnp.sum(cq * cq, axis=-1, keepdims=True) * (1.0 / Q_LORA)
    nq = (cq * lax.rsqrt(ms + EPS) * gq_ref[...]).astype(BF16)
    q = _dot(nq, wq_ref[...]) * cos4 + _dot(nq, wqrot_ref[...]) * sin4
    q_ref[...] = (q * ((A_NOPE + A_ROPE) ** -0.5 * LOG2E)).astype(BF16)
    ckv = ckv_ref[...]
    ms = jnp.mean(ckv * ckv, axis=-1, keepdims=True)
    nkv = (ckv * lax.rsqrt(ms + EPS) * gkv_ref[...]).astype(BF16)
    kr = kr_ref[...] * cos + krrot_ref[...] * sin
    k = _dot(nkv, wk_ref[...]) + jnp.concatenate([kr] * A_HEADS, axis=1)
    k_ref[...] = k.astype(BF16)
    v_ref[...] = _dot(nkv, wv_ref[...]).astype(BF16)


def _mla_prep(p32, cos_t, sin_t, gq, gkv, wq, wqrot, wk, wv, tm=512):
    bsz, seq, _ = p32.shape
    w4 = A_HEADS * LANES
    row = lambda b, i: (b, i, 0)
    const = lambda b, i: (0, 0)
    return pl.pallas_call(
        _mla_prep_kernel,
        grid=(bsz, seq // tm),
        in_specs=[pl.BlockSpec((None, tm, 256), lambda b, i: (b, i, P_CQ // 256)),
                  pl.BlockSpec((None, tm, LANES), lambda b, i: (b, i, P_CKV // LANES)),
                  pl.BlockSpec((None, tm, LANES), lambda b, i: (b, i, P_KR // LANES)),
                  pl.BlockSpec((None, tm, LANES), lambda b, i: (b, i, P_KRROT // LANES)),
                  pl.BlockSpec((tm, LANES), lambda b, i: (i, 0)),
                  pl.BlockSpec((tm, LANES), lambda b, i: (i, 0)),
                  pl.BlockSpec((1, 256), const),
                  pl.BlockSpec((1, KV_LORA), const),
                  pl.BlockSpec((256, w4), const),
                  pl.BlockSpec((256, w4), const),
                  pl.BlockSpec((KV_LORA, w4), const),
                  pl.BlockSpec((KV_LORA, w4), const)],
        out_specs=[pl.BlockSpec((None, tm, w4), row)] * 3,
        out_shape=[jax.ShapeDtypeStruct((bsz, seq, w4), BF16)] * 3,
        compiler_params=_params(("arbitrary", "arbitrary")),
        name="mla_prep",
    )(p32, p32, p32, p32, cos_t, sin_t, gq, gkv, wq, wqrot, wk, wv)


def _silu(g):
    return g * (1.0 / (1.0 + jnp.exp(-g)))


def _by_causal_width(i, seq, body):
    per = KV_CHUNK // QB
    for wb in range(seq // KV_CHUNK):
        pl.when(i // per == wb)(functools.partial(body, (wb + 1) * KV_CHUNK))


def _mask_tail(mask_ref, i, width):
    first = (width - KV_CHUNK) // QB
    return jnp.concatenate([mask_ref[jnp.clip(i - j, -1, 1) + 1] for j in range(first, width // QB)], axis=1)


def _bias_row(bias_ref, h, i, width):
    return jnp.concatenate([bias_ref[h, jnp.maximum(i - j, -1) + 1] for j in range(width // QB)], axis=1)


def _softmax_pv(s, v):
    m = jnp.max(s, axis=-1, keepdims=True)
    e = jnp.exp2(s - m)
    l = jnp.sum(e, axis=-1, keepdims=True)
    return _dot(e.astype(BF16), v), l


def _lane_mask(width, seg, dtype):
    lane = lax.broadcasted_iota(jnp.int32, (1, LANES), 1)
    return jnp.where((lane >= seg * width) & (lane < (seg + 1) * width), 1.0, 0.0).astype(dtype)


def _mla_attn_kernel(q_ref, k_ref, v_ref, g_ref, mask_ref, o_ref):
    i = pl.program_id(1)
    seq = k_ref.shape[0]

    def body(width):
        head_w = width - KV_CHUNK
        tail = _mask_tail(mask_ref, i, width)
        pairs = []
        for g in range(A_HEADS // 2):
            acc = None
            for h in (2 * g, 2 * g + 1):
                sl = slice(h * LANES, (h + 1) * LANES)
                s = _dot_nt(q_ref[:, sl], k_ref[:width, sl])
                s = jnp.concatenate([s[:, :head_w], s[:, head_w:] + tail], axis=1) if head_w else s + tail
                pv, l = _softmax_pv(s, v_ref[:width, sl])
                o = pv / l
                acc = o if acc is None else acc + o
            pairs.append(acc)
        y = jnp.concatenate(pairs, axis=1)
        o_ref[...] = (y * _silu(g_ref[...])).astype(BF16)

    _by_causal_width(i, seq, body)


def _mla_attn(q, k, v, p32, mask_tiles):
    bsz, seq, w4 = q.shape
    return pl.pallas_call(
        _mla_attn_kernel,
        grid=(bsz, seq // QB),
        in_specs=[pl.BlockSpec((None, QB, w4), lambda b, i: (b, i, 0)),
                  pl.BlockSpec((None, seq, w4), lambda b, i: (b, 0, 0)),
                  pl.BlockSpec((None, seq, w4), lambda b, i: (b, 0, 0)),
                  pl.BlockSpec((None, QB, 256), lambda b, i: (b, i, P_AG // 256)),
                  pl.BlockSpec((3, QB, QB), lambda b, i: (0, 0, 0))],
        out_specs=pl.BlockSpec((None, QB, BRANCH_WIDTH), lambda b, i: (b, i, 0)),
        out_shape=jax.ShapeDtypeStruct((bsz, seq, BRANCH_WIDTH), BF16),
        compiler_params=_params(("arbitrary", "arbitrary")),
        name="mla_attn",
    )(q, k, v, p32, mask_tiles)


def _diff_attn_kernel(q_ref, k_ref, v_ref, g_ref, bias_ref, lam_ref, subln_ref, o_ref, *, lambda_init):
    i = pl.program_id(1)
    seq = k_ref.shape[0]

    def body(width):
        lp = lam_ref[...]
        lam = (jnp.exp(jnp.sum(lp[0:1] * lp[1:2], axis=-1, keepdims=True))
               - jnp.exp(jnp.sum(lp[2:3] * lp[3:4], axis=-1, keepdims=True)) + lambda_init)
        subln = subln_ref[...]
        pairs = []
        for g in range(2):
            sl = slice(g * LANES, (g + 1) * LANES)
            qg = q_ref[:, sl] * (D_QK ** -0.5 * LOG2E)
            kg = k_ref[:width, sl]
            vg = v_ref[:width, sl]
            acc = None
            for hh in range(2):
                h = 2 * g + hh
                bias = _bias_row(bias_ref, h, i, width)
                outs = []
                for mm in range(2):
                    qm = (qg * _lane_mask(D_QK, 2 * hh + mm, F32)).astype(BF16)
                    pv, l = _softmax_pv(_dot_nt(qm, kg) + bias, vg)
                    outs.append(pv / l)
                hm = _lane_mask(HEAD_DIM, hh, F32)
                a = (outs[0] - lam * outs[1]) * hm
                ms = jnp.sum(a * a, axis=-1, keepdims=True) * (1.0 / HEAD_DIM)
                a = a * lax.rsqrt(ms + EPS)
                acc = a if acc is None else acc + a
            pairs.append(acc * subln * (1.0 - lambda_init))
        y = jnp.concatenate(pairs, axis=1)
        o_ref[...] = (y * _silu(g_ref[...])).astype(BF16)

    _by_causal_width(i, seq, body)


def _diff_attn(p32, p16, bias, lam_params, subln, lambda_init):
    bsz, seq, _ = p32.shape
    nq = seq // QB
    return pl.pallas_call(
        functools.partial(_diff_attn_kernel, lambda_init=lambda_init),
        grid=(bsz, nq),
        in_specs=[pl.BlockSpec((None, QB, 256), lambda b, i: (b, i, P_DQ // 256)),
                  pl.BlockSpec((None, seq, 256), lambda b, i: (b, 0, P_DK // 256)),
                  pl.BlockSpec((None, seq, 256), lambda b, i: (b, 0, P_DV // 256)),
                  pl.BlockSpec((None, QB, 256), lambda b, i: (b, i, P_DG // 256)),
                  pl.BlockSpec((4, nq + 1, QB, QB), lambda b, i: (0, 0, 0, 0)),
                  pl.BlockSpec((4, D_QK), lambda b, i: (0, 0)),
                  pl.BlockSpec((1, LANES), lambda b, i: (0, 0))],
        out_specs=pl.BlockSpec((None, QB, BRANCH_WIDTH), lambda b, i: (b, i, 0)),
        out_shape=jax.ShapeDtypeStruct((bsz, seq, BRANCH_WIDTH), BF16),
        compiler_params=_params(("arbitrary", "arbitrary")),
        name="diff_attn",
    )(p32, p16, p16, p32, bias, lam_params, subln)


def _sortable_to_float(key):
    return pltpu.bitcast(jnp.where(key < 0, key ^ jnp.int32(0x7FFFFFFF), key), F32)


def _kth_largest(score_ref, width, k_top):
    rows = QB // BISECT_GROUPS

    def count_ge(g, key):
        thr = _sortable_to_float(key)
        x = score_ref[g * rows:(g + 1) * rows, :width]
        return jnp.sum(jnp.where(x >= thr, 1.0, 0.0), axis=-1, keepdims=True)

    int_min = jnp.full((rows, 1), -2 ** 31, jnp.int32)
    zero = jnp.zeros((rows, 1), jnp.int32)
    ts = [jnp.where(count_ge(g, zero) >= k_top, zero, int_min) for g in range(BISECT_GROUPS)]
    for bit in range(30, -1, -1):
        for g in range(BISECT_GROUPS):
            cand = ts[g] + jnp.int32(1 << bit)
            ts[g] = jnp.where(count_ge(g, cand) >= k_top, cand, ts[g])
    return jnp.concatenate([_sortable_to_float(t) for t in ts], axis=0)


def _first_ties(eq, need, col):
    rows = eq.shape[0]
    eqf = jnp.where(eq, 1.0, 0.0)
    nbits = int(eq.shape[1]).bit_length()

    def body(it, j):
        cand = j + (jnp.int32(1) << (nbits - 1 - it))
        cnt = jnp.sum(jnp.where(col < cand, eqf, 0.0), axis=-1, keepdims=True)
        return jnp.where(cnt <= need, cand, j)

    j = lax.fori_loop(0, nbits, body, jnp.zeros((rows, 1), jnp.int32))
    return eq & (col < j)


def _sparse_attn_kernel(q_ref, k_ref, v_ref, qi_ref, ki_ref, wi_ref, g_ref, bias_ref, o_ref,
                        score_ref, neg_ref, *, k_top):
    i = pl.program_id(1)
    seq = k_ref.shape[0]

    def body(width):
        head_w = width - KV_CHUNK
        ki = ki_ref[:width, :]
        wi = wi_ref[...] * (IDX_DIM ** -0.5 * IDX_HEADS ** -0.5)
        score = None
        for h in range(IDX_HEADS):
            qm = qi_ref[:, (h // 4) * LANES:(h // 4 + 1) * LANES] * _lane_mask(IDX_DIM, h % 4, BF16)
            term = jnp.maximum(_dot_nt(qm, ki), 0.0) * wi[:, h:h + 1]
            score = term if score is None else score + term
        t = i * QB + lax.broadcasted_iota(jnp.int32, (QB, KV_CHUNK), 0)
        tail_ok = head_w + lax.broadcasted_iota(jnp.int32, (QB, KV_CHUNK), 1) <= t
        if head_w:
            score_ref[:, :head_w] = score[:, :head_w]
        score_ref[:, head_w:width] = jnp.where(tail_ok, score[:, head_w:], NEG_INF)
        thr = _kth_largest(score_ref, width, k_top)
        keep_tail = (score_ref[:, head_w:width] >= thr) & tail_ok
        cnt = jnp.sum(jnp.where(keep_tail, 1.0, 0.0), axis=-1, keepdims=True)
        neg_ref[:, head_w:width] = jnp.where(keep_tail, 0.0, NEG_INF)
        if head_w:
            keep_head = score_ref[:, :head_w] >= thr
            cnt = cnt + jnp.sum(jnp.where(keep_head, 1.0, 0.0), axis=-1, keepdims=True)
            neg_ref[:, :head_w] = jnp.where(keep_head, 0.0, NEG_INF)

        @pl.when(jnp.max(cnt) > k_top)
        def _():
            x = score_ref[:, :width]
            gt = x > thr
            need = k_top - jnp.sum(jnp.where(gt, 1.0, 0.0), axis=-1, keepdims=True)
            col = lax.broadcasted_iota(jnp.int32, (QB, width), 1)
            neg_ref[:, :width] = jnp.where(gt | _first_ties(x == thr, need, col), 0.0, NEG_INF)

        pairs = []
        for g in range(2):
            sl = slice(g * LANES, (g + 1) * LANES)
            qg = q_ref[:, sl] * (HEAD_DIM ** -0.5 * LOG2E)
            kg = k_ref[:width, sl]
            vg = v_ref[:width, sl]
            acc = None
            for hh in range(2):
                hm = _lane_mask(HEAD_DIM, hh, F32)
                s = _dot_nt((qg * hm).astype(BF16), kg)
                s = s + (_bias_row(bias_ref, 2 * g + hh, i, width) + neg_ref[:, :width])
                pv, l = _softmax_pv(s, vg)
                o = pv / l * hm
                acc = o if acc is None else acc + o
            pairs.append(acc)
        y = jnp.concatenate(pairs, axis=1)
        o_ref[...] = (y * _silu(g_ref[...])).astype(BF16)

    _by_causal_width(i, seq, body)


def _sparse_attn(p32, p16, bias, k_top):
    bsz, seq, _ = p32.shape
    nq = seq // QB
    return pl.pallas_call(
        functools.partial(_sparse_attn_kernel, k_top=k_top),
        grid=(bsz, nq),
        in_specs=[pl.BlockSpec((None, QB, 256), lambda b, i: (b, i, P_CQ2 // 256)),
                  pl.BlockSpec((None, seq, 256), lambda b, i: (b, 0, P_CK // 256)),
                  pl.BlockSpec((None, seq, 256), lambda b, i: (b, 0, P_CV // 256)),
                  pl.BlockSpec((None, QB, 256), lambda b, i: (b, i, P_CQI // 256)),
                  pl.BlockSpec((None, seq, LANES), lambda b, i: (b, 0, P_CKI // LANES)),
                  pl.BlockSpec((None, QB, LANES), lambda b, i: (b, i, P_CWI // LANES)),
                  pl.BlockSpec((None, QB, 256), lambda b, i: (b, i, P_CG // 256)),
                  pl.BlockSpec((4, nq + 1, QB, QB), lambda b, i: (0, 0, 0, 0))],
        out_specs=pl.BlockSpec((None, QB, BRANCH_WIDTH), lambda b, i: (b, i, 0)),
        out_shape=jax.ShapeDtypeStruct((bsz, seq, BRANCH_WIDTH), BF16),
        scratch_shapes=[pltpu.VMEM((QB, seq), F32)] * 2,
        compiler_params=_params(("arbitrary", "arbitrary")),
        name="sparse_attn",
    )(p32, p16, p16, p16, p16, p32, p32, bias)


def _dilated_kernel(q0_ref, q1_ref, k0_ref, k1_ref, v0_ref, v1_ref, g_ref, bias_ref, o_ref,
                    m_ref, l_ref, acc_ref):
    seq = q0_ref.shape[0]
    q_refs, k_refs, v_refs = (q0_ref, q1_ref), (k0_ref, k1_ref), (v0_ref, v1_ref)
    lane = lax.broadcasted_iota(jnp.int32, (QB, LANES), 1)
    qmasks = [_lane_mask(HEAD_DIM, hh, F32) * (HEAD_DIM ** -0.5) for hh in range(2)]
    for p, (window, d) in enumerate(DILATED_PATTERNS):
        nb = seq // d // QB

        def tile(n, carry, p=p, d=d):
            r = n % d
            j = n // d
            start = r + d * QB * j
            prev = jnp.maximum(start - d * QB, r)
            has_prev = jnp.minimum(j, 1)
            cur_rows = pl.ds(start, QB, stride=d)
            prev_rows = pl.ds(prev, QB, stride=d)
            for g in range(2):
                q = q_refs[g][cur_rows, :]
                kcat = jnp.concatenate([k_refs[g][prev_rows, :], k_refs[g][cur_rows, :]], axis=0).astype(BF16)
                vcat = jnp.concatenate([v_refs[g][prev_rows, :], v_refs[g][cur_rows, :]], axis=0).astype(BF16)
                ms, ls, accs = [], [], []
                for hh in range(2):
                    s = _dot_nt((q * qmasks[hh]).astype(BF16), kcat) + bias_ref[2 * g + hh, 2 * p + has_prev]
                    m = jnp.max(s, axis=-1, keepdims=True)
                    e = jnp.exp(s - m)
                    ms.append(m)
                    ls.append(jnp.sum(e, axis=-1, keepdims=True))
                    accs.append(_dot(e.astype(BF16), vcat))
                first = lane < HEAD_DIM
                m_new = jnp.where(first, ms[0], ms[1])
                l_new = jnp.where(first, ls[0], ls[1])
                a_new = jnp.where(first, accs[0], accs[1])
                if p == 0:
                    m_ref[g, cur_rows, :] = m_new
                    l_ref[g, cur_rows, :] = l_new
                    acc_ref[g, cur_rows, :] = a_new
                else:
                    m_old = m_ref[g, cur_rows, :]
                    m_tot = jnp.maximum(m_old, m_new)
                    w_old = jnp.exp(m_old - m_tot)
                    w_new = jnp.exp(m_new - m_tot)
                    m_ref[g, cur_rows, :] = m_tot
                    l_ref[g, cur_rows, :] = w_old * l_ref[g, cur_rows, :] + w_new * l_new
                    acc_ref[g, cur_rows, :] = w_old * acc_ref[g, cur_rows, :] + w_new * a_new
            return carry

        lax.fori_loop(0, d * nb, tile, 0)
    gate = _silu(g_ref[...])
    for g in range(2):
        sl = slice(g * LANES, (g + 1) * LANES)
        o_ref[:, sl] = (acc_ref[g] / l_ref[g] * gate[:, sl]).astype(BF16)


def _dilated_attn(p32, bias):
    bsz, seq, _ = p32.shape

    def slab(off):
        return pl.BlockSpec((None, seq, LANES), lambda b, off=off: (b, 0, off // LANES))

    return pl.pallas_call(
        _dilated_kernel,
        grid=(bsz,),
        in_specs=[slab(P_BQ), slab(P_BQ + LANES), slab(P_BK), slab(P_BK + LANES),
                  slab(P_BV), slab(P_BV + LANES),
                  pl.BlockSpec((None, seq, 256), lambda b: (b, 0, P_BG // 256)),
                  pl.BlockSpec((4, 2 * len(DILATED_PATTERNS), QB, 2 * QB), lambda b: (0, 0, 0, 0))],
        out_specs=pl.BlockSpec((None, seq, BRANCH_WIDTH), lambda b: (b, 0, 0)),
        out_shape=jax.ShapeDtypeStruct((bsz, seq, BRANCH_WIDTH), BF16),
        scratch_shapes=[pltpu.VMEM((2, seq, LANES), F32)] * 3,
        compiler_params=_params(("arbitrary",)),
        name="dilated_attn",
    )(p32, p32, p32, p32, p32, p32, p32, bias)


def _out_proj_kernel(ya_ref, yb_ref, yc_ref, yd_ref, w_ref, h_ref, g_ref, o_ref):
    y = None
    for n, ref in enumerate((ya_ref, yb_ref, yc_ref, yd_ref)):
        t = _dot(ref[...], w_ref[n * BRANCH_WIDTH:(n + 1) * BRANCH_WIDTH, :])
        y = t if y is None else y + t
    ms = jnp.mean(y * y, axis=-1, keepdims=True)
    o_ref[...] = h_ref[...] + y * lax.rsqrt(ms + EPS) * g_ref[...]


def _out_proj(ya, yb, yc, yd, w, h, g, tm=512):
    bsz, seq, _ = h.shape
    yspec = pl.BlockSpec((None, tm, BRANCH_WIDTH), lambda b, i: (b, i, 0))
    return pl.pallas_call(
        _out_proj_kernel,
        grid=(bsz, seq // tm),
        in_specs=[yspec, yspec, yspec, yspec,
                  pl.BlockSpec((4 * BRANCH_WIDTH, D_MODEL), lambda b, i: (0, 0)),
                  pl.BlockSpec((None, tm, D_MODEL), lambda b, i: (b, i, 0)),
                  pl.BlockSpec((1, D_MODEL), lambda b, i: (0, 0))],
        out_specs=pl.BlockSpec((None, tm, D_MODEL), lambda b, i: (b, i, 0)),
        out_shape=jax.ShapeDtypeStruct(h.shape, F32),
        compiler_params=_params(("arbitrary", "arbitrary")),
        name="out_proj",
    )(ya, yb, yc, yd, w, h, g)


def kernel(x, w_in, w_out, norm_pre, norm_post, mla_q_norm, mla_kv_norm, mla_w_uq, mla_w_ukv,
           diff_lambda, diff_subln, rel_bias):
    bsz, seq, _ = x.shape
    depth = w_in.shape[0]
    nq = seq // QB
    k_top = min(IDX_TOPK_MAX, seq // 4)
    cos_t, sin_t = _rope_tables(seq)
    causal_buckets = jnp.asarray(_bucket_tiles_causal(nq))
    bias_b = _bias_expand(rel_bias, jnp.asarray(_bucket_tiles_dilated()), 0, 4)
    bias_c = _bias_expand(rel_bias, causal_buckets, 4, 4, LOG2E)
    bias_d = _bias_expand(rel_bias, causal_buckets, 8, 4, LOG2E)
    mask_tiles = jnp.asarray(_mask_tiles())
    h = x
    for layer in range(depth):
        w_arr = _arrange_w_in(w_in[layer]).astype(BF16)
        wq, wqrot, wk, wv = _arrange_mla(mla_w_uq[layer], mla_w_ukv[layer])
        gq = jnp.concatenate([mla_q_norm[layer], jnp.ones((256 - Q_LORA,), F32)])[None, :]
        gkv = mla_kv_norm[layer][None, :]
        p32, p16 = _in_proj(h, norm_pre[layer][None, :], w_arr)
        qa, ka, va = _mla_prep(p32, cos_t, sin_t, gq, gkv, wq, wqrot, wk, wv)
        y_a = _mla_attn(qa, ka, va, p32, mask_tiles)
        y_b = _dilated_attn(p32, bias_b)
        y_c = _sparse_attn(p32, p16, bias_c, k_top)
        lambda_init = 0.8 - 0.6 * math.exp(-0.3 * layer)
        subln = jnp.concatenate([diff_subln[layer], diff_subln[layer]])[None, :]
        y_d = _diff_attn(p32, p16, bias_d, diff_lambda[layer], subln, lambda_init)
        h = _out_proj(y_a, y_b, y_c, y_d, w_out[layer].astype(BF16), h, norm_post[layer][None, :])
    return h
```

```python
import functools
import math

import jax
import jax.numpy as jnp
import numpy as np
from jax import lax
from jax.experimental import pallas as pl
from jax.experimental.pallas import tpu as pltpu

F32 = jnp.float32
BF16 = jnp.bfloat16

D_MODEL = 1024
A_HEADS, A_NOPE, A_ROPE, A_V = 4, 64, 32, 64
Q_LORA, KV_LORA = 192, 128
ROPE_THETA = 10000.0
HEAD_DIM = 64
DILATED_PATTERNS = ((128, 1), (512, 4), (2048, 16))
IDX_HEADS, IDX_DIM, IDX_TOPK_MAX = 8, 32, 256
D_QK = 32
BRANCH_WIDTH = 256
NUM_BUCKETS, MAX_DISTANCE = 32, 2048
NEG_INF = -1e30
EPS = 1e-6
LOG2E = math.log2(math.e)
KV_CHUNK = 512
COL_ACC_ROWS = 64
LANES = 128
QB = 128
VMEM_LIMIT = 56 * 1024 * 1024

_SPLIT = (Q_LORA, KV_LORA, A_ROPE, 256, 256, 256, 256, 256, 256, 256, 256, IDX_HEADS * IDX_DIM, IDX_DIM,
          IDX_HEADS, 256, 256, 256, 256, 256)
_OFF = np.concatenate([[0], np.cumsum(_SPLIT)]).tolist()
(_A_CQ, _A_CKV, _A_KR, _A_G, _B_Q, _B_K, _B_V, _B_G, _C_Q, _C_K, _C_V, _C_QI, _C_KI, _C_WI, _C_G,
 _D_Q, _D_K, _D_V, _D_G) = range(19)

P_CQ, P_AG, P_CKV, P_KR, P_KRROT, P_CKI = 0, 256, 512, 640, 768, 896
P_BQ, P_BK, P_BV, P_BG = 1024, 1280, 1536, 1792
P_CQ2, P_CK, P_CV, P_CQI, P_CG = 2048, 2304, 2560, 2816, 3072
P_DQ, P_DK, P_DV, P_DG = 3328, 3584, 3840, 4096
P_CWI = 4352
NCOL = 4480


def _dot(a, b):
    return jnp.dot(a, b, preferred_element_type=F32)


def _dot_nt(a, b):
    return lax.dot_general(a, b, (((1,), (1,)), ((), ())), preferred_element_type=F32)


def _params(sem):
    return pltpu.CompilerParams(dimension_semantics=sem, vmem_limit_bytes=VMEM_LIMIT)


def _rot_cols(w):
    half = w.shape[-1] // 2
    return jnp.concatenate([-w[..., half:], w[..., :half]], axis=-1)


def _arrange_w_in(w):
    def seg(i):
        return w[:, _OFF[i]:_OFF[i + 1]]

    def z(n):
        return jnp.zeros((w.shape[0], n), w.dtype)

    kr = seg(_A_KR)
    cols = [
        seg(_A_CQ), z(64),
        seg(_A_G),
        seg(_A_CKV),
        z(64), kr, z(32),
        z(64), _rot_cols(kr), z(32),
        seg(_C_KI), seg(_C_KI), seg(_C_KI), seg(_C_KI),
        seg(_B_Q), seg(_B_K), seg(_B_V), seg(_B_G),
        seg(_C_Q), seg(_C_K), seg(_C_V), seg(_C_QI), seg(_C_G),
        seg(_D_Q), seg(_D_K), seg(_D_V), seg(_D_G),
        seg(_C_WI), z(LANES - IDX_HEADS),
    ]
    out = jnp.concatenate(cols, axis=1)
    assert out.shape[1] == NCOL
    return out


def _arrange_mla(w_uq, w_ukv):
    wq = w_uq.reshape(Q_LORA, A_HEADS, A_NOPE + A_ROPE)
    nope, rope = wq[..., :A_NOPE], wq[..., A_NOPE:]
    zq = jnp.zeros((Q_LORA, A_HEADS, LANES - A_NOPE - A_ROPE), w_uq.dtype)
    wq_main = jnp.concatenate([nope, rope, zq], axis=-1).reshape(Q_LORA, A_HEADS * LANES)
    wq_rot = jnp.concatenate([jnp.zeros_like(nope), _rot_cols(rope), zq], axis=-1).reshape(Q_LORA, A_HEADS * LANES)
    pad = jnp.zeros((256 - Q_LORA, A_HEADS * LANES), w_uq.dtype)
    wq_main = jnp.concatenate([wq_main, pad], axis=0)
    wq_rot = jnp.concatenate([wq_rot, pad], axis=0)
    wkv = w_ukv.reshape(KV_LORA, A_HEADS, A_NOPE + A_V)
    knope, v = wkv[..., :A_NOPE], wkv[..., A_NOPE:]
    wk = jnp.concatenate([knope, jnp.zeros_like(knope)], axis=-1).reshape(KV_LORA, A_HEADS * LANES)
    zv = jnp.zeros_like(v)
    wv = jnp.stack([jnp.concatenate([v[:, h], zv[:, h]] if h % 2 == 0 else [zv[:, h], v[:, h]], axis=-1)
                    for h in range(A_HEADS)], axis=1).reshape(KV_LORA, A_HEADS * LANES)
    return wq_main.astype(BF16), wq_rot.astype(BF16), wk.astype(BF16), wv.astype(BF16)


def _rope_tables(seq):
    inv = ROPE_THETA ** (-jnp.arange(0, A_ROPE, 2, dtype=F32) / A_ROPE)
    ang = jnp.arange(seq, dtype=F32)[:, None] * inv[None, :]
    cos, sin = jnp.cos(ang), jnp.sin(ang)
    one = jnp.ones((seq, A_NOPE), F32)
    zero = jnp.zeros((seq, LANES - A_NOPE - A_ROPE), F32)
    cos_t = jnp.concatenate([one, cos, cos, zero], axis=1)
    sin_t = jnp.concatenate([jnp.zeros_like(one), sin, sin, zero], axis=1)
    return cos_t, sin_t


def _t5_bucket_np(rel):
    n = np.maximum(rel, 0)
    max_exact = NUM_BUCKETS // 2
    nf = np.maximum(n, max_exact).astype(np.float64)
    large = max_exact + (np.log(nf / max_exact) / math.log(MAX_DISTANCE / max_exact)
                         * (NUM_BUCKETS - max_exact)).astype(np.int32)
    large = np.minimum(large, NUM_BUCKETS - 1)
    return np.where(n < max_exact, n, large).astype(np.int32)


MASKED_BUCKET = NUM_BUCKETS


def _bucket_tiles_causal(nq):
    q = np.arange(QB)[:, None]
    k = np.arange(QB)[None, :]
    tiles = [np.full((QB, QB), MASKED_BUCKET, np.int32)]
    for d in range(nq):
        rel = QB * d + q - k
        tiles.append(np.where(rel >= 0, _t5_bucket_np(rel), MASKED_BUCKET).astype(np.int32))
    return np.stack(tiles)


def _mask_tiles():
    q = np.arange(QB)[:, None]
    k = np.arange(QB)[None, :]
    diag = np.where(k <= q, 0.0, NEG_INF)
    return np.stack([np.full((QB, QB), NEG_INF), diag, np.zeros((QB, QB))]).astype(np.float32)


def _bucket_tiles_dilated():
    q = np.arange(QB)[:, None]
    k = np.arange(2 * QB)[None, :]
    rel = q + QB - k
    tiles = []
    for (window, d) in DILATED_PATTERNS:
        in_band = (rel >= 0) & (rel <= window // d)
        for has_prev in (False, True):
            ok = in_band & (has_prev | (k >= QB))
            tiles.append(np.where(ok, _t5_bucket_np(rel * d), MASKED_BUCKET).astype(np.int32))
    return np.stack(tiles)


def _bias_expand_kernel(table_ref, bucket_ref, out_ref, *, head0, scale):
    h = pl.program_id(0) + head0
    bk = bucket_ref[...]
    acc = jnp.where(bk == MASKED_BUCKET, NEG_INF, 0.0)
    for b in range(NUM_BUCKETS):
        acc = jnp.where(bk == b, table_ref[b, h] * scale, acc)
    out_ref[...] = acc


def _bias_expand(table, buckets, head0, nheads, scale=1.0):
    n, r, c = buckets.shape
    return pl.pallas_call(
        functools.partial(_bias_expand_kernel, head0=head0, scale=scale),
        grid=(nheads,),
        in_specs=[pl.BlockSpec(memory_space=pltpu.SMEM),
                  pl.BlockSpec((n, r, c), lambda h: (0, 0, 0))],
        out_specs=pl.BlockSpec((None, n, r, c), lambda h: (h, 0, 0, 0)),
        out_shape=jax.ShapeDtypeStruct((nheads, n, r, c), F32),
        compiler_params=_params(("arbitrary",)),
        name="bias_expand",
    )(table, buckets)


def _in_proj_kernel(x_ref, g_ref, w_ref, o32_ref, o16_ref):
    x = x_ref[...]
    ms = jnp.mean(x * x, axis=-1, keepdims=True)
    xn = x * lax.rsqrt(ms + EPS) * g_ref[...]
    p = _dot(xn.astype(BF16), w_ref[...])
    o32_ref[...] = p
    o16_ref[...] = p.astype(BF16)


def _in_proj(h, g, w, tm=256):
    bsz, seq, _ = h.shape
    return pl.pallas_call(
        _in_proj_kernel,
        grid=(bsz, seq // tm),
        in_specs=[pl.BlockSpec((None, tm, D_MODEL), lambda b, i: (b, i, 0)),
                  pl.BlockSpec((1, D_MODEL), lambda b, i: (0, 0)),
                  pl.BlockSpec((D_MODEL, NCOL), lambda b, i: (0, 0))],
        out_specs=[pl.BlockSpec((None, tm, NCOL), lambda b, i: (b, i, 0)),
                   pl.BlockSpec((None, tm, NCOL), lambda b, i: (b, i, 0))],
        out_shape=[jax.ShapeDtypeStruct((bsz, seq, NCOL), F32),
                   jax.ShapeDtypeStruct((bsz, seq, NCOL), BF16)],
        compiler_params=_params(("arbitrary", "arbitrary")),
        name="in_proj",
    )(h, g, w)


def _mla_prep_kernel(cq_ref, ckv_ref, kr_ref, krrot_ref, cos_ref, sin_ref, gq_ref, gkv_ref,
                     wq_ref, wqrot_ref, wk_ref, wv_ref, q_ref, k_ref, v_ref):
    cos = cos_ref[...]
    sin = sin_ref[...]
    cos4 = jnp.concatenate([cos] * A_HEADS, axis=1)
    sin4 = jnp.concatenate([sin] * A_HEADS, axis=1)
    cq = cq_ref[...]
    ms = jnp.sum(cq * cq, axis=-1, keepdims=True) * (1.0 / Q_LORA)
    nq = (cq * lax.rsqrt(ms + EPS) * gq_ref[...]).astype(BF16)
    q = _dot(nq, wq_ref[...]) * cos4 + _dot(nq, wqrot_ref[...]) * sin4
    q_ref[...] = (q * ((A_NOPE + A_ROPE) ** -0.5 * LOG2E)).astype(BF16)
    ckv = ckv_ref[...]
    ms = jnp.mean(ckv * ckv, axis=-1, keepdims=True)
    nkv = (ckv * lax.rsqrt(ms + EPS) * gkv_ref[...]).astype(BF16)
    kr = kr_ref[...] * cos + krrot_ref[...] * sin
    k = _dot(nkv, wk_ref[...]) + jnp.concatenate([kr] * A_HEADS, axis=1)
    k_ref[...] = k.astype(BF16)
    v_ref[...] = _dot(nkv, wv_ref[...]).astype(BF16)


def _mla_prep(p32, cos_t, sin_t, gq, gkv, wq, wqrot, wk, wv, tm=512):
    bsz, seq, _ = p32.shape
    w4 = A_HEADS * LANES
    row = lambda b, i: (b, i, 0)
    const = lambda b, i: (0, 0)
    return pl.pallas_call(
        _mla_prep_kernel,
        grid=(bsz, seq // tm),
        in_specs=[pl.BlockSpec((None, tm, 256), lambda b, i: (b, i, P_CQ // 256)),
                  pl.BlockSpec((None, tm, LANES), lambda b, i: (b, i, P_CKV // LANES)),
                  pl.BlockSpec((None, tm, LANES), lambda b, i: (b, i, P_KR // LANES)),
                  pl.BlockSpec((None, tm, LANES), lambda b, i: (b, i, P_KRROT // LANES)),
                  pl.BlockSpec((tm, LANES), lambda b, i: (i, 0)),
                  pl.BlockSpec((tm, LANES), lambda b, i: (i, 0)),
                  pl.BlockSpec((1, 256), const),
                  pl.BlockSpec((1, KV_LORA), const),
                  pl.BlockSpec((256, w4), const),
                  pl.BlockSpec((256, w4), const),
                  pl.BlockSpec((KV_LORA, w4), const),
                  pl.BlockSpec((KV_LORA, w4), const)],
        out_specs=[pl.BlockSpec((None, tm, w4), row)] * 3,
        out_shape=[jax.ShapeDtypeStruct((bsz, seq, w4), BF16)] * 3,
        compiler_params=_params(("arbitrary", "arbitrary")),
        name="mla_prep",
    )(p32, p32, p32, p32, cos_t, sin_t, gq, gkv, wq, wqrot, wk, wv)


def _silu(g):
    return g * (1.0 / (1.0 + jnp.exp(-g)))


def _by_causal_width(i, seq, body):
    per = KV_CHUNK // QB
    for wb in range(seq // KV_CHUNK):
        pl.when(i // per == wb)(functools.partial(body, (wb + 1) * KV_CHUNK))


def _mask_tail(mask_ref, i, width):
    first = (width - KV_CHUNK) // QB
    return jnp.concatenate([mask_ref[jnp.clip(i - j, -1, 1) + 1] for j in range(first, width // QB)], axis=1)


def _bias_row(bias_ref, h, i, width):
    return jnp.concatenate([bias_ref[h, jnp.maximum(i - j, -1) + 1] for j in range(width // QB)], axis=1)


def _col_reduce(x, op):
    rows, lanes = x.shape
    part = op(x.reshape(rows // COL_ACC_ROWS, COL_ACC_ROWS, lanes), axis=0)
    return op(part, axis=0, keepdims=True)


def _bias_col(bias_ref, h, i, width):
    return jnp.concatenate([bias_ref[h, jnp.maximum(i - j, -1) + 1] for j in range(width // QB)], axis=0)


def _softmax_pv(s, v):
    m = jnp.max(s, axis=-1, keepdims=True)
    e = jnp.exp2(s - m)
    l = jnp.sum(e, axis=-1, keepdims=True)
    return _dot(e.astype(BF16), v), l


def _lane_mask(width, seg, dtype):
    lane = lax.broadcasted_iota(jnp.int32, (1, LANES), 1)
    return jnp.where((lane >= seg * width) & (lane < (seg + 1) * width), 1.0, 0.0).astype(dtype)


def _mla_attn_kernel(q_ref, k_ref, v_ref, g_ref, mask_ref, o_ref):
    i = pl.program_id(1)
    seq = k_ref.shape[0]

    def body(width):
        head_w = width - KV_CHUNK
        tail = _mask_tail(mask_ref, i, width)
        pairs = []
        for g in range(A_HEADS // 2):
            acc = None
            for h in (2 * g, 2 * g + 1):
                sl = slice(h * LANES, (h + 1) * LANES)
                s = _dot_nt(q_ref[:, sl], k_ref[:width, sl])
                s = jnp.concatenate([s[:, :head_w], s[:, head_w:] + tail], axis=1) if head_w else s + tail
                pv, l = _softmax_pv(s, v_ref[:width, sl])
                o = pv / l
                acc = o if acc is None else acc + o
            pairs.append(acc)
        y = jnp.concatenate(pairs, axis=1)
        o_ref[...] = (y * _silu(g_ref[...])).astype(BF16)

    _by_causal_width(i, seq, body)


def _mla_attn(q, k, v, p32, mask_tiles):
    bsz, seq, w4 = q.shape
    return pl.pallas_call(
        _mla_attn_kernel,
        grid=(bsz, seq // QB),
        in_specs=[pl.BlockSpec((None, QB, w4), lambda b, i: (b, i, 0)),
                  pl.BlockSpec((None, seq, w4), lambda b, i: (b, 0, 0)),
                  pl.BlockSpec((None, seq, w4), lambda b, i: (b, 0, 0)),
                  pl.BlockSpec((None, QB, 256), lambda b, i: (b, i, P_AG // 256)),
                  pl.BlockSpec((3, QB, QB), lambda b, i: (0, 0, 0))],
        out_specs=pl.BlockSpec((None, QB, BRANCH_WIDTH), lambda b, i: (b, i, 0)),
        out_shape=jax.ShapeDtypeStruct((bsz, seq, BRANCH_WIDTH), BF16),
        compiler_params=_params(("arbitrary", "arbitrary")),
        name="mla_attn",
    )(q, k, v, p32, mask_tiles)


def _diff_attn_kernel(q_ref, k_ref, v_ref, g_ref, bias_ref, lam_ref, subln_ref, o_ref, *, lambda_init):
    i = pl.program_id(1)
    seq = k_ref.shape[0]

    def body(width):
        lp = lam_ref[...]
        lam = (jnp.exp(jnp.sum(lp[0:1] * lp[1:2], axis=-1, keepdims=True))
               - jnp.exp(jnp.sum(lp[2:3] * lp[3:4], axis=-1, keepdims=True)) + lambda_init)
        subln = subln_ref[...]
        pairs = []
        for g in range(2):
            sl = slice(g * LANES, (g + 1) * LANES)
            qg = q_ref[:, sl] * (D_QK ** -0.5 * LOG2E)
            kg = k_ref[:width, sl]
            vg = v_ref[:width, sl]
            acc = None
            for hh in range(2):
                h = 2 * g + hh
                bias = _bias_row(bias_ref, h, i, width)
                outs = []
                for mm in range(2):
                    qm = (qg * _lane_mask(D_QK, 2 * hh + mm, F32)).astype(BF16)
                    pv, l = _softmax_pv(_dot_nt(qm, kg) + bias, vg)
                    outs.append(pv / l)
                hm = _lane_mask(HEAD_DIM, hh, F32)
                a = (outs[0] - lam * outs[1]) * hm
                ms = jnp.sum(a * a, axis=-1, keepdims=True) * (1.0 / HEAD_DIM)
                a = a * lax.rsqrt(ms + EPS)
                acc = a if acc is None else acc + a
            pairs.append(acc * subln * (1.0 - lambda_init))
        y = jnp.concatenate(pairs, axis=1)
        o_ref[...] = (y * _silu(g_ref[...])).astype(BF16)

    _by_causal_width(i, seq, body)


def _diff_attn(p32, p16, bias, lam_params, subln, lambda_init):
    bsz, seq, _ = p32.shape
    nq = seq // QB
    return pl.pallas_call(
        functools.partial(_diff_attn_kernel, lambda_init=lambda_init),
        grid=(bsz, nq),
        in_specs=[pl.BlockSpec((None, QB, 256), lambda b, i: (b, i, P_DQ // 256)),
                  pl.BlockSpec((None, seq, 256), lambda b, i: (b, 0, P_DK // 256)),
                  pl.BlockSpec((None, seq, 256), lambda b, i: (b, 0, P_DV // 256)),
                  pl.BlockSpec((None, QB, 256), lambda b, i: (b, i, P_DG // 256)),
                  pl.BlockSpec((4, nq + 1, QB, QB), lambda b, i: (0, 0, 0, 0)),
                  pl.BlockSpec((4, D_QK), lambda b, i: (0, 0)),
                  pl.BlockSpec((1, LANES), lambda b, i: (0, 0))],
        out_specs=pl.BlockSpec((None, QB, BRANCH_WIDTH), lambda b, i: (b, i, 0)),
        out_shape=jax.ShapeDtypeStruct((bsz, seq, BRANCH_WIDTH), BF16),
        compiler_params=_params(("arbitrary", "arbitrary")),
        name="diff_attn",
    )(p32, p16, p16, p32, bias, lam_params, subln)


def _sortable_to_float(key):
    return pltpu.bitcast(jnp.where(key < 0, key ^ jnp.int32(0x7FFFFFFF), key), F32)


def _kth_largest(score_ref, width, k_top):
    def count_ge(key):
        thr = _sortable_to_float(key)
        return _col_reduce(jnp.where(score_ref[:width, :] >= thr, 1.0, 0.0), jnp.sum)

    int_min = jnp.full((1, QB), -2 ** 31, jnp.int32)
    zero = jnp.zeros((1, QB), jnp.int32)
    t = jnp.where(count_ge(zero) >= k_top, zero, int_min)
    for bit in range(30, -1, -1):
        cand = t + jnp.int32(1 << bit)
        t = jnp.where(count_ge(cand) >= k_top, cand, t)
    return _sortable_to_float(t)


def _first_ties(eq, need, row):
    eqf = jnp.where(eq, 1.0, 0.0)
    nbits = int(eq.shape[0]).bit_length()

    def body(it, j):
        cand = j + (jnp.int32(1) << (nbits - 1 - it))
        cnt = _col_reduce(jnp.where(row < cand, eqf, 0.0), jnp.sum)
        return jnp.where(cnt <= need, cand, j)

    j = lax.fori_loop(0, nbits, body, jnp.zeros((1, eq.shape[1]), jnp.int32))
    return eq & (row < j)


def _sparse_attn_kernel(q_ref, k_ref, v_ref, qi_ref, ki_ref, wi_ref, g_ref, bias_ref, o_ref,
                        score_ref, neg_ref, vt_ref, *, k_top):
    i = pl.program_id(1)
    seq = k_ref.shape[0]

    @pl.when(i == 0)
    def _():
        for j in range(seq // QB):
            for g in range(2):
                tile = v_ref[j * QB:(j + 1) * QB, g * LANES:(g + 1) * LANES]
                vt_ref[g * LANES:(g + 1) * LANES, j * QB:(j + 1) * QB] = tile.T.astype(BF16)

    def body(width):
        head_w = width - KV_CHUNK
        ki = ki_ref[:width, :]
        w_t = wi_ref[...].T * (IDX_DIM ** -0.5 * IDX_HEADS ** -0.5)
        score = None
        for h in range(0, IDX_HEADS, 2):
            qg = qi_ref[:, (h // 4) * LANES:(h // 4 + 1) * LANES]
            qcat = jnp.concatenate([qg * _lane_mask(IDX_DIM, h % 4, BF16),
                                    qg * _lane_mask(IDX_DIM, h % 4 + 1, BF16)], axis=0)
            logit = _dot_nt(ki, qcat)
            term = (jnp.maximum(logit[:, :QB], 0.0) * w_t[h:h + 1]
                    + jnp.maximum(logit[:, QB:], 0.0) * w_t[h + 1:h + 2])
            score = term if score is None else score + term
        s_idx = head_w + lax.broadcasted_iota(jnp.int32, (KV_CHUNK, QB), 0)
        tail_ok = s_idx <= i * QB + lax.broadcasted_iota(jnp.int32, (KV_CHUNK, QB), 1)
        if head_w:
            score_ref[:head_w, :] = score[:head_w]
        score_ref[head_w:width, :] = jnp.where(tail_ok, score[head_w:], NEG_INF)
        s2s = []
        for g in range(2):
            sl = slice(g * LANES, (g + 1) * LANES)
            qg = q_ref[:, sl] * (HEAD_DIM ** -0.5 * LOG2E)
            qcat = jnp.concatenate([(qg * _lane_mask(HEAD_DIM, hh, F32)).astype(BF16) for hh in range(2)], axis=0)
            s2s.append(_dot_nt(k_ref[:width, sl], qcat))
        thr = _kth_largest(score_ref, width, k_top)
        keep_tail = (score_ref[head_w:width, :] >= thr) & tail_ok
        cnt = _col_reduce(jnp.where(keep_tail, 1.0, 0.0), jnp.sum)
        neg_ref[head_w:width, :] = jnp.where(keep_tail, 0.0, NEG_INF)
        if head_w:
            keep_head = score_ref[:head_w, :] >= thr
            cnt = cnt + _col_reduce(jnp.where(keep_head, 1.0, 0.0), jnp.sum)
            neg_ref[:head_w, :] = jnp.where(keep_head, 0.0, NEG_INF)

        @pl.when(jnp.max(cnt) > k_top)
        def _():
            x = score_ref[:width, :]
            gt = x > thr
            need = k_top - _col_reduce(jnp.where(gt, 1.0, 0.0), jnp.sum)
            row = lax.broadcasted_iota(jnp.int32, (width, QB), 0)
            neg_ref[:width, :] = jnp.where(gt | _first_ties(x == thr, need, row), 0.0, NEG_INF)

        outs = []
        for g in range(2):
            sl = slice(g * LANES, (g + 1) * LANES)
            s2 = s2s[g]
            es, ls = [], []
            for hh in range(2):
                s = s2[:, hh * QB:(hh + 1) * QB] + (_bias_col(bias_ref, 2 * g + hh, i, width) + neg_ref[:width, :])
                e = jnp.exp2(s - _col_reduce(s, jnp.max))
                ls.append(_col_reduce(e, jnp.sum))
                es.append(e.astype(BF16))
            ot = _dot(vt_ref[sl, :width], jnp.concatenate(es, axis=1))
            outs.append(ot[:HEAD_DIM, :QB] / ls[0])
            outs.append(ot[HEAD_DIM:, QB:] / ls[1])
        y_t = jnp.concatenate(outs, axis=0)
        y = jnp.concatenate([y_t[:LANES].T, y_t[LANES:].T], axis=1)
        o_ref[...] = (y * _silu(g_ref[...])).astype(BF16)

    _by_causal_width(i, seq, body)


def _sparse_attn(p32, p16, bias, k_top):
    bsz, seq, _ = p32.shape
    nq = seq // QB
    return pl.pallas_call(
        functools.partial(_sparse_attn_kernel, k_top=k_top),
        grid=(bsz, nq),
        in_specs=[pl.BlockSpec((None, QB, 256), lambda b, i: (b, i, P_CQ2 // 256)),
                  pl.BlockSpec((None, seq, 256), lambda b, i: (b, 0, P_CK // 256)),
                  pl.BlockSpec((None, seq, 256), lambda b, i: (b, 0, P_CV // 256)),
                  pl.BlockSpec((None, QB, 256), lambda b, i: (b, i, P_CQI // 256)),
                  pl.BlockSpec((None, seq, LANES), lambda b, i: (b, 0, P_CKI // LANES)),
                  pl.BlockSpec((None, QB, LANES), lambda b, i: (b, i, P_CWI // LANES)),
                  pl.BlockSpec((None, QB, 256), lambda b, i: (b, i, P_CG // 256)),
                  pl.BlockSpec((4, nq + 1, QB, QB), lambda b, i: (0, 0, 0, 0))],
        out_specs=pl.BlockSpec((None, QB, BRANCH_WIDTH), lambda b, i: (b, i, 0)),
        out_shape=jax.ShapeDtypeStruct((bsz, seq, BRANCH_WIDTH), BF16),
        scratch_shapes=[pltpu.VMEM((seq, QB), F32), pltpu.VMEM((seq, QB), F32),
                        pltpu.VMEM((BRANCH_WIDTH, seq), BF16)],
        compiler_params=_params(("arbitrary", "arbitrary")),
        name="sparse_attn",
    )(p32, p16, p32, p16, p16, p32, p32, bias)


def _dilated_kernel(q0_ref, q1_ref, k0_ref, k1_ref, v0_ref, v1_ref, g_ref, bias_ref, o_ref,
                    m_ref, l_ref, acc_ref):
    seq = q0_ref.shape[0]
    q_refs, k_refs, v_refs = (q0_ref, q1_ref), (k0_ref, k1_ref), (v0_ref, v1_ref)
    lane = lax.broadcasted_iota(jnp.int32, (QB, LANES), 1)
    qmasks = [_lane_mask(HEAD_DIM, hh, F32) * (HEAD_DIM ** -0.5) for hh in range(2)]
    for p, (window, d) in enumerate(DILATED_PATTERNS):
        nb = seq // d // QB

        def tile(n, carry, p=p, d=d):
            r = n % d
            j = n // d
            start = r + d * QB * j
            prev = jnp.maximum(start - d * QB, r)
            has_prev = jnp.minimum(j, 1)
            cur_rows = pl.ds(start, QB, stride=d)
            prev_rows = pl.ds(prev, QB, stride=d)
            for g in range(2):
                q = q_refs[g][cur_rows, :]
                kcat = jnp.concatenate([k_refs[g][prev_rows, :], k_refs[g][cur_rows, :]], axis=0).astype(BF16)
                vcat = jnp.concatenate([v_refs[g][prev_rows, :], v_refs[g][cur_rows, :]], axis=0).astype(BF16)
                ms, ls, accs = [], [], []
                for hh in range(2):
                    s = _dot_nt((q * qmasks[hh]).astype(BF16), kcat) + bias_ref[2 * g + hh, 2 * p + has_prev]
                    m = jnp.max(s, axis=-1, keepdims=True)
                    e = jnp.exp(s - m)
                    ms.append(m)
                    ls.append(jnp.sum(e, axis=-1, keepdims=True))
                    accs.append(_dot(e.astype(BF16), vcat))
                first = lane < HEAD_DIM
                m_new = jnp.where(first, ms[0], ms[1])
                l_new = jnp.where(first, ls[0], ls[1])
                a_new = jnp.where(first, accs[0], accs[1])
                if p == 0:
                    m_ref[g, cur_rows, :] = m_new
                    l_ref[g, cur_rows, :] = l_new
                    acc_ref[g, cur_rows, :] = a_new
                else:
                    m_old = m_ref[g, cur_rows, :]
                    m_tot = jnp.maximum(m_old, m_new)
                    w_old = jnp.exp(m_old - m_tot)
                    w_new = jnp.exp(m_new - m_tot)
                    m_ref[g, cur_rows, :] = m_tot
                    l_ref[g, cur_rows, :] = w_old * l_ref[g, cur_rows, :] + w_new * l_new
                    acc_ref[g, cur_rows, :] = w_old * acc_ref[g, cur_rows, :] + w_new * a_new
            return carry

        lax.fori_loop(0, d * nb, tile, 0)
    gate = _silu(g_ref[...])
    for g in range(2):
        sl = slice(g * LANES, (g + 1) * LANES)
        o_ref[:, sl] = (acc_ref[g] / l_ref[g] * gate[:, sl]).astype(BF16)


def _dilated_attn(p32, bias):
    bsz, seq, _ = p32.shape

    def slab(off):
        return pl.BlockSpec((None, seq, LANES), lambda b, off=off: (b, 0, off // LANES))

    return pl.pallas_call(
        _dilated_kernel,
        grid=(bsz,),
        in_specs=[slab(P_BQ), slab(P_BQ + LANES), slab(P_BK), slab(P_BK + LANES),
                  slab(P_BV), slab(P_BV + LANES),
                  pl.BlockSpec((None, seq, 256), lambda b: (b, 0, P_BG // 256)),
                  pl.BlockSpec((4, 2 * len(DILATED_PATTERNS), QB, 2 * QB), lambda b: (0, 0, 0, 0))],
        out_specs=pl.BlockSpec((None, seq, BRANCH_WIDTH), lambda b: (b, 0, 0)),
        out_shape=jax.ShapeDtypeStruct((bsz, seq, BRANCH_WIDTH), BF16),
        scratch_shapes=[pltpu.VMEM((2, seq, LANES), F32)] * 3,
        compiler_params=_params(("arbitrary",)),
        name="dilated_attn",
    )(p32, p32, p32, p32, p32, p32, p32, bias)


def _out_proj_kernel(ya_ref, yb_ref, yc_ref, yd_ref, w_ref, h_ref, g_ref, o_ref):
    y = None
    for n, ref in enumerate((ya_ref, yb_ref, yc_ref, yd_ref)):
        t = _dot(ref[...], w_ref[n * BRANCH_WIDTH:(n + 1) * BRANCH_WIDTH, :])
        y = t if y is None else y + t
    ms = jnp.mean(y * y, axis=-1, keepdims=True)
    o_ref[...] = h_ref[...] + y * lax.rsqrt(ms + EPS) * g_ref[...]


def _out_proj(ya, yb, yc, yd, w, h, g, tm=512):
    bsz, seq, _ = h.shape
    yspec = pl.BlockSpec((None, tm, BRANCH_WIDTH), lambda b, i: (b, i, 0))
    return pl.pallas_call(
        _out_proj_kernel,
        grid=(bsz, seq // tm),
        in_specs=[yspec, yspec, yspec, yspec,
                  pl.BlockSpec((4 * BRANCH_WIDTH, D_MODEL), lambda b, i: (0, 0)),
                  pl.BlockSpec((None, tm, D_MODEL), lambda b, i: (b, i, 0)),
                  pl.BlockSpec((1, D_MODEL), lambda b, i: (0, 0))],
        out_specs=pl.BlockSpec((None, tm, D_MODEL), lambda b, i: (b, i, 0)),
        out_shape=jax.ShapeDtypeStruct(h.shape, F32),
        compiler_params=_params(("arbitrary", "arbitrary")),
        name="out_proj",
    )(ya, yb, yc, yd, w, h, g)


def kernel(x, w_in, w_out, norm_pre, norm_post, mla_q_norm, mla_kv_norm, mla_w_uq, mla_w_ukv,
           diff_lambda, diff_subln, rel_bias):
    bsz, seq, _ = x.shape
    depth = w_in.shape[0]
    nq = seq // QB
    k_top = min(IDX_TOPK_MAX, seq // 4)
    cos_t, sin_t = _rope_tables(seq)
    causal_buckets = jnp.asarray(_bucket_tiles_causal(nq))
    bias_b = _bias_expand(rel_bias, jnp.asarray(_bucket_tiles_dilated()), 0, 4)
    causal_buckets_t = jnp.asarray(np.swapaxes(_bucket_tiles_causal(nq), 1, 2))
    bias_c = _bias_expand(rel_bias, causal_buckets_t, 4, 4, LOG2E)
    bias_d = _bias_expand(rel_bias, causal_buckets, 8, 4, LOG2E)
    mask_tiles = jnp.asarray(_mask_tiles())
    h = x
    for layer in range(depth):
        w_arr = _arrange_w_in(w_in[layer]).astype(BF16)
        wq, wqrot, wk, wv = _arrange_mla(mla_w_uq[layer], mla_w_ukv[layer])
        gq = jnp.concatenate([mla_q_norm[layer], jnp.ones((256 - Q_LORA,), F32)])[None, :]
        gkv = mla_kv_norm[layer][None, :]
        p32, p16 = _in_proj(h, norm_pre[layer][None, :], w_arr)
        qa, ka, va = _mla_prep(p32, cos_t, sin_t, gq, gkv, wq, wqrot, wk, wv)
        y_a = _mla_attn(qa, ka, va, p32, mask_tiles)
        y_b = _dilated_attn(p32, bias_b)
        y_c = _sparse_attn(p32, p16, bias_c, k_top)
        lambda_init = 0.8 - 0.6 * math.exp(-0.3 * layer)
        subln = jnp.concatenate([diff_subln[layer], diff_subln[layer]])[None, :]
        y_d = _diff_attn(p32, p16, bias_d, diff_lambda[layer], subln, lambda_init)
        h = _out_proj(y_a, y_b, y_c, y_d, w_out[layer].astype(BF16), h, norm_post[layer][None, :])
    return h
```

```python
import functools
import math

import jax
import jax.numpy as jnp
import numpy as np
from jax import lax
from jax.experimental import pallas as pl
from jax.experimental.pallas import tpu as pltpu

F32 = jnp.float32
BF16 = jnp.bfloat16

D_MODEL = 1024
A_HEADS, A_NOPE, A_ROPE, A_V = 4, 64, 32, 64
Q_LORA, KV_LORA = 192, 128
ROPE_THETA = 10000.0
HEAD_DIM = 64
DILATED_PATTERNS = ((128, 1), (512, 4), (2048, 16))
IDX_HEADS, IDX_DIM, IDX_TOPK_MAX = 8, 32, 256
D_QK = 32
BRANCH_WIDTH = 256
NUM_BUCKETS, MAX_DISTANCE = 32, 2048
NEG_INF = -1e30
EPS = 1e-6
LOG2E = math.log2(math.e)
KV_CHUNK = 512
COL_ACC_ROWS = 64
LANES = 128
QB = 128
VMEM_LIMIT = 56 * 1024 * 1024

_SPLIT = (Q_LORA, KV_LORA, A_ROPE, 256, 256, 256, 256, 256, 256, 256, 256, IDX_HEADS * IDX_DIM, IDX_DIM,
          IDX_HEADS, 256, 256, 256, 256, 256)
_OFF = np.concatenate([[0], np.cumsum(_SPLIT)]).tolist()
(_A_CQ, _A_CKV, _A_KR, _A_G, _B_Q, _B_K, _B_V, _B_G, _C_Q, _C_K, _C_V, _C_QI, _C_KI, _C_WI, _C_G,
 _D_Q, _D_K, _D_V, _D_G) = range(19)

P_CQ, P_AG, P_CKV, P_KR, P_KRROT, P_CKI = 0, 256, 512, 640, 768, 896
P_BQ, P_BK, P_BV, P_BG = 1024, 1280, 1536, 1792
P_CQ2, P_CK, P_CV, P_CQI, P_CG = 2048, 2304, 2560, 2816, 3072
P_DQ, P_DK, P_DV, P_DG = 3328, 3584, 3840, 4096
P_CWI = 4352
NCOL = 4480


def _dot(a, b):
    return jnp.dot(a, b, preferred_element_type=F32)


def _dot_nt(a, b):
    return lax.dot_general(a, b, (((1,), (1,)), ((), ())), preferred_element_type=F32)


def _params(sem):
    return pltpu.CompilerParams(dimension_semantics=sem, vmem_limit_bytes=VMEM_LIMIT)


def _rot_cols(w):
    half = w.shape[-1] // 2
    return jnp.concatenate([-w[..., half:], w[..., :half]], axis=-1)


def _arrange_w_in(w):
    def seg(i):
        return w[:, _OFF[i]:_OFF[i + 1]]

    def z(n):
        return jnp.zeros((w.shape[0], n), w.dtype)

    kr = seg(_A_KR)
    cols = [
        seg(_A_CQ), z(64),
        seg(_A_G),
        seg(_A_CKV),
        z(64), kr, z(32),
        z(64), _rot_cols(kr), z(32),
        seg(_C_KI), seg(_C_KI), seg(_C_KI), seg(_C_KI),
        seg(_B_Q), seg(_B_K), seg(_B_V), seg(_B_G),
        seg(_C_Q), seg(_C_K), seg(_C_V), seg(_C_QI), seg(_C_G),
        seg(_D_Q), seg(_D_K), seg(_D_V), seg(_D_G),
        seg(_C_WI), z(LANES - IDX_HEADS),
    ]
    out = jnp.concatenate(cols, axis=1)
    assert out.shape[1] == NCOL
    return out


def _arrange_mla(w_uq, w_ukv):
    wq = w_uq.reshape(Q_LORA, A_HEADS, A_NOPE + A_ROPE)
    nope, rope = wq[..., :A_NOPE], wq[..., A_NOPE:]
    zq = jnp.zeros((Q_LORA, A_HEADS, LANES - A_NOPE - A_ROPE), w_uq.dtype)
    wq_main = jnp.concatenate([nope, rope, zq], axis=-1).reshape(Q_LORA, A_HEADS * LANES)
    wq_rot = jnp.concatenate([jnp.zeros_like(nope), _rot_cols(rope), zq], axis=-1).reshape(Q_LORA, A_HEADS * LANES)
    pad = jnp.zeros((256 - Q_LORA, A_HEADS * LANES), w_uq.dtype)
    wq_main = jnp.concatenate([wq_main, pad], axis=0)
    wq_rot = jnp.concatenate([wq_rot, pad], axis=0)
    wkv = w_ukv.reshape(KV_LORA, A_HEADS, A_NOPE + A_V)
    knope, v = wkv[..., :A_NOPE], wkv[..., A_NOPE:]
    wk = jnp.concatenate([knope, jnp.zeros_like(knope)], axis=-1).reshape(KV_LORA, A_HEADS * LANES)
    zv = jnp.zeros_like(v)
    wv = jnp.stack([jnp.concatenate([v[:, h], zv[:, h]] if h % 2 == 0 else [zv[:, h], v[:, h]], axis=-1)
                    for h in range(A_HEADS)], axis=1).reshape(KV_LORA, A_HEADS * LANES)
    return wq_main.astype(BF16), wq_rot.astype(BF16), wk.astype(BF16), wv.astype(BF16)


def _rope_tables(seq):
    inv = ROPE_THETA ** (-jnp.arange(0, A_ROPE, 2, dtype=F32) / A_ROPE)
    ang = jnp.arange(seq, dtype=F32)[:, None] * inv[None, :]
    cos, sin = jnp.cos(ang), jnp.sin(ang)
    one = jnp.ones((seq, A_NOPE), F32)
    zero = jnp.zeros((seq, LANES - A_NOPE - A_ROPE), F32)
    cos_t = jnp.concatenate([one, cos, cos, zero], axis=1)
    sin_t = jnp.concatenate([jnp.zeros_like(one), sin, sin, zero], axis=1)
    return cos_t, sin_t


def _t5_bucket_np(rel):
    n = np.maximum(rel, 0)
    max_exact = NUM_BUCKETS // 2
    nf = np.maximum(n, max_exact).astype(np.float64)
    large = max_exact + (np.log(nf / max_exact) / math.log(MAX_DISTANCE / max_exact)
                         * (NUM_BUCKETS - max_exact)).astype(np.int32)
    large = np.minimum(large, NUM_BUCKETS - 1)
    return np.where(n < max_exact, n, large).astype(np.int32)


MASKED_BUCKET = NUM_BUCKETS


def _bucket_tiles_causal(nq):
    q = np.arange(QB)[:, None]
    k = np.arange(QB)[None, :]
    tiles = [np.full((QB, QB), MASKED_BUCKET, np.int32)]
    for d in range(nq):
        rel = QB * d + q - k
        tiles.append(np.where(rel >= 0, _t5_bucket_np(rel), MASKED_BUCKET).astype(np.int32))
    return np.stack(tiles)


def _mask_tiles():
    q = np.arange(QB)[:, None]
    k = np.arange(QB)[None, :]
    diag = np.where(k <= q, 0.0, NEG_INF)
    return np.stack([np.full((QB, QB), NEG_INF), diag, np.zeros((QB, QB))]).astype(np.float32)


def _bucket_tiles_dilated():
    q = np.arange(QB)[:, None]
    k = np.arange(2 * QB)[None, :]
    rel = q + QB - k
    tiles = []
    for (window, d) in DILATED_PATTERNS:
        in_band = (rel >= 0) & (rel <= window // d)
        for has_prev in (False, True):
            ok = in_band & (has_prev | (k >= QB))
            tiles.append(np.where(ok, _t5_bucket_np(rel * d), MASKED_BUCKET).astype(np.int32))
    return np.stack(tiles)


def _bias_expand_kernel(table_ref, bucket_ref, out_ref, *, head0, scale):
    h = pl.program_id(0) + head0
    bk = bucket_ref[...]
    acc = jnp.where(bk == MASKED_BUCKET, NEG_INF, 0.0)
    for b in range(NUM_BUCKETS):
        acc = jnp.where(bk == b, table_ref[b, h] * scale, acc)
    out_ref[...] = acc


def _bias_expand(table, buckets, head0, nheads, scale=1.0):
    n, r, c = buckets.shape
    return pl.pallas_call(
        functools.partial(_bias_expand_kernel, head0=head0, scale=scale),
        grid=(nheads,),
        in_specs=[pl.BlockSpec(memory_space=pltpu.SMEM),
                  pl.BlockSpec((n, r, c), lambda h: (0, 0, 0))],
        out_specs=pl.BlockSpec((None, n, r, c), lambda h: (h, 0, 0, 0)),
        out_shape=jax.ShapeDtypeStruct((nheads, n, r, c), F32),
        compiler_params=_params(("arbitrary",)),
        name="bias_expand",
    )(table, buckets)


def _in_proj_kernel(x_ref, g_ref, w_ref, o32_ref, o16_ref):
    x = x_ref[...]
    ms = jnp.mean(x * x, axis=-1, keepdims=True)
    xn = x * lax.rsqrt(ms + EPS) * g_ref[...]
    p = _dot(xn.astype(BF16), w_ref[...])
    o32_ref[...] = p
    o16_ref[...] = p.astype(BF16)


def _in_proj(h, g, w, tm=256):
    bsz, seq, _ = h.shape
    return pl.pallas_call(
        _in_proj_kernel,
        grid=(bsz, seq // tm),
        in_specs=[pl.BlockSpec((None, tm, D_MODEL), lambda b, i: (b, i, 0)),
                  pl.BlockSpec((1, D_MODEL), lambda b, i: (0, 0)),
                  pl.BlockSpec((D_MODEL, NCOL), lambda b, i: (0, 0))],
        out_specs=[pl.BlockSpec((None, tm, NCOL), lambda b, i: (b, i, 0)),
                   pl.BlockSpec((None, tm, NCOL), lambda b, i: (b, i, 0))],
        out_shape=[jax.ShapeDtypeStruct((bsz, seq, NCOL), F32),
                   jax.ShapeDtypeStruct((bsz, seq, NCOL), BF16)],
        compiler_params=_params(("arbitrary", "arbitrary")),
        name="in_proj",
    )(h, g, w)


def _mla_prep_kernel(cq_ref, ckv_ref, kr_ref, krrot_ref, cos_ref, sin_ref, gq_ref, gkv_ref,
                     wq_ref, wqrot_ref, wk_ref, wv_ref, q_ref, k_ref, v_ref):
    cos = cos_ref[...]
    sin = sin_ref[...]
    cos4 = jnp.concatenate([cos] * A_HEADS, axis=1)
    sin4 = jnp.concatenate([sin] * A_HEADS, axis=1)
    cq = cq_ref[...]
    ms = jnp.sum(cq * cq, axis=-1, keepdims=True) * (1.0 / Q_LORA)
    nq = (cq * lax.rsqrt(ms + EPS) * gq_ref[...]).astype(BF16)
    q = _dot(nq, wq_ref[...]) * cos4 + _dot(nq, wqrot_ref[...]) * sin4
    q_ref[...] = (q * ((A_NOPE + A_ROPE) ** -0.5 * LOG2E)).astype(BF16)
    ckv = ckv_ref[...]
    ms = jnp.mean(ckv * ckv, axis=-1, keepdims=True)
    nkv = (ckv * lax.rsqrt(ms + EPS) * gkv_ref[...]).astype(BF16)
    kr = kr_ref[...] * cos + krrot_ref[...] * sin
    k = _dot(nkv, wk_ref[...]) + jnp.concatenate([kr] * A_HEADS, axis=1)
    k_ref[...] = k.astype(BF16)
    v_ref[...] = _dot(nkv, wv_ref[...]).astype(BF16)


def _mla_prep(p32, cos_t, sin_t, gq, gkv, wq, wqrot, wk, wv, tm=512):
    bsz, seq, _ = p32.shape
    w4 = A_HEADS * LANES
    row = lambda b, i: (b, i, 0)
    const = lambda b, i: (0, 0)
    return pl.pallas_call(
        _mla_prep_kernel,
        grid=(bsz, seq // tm),
        in_specs=[pl.BlockSpec((None, tm, 256), lambda b, i: (b, i, P_CQ // 256)),
                  pl.BlockSpec((None, tm, LANES), lambda b, i: (b, i, P_CKV // LANES)),
                  pl.BlockSpec((None, tm, LANES), lambda b, i: (b, i, P_KR // LANES)),
                  pl.BlockSpec((None, tm, LANES), lambda b, i: (b, i, P_KRROT // LANES)),
                  pl.BlockSpec((tm, LANES), lambda b, i: (i, 0)),
                  pl.BlockSpec((tm, LANES), lambda b, i: (i, 0)),
                  pl.BlockSpec((1, 256), const),
                  pl.BlockSpec((1, KV_LORA), const),
                  pl.BlockSpec((256, w4), const),
                  pl.BlockSpec((256, w4), const),
                  pl.BlockSpec((KV_LORA, w4), const),
                  pl.BlockSpec((KV_LORA, w4), const)],
        out_specs=[pl.BlockSpec((None, tm, w4), row)] * 3,
        out_shape=[jax.ShapeDtypeStruct((bsz, seq, w4), BF16)] * 3,
        compiler_params=_params(("arbitrary", "arbitrary")),
        name="mla_prep",
    )(p32, p32, p32, p32, cos_t, sin_t, gq, gkv, wq, wqrot, wk, wv)


def _silu(g):
    return g * (1.0 / (1.0 + jnp.exp(-g)))


def _by_causal_width(i, seq, body):
    per = KV_CHUNK // QB
    for wb in range(seq // KV_CHUNK):
        pl.when(i // per == wb)(functools.partial(body, (wb + 1) * KV_CHUNK))


def _mask_tail(mask_ref, i, width):
    first = (width - KV_CHUNK) // QB
    return jnp.concatenate([mask_ref[jnp.clip(i - j, -1, 1) + 1] for j in range(first, width // QB)], axis=1)


def _bias_row(bias_ref, h, i, width):
    return jnp.concatenate([bias_ref[h, jnp.maximum(i - j, -1) + 1] for j in range(width // QB)], axis=1)


def _col_reduce(x, op):
    rows, lanes = x.shape
    part = op(x.reshape(rows // COL_ACC_ROWS, COL_ACC_ROWS, lanes), axis=0)
    return op(part, axis=0, keepdims=True)


def _bias_col(bias_ref, h, i, width):
    return jnp.concatenate([bias_ref[h, jnp.maximum(i - j, -1) + 1] for j in range(width // QB)], axis=0)


def _softmax_pv(s, v):
    m = jnp.max(s, axis=-1, keepdims=True)
    e = jnp.exp2(s - m)
    l = jnp.sum(e, axis=-1, keepdims=True)
    return _dot(e.astype(BF16), v), l


def _lane_mask(width, seg, dtype):
    lane = lax.broadcasted_iota(jnp.int32, (1, LANES), 1)
    return jnp.where((lane >= seg * width) & (lane < (seg + 1) * width), 1.0, 0.0).astype(dtype)


def _mla_attn_kernel(q_ref, k_ref, v_ref, g_ref, mask_ref, o_ref):
    i = pl.program_id(1)
    seq = k_ref.shape[0]

    def body(width):
        head_w = width - KV_CHUNK
        tail = _mask_tail(mask_ref, i, width)
        pairs = []
        for g in range(A_HEADS // 2):
            acc = None
            for h in (2 * g, 2 * g + 1):
                sl = slice(h * LANES, (h + 1) * LANES)
                s = _dot_nt(q_ref[:, sl], k_ref[:width, sl])
                s = jnp.concatenate([s[:, :head_w], s[:, head_w:] + tail], axis=1) if head_w else s + tail
                pv, l = _softmax_pv(s, v_ref[:width, sl])
                o = pv / l
                acc = o if acc is None else acc + o
            pairs.append(acc)
        y = jnp.concatenate(pairs, axis=1)
        o_ref[...] = (y * _silu(g_ref[...])).astype(BF16)

    _by_causal_width(i, seq, body)


def _mla_attn(q, k, v, p32, mask_tiles):
    bsz, seq, w4 = q.shape
    return pl.pallas_call(
        _mla_attn_kernel,
        grid=(bsz, seq // QB),
        in_specs=[pl.BlockSpec((None, QB, w4), lambda b, i: (b, i, 0)),
                  pl.BlockSpec((None, seq, w4), lambda b, i: (b, 0, 0)),
                  pl.BlockSpec((None, seq, w4), lambda b, i: (b, 0, 0)),
                  pl.BlockSpec((None, QB, 256), lambda b, i: (b, i, P_AG // 256)),
                  pl.BlockSpec((3, QB, QB), lambda b, i: (0, 0, 0))],
        out_specs=pl.BlockSpec((None, QB, BRANCH_WIDTH), lambda b, i: (b, i, 0)),
        out_shape=jax.ShapeDtypeStruct((bsz, seq, BRANCH_WIDTH), BF16),
        compiler_params=_params(("arbitrary", "arbitrary")),
        name="mla_attn",
    )(q, k, v, p32, mask_tiles)


def _transpose_values(v_ref, vt_ref):
    for j in range(v_ref.shape[0] // QB):
        for g in range(v_ref.shape[1] // LANES):
            tile = v_ref[j * QB:(j + 1) * QB, g * LANES:(g + 1) * LANES]
            vt_ref[g * LANES:(g + 1) * LANES, j * QB:(j + 1) * QB] = tile.T.astype(BF16)


def _diff_attn_kernel(q_ref, k_ref, v_ref, g_ref, bias_ref, lam_ref, subln_ref, o_ref, vt_ref, *, lambda_init):
    i = pl.program_id(1)
    seq = k_ref.shape[0]

    @pl.when(i == 0)
    def _():
        _transpose_values(v_ref, vt_ref)

    def body(width):
        lp = lam_ref[...]
        lam = (jnp.exp(jnp.sum(lp[0:1] * lp[1:2], axis=-1, keepdims=True))
               - jnp.exp(jnp.sum(lp[2:3] * lp[3:4], axis=-1, keepdims=True)) + lambda_init)
        scores = []
        for h in range(4):
            sl = slice((h // 2) * LANES, (h // 2 + 1) * LANES)
            qg = q_ref[:, sl] * (D_QK ** -0.5 * LOG2E)
            qcat = jnp.concatenate([(qg * _lane_mask(D_QK, 2 * (h % 2) + mm, F32)).astype(BF16)
                                    for mm in range(2)], axis=0)
            scores.append(_dot_nt(k_ref[:width, sl], qcat))
        outs = []
        for h in range(4):
            bias = _bias_col(bias_ref, h, i, width)
            es, ls = [], []
            for mm in range(2):
                s = scores[h][:, mm * QB:(mm + 1) * QB] + bias
                e = jnp.exp2(s - _col_reduce(s, jnp.max))
                ls.append(_col_reduce(e, jnp.sum))
                es.append(e.astype(BF16))
            ot = _dot(vt_ref[h * HEAD_DIM:(h + 1) * HEAD_DIM, :width], jnp.concatenate(es, axis=1))
            a = ot[:, :QB] / ls[0] - lam * (ot[:, QB:] / ls[1])
            ms = jnp.sum(a * a, axis=0, keepdims=True) * (1.0 / HEAD_DIM)
            outs.append(a * lax.rsqrt(ms + EPS))
        y_t = jnp.concatenate(outs, axis=0)
        y = jnp.concatenate([y_t[:LANES].T, y_t[LANES:].T], axis=1)
        o_ref[...] = (y * (subln_ref[...] * (1.0 - lambda_init)) * _silu(g_ref[...])).astype(BF16)

    _by_causal_width(i, seq, body)


def _diff_attn(p32, p16, bias, lam_params, subln, lambda_init):
    bsz, seq, _ = p32.shape
    nq = seq // QB
    return pl.pallas_call(
        functools.partial(_diff_attn_kernel, lambda_init=lambda_init),
        grid=(bsz, nq),
        in_specs=[pl.BlockSpec((None, QB, 256), lambda b, i: (b, i, P_DQ // 256)),
                  pl.BlockSpec((None, seq, 256), lambda b, i: (b, 0, P_DK // 256)),
                  pl.BlockSpec((None, seq, 256), lambda b, i: (b, 0, P_DV // 256)),
                  pl.BlockSpec((None, QB, 256), lambda b, i: (b, i, P_DG // 256)),
                  pl.BlockSpec((4, nq + 1, QB, QB), lambda b, i: (0, 0, 0, 0)),
                  pl.BlockSpec((4, D_QK), lambda b, i: (0, 0)),
                  pl.BlockSpec((1, BRANCH_WIDTH), lambda b, i: (0, 0))],
        out_specs=pl.BlockSpec((None, QB, BRANCH_WIDTH), lambda b, i: (b, i, 0)),
        out_shape=jax.ShapeDtypeStruct((bsz, seq, BRANCH_WIDTH), BF16),
        scratch_shapes=[pltpu.VMEM((BRANCH_WIDTH, seq), BF16)],
        compiler_params=_params(("arbitrary", "arbitrary")),
        name="diff_attn",
    )(p32, p16, p32, p32, bias, lam_params, subln)


def _sortable_to_float(key):
    return pltpu.bitcast(jnp.where(key < 0, key ^ jnp.int32(0x7FFFFFFF), key), F32)


def _kth_largest(score_ref, width, k_top):
    def count_ge(key):
        thr = _sortable_to_float(key)
        return _col_reduce(jnp.where(score_ref[:width, :] >= thr, 1.0, 0.0), jnp.sum)

    int_min = jnp.full((1, QB), -2 ** 31, jnp.int32)
    zero = jnp.zeros((1, QB), jnp.int32)
    t = jnp.where(count_ge(zero) >= k_top, zero, int_min)
    for bit in range(30, -1, -1):
        cand = t + jnp.int32(1 << bit)
        t = jnp.where(count_ge(cand) >= k_top, cand, t)
    return _sortable_to_float(t)


def _first_ties(eq, need, row):
    eqf = jnp.where(eq, 1.0, 0.0)
    nbits = int(eq.shape[0]).bit_length()

    def body(it, j):
        cand = j + (jnp.int32(1) << (nbits - 1 - it))
        cnt = _col_reduce(jnp.where(row < cand, eqf, 0.0), jnp.sum)
        return jnp.where(cnt <= need, cand, j)

    j = lax.fori_loop(0, nbits, body, jnp.zeros((1, eq.shape[1]), jnp.int32))
    return eq & (row < j)


def _sparse_attn_kernel(q_ref, k_ref, v_ref, qi_ref, ki_ref, wi_ref, g_ref, bias_ref, o_ref,
                        score_ref, neg_ref, vt_ref, *, k_top):
    i = pl.program_id(1)
    seq = k_ref.shape[0]

    @pl.when(i == 0)
    def _():
        _transpose_values(v_ref, vt_ref)

    def body(width):
        head_w = width - KV_CHUNK
        ki = ki_ref[:width, :]
        w_t = wi_ref[...].T * (IDX_DIM ** -0.5 * IDX_HEADS ** -0.5)
        score = None
        for h in range(0, IDX_HEADS, 2):
            qg = qi_ref[:, (h // 4) * LANES:(h // 4 + 1) * LANES]
            qcat = jnp.concatenate([qg * _lane_mask(IDX_DIM, h % 4, BF16),
                                    qg * _lane_mask(IDX_DIM, h % 4 + 1, BF16)], axis=0)
            logit = _dot_nt(ki, qcat)
            term = (jnp.maximum(logit[:, :QB], 0.0) * w_t[h:h + 1]
                    + jnp.maximum(logit[:, QB:], 0.0) * w_t[h + 1:h + 2])
            score = term if score is None else score + term
        s_idx = head_w + lax.broadcasted_iota(jnp.int32, (KV_CHUNK, QB), 0)
        tail_ok = s_idx <= i * QB + lax.broadcasted_iota(jnp.int32, (KV_CHUNK, QB), 1)
        if head_w:
            score_ref[:head_w, :] = score[:head_w]
        score_ref[head_w:width, :] = jnp.where(tail_ok, score[head_w:], NEG_INF)
        s2s = []
        for g in range(2):
            sl = slice(g * LANES, (g + 1) * LANES)
            qg = q_ref[:, sl] * (HEAD_DIM ** -0.5 * LOG2E)
            qcat = jnp.concatenate([(qg * _lane_mask(HEAD_DIM, hh, F32)).astype(BF16) for hh in range(2)], axis=0)
            s2s.append(_dot_nt(k_ref[:width, sl], qcat))
        thr = _kth_largest(score_ref, width, k_top)
        keep_tail = (score_ref[head_w:width, :] >= thr) & tail_ok
        cnt = _col_reduce(jnp.where(keep_tail, 1.0, 0.0), jnp.sum)
        neg_ref[head_w:width, :] = jnp.where(keep_tail, 0.0, NEG_INF)
        if head_w:
            keep_head = score_ref[:head_w, :] >= thr
            cnt = cnt + _col_reduce(jnp.where(keep_head, 1.0, 0.0), jnp.sum)
            neg_ref[:head_w, :] = jnp.where(keep_head, 0.0, NEG_INF)

        @pl.when(jnp.max(cnt) > k_top)
        def _():
            x = score_ref[:width, :]
            gt = x > thr
            need = k_top - _col_reduce(jnp.where(gt, 1.0, 0.0), jnp.sum)
            row = lax.broadcasted_iota(jnp.int32, (width, QB), 0)
            neg_ref[:width, :] = jnp.where(gt | _first_ties(x == thr, need, row), 0.0, NEG_INF)

        outs = []
        for g in range(2):
            sl = slice(g * LANES, (g + 1) * LANES)
            s2 = s2s[g]
            es, ls = [], []
            for hh in range(2):
                s = s2[:, hh * QB:(hh + 1) * QB] + (_bias_col(bias_ref, 2 * g + hh, i, width) + neg_ref[:width, :])
                e = jnp.exp2(s - _col_reduce(s, jnp.max))
                ls.append(_col_reduce(e, jnp.sum))
                es.append(e.astype(BF16))
            ot = _dot(vt_ref[sl, :width], jnp.concatenate(es, axis=1))
            outs.append(ot[:HEAD_DIM, :QB] / ls[0])
            outs.append(ot[HEAD_DIM:, QB:] / ls[1])
        y_t = jnp.concatenate(outs, axis=0)
        y = jnp.concatenate([y_t[:LANES].T, y_t[LANES:].T], axis=1)
        o_ref[...] = (y * _silu(g_ref[...])).astype(BF16)

    _by_causal_width(i, seq, body)


def _sparse_attn(p32, p16, bias, k_top):
    bsz, seq, _ = p32.shape
    nq = seq // QB
    return pl.pallas_call(
        functools.partial(_sparse_attn_kernel, k_top=k_top),
        grid=(bsz, nq),
        in_specs=[pl.BlockSpec((None, QB, 256), lambda b, i: (b, i, P_CQ2 // 256)),
                  pl.BlockSpec((None, seq, 256), lambda b, i: (b, 0, P_CK // 256)),
                  pl.BlockSpec((None, seq, 256), lambda b, i: (b, 0, P_CV // 256)),
                  pl.BlockSpec((None, QB, 256), lambda b, i: (b, i, P_CQI // 256)),
                  pl.BlockSpec((None, seq, LANES), lambda b, i: (b, 0, P_CKI // LANES)),
                  pl.BlockSpec((None, QB, LANES), lambda b, i: (b, i, P_CWI // LANES)),
                  pl.BlockSpec((None, QB, 256), lambda b, i: (b, i, P_CG // 256)),
                  pl.BlockSpec((4, nq + 1, QB, QB), lambda b, i: (0, 0, 0, 0))],
        out_specs=pl.BlockSpec((None, QB, BRANCH_WIDTH), lambda b, i: (b, i, 0)),
        out_shape=jax.ShapeDtypeStruct((bsz, seq, BRANCH_WIDTH), BF16),
        scratch_shapes=[pltpu.VMEM((seq, QB), F32), pltpu.VMEM((seq, QB), F32),
                        pltpu.VMEM((BRANCH_WIDTH, seq), BF16)],
        compiler_params=_params(("arbitrary", "arbitrary")),
        name="sparse_attn",
    )(p32, p16, p32, p16, p16, p32, p32, bias)


def _dilated_kernel(q0_ref, q1_ref, k0_ref, k1_ref, v0_ref, v1_ref, g_ref, bias_ref, o_ref,
                    m_ref, l_ref, acc_ref):
    seq = q0_ref.shape[0]
    q_refs, k_refs, v_refs = (q0_ref, q1_ref), (k0_ref, k1_ref), (v0_ref, v1_ref)
    lane = lax.broadcasted_iota(jnp.int32, (QB, LANES), 1)
    qmasks = [_lane_mask(HEAD_DIM, hh, F32) * (HEAD_DIM ** -0.5) for hh in range(2)]
    for p, (window, d) in enumerate(DILATED_PATTERNS):
        nb = seq // d // QB

        def tile(n, carry, p=p, d=d):
            r = n % d
            j = n // d
            start = r + d * QB * j
            prev = jnp.maximum(start - d * QB, r)
            has_prev = jnp.minimum(j, 1)
            cur_rows = pl.ds(start, QB, stride=d)
            prev_rows = pl.ds(prev, QB, stride=d)
            for g in range(2):
                q = q_refs[g][cur_rows, :]
                kcat = jnp.concatenate([k_refs[g][prev_rows, :], k_refs[g][cur_rows, :]], axis=0).astype(BF16)
                vcat = jnp.concatenate([v_refs[g][prev_rows, :], v_refs[g][cur_rows, :]], axis=0).astype(BF16)
                ms, ls, accs = [], [], []
                for hh in range(2):
                    s = _dot_nt((q * qmasks[hh]).astype(BF16), kcat) + bias_ref[2 * g + hh, 2 * p + has_prev]
                    m = jnp.max(s, axis=-1, keepdims=True)
                    e = jnp.exp(s - m)
                    ms.append(m)
                    ls.append(jnp.sum(e, axis=-1, keepdims=True))
                    accs.append(_dot(e.astype(BF16), vcat))
                first = lane < HEAD_DIM
                m_new = jnp.where(first, ms[0], ms[1])
                l_new = jnp.where(first, ls[0], ls[1])
                a_new = jnp.where(first, accs[0], accs[1])
                if p == 0:
                    m_ref[g, cur_rows, :] = m_new
                    l_ref[g, cur_rows, :] = l_new
                    acc_ref[g, cur_rows, :] = a_new
                else:
                    m_old = m_ref[g, cur_rows, :]
                    m_tot = jnp.maximum(m_old, m_new)
                    w_old = jnp.exp(m_old - m_tot)
                    w_new = jnp.exp(m_new - m_tot)
                    m_ref[g, cur_rows, :] = m_tot
                    l_ref[g, cur_rows, :] = w_old * l_ref[g, cur_rows, :] + w_new * l_new
                    acc_ref[g, cur_rows, :] = w_old * acc_ref[g, cur_rows, :] + w_new * a_new
            return carry

        lax.fori_loop(0, d * nb, tile, 0)
    gate = _silu(g_ref[...])
    for g in range(2):
        sl = slice(g * LANES, (g + 1) * LANES)
        o_ref[:, sl] = (acc_ref[g] / l_ref[g] * gate[:, sl]).astype(BF16)


def _dilated_attn(p32, bias):
    bsz, seq, _ = p32.shape

    def slab(off):
        return pl.BlockSpec((None, seq, LANES), lambda b, off=off: (b, 0, off // LANES))

    return pl.pallas_call(
        _dilated_kernel,
        grid=(bsz,),
        in_specs=[slab(P_BQ), slab(P_BQ + LANES), slab(P_BK), slab(P_BK + LANES),
                  slab(P_BV), slab(P_BV + LANES),
                  pl.BlockSpec((None, seq, 256), lambda b: (b, 0, P_BG // 256)),
                  pl.BlockSpec((4, 2 * len(DILATED_PATTERNS), QB, 2 * QB), lambda b: (0, 0, 0, 0))],
        out_specs=pl.BlockSpec((None, seq, BRANCH_WIDTH), lambda b: (b, 0, 0)),
        out_shape=jax.ShapeDtypeStruct((bsz, seq, BRANCH_WIDTH), BF16),
        scratch_shapes=[pltpu.VMEM((2, seq, LANES), F32)] * 3,
        compiler_params=_params(("arbitrary",)),
        name="dilated_attn",
    )(p32, p32, p32, p32, p32, p32, p32, bias)


def _out_proj_kernel(ya_ref, yb_ref, yc_ref, yd_ref, w_ref, h_ref, g_ref, o_ref):
    y = None
    for n, ref in enumerate((ya_ref, yb_ref, yc_ref, yd_ref)):
        t = _dot(ref[...], w_ref[n * BRANCH_WIDTH:(n + 1) * BRANCH_WIDTH, :])
        y = t if y is None else y + t
    ms = jnp.mean(y * y, axis=-1, keepdims=True)
    o_ref[...] = h_ref[...] + y * lax.rsqrt(ms + EPS) * g_ref[...]


def _out_proj(ya, yb, yc, yd, w, h, g, tm=512):
    bsz, seq, _ = h.shape
    yspec = pl.BlockSpec((None, tm, BRANCH_WIDTH), lambda b, i: (b, i, 0))
    return pl.pallas_call(
        _out_proj_kernel,
        grid=(bsz, seq // tm),
        in_specs=[yspec, yspec, yspec, yspec,
                  pl.BlockSpec((4 * BRANCH_WIDTH, D_MODEL), lambda b, i: (0, 0)),
                  pl.BlockSpec((None, tm, D_MODEL), lambda b, i: (b, i, 0)),
                  pl.BlockSpec((1, D_MODEL), lambda b, i: (0, 0))],
        out_specs=pl.BlockSpec((None, tm, D_MODEL), lambda b, i: (b, i, 0)),
        out_shape=jax.ShapeDtypeStruct(h.shape, F32),
        compiler_params=_params(("arbitrary", "arbitrary")),
        name="out_proj",
    )(ya, yb, yc, yd, w, h, g)


def kernel(x, w_in, w_out, norm_pre, norm_post, mla_q_norm, mla_kv_norm, mla_w_uq, mla_w_ukv,
           diff_lambda, diff_subln, rel_bias):
    bsz, seq, _ = x.shape
    depth = w_in.shape[0]
    nq = seq // QB
    k_top = min(IDX_TOPK_MAX, seq // 4)
    cos_t, sin_t = _rope_tables(seq)
    bias_b = _bias_expand(rel_bias, jnp.asarray(_bucket_tiles_dilated()), 0, 4)
    causal_buckets_t = jnp.asarray(np.swapaxes(_bucket_tiles_causal(nq), 1, 2))
    bias_c = _bias_expand(rel_bias, causal_buckets_t, 4, 4, LOG2E)
    bias_d = _bias_expand(rel_bias, causal_buckets_t, 8, 4, LOG2E)
    mask_tiles = jnp.asarray(_mask_tiles())
    h = x
    for layer in range(depth):
        w_arr = _arrange_w_in(w_in[layer]).astype(BF16)
        wq, wqrot, wk, wv = _arrange_mla(mla_w_uq[layer], mla_w_ukv[layer])
        gq = jnp.concatenate([mla_q_norm[layer], jnp.ones((256 - Q_LORA,), F32)])[None, :]
        gkv = mla_kv_norm[layer][None, :]
        p32, p16 = _in_proj(h, norm_pre[layer][None, :], w_arr)
        qa, ka, va = _mla_prep(p32, cos_t, sin_t, gq, gkv, wq, wqrot, wk, wv)
        y_a = _mla_attn(qa, ka, va, p32, mask_tiles)
        y_b = _dilated_attn(p32, bias_b)
        y_c = _sparse_attn(p32, p16, bias_c, k_top)
        lambda_init = 0.8 - 0.6 * math.exp(-0.3 * layer)
        subln = jnp.tile(diff_subln[layer], BRANCH_WIDTH // HEAD_DIM)[None, :]
        y_d = _diff_attn(p32, p16, bias_d, diff_lambda[layer], subln, lambda_init)
        h = _out_proj(y_a, y_b, y_c, y_d, w_out[layer].astype(BF16), h, norm_post[layer][None, :])
    return h
```

```python
import functools
import math

import jax
import jax.numpy as jnp
import numpy as np
from jax import lax
from jax.experimental import pallas as pl
from jax.experimental.pallas import tpu as pltpu

F32 = jnp.float32
BF16 = jnp.bfloat16

D_MODEL = 1024
A_HEADS, A_NOPE, A_ROPE, A_V = 4, 64, 32, 64
Q_LORA, KV_LORA = 192, 128
ROPE_THETA = 10000.0
HEAD_DIM = 64
DILATED_PATTERNS = ((128, 1), (512, 4), (2048, 16))
IDX_HEADS, IDX_DIM, IDX_TOPK_MAX = 8, 32, 256
D_QK = 32
BRANCH_WIDTH = 256
NUM_BUCKETS, MAX_DISTANCE = 32, 2048
NEG_INF = -1e30
EPS = 1e-6
LOG2E = math.log2(math.e)
KV_CHUNK = 512
COL_ACC_ROWS = 64
DIL_UNROLL = 4
LANES = 128
QB = 128
VMEM_LIMIT = 56 * 1024 * 1024

_SPLIT = (Q_LORA, KV_LORA, A_ROPE, 256, 256, 256, 256, 256, 256, 256, 256, IDX_HEADS * IDX_DIM, IDX_DIM,
          IDX_HEADS, 256, 256, 256, 256, 256)
_OFF = np.concatenate([[0], np.cumsum(_SPLIT)]).tolist()
(_A_CQ, _A_CKV, _A_KR, _A_G, _B_Q, _B_K, _B_V, _B_G, _C_Q, _C_K, _C_V, _C_QI, _C_KI, _C_WI, _C_G,
 _D_Q, _D_K, _D_V, _D_G) = range(19)

P_CQ, P_AG, P_CKV, P_KR, P_KRROT, P_CKI = 0, 256, 512, 640, 768, 896
P_BQ, P_BK, P_BV, P_BG = 1024, 1280, 1536, 1792
P_CQ2, P_CK, P_CV, P_CQI, P_CG = 2048, 2304, 2560, 2816, 3072
P_DQ, P_DK, P_DV, P_DG = 3328, 3584, 3840, 4096
P_CWI = 4352
NCOL = 4480


def _dot(a, b):
    return jnp.dot(a, b, preferred_element_type=F32)


def _dot_nt(a, b):
    return lax.dot_general(a, b, (((1,), (1,)), ((), ())), preferred_element_type=F32)


def _params(sem):
    return pltpu.CompilerParams(dimension_semantics=sem, vmem_limit_bytes=VMEM_LIMIT)


def _rot_cols(w):
    half = w.shape[-1] // 2
    return jnp.concatenate([-w[..., half:], w[..., :half]], axis=-1)


def _arrange_w_in(w):
    def seg(i):
        return w[:, _OFF[i]:_OFF[i + 1]]

    def z(n):
        return jnp.zeros((w.shape[0], n), w.dtype)

    kr = seg(_A_KR)
    cols = [
        seg(_A_CQ), z(64),
        seg(_A_G),
        seg(_A_CKV),
        z(64), kr, z(32),
        z(64), _rot_cols(kr), z(32),
        seg(_C_KI), seg(_C_KI), seg(_C_KI), seg(_C_KI),
        seg(_B_Q), seg(_B_K), seg(_B_V), seg(_B_G),
        seg(_C_Q), seg(_C_K), seg(_C_V), seg(_C_QI), seg(_C_G),
        seg(_D_Q), seg(_D_K), seg(_D_V), seg(_D_G),
        seg(_C_WI), z(LANES - IDX_HEADS),
    ]
    out = jnp.concatenate(cols, axis=1)
    assert out.shape[1] == NCOL
    return out


def _arrange_mla(w_uq, w_ukv):
    wq = w_uq.reshape(Q_LORA, A_HEADS, A_NOPE + A_ROPE)
    nope, rope = wq[..., :A_NOPE], wq[..., A_NOPE:]
    zq = jnp.zeros((Q_LORA, A_HEADS, LANES - A_NOPE - A_ROPE), w_uq.dtype)
    wq_main = jnp.concatenate([nope, rope, zq], axis=-1).reshape(Q_LORA, A_HEADS * LANES)
    wq_rot = jnp.concatenate([jnp.zeros_like(nope), _rot_cols(rope), zq], axis=-1).reshape(Q_LORA, A_HEADS * LANES)
    pad = jnp.zeros((256 - Q_LORA, A_HEADS * LANES), w_uq.dtype)
    wq_main = jnp.concatenate([wq_main, pad], axis=0)
    wq_rot = jnp.concatenate([wq_rot, pad], axis=0)
    wkv = w_ukv.reshape(KV_LORA, A_HEADS, A_NOPE + A_V)
    knope, v = wkv[..., :A_NOPE], wkv[..., A_NOPE:]
    wk = jnp.concatenate([knope, jnp.zeros_like(knope)], axis=-1).reshape(KV_LORA, A_HEADS * LANES)
    zv = jnp.zeros_like(v)
    wv = jnp.stack([jnp.concatenate([v[:, h], zv[:, h]] if h % 2 == 0 else [zv[:, h], v[:, h]], axis=-1)
                    for h in range(A_HEADS)], axis=1).reshape(KV_LORA, A_HEADS * LANES)
    return wq_main.astype(BF16), wq_rot.astype(BF16), wk.astype(BF16), wv.astype(BF16)


def _rope_tables(seq):
    inv = ROPE_THETA ** (-jnp.arange(0, A_ROPE, 2, dtype=F32) / A_ROPE)
    ang = jnp.arange(seq, dtype=F32)[:, None] * inv[None, :]
    cos, sin = jnp.cos(ang), jnp.sin(ang)
    one = jnp.ones((seq, A_NOPE), F32)
    zero = jnp.zeros((seq, LANES - A_NOPE - A_ROPE), F32)
    cos_t = jnp.concatenate([one, cos, cos, zero], axis=1)
    sin_t = jnp.concatenate([jnp.zeros_like(one), sin, sin, zero], axis=1)
    return cos_t, sin_t


def _t5_bucket_np(rel):
    n = np.maximum(rel, 0)
    max_exact = NUM_BUCKETS // 2
    nf = np.maximum(n, max_exact).astype(np.float64)
    large = max_exact + (np.log(nf / max_exact) / math.log(MAX_DISTANCE / max_exact)
                         * (NUM_BUCKETS - max_exact)).astype(np.int32)
    large = np.minimum(large, NUM_BUCKETS - 1)
    return np.where(n < max_exact, n, large).astype(np.int32)


MASKED_BUCKET = NUM_BUCKETS


def _bucket_tiles_causal(nq):
    q = np.arange(QB)[:, None]
    k = np.arange(QB)[None, :]
    tiles = [np.full((QB, QB), MASKED_BUCKET, np.int32)]
    for d in range(nq):
        rel = QB * d + q - k
        tiles.append(np.where(rel >= 0, _t5_bucket_np(rel), MASKED_BUCKET).astype(np.int32))
    return np.stack(tiles)


def _mask_tiles():
    q = np.arange(QB)[:, None]
    k = np.arange(QB)[None, :]
    diag = np.where(k <= q, 0.0, NEG_INF)
    return np.stack([np.full((QB, QB), NEG_INF), diag, np.zeros((QB, QB))]).astype(np.float32)


def _bucket_tiles_dilated():
    q = np.arange(QB)[:, None]
    k = np.arange(2 * QB)[None, :]
    rel = q + QB - k
    tiles = []
    for (window, d) in DILATED_PATTERNS:
        in_band = (rel >= 0) & (rel <= window // d)
        for has_prev in (False, True):
            ok = in_band & (has_prev | (k >= QB))
            tiles.append(np.where(ok, _t5_bucket_np(rel * d), MASKED_BUCKET).astype(np.int32))
    return np.stack(tiles)


def _bias_expand_kernel(table_ref, bucket_ref, out_ref, *, head0, scale):
    h = pl.program_id(0) + head0
    bk = bucket_ref[...]
    acc = jnp.where(bk == MASKED_BUCKET, NEG_INF, 0.0)
    for b in range(NUM_BUCKETS):
        acc = jnp.where(bk == b, table_ref[b, h] * scale, acc)
    out_ref[...] = acc


def _bias_expand(table, buckets, head0, nheads, scale=1.0):
    n, r, c = buckets.shape
    return pl.pallas_call(
        functools.partial(_bias_expand_kernel, head0=head0, scale=scale),
        grid=(nheads,),
        in_specs=[pl.BlockSpec(memory_space=pltpu.SMEM),
                  pl.BlockSpec((n, r, c), lambda h: (0, 0, 0))],
        out_specs=pl.BlockSpec((None, n, r, c), lambda h: (h, 0, 0, 0)),
        out_shape=jax.ShapeDtypeStruct((nheads, n, r, c), F32),
        compiler_params=_params(("arbitrary",)),
        name="bias_expand",
    )(table, buckets)


def _in_proj_kernel(x_ref, g_ref, w_ref, o32_ref, o16_ref):
    x = x_ref[...]
    ms = jnp.mean(x * x, axis=-1, keepdims=True)
    xn = x * lax.rsqrt(ms + EPS) * g_ref[...]
    p = _dot(xn.astype(BF16), w_ref[...])
    o32_ref[...] = p
    o16_ref[...] = p.astype(BF16)


def _in_proj(h, g, w, tm=256):
    bsz, seq, _ = h.shape
    return pl.pallas_call(
        _in_proj_kernel,
        grid=(bsz, seq // tm),
        in_specs=[pl.BlockSpec((None, tm, D_MODEL), lambda b, i: (b, i, 0)),
                  pl.BlockSpec((1, D_MODEL), lambda b, i: (0, 0)),
                  pl.BlockSpec((D_MODEL, NCOL), lambda b, i: (0, 0))],
        out_specs=[pl.BlockSpec((None, tm, NCOL), lambda b, i: (b, i, 0)),
                   pl.BlockSpec((None, tm, NCOL), lambda b, i: (b, i, 0))],
        out_shape=[jax.ShapeDtypeStruct((bsz, seq, NCOL), F32),
                   jax.ShapeDtypeStruct((bsz, seq, NCOL), BF16)],
        compiler_params=_params(("arbitrary", "arbitrary")),
        name="in_proj",
    )(h, g, w)


def _mla_prep_kernel(cq_ref, ckv_ref, kr_ref, krrot_ref, cos_ref, sin_ref, gq_ref, gkv_ref,
                     wq_ref, wqrot_ref, wk_ref, wv_ref, q_ref, k_ref, v_ref):
    cos = cos_ref[...]
    sin = sin_ref[...]
    cos4 = jnp.concatenate([cos] * A_HEADS, axis=1)
    sin4 = jnp.concatenate([sin] * A_HEADS, axis=1)
    cq = cq_ref[...]
    ms = jnp.sum(cq * cq, axis=-1, keepdims=True) * (1.0 / Q_LORA)
    nq = (cq * lax.rsqrt(ms + EPS) * gq_ref[...]).astype(BF16)
    q = _dot(nq, wq_ref[...]) * cos4 + _dot(nq, wqrot_ref[...]) * sin4
    q_ref[...] = (q * ((A_NOPE + A_ROPE) ** -0.5 * LOG2E)).astype(BF16)
    ckv = ckv_ref[...]
    ms = jnp.mean(ckv * ckv, axis=-1, keepdims=True)
    nkv = (ckv * lax.rsqrt(ms + EPS) * gkv_ref[...]).astype(BF16)
    kr = kr_ref[...] * cos + krrot_ref[...] * sin
    k = _dot(nkv, wk_ref[...]) + jnp.concatenate([kr] * A_HEADS, axis=1)
    k_ref[...] = k.astype(BF16)
    v_ref[...] = _dot(nkv, wv_ref[...]).astype(BF16)


def _mla_prep(p32, cos_t, sin_t, gq, gkv, wq, wqrot, wk, wv, tm=512):
    bsz, seq, _ = p32.shape
    w4 = A_HEADS * LANES
    row = lambda b, i: (b, i, 0)
    const = lambda b, i: (0, 0)
    return pl.pallas_call(
        _mla_prep_kernel,
        grid=(bsz, seq // tm),
        in_specs=[pl.BlockSpec((None, tm, 256), lambda b, i: (b, i, P_CQ // 256)),
                  pl.BlockSpec((None, tm, LANES), lambda b, i: (b, i, P_CKV // LANES)),
                  pl.BlockSpec((None, tm, LANES), lambda b, i: (b, i, P_KR // LANES)),
                  pl.BlockSpec((None, tm, LANES), lambda b, i: (b, i, P_KRROT // LANES)),
                  pl.BlockSpec((tm, LANES), lambda b, i: (i, 0)),
                  pl.BlockSpec((tm, LANES), lambda b, i: (i, 0)),
                  pl.BlockSpec((1, 256), const),
                  pl.BlockSpec((1, KV_LORA), const),
                  pl.BlockSpec((256, w4), const),
                  pl.BlockSpec((256, w4), const),
                  pl.BlockSpec((KV_LORA, w4), const),
                  pl.BlockSpec((KV_LORA, w4), const)],
        out_specs=[pl.BlockSpec((None, tm, w4), row)] * 3,
        out_shape=[jax.ShapeDtypeStruct((bsz, seq, w4), BF16)] * 3,
        compiler_params=_params(("arbitrary", "arbitrary")),
        name="mla_prep",
    )(p32, p32, p32, p32, cos_t, sin_t, gq, gkv, wq, wqrot, wk, wv)


def _silu(g):
    return g * (1.0 / (1.0 + jnp.exp(-g)))


def _by_causal_width(i, seq, body):
    per = KV_CHUNK // QB
    for wb in range(seq // KV_CHUNK):
        pl.when(i // per == wb)(functools.partial(body, (wb + 1) * KV_CHUNK))


def _mask_tail(mask_ref, i, width):
    first = (width - KV_CHUNK) // QB
    return jnp.concatenate([mask_ref[jnp.clip(i - j, -1, 1) + 1] for j in range(first, width // QB)], axis=1)


def _bias_row(bias_ref, h, i, width):
    return jnp.concatenate([bias_ref[h, jnp.maximum(i - j, -1) + 1] for j in range(width // QB)], axis=1)


def _col_reduce(x, op):
    rows, lanes = x.shape
    part = op(x.reshape(rows // COL_ACC_ROWS, COL_ACC_ROWS, lanes), axis=0)
    return op(part, axis=0, keepdims=True)


def _bias_col(bias_ref, h, i, width):
    return jnp.concatenate([bias_ref[h, jnp.maximum(i - j, -1) + 1] for j in range(width // QB)], axis=0)


def _softmax_pv(s, v):
    m = jnp.max(s, axis=-1, keepdims=True)
    e = jnp.exp2(s - m)
    l = jnp.sum(e, axis=-1, keepdims=True)
    return _dot(e.astype(BF16), v), l


def _lane_mask(width, seg, dtype):
    lane = lax.broadcasted_iota(jnp.int32, (1, LANES), 1)
    return jnp.where((lane >= seg * width) & (lane < (seg + 1) * width), 1.0, 0.0).astype(dtype)


def _mla_attn_kernel(q_ref, k_ref, v_ref, g_ref, mask_ref, o_ref):
    i = pl.program_id(1)
    seq = k_ref.shape[0]

    def body(width):
        head_w = width - KV_CHUNK
        tail = _mask_tail(mask_ref, i, width)
        pairs = []
        for g in range(A_HEADS // 2):
            acc = None
            for h in (2 * g, 2 * g + 1):
                sl = slice(h * LANES, (h + 1) * LANES)
                s = _dot_nt(q_ref[:, sl], k_ref[:width, sl])
                s = jnp.concatenate([s[:, :head_w], s[:, head_w:] + tail], axis=1) if head_w else s + tail
                pv, l = _softmax_pv(s, v_ref[:width, sl])
                o = pv / l
                acc = o if acc is None else acc + o
            pairs.append(acc)
        y = jnp.concatenate(pairs, axis=1)
        o_ref[...] = (y * _silu(g_ref[...])).astype(BF16)

    _by_causal_width(i, seq, body)


def _mla_attn(q, k, v, p32, mask_tiles):
    bsz, seq, w4 = q.shape
    return pl.pallas_call(
        _mla_attn_kernel,
        grid=(bsz, seq // QB),
        in_specs=[pl.BlockSpec((None, QB, w4), lambda b, i: (b, i, 0)),
                  pl.BlockSpec((None, seq, w4), lambda b, i: (b, 0, 0)),
                  pl.BlockSpec((None, seq, w4), lambda b, i: (b, 0, 0)),
                  pl.BlockSpec((None, QB, 256), lambda b, i: (b, i, P_AG // 256)),
                  pl.BlockSpec((3, QB, QB), lambda b, i: (0, 0, 0))],
        out_specs=pl.BlockSpec((None, QB, BRANCH_WIDTH), lambda b, i: (b, i, 0)),
        out_shape=jax.ShapeDtypeStruct((bsz, seq, BRANCH_WIDTH), BF16),
        compiler_params=_params(("arbitrary", "arbitrary")),
        name="mla_attn",
    )(q, k, v, p32, mask_tiles)


def _transpose_values(v_ref, vt_ref):
    for j in range(v_ref.shape[0] // QB):
        for g in range(v_ref.shape[1] // LANES):
            tile = v_ref[j * QB:(j + 1) * QB, g * LANES:(g + 1) * LANES]
            vt_ref[g * LANES:(g + 1) * LANES, j * QB:(j + 1) * QB] = tile.T.astype(BF16)


def _diff_attn_kernel(q_ref, k_ref, v_ref, g_ref, bias_ref, lam_ref, subln_ref, o_ref, vt_ref, *, lambda_init):
    i = pl.program_id(1)
    seq = k_ref.shape[0]

    @pl.when(i == 0)
    def _():
        _transpose_values(v_ref, vt_ref)

    def body(width):
        lp = lam_ref[...]
        lam = (jnp.exp(jnp.sum(lp[0:1] * lp[1:2], axis=-1, keepdims=True))
               - jnp.exp(jnp.sum(lp[2:3] * lp[3:4], axis=-1, keepdims=True)) + lambda_init)
        scores = []
        for h in range(4):
            sl = slice((h // 2) * LANES, (h // 2 + 1) * LANES)
            qg = q_ref[:, sl] * (D_QK ** -0.5 * LOG2E)
            qcat = jnp.concatenate([(qg * _lane_mask(D_QK, 2 * (h % 2) + mm, F32)).astype(BF16)
                                    for mm in range(2)], axis=0)
            scores.append(_dot_nt(k_ref[:width, sl], qcat))
        outs = []
        for h in range(4):
            bias = _bias_col(bias_ref, h, i, width)
            es, ls = [], []
            for mm in range(2):
                s = scores[h][:, mm * QB:(mm + 1) * QB] + bias
                e = jnp.exp2(s - _col_reduce(s, jnp.max))
                ls.append(_col_reduce(e, jnp.sum))
                es.append(e.astype(BF16))
            ot = _dot(vt_ref[h * HEAD_DIM:(h + 1) * HEAD_DIM, :width], jnp.concatenate(es, axis=1))
            a = ot[:, :QB] / ls[0] - lam * (ot[:, QB:] / ls[1])
            ms = jnp.sum(a * a, axis=0, keepdims=True) * (1.0 / HEAD_DIM)
            outs.append(a * lax.rsqrt(ms + EPS))
        y_t = jnp.concatenate(outs, axis=0)
        y = jnp.concatenate([y_t[:LANES].T, y_t[LANES:].T], axis=1)
        o_ref[...] = (y * (subln_ref[...] * (1.0 - lambda_init)) * _silu(g_ref[...])).astype(BF16)

    _by_causal_width(i, seq, body)


def _diff_attn(p32, p16, bias, lam_params, subln, lambda_init):
    bsz, seq, _ = p32.shape
    nq = seq // QB
    return pl.pallas_call(
        functools.partial(_diff_attn_kernel, lambda_init=lambda_init),
        grid=(bsz, nq),
        in_specs=[pl.BlockSpec((None, QB, 256), lambda b, i: (b, i, P_DQ // 256)),
                  pl.BlockSpec((None, seq, 256), lambda b, i: (b, 0, P_DK // 256)),
                  pl.BlockSpec((None, seq, 256), lambda b, i: (b, 0, P_DV // 256)),
                  pl.BlockSpec((None, QB, 256), lambda b, i: (b, i, P_DG // 256)),
                  pl.BlockSpec((4, nq + 1, QB, QB), lambda b, i: (0, 0, 0, 0)),
                  pl.BlockSpec((4, D_QK), lambda b, i: (0, 0)),
                  pl.BlockSpec((1, BRANCH_WIDTH), lambda b, i: (0, 0))],
        out_specs=pl.BlockSpec((None, QB, BRANCH_WIDTH), lambda b, i: (b, i, 0)),
        out_shape=jax.ShapeDtypeStruct((bsz, seq, BRANCH_WIDTH), BF16),
        scratch_shapes=[pltpu.VMEM((BRANCH_WIDTH, seq), BF16)],
        compiler_params=_params(("arbitrary", "arbitrary")),
        name="diff_attn",
    )(p32, p16, p32, p32, bias, lam_params, subln)


def _sortable_to_float(key):
    return pltpu.bitcast(jnp.where(key < 0, key ^ jnp.int32(0x7FFFFFFF), key), F32)


def _kth_largest(score_ref, width, k_top):
    def count_ge(key):
        thr = _sortable_to_float(key)
        return _col_reduce(jnp.where(score_ref[:width, :] >= thr, 1.0, 0.0), jnp.sum)

    int_min = jnp.full((1, QB), -2 ** 31, jnp.int32)
    zero = jnp.zeros((1, QB), jnp.int32)
    t = jnp.where(count_ge(zero) >= k_top, zero, int_min)
    for bit in range(30, -1, -1):
        cand = t + jnp.int32(1 << bit)
        t = jnp.where(count_ge(cand) >= k_top, cand, t)
    return _sortable_to_float(t)


def _first_ties(eq, need, row):
    eqf = jnp.where(eq, 1.0, 0.0)
    nbits = int(eq.shape[0]).bit_length()

    def body(it, j):
        cand = j + (jnp.int32(1) << (nbits - 1 - it))
        cnt = _col_reduce(jnp.where(row < cand, eqf, 0.0), jnp.sum)
        return jnp.where(cnt <= need, cand, j)

    j = lax.fori_loop(0, nbits, body, jnp.zeros((1, eq.shape[1]), jnp.int32))
    return eq & (row < j)


def _sparse_attn_kernel(q_ref, k_ref, v_ref, qi_ref, ki_ref, wi_ref, g_ref, bias_ref, o_ref,
                        score_ref, neg_ref, vt_ref, *, k_top):
    i = pl.program_id(1)
    seq = k_ref.shape[0]

    @pl.when(i == 0)
    def _():
        _transpose_values(v_ref, vt_ref)

    def body(width):
        head_w = width - KV_CHUNK
        ki = ki_ref[:width, :]
        w_t = wi_ref[...].T * (IDX_DIM ** -0.5 * IDX_HEADS ** -0.5)
        score = None
        for h in range(0, IDX_HEADS, 2):
            qg = qi_ref[:, (h // 4) * LANES:(h // 4 + 1) * LANES]
            qcat = jnp.concatenate([qg * _lane_mask(IDX_DIM, h % 4, BF16),
                                    qg * _lane_mask(IDX_DIM, h % 4 + 1, BF16)], axis=0)
            logit = _dot_nt(ki, qcat)
            term = (jnp.maximum(logit[:, :QB], 0.0) * w_t[h:h + 1]
                    + jnp.maximum(logit[:, QB:], 0.0) * w_t[h + 1:h + 2])
            score = term if score is None else score + term
        s_idx = head_w + lax.broadcasted_iota(jnp.int32, (KV_CHUNK, QB), 0)
        tail_ok = s_idx <= i * QB + lax.broadcasted_iota(jnp.int32, (KV_CHUNK, QB), 1)
        if head_w:
            score_ref[:head_w, :] = score[:head_w]
        score_ref[head_w:width, :] = jnp.where(tail_ok, score[head_w:], NEG_INF)
        s2s = []
        for g in range(2):
            sl = slice(g * LANES, (g + 1) * LANES)
            qg = q_ref[:, sl] * (HEAD_DIM ** -0.5 * LOG2E)
            qcat = jnp.concatenate([(qg * _lane_mask(HEAD_DIM, hh, F32)).astype(BF16) for hh in range(2)], axis=0)
            s2s.append(_dot_nt(k_ref[:width, sl], qcat))
        thr = _kth_largest(score_ref, width, k_top)
        keep_tail = (score_ref[head_w:width, :] >= thr) & tail_ok
        cnt = _col_reduce(jnp.where(keep_tail, 1.0, 0.0), jnp.sum)
        neg_ref[head_w:width, :] = jnp.where(keep_tail, 0.0, NEG_INF)
        if head_w:
            keep_head = score_ref[:head_w, :] >= thr
            cnt = cnt + _col_reduce(jnp.where(keep_head, 1.0, 0.0), jnp.sum)
            neg_ref[:head_w, :] = jnp.where(keep_head, 0.0, NEG_INF)

        @pl.when(jnp.max(cnt) > k_top)
        def _():
            x = score_ref[:width, :]
            gt = x > thr
            need = k_top - _col_reduce(jnp.where(gt, 1.0, 0.0), jnp.sum)
            row = lax.broadcasted_iota(jnp.int32, (width, QB), 0)
            neg_ref[:width, :] = jnp.where(gt | _first_ties(x == thr, need, row), 0.0, NEG_INF)

        outs = []
        for g in range(2):
            sl = slice(g * LANES, (g + 1) * LANES)
            s2 = s2s[g]
            es, ls = [], []
            for hh in range(2):
                s = s2[:, hh * QB:(hh + 1) * QB] + (_bias_col(bias_ref, 2 * g + hh, i, width) + neg_ref[:width, :])
                e = jnp.exp2(s - _col_reduce(s, jnp.max))
                ls.append(_col_reduce(e, jnp.sum))
                es.append(e.astype(BF16))
            ot = _dot(vt_ref[sl, :width], jnp.concatenate(es, axis=1))
            outs.append(ot[:HEAD_DIM, :QB] / ls[0])
            outs.append(ot[HEAD_DIM:, QB:] / ls[1])
        y_t = jnp.concatenate(outs, axis=0)
        y = jnp.concatenate([y_t[:LANES].T, y_t[LANES:].T], axis=1)
        o_ref[...] = (y * _silu(g_ref[...])).astype(BF16)

    _by_causal_width(i, seq, body)


def _sparse_attn(p32, p16, bias, k_top):
    bsz, seq, _ = p32.shape
    nq = seq // QB
    return pl.pallas_call(
        functools.partial(_sparse_attn_kernel, k_top=k_top),
        grid=(bsz, nq),
        in_specs=[pl.BlockSpec((None, QB, 256), lambda b, i: (b, i, P_CQ2 // 256)),
                  pl.BlockSpec((None, seq, 256), lambda b, i: (b, 0, P_CK // 256)),
                  pl.BlockSpec((None, seq, 256), lambda b, i: (b, 0, P_CV // 256)),
                  pl.BlockSpec((None, QB, 256), lambda b, i: (b, i, P_CQI // 256)),
                  pl.BlockSpec((None, seq, LANES), lambda b, i: (b, 0, P_CKI // LANES)),
                  pl.BlockSpec((None, QB, LANES), lambda b, i: (b, i, P_CWI // LANES)),
                  pl.BlockSpec((None, QB, 256), lambda b, i: (b, i, P_CG // 256)),
                  pl.BlockSpec((4, nq + 1, QB, QB), lambda b, i: (0, 0, 0, 0))],
        out_specs=pl.BlockSpec((None, QB, BRANCH_WIDTH), lambda b, i: (b, i, 0)),
        out_shape=jax.ShapeDtypeStruct((bsz, seq, BRANCH_WIDTH), BF16),
        scratch_shapes=[pltpu.VMEM((seq, QB), F32), pltpu.VMEM((seq, QB), F32),
                        pltpu.VMEM((BRANCH_WIDTH, seq), BF16)],
        compiler_params=_params(("arbitrary", "arbitrary")),
        name="sparse_attn",
    )(p32, p16, p32, p16, p16, p32, p32, bias)


def _dilated_kernel(q0_ref, q1_ref, k0_ref, k1_ref, v0_ref, v1_ref, g_ref, bias_ref, o_ref,
                    m_ref, l_ref, acc_ref):
    seq = q0_ref.shape[0]
    q_refs, k_refs, v_refs = (q0_ref, q1_ref), (k0_ref, k1_ref), (v0_ref, v1_ref)
    first = lax.broadcasted_iota(jnp.int32, (QB, LANES), 1) < HEAD_DIM
    qmasks = [_lane_mask(HEAD_DIM, hh, F32) * (HEAD_DIM ** -0.5 * LOG2E) for hh in range(2)]
    for p, (window, d) in enumerate(DILATED_PATTERNS):
        nb = seq // d // QB
        with_prev = nb > 1

        def group(n0, carry, p=p, d=d, with_prev=with_prev):
            work = []
            for u in range(DIL_UNROLL):
                n = n0 * DIL_UNROLL + u
                r = n % d
                j = n // d
                start = r + d * QB * j
                cur_rows = pl.ds(start, QB, stride=d)
                prev_rows = pl.ds(jnp.maximum(start - d * QB, r), QB, stride=d)
                for g in range(2):
                    q = q_refs[g][cur_rows, :]
                    kcat = k_refs[g][cur_rows, :]
                    vcat = v_refs[g][cur_rows, :]
                    if with_prev:
                        kcat = jnp.concatenate([k_refs[g][prev_rows, :], kcat], axis=0)
                        vcat = jnp.concatenate([v_refs[g][prev_rows, :], vcat], axis=0)
                    kcat = kcat.astype(BF16)
                    scores = []
                    for hh in range(2):
                        if with_prev:
                            bias = bias_ref[2 * g + hh, 2 * p + jnp.minimum(j, 1)]
                        else:
                            bias = bias_ref[2 * g + hh, 2 * p, :, QB:]
                        scores.append(_dot_nt((q * qmasks[hh]).astype(BF16), kcat) + bias)
                    work.append((cur_rows, g, vcat.astype(BF16), scores))
            for cur_rows, g, vcat, scores in work:
                ms, ls, accs = [], [], []
                for s in scores:
                    m = jnp.max(s, axis=-1, keepdims=True)
                    e = jnp.exp2(s - m)
                    ms.append(m)
                    ls.append(jnp.sum(e, axis=-1, keepdims=True))
                    accs.append(_dot(e.astype(BF16), vcat))
                m_new = jnp.where(first, ms[0], ms[1])
                l_new = jnp.where(first, ls[0], ls[1])
                a_new = jnp.where(first, accs[0], accs[1])
                if p == 0:
                    m_ref[g, cur_rows, :] = m_new
                    l_ref[g, cur_rows, :] = l_new
                    acc_ref[g, cur_rows, :] = a_new
                else:
                    m_old = m_ref[g, cur_rows, :]
                    m_tot = jnp.maximum(m_old, m_new)
                    w_old = jnp.exp2(m_old - m_tot)
                    w_new = jnp.exp2(m_new - m_tot)
                    m_ref[g, cur_rows, :] = m_tot
                    l_ref[g, cur_rows, :] = w_old * l_ref[g, cur_rows, :] + w_new * l_new
                    acc_ref[g, cur_rows, :] = w_old * acc_ref[g, cur_rows, :] + w_new * a_new
            return carry

        lax.fori_loop(0, d * nb // DIL_UNROLL, group, 0)
    gate = _silu(g_ref[...])
    for g in range(2):
        sl = slice(g * LANES, (g + 1) * LANES)
        o_ref[:, sl] = (acc_ref[g] / l_ref[g] * gate[:, sl]).astype(BF16)


def _dilated_attn(p32, bias):
    bsz, seq, _ = p32.shape

    def slab(off):
        return pl.BlockSpec((None, seq, LANES), lambda b, off=off: (b, 0, off // LANES))

    return pl.pallas_call(
        _dilated_kernel,
        grid=(bsz,),
        in_specs=[slab(P_BQ), slab(P_BQ + LANES), slab(P_BK), slab(P_BK + LANES),
                  slab(P_BV), slab(P_BV + LANES),
                  pl.BlockSpec((None, seq, 256), lambda b: (b, 0, P_BG // 256)),
                  pl.BlockSpec((4, 2 * len(DILATED_PATTERNS), QB, 2 * QB), lambda b: (0, 0, 0, 0))],
        out_specs=pl.BlockSpec((None, seq, BRANCH_WIDTH), lambda b: (b, 0, 0)),
        out_shape=jax.ShapeDtypeStruct((bsz, seq, BRANCH_WIDTH), BF16),
        scratch_shapes=[pltpu.VMEM((2, seq, LANES), F32)] * 3,
        compiler_params=_params(("arbitrary",)),
        name="dilated_attn",
    )(p32, p32, p32, p32, p32, p32, p32, bias)


def _out_proj_kernel(ya_ref, yb_ref, yc_ref, yd_ref, w_ref, h_ref, g_ref, o_ref):
    y = None
    for n, ref in enumerate((ya_ref, yb_ref, yc_ref, yd_ref)):
        t = _dot(ref[...], w_ref[n * BRANCH_WIDTH:(n + 1) * BRANCH_WIDTH, :])
        y = t if y is None else y + t
    ms = jnp.mean(y * y, axis=-1, keepdims=True)
    o_ref[...] = h_ref[...] + y * lax.rsqrt(ms + EPS) * g_ref[...]


def _out_proj(ya, yb, yc, yd, w, h, g, tm=512):
    bsz, seq, _ = h.shape
    yspec = pl.BlockSpec((None, tm, BRANCH_WIDTH), lambda b, i: (b, i, 0))
    return pl.pallas_call(
        _out_proj_kernel,
        grid=(bsz, seq // tm),
        in_specs=[yspec, yspec, yspec, yspec,
                  pl.BlockSpec((4 * BRANCH_WIDTH, D_MODEL), lambda b, i: (0, 0)),
                  pl.BlockSpec((None, tm, D_MODEL), lambda b, i: (b, i, 0)),
                  pl.BlockSpec((1, D_MODEL), lambda b, i: (0, 0))],
        out_specs=pl.BlockSpec((None, tm, D_MODEL), lambda b, i: (b, i, 0)),
        out_shape=jax.ShapeDtypeStruct(h.shape, F32),
        compiler_params=_params(("arbitrary", "arbitrary")),
        name="out_proj",
    )(ya, yb, yc, yd, w, h, g)


def kernel(x, w_in, w_out, norm_pre, norm_post, mla_q_norm, mla_kv_norm, mla_w_uq, mla_w_ukv,
           diff_lambda, diff_subln, rel_bias):
    bsz, seq, _ = x.shape
    depth = w_in.shape[0]
    nq = seq // QB
    k_top = min(IDX_TOPK_MAX, seq // 4)
    cos_t, sin_t = _rope_tables(seq)
    bias_b = _bias_expand(rel_bias, jnp.asarray(_bucket_tiles_dilated()), 0, 4, LOG2E)
    causal_buckets_t = jnp.asarray(np.swapaxes(_bucket_tiles_causal(nq), 1, 2))
    bias_c = _bias_expand(rel_bias, causal_buckets_t, 4, 4, LOG2E)
    bias_d = _bias_expand(rel_bias, causal_buckets_t, 8, 4, LOG2E)
    mask_tiles = jnp.asarray(_mask_tiles())
    h = x
    for layer in range(depth):
        w_arr = _arrange_w_in(w_in[layer]).astype(BF16)
        wq, wqrot, wk, wv = _arrange_mla(mla_w_uq[layer], mla_w_ukv[layer])
        gq = jnp.concatenate([mla_q_norm[layer], jnp.ones((256 - Q_LORA,), F32)])[None, :]
        gkv = mla_kv_norm[layer][None, :]
        p32, p16 = _in_proj(h, norm_pre[layer][None, :], w_arr)
        qa, ka, va = _mla_prep(p32, cos_t, sin_t, gq, gkv, wq, wqrot, wk, wv)
        y_a = _mla_attn(qa, ka, va, p32, mask_tiles)
        y_b = _dilated_attn(p32, bias_b)
        y_c = _sparse_attn(p32, p16, bias_c, k_top)
        lambda_init = 0.8 - 0.6 * math.exp(-0.3 * layer)
        subln = jnp.tile(diff_subln[layer], BRANCH_WIDTH // HEAD_DIM)[None, :]
        y_d = _diff_attn(p32, p16, bias_d, diff_lambda[layer], subln, lambda_init)
        h = _out_proj(y_a, y_b, y_c, y_d, w_out[layer].astype(BF16), h, norm_post[layer][None, :])
    return h
```

```python
import functools
import math

import jax
import jax.numpy as jnp
import numpy as np
from jax import lax
from jax.experimental import pallas as pl
from jax.experimental.pallas import tpu as pltpu

F32 = jnp.float32
BF16 = jnp.bfloat16

D_MODEL = 1024
A_HEADS, A_NOPE, A_ROPE, A_V = 4, 64, 32, 64
Q_LORA, KV_LORA = 192, 128
ROPE_THETA = 10000.0
HEAD_DIM = 64
DILATED_PATTERNS = ((128, 1), (512, 4), (2048, 16))
IDX_HEADS, IDX_DIM, IDX_TOPK_MAX = 8, 32, 256
D_QK = 32
BRANCH_WIDTH = 256
NUM_BUCKETS, MAX_DISTANCE = 32, 2048
NEG_INF = -1e30
EPS = 1e-6
LOG2E = math.log2(math.e)
KV_CHUNK = 512
COL_ACC_ROWS = 64
DIL_UNROLL = 4
LANES = 128
QB = 128
VMEM_LIMIT = 56 * 1024 * 1024

_SPLIT = (Q_LORA, KV_LORA, A_ROPE, 256, 256, 256, 256, 256, 256, 256, 256, IDX_HEADS * IDX_DIM, IDX_DIM,
          IDX_HEADS, 256, 256, 256, 256, 256)
_OFF = np.concatenate([[0], np.cumsum(_SPLIT)]).tolist()
(_A_CQ, _A_CKV, _A_KR, _A_G, _B_Q, _B_K, _B_V, _B_G, _C_Q, _C_K, _C_V, _C_QI, _C_KI, _C_WI, _C_G,
 _D_Q, _D_K, _D_V, _D_G) = range(19)

P_CQ, P_AG, P_CKV, P_KR, P_KRROT, P_CKI = 0, 256, 512, 640, 768, 896
P_BQ, P_BK, P_BV, P_BG = 1024, 1280, 1536, 1792
P_CQ2, P_CK, P_CV, P_CQI, P_CG = 2048, 2304, 2560, 2816, 3072
P_DQ, P_DK, P_DV, P_DG = 3328, 3584, 3840, 4096
P_CWI = 4352
NCOL = 4480


def _dot(a, b):
    return jnp.dot(a, b, preferred_element_type=F32)


def _dot_nt(a, b):
    return lax.dot_general(a, b, (((1,), (1,)), ((), ())), preferred_element_type=F32)


def _params(sem):
    return pltpu.CompilerParams(dimension_semantics=sem, vmem_limit_bytes=VMEM_LIMIT)


def _rot_cols(w):
    half = w.shape[-1] // 2
    return jnp.concatenate([-w[..., half:], w[..., :half]], axis=-1)


def _arrange_w_in(w):
    def seg(i):
        return w[:, _OFF[i]:_OFF[i + 1]]

    def z(n):
        return jnp.zeros((w.shape[0], n), w.dtype)

    kr = seg(_A_KR)
    cols = [
        seg(_A_CQ), z(64),
        seg(_A_G),
        seg(_A_CKV),
        z(64), kr, z(32),
        z(64), _rot_cols(kr), z(32),
        seg(_C_KI), seg(_C_KI), seg(_C_KI), seg(_C_KI),
        seg(_B_Q), seg(_B_K), seg(_B_V), seg(_B_G),
        seg(_C_Q), seg(_C_K), seg(_C_V), seg(_C_QI), seg(_C_G),
        seg(_D_Q), seg(_D_K), seg(_D_V), seg(_D_G),
        seg(_C_WI), z(LANES - IDX_HEADS),
    ]
    out = jnp.concatenate(cols, axis=1)
    assert out.shape[1] == NCOL
    return out


def _arrange_mla(w_uq, w_ukv):
    wq = w_uq.reshape(Q_LORA, A_HEADS, A_NOPE + A_ROPE)
    nope, rope = wq[..., :A_NOPE], wq[..., A_NOPE:]
    zq = jnp.zeros((Q_LORA, A_HEADS, LANES - A_NOPE - A_ROPE), w_uq.dtype)
    wq_main = jnp.concatenate([nope, rope, zq], axis=-1).reshape(Q_LORA, A_HEADS * LANES)
    wq_rot = jnp.concatenate([jnp.zeros_like(nope), _rot_cols(rope), zq], axis=-1).reshape(Q_LORA, A_HEADS * LANES)
    pad = jnp.zeros((256 - Q_LORA, A_HEADS * LANES), w_uq.dtype)
    wq_main = jnp.concatenate([wq_main, pad], axis=0)
    wq_rot = jnp.concatenate([wq_rot, pad], axis=0)
    wkv = w_ukv.reshape(KV_LORA, A_HEADS, A_NOPE + A_V)
    knope, v = wkv[..., :A_NOPE], wkv[..., A_NOPE:]
    wk = jnp.concatenate([knope, jnp.zeros_like(knope)], axis=-1).reshape(KV_LORA, A_HEADS * LANES)
    wv_t = v.reshape(KV_LORA, A_HEADS * A_V).T
    return wq_main.astype(BF16), wq_rot.astype(BF16), wk.astype(BF16), wv_t.astype(BF16)


def _rope_tables(seq):
    inv = ROPE_THETA ** (-jnp.arange(0, A_ROPE, 2, dtype=F32) / A_ROPE)
    ang = jnp.arange(seq, dtype=F32)[:, None] * inv[None, :]
    cos, sin = jnp.cos(ang), jnp.sin(ang)
    one = jnp.ones((seq, A_NOPE), F32)
    zero = jnp.zeros((seq, LANES - A_NOPE - A_ROPE), F32)
    cos_t = jnp.concatenate([one, cos, cos, zero], axis=1)
    sin_t = jnp.concatenate([jnp.zeros_like(one), sin, sin, zero], axis=1)
    return cos_t, sin_t


def _t5_bucket_np(rel):
    n = np.maximum(rel, 0)
    max_exact = NUM_BUCKETS // 2
    nf = np.maximum(n, max_exact).astype(np.float64)
    large = max_exact + (np.log(nf / max_exact) / math.log(MAX_DISTANCE / max_exact)
                         * (NUM_BUCKETS - max_exact)).astype(np.int32)
    large = np.minimum(large, NUM_BUCKETS - 1)
    return np.where(n < max_exact, n, large).astype(np.int32)


MASKED_BUCKET = NUM_BUCKETS


def _bucket_tiles_causal(nq):
    q = np.arange(QB)[:, None]
    k = np.arange(QB)[None, :]
    tiles = [np.full((QB, QB), MASKED_BUCKET, np.int32)]
    for d in range(nq):
        rel = QB * d + q - k
        tiles.append(np.where(rel >= 0, _t5_bucket_np(rel), MASKED_BUCKET).astype(np.int32))
    return np.stack(tiles)


def _mask_tiles():
    k = np.arange(QB)[:, None]
    q = np.arange(QB)[None, :]
    diag = np.where(k <= q, 0.0, NEG_INF)
    return np.stack([np.full((QB, QB), NEG_INF), diag, np.zeros((QB, QB))]).astype(np.float32)


def _bucket_tiles_dilated():
    q = np.arange(QB)[:, None]
    k = np.arange(2 * QB)[None, :]
    rel = q + QB - k
    tiles = []
    for (window, d) in DILATED_PATTERNS:
        in_band = (rel >= 0) & (rel <= window // d)
        for has_prev in (False, True):
            ok = in_band & (has_prev | (k >= QB))
            tiles.append(np.where(ok, _t5_bucket_np(rel * d), MASKED_BUCKET).astype(np.int32))
    return np.stack(tiles)


def _bias_expand_kernel(table_ref, bucket_ref, out_ref, *, head0, scale):
    h = pl.program_id(0) + head0
    bk = bucket_ref[...]
    acc = jnp.where(bk == MASKED_BUCKET, NEG_INF, 0.0)
    for b in range(NUM_BUCKETS):
        acc = jnp.where(bk == b, table_ref[b, h] * scale, acc)
    out_ref[...] = acc


def _bias_expand(table, buckets, head0, nheads, scale=1.0):
    n, r, c = buckets.shape
    return pl.pallas_call(
        functools.partial(_bias_expand_kernel, head0=head0, scale=scale),
        grid=(nheads,),
        in_specs=[pl.BlockSpec(memory_space=pltpu.SMEM),
                  pl.BlockSpec((n, r, c), lambda h: (0, 0, 0))],
        out_specs=pl.BlockSpec((None, n, r, c), lambda h: (h, 0, 0, 0)),
        out_shape=jax.ShapeDtypeStruct((nheads, n, r, c), F32),
        compiler_params=_params(("arbitrary",)),
        name="bias_expand",
    )(table, buckets)


def _in_proj_kernel(x_ref, g_ref, w_ref, o32_ref, o16_ref):
    x = x_ref[...]
    ms = jnp.mean(x * x, axis=-1, keepdims=True)
    xn = x * lax.rsqrt(ms + EPS) * g_ref[...]
    p = _dot(xn.astype(BF16), w_ref[...])
    o32_ref[...] = p
    o16_ref[...] = p.astype(BF16)


def _in_proj(h, g, w, tm=256):
    bsz, seq, _ = h.shape
    return pl.pallas_call(
        _in_proj_kernel,
        grid=(bsz, seq // tm),
        in_specs=[pl.BlockSpec((None, tm, D_MODEL), lambda b, i: (b, i, 0)),
                  pl.BlockSpec((1, D_MODEL), lambda b, i: (0, 0)),
                  pl.BlockSpec((D_MODEL, NCOL), lambda b, i: (0, 0))],
        out_specs=[pl.BlockSpec((None, tm, NCOL), lambda b, i: (b, i, 0)),
                   pl.BlockSpec((None, tm, NCOL), lambda b, i: (b, i, 0))],
        out_shape=[jax.ShapeDtypeStruct((bsz, seq, NCOL), F32),
                   jax.ShapeDtypeStruct((bsz, seq, NCOL), BF16)],
        compiler_params=_params(("arbitrary", "arbitrary")),
        name="in_proj",
    )(h, g, w)


def _mla_prep_kernel(cq_ref, ckv_ref, kr_ref, krrot_ref, cos_ref, sin_ref, gq_ref, gkv_ref,
                     wq_ref, wqrot_ref, wk_ref, wv_ref, q_ref, k_ref, v_ref):
    cos = cos_ref[...]
    sin = sin_ref[...]
    cos4 = jnp.concatenate([cos] * A_HEADS, axis=1)
    sin4 = jnp.concatenate([sin] * A_HEADS, axis=1)
    cq = cq_ref[...]
    ms = jnp.sum(cq * cq, axis=-1, keepdims=True) * (1.0 / Q_LORA)
    nq = (cq * lax.rsqrt(ms + EPS) * gq_ref[...]).astype(BF16)
    q = _dot(nq, wq_ref[...]) * cos4 + _dot(nq, wqrot_ref[...]) * sin4
    q_ref[...] = (q * ((A_NOPE + A_ROPE) ** -0.5 * LOG2E)).astype(BF16)
    ckv = ckv_ref[...]
    ms = jnp.mean(ckv * ckv, axis=-1, keepdims=True)
    nkv = (ckv * lax.rsqrt(ms + EPS) * gkv_ref[...]).astype(BF16)
    kr = kr_ref[...] * cos + krrot_ref[...] * sin
    k = _dot(nkv, wk_ref[...]) + jnp.concatenate([kr] * A_HEADS, axis=1)
    k_ref[...] = k.astype(BF16)
    v_ref[...] = _dot_nt(wv_ref[...], nkv).astype(BF16)


def _mla_prep(p32, cos_t, sin_t, gq, gkv, wq, wqrot, wk, wv_t, tm=512):
    bsz, seq, _ = p32.shape
    w4 = A_HEADS * LANES
    row = lambda b, i: (b, i, 0)
    const = lambda b, i: (0, 0)
    return pl.pallas_call(
        _mla_prep_kernel,
        grid=(bsz, seq // tm),
        in_specs=[pl.BlockSpec((None, tm, 256), lambda b, i: (b, i, P_CQ // 256)),
                  pl.BlockSpec((None, tm, LANES), lambda b, i: (b, i, P_CKV // LANES)),
                  pl.BlockSpec((None, tm, LANES), lambda b, i: (b, i, P_KR // LANES)),
                  pl.BlockSpec((None, tm, LANES), lambda b, i: (b, i, P_KRROT // LANES)),
                  pl.BlockSpec((tm, LANES), lambda b, i: (i, 0)),
                  pl.BlockSpec((tm, LANES), lambda b, i: (i, 0)),
                  pl.BlockSpec((1, 256), const),
                  pl.BlockSpec((1, KV_LORA), const),
                  pl.BlockSpec((256, w4), const),
                  pl.BlockSpec((256, w4), const),
                  pl.BlockSpec((KV_LORA, w4), const),
                  pl.BlockSpec((A_HEADS * A_V, KV_LORA), const)],
        out_specs=[pl.BlockSpec((None, tm, w4), row), pl.BlockSpec((None, tm, w4), row),
                   pl.BlockSpec((None, A_HEADS * A_V, tm), lambda b, i: (b, 0, i))],
        out_shape=[jax.ShapeDtypeStruct((bsz, seq, w4), BF16), jax.ShapeDtypeStruct((bsz, seq, w4), BF16),
                   jax.ShapeDtypeStruct((bsz, A_HEADS * A_V, seq), BF16)],
        compiler_params=_params(("arbitrary", "arbitrary")),
        name="mla_prep",
    )(p32, p32, p32, p32, cos_t, sin_t, gq, gkv, wq, wqrot, wk, wv_t)


def _silu(g):
    return g * (1.0 / (1.0 + jnp.exp(-g)))


def _by_causal_width(i, seq, body):
    per = KV_CHUNK // QB
    for wb in range(seq // KV_CHUNK):
        pl.when(i // per == wb)(functools.partial(body, (wb + 1) * KV_CHUNK))


def _mask_tail_t(mask_ref, i, width):
    first = (width - KV_CHUNK) // QB
    return jnp.concatenate([mask_ref[jnp.clip(i - j, -1, 1) + 1] for j in range(first, width // QB)], axis=0)


def _col_reduce(x, op):
    rows, lanes = x.shape
    part = op(x.reshape(rows // COL_ACC_ROWS, COL_ACC_ROWS, lanes), axis=0)
    return op(part, axis=0, keepdims=True)


def _bias_col(bias_ref, h, i, width):
    return jnp.concatenate([bias_ref[h, jnp.maximum(i - j, -1) + 1] for j in range(width // QB)], axis=0)


def _lane_mask(width, seg, dtype):
    lane = lax.broadcasted_iota(jnp.int32, (1, LANES), 1)
    return jnp.where((lane >= seg * width) & (lane < (seg + 1) * width), 1.0, 0.0).astype(dtype)


def _mla_attn_kernel(q_ref, k_ref, vt_ref, g_ref, mask_ref, o_ref):
    i = pl.program_id(1)
    seq = k_ref.shape[0]

    def body(width):
        head_w = width - KV_CHUNK
        zero = jnp.zeros((QB, LANES), BF16)
        scores = []
        for g in range(A_HEADS // 2):
            q0 = q_ref[:, 2 * g * LANES:(2 * g + 1) * LANES]
            q1 = q_ref[:, (2 * g + 1) * LANES:(2 * g + 2) * LANES]
            qbd = jnp.concatenate([jnp.concatenate([q0, zero], axis=1),
                                   jnp.concatenate([zero, q1], axis=1)], axis=0)
            scores.append(_dot_nt(k_ref[:width, 2 * g * LANES:(2 * g + 2) * LANES], qbd))
        tail = _mask_tail_t(mask_ref, i, width)
        outs = []
        for g in range(A_HEADS // 2):
            es, ls = [], []
            for hh in range(2):
                s = scores[g][:, hh * QB:(hh + 1) * QB]
                s = jnp.concatenate([s[:head_w], s[head_w:] + tail], axis=0) if head_w else s + tail
                e = jnp.exp2(s - _col_reduce(s, jnp.max))
                ls.append(_col_reduce(e, jnp.sum))
                es.append(e.astype(BF16))
            ot = _dot(vt_ref[g * LANES:(g + 1) * LANES, :width], jnp.concatenate(es, axis=1))
            outs.append(ot[:A_V, :QB] / ls[0])
            outs.append(ot[A_V:, QB:] / ls[1])
        y_t = jnp.concatenate(outs, axis=0)
        y = jnp.concatenate([y_t[:LANES].T, y_t[LANES:].T], axis=1)
        o_ref[...] = (y * _silu(g_ref[...])).astype(BF16)

    _by_causal_width(i, seq, body)


def _mla_attn(q, k, v_t, p32, mask_tiles):
    bsz, seq, w4 = q.shape
    return pl.pallas_call(
        _mla_attn_kernel,
        grid=(bsz, seq // QB),
        in_specs=[pl.BlockSpec((None, QB, w4), lambda b, i: (b, i, 0)),
                  pl.BlockSpec((None, seq, w4), lambda b, i: (b, 0, 0)),
                  pl.BlockSpec((None, A_HEADS * A_V, seq), lambda b, i: (b, 0, 0)),
                  pl.BlockSpec((None, QB, 256), lambda b, i: (b, i, P_AG // 256)),
                  pl.BlockSpec((3, QB, QB), lambda b, i: (0, 0, 0))],
        out_specs=pl.BlockSpec((None, QB, BRANCH_WIDTH), lambda b, i: (b, i, 0)),
        out_shape=jax.ShapeDtypeStruct((bsz, seq, BRANCH_WIDTH), BF16),
        compiler_params=_params(("arbitrary", "arbitrary")),
        name="mla_attn",
    )(q, k, v_t, p32, mask_tiles)


def _transpose_values(v_ref, vt_ref):
    for j in range(v_ref.shape[0] // QB):
        for g in range(v_ref.shape[1] // LANES):
            tile = v_ref[j * QB:(j + 1) * QB, g * LANES:(g + 1) * LANES]
            vt_ref[g * LANES:(g + 1) * LANES, j * QB:(j + 1) * QB] = tile.T.astype(BF16)


def _diff_attn_kernel(q_ref, k_ref, v_ref, g_ref, bias_ref, lam_ref, subln_ref, o_ref, vt_ref, *, lambda_init):
    i = pl.program_id(1)
    seq = k_ref.shape[0]

    @pl.when(i == 0)
    def _():
        _transpose_values(v_ref, vt_ref)

    def body(width):
        lp = lam_ref[...]
        lam = (jnp.exp(jnp.sum(lp[0:1] * lp[1:2], axis=-1, keepdims=True))
               - jnp.exp(jnp.sum(lp[2:3] * lp[3:4], axis=-1, keepdims=True)) + lambda_init)
        scores = []
        for h in range(4):
            sl = slice((h // 2) * LANES, (h // 2 + 1) * LANES)
            qg = q_ref[:, sl] * (D_QK ** -0.5 * LOG2E)
            qcat = jnp.concatenate([(qg * _lane_mask(D_QK, 2 * (h % 2) + mm, F32)).astype(BF16)
                                    for mm in range(2)], axis=0)
            scores.append(_dot_nt(k_ref[:width, sl], qcat))
        outs = []
        for h in range(4):
            bias = _bias_col(bias_ref, h, i, width)
            es, ls = [], []
            for mm in range(2):
                s = scores[h][:, mm * QB:(mm + 1) * QB] + bias
                e = jnp.exp2(s - _col_reduce(s, jnp.max))
                ls.append(_col_reduce(e, jnp.sum))
                es.append(e.astype(BF16))
            ot = _dot(vt_ref[h * HEAD_DIM:(h + 1) * HEAD_DIM, :width], jnp.concatenate(es, axis=1))
            a = ot[:, :QB] / ls[0] - lam * (ot[:, QB:] / ls[1])
            ms = jnp.sum(a * a, axis=0, keepdims=True) * (1.0 / HEAD_DIM)
            outs.append(a * lax.rsqrt(ms + EPS))
        y_t = jnp.concatenate(outs, axis=0)
        y = jnp.concatenate([y_t[:LANES].T, y_t[LANES:].T], axis=1)
        o_ref[...] = (y * (subln_ref[...] * (1.0 - lambda_init)) * _silu(g_ref[...])).astype(BF16)

    _by_causal_width(i, seq, body)


def _diff_attn(p32, p16, bias, lam_params, subln, lambda_init):
    bsz, seq, _ = p32.shape
    nq = seq // QB
    return pl.pallas_call(
        functools.partial(_diff_attn_kernel, lambda_init=lambda_init),
        grid=(bsz, nq),
        in_specs=[pl.BlockSpec((None, QB, 256), lambda b, i: (b, i, P_DQ // 256)),
                  pl.BlockSpec((None, seq, 256), lambda b, i: (b, 0, P_DK // 256)),
                  pl.BlockSpec((None, seq, 256), lambda b, i: (b, 0, P_DV // 256)),
                  pl.BlockSpec((None, QB, 256), lambda b, i: (b, i, P_DG // 256)),
                  pl.BlockSpec((4, nq + 1, QB, QB), lambda b, i: (0, 0, 0, 0)),
                  pl.BlockSpec((4, D_QK), lambda b, i: (0, 0)),
                  pl.BlockSpec((1, BRANCH_WIDTH), lambda b, i: (0, 0))],
        out_specs=pl.BlockSpec((None, QB, BRANCH_WIDTH), lambda b, i: (b, i, 0)),
        out_shape=jax.ShapeDtypeStruct((bsz, seq, BRANCH_WIDTH), BF16),
        scratch_shapes=[pltpu.VMEM((BRANCH_WIDTH, seq), BF16)],
        compiler_params=_params(("arbitrary", "arbitrary")),
        name="diff_attn",
    )(p32, p16, p32, p32, bias, lam_params, subln)


def _sortable_to_float(key):
    return pltpu.bitcast(jnp.where(key < 0, key ^ jnp.int32(0x7FFFFFFF), key), F32)


def _kth_largest(score_ref, width, k_top):
    def count_ge(key):
        thr = _sortable_to_float(key)
        return _col_reduce(jnp.where(score_ref[:width, :] >= thr, 1.0, 0.0), jnp.sum)

    int_min = jnp.full((1, QB), -2 ** 31, jnp.int32)
    zero = jnp.zeros((1, QB), jnp.int32)
    t = jnp.where(count_ge(zero) >= k_top, zero, int_min)
    for bit in range(30, -1, -1):
        cand = t + jnp.int32(1 << bit)
        t = jnp.where(count_ge(cand) >= k_top, cand, t)
    return _sortable_to_float(t)


def _first_ties(eq, need, row):
    eqf = jnp.where(eq, 1.0, 0.0)
    nbits = int(eq.shape[0]).bit_length()

    def body(it, j):
        cand = j + (jnp.int32(1) << (nbits - 1 - it))
        cnt = _col_reduce(jnp.where(row < cand, eqf, 0.0), jnp.sum)
        return jnp.where(cnt <= need, cand, j)

    j = lax.fori_loop(0, nbits, body, jnp.zeros((1, eq.shape[1]), jnp.int32))
    return eq & (row < j)


def _sparse_attn_kernel(q_ref, k_ref, v_ref, qi_ref, ki_ref, wi_ref, g_ref, bias_ref, o_ref,
                        score_ref, neg_ref, vt_ref, *, k_top):
    i = pl.program_id(1)
    seq = k_ref.shape[0]

    @pl.when(i == 0)
    def _():
        _transpose_values(v_ref, vt_ref)

    def body(width):
        head_w = width - KV_CHUNK
        ki = ki_ref[:width, :]
        w_t = wi_ref[...].T * (IDX_DIM ** -0.5 * IDX_HEADS ** -0.5)
        score = None
        for h in range(0, IDX_HEADS, 2):
            qg = qi_ref[:, (h // 4) * LANES:(h // 4 + 1) * LANES]
            qcat = jnp.concatenate([qg * _lane_mask(IDX_DIM, h % 4, BF16),
                                    qg * _lane_mask(IDX_DIM, h % 4 + 1, BF16)], axis=0)
            logit = _dot_nt(ki, qcat)
            term = (jnp.maximum(logit[:, :QB], 0.0) * w_t[h:h + 1]
                    + jnp.maximum(logit[:, QB:], 0.0) * w_t[h + 1:h + 2])
            score = term if score is None else score + term
        s_idx = head_w + lax.broadcasted_iota(jnp.int32, (KV_CHUNK, QB), 0)
        tail_ok = s_idx <= i * QB + lax.broadcasted_iota(jnp.int32, (KV_CHUNK, QB), 1)
        if head_w:
            score_ref[:head_w, :] = score[:head_w]
        score_ref[head_w:width, :] = jnp.where(tail_ok, score[head_w:], NEG_INF)
        s2s = []
        for g in range(2):
            sl = slice(g * LANES, (g + 1) * LANES)
            qg = q_ref[:, sl] * (HEAD_DIM ** -0.5 * LOG2E)
            qcat = jnp.concatenate([(qg * _lane_mask(HEAD_DIM, hh, F32)).astype(BF16) for hh in range(2)], axis=0)
            s2s.append(_dot_nt(k_ref[:width, sl], qcat))
        thr = _kth_largest(score_ref, width, k_top)
        keep_tail = (score_ref[head_w:width, :] >= thr) & tail_ok
        cnt = _col_reduce(jnp.where(keep_tail, 1.0, 0.0), jnp.sum)
        neg_ref[head_w:width, :] = jnp.where(keep_tail, 0.0, NEG_INF)
        if head_w:
            keep_head = score_ref[:head_w, :] >= thr
            cnt = cnt + _col_reduce(jnp.where(keep_head, 1.0, 0.0), jnp.sum)
            neg_ref[:head_w, :] = jnp.where(keep_head, 0.0, NEG_INF)

        @pl.when(jnp.max(cnt) > k_top)
        def _():
            x = score_ref[:width, :]
            gt = x > thr
            need = k_top - _col_reduce(jnp.where(gt, 1.0, 0.0), jnp.sum)
            row = lax.broadcasted_iota(jnp.int32, (width, QB), 0)
            neg_ref[:width, :] = jnp.where(gt | _first_ties(x == thr, need, row), 0.0, NEG_INF)

        outs = []
        for g in range(2):
            sl = slice(g * LANES, (g + 1) * LANES)
            s2 = s2s[g]
            es, ls = [], []
            for hh in range(2):
                s = s2[:, hh * QB:(hh + 1) * QB] + (_bias_col(bias_ref, 2 * g + hh, i, width) + neg_ref[:width, :])
                e = jnp.exp2(s - _col_reduce(s, jnp.max))
                ls.append(_col_reduce(e, jnp.sum))
                es.append(e.astype(BF16))
            ot = _dot(vt_ref[sl, :width], jnp.concatenate(es, axis=1))
            outs.append(ot[:HEAD_DIM, :QB] / ls[0])
            outs.append(ot[HEAD_DIM:, QB:] / ls[1])
        y_t = jnp.concatenate(outs, axis=0)
        y = jnp.concatenate([y_t[:LANES].T, y_t[LANES:].T], axis=1)
        o_ref[...] = (y * _silu(g_ref[...])).astype(BF16)

    _by_causal_width(i, seq, body)


def _sparse_attn(p32, p16, bias, k_top):
    bsz, seq, _ = p32.shape
    nq = seq // QB
    return pl.pallas_call(
        functools.partial(_sparse_attn_kernel, k_top=k_top),
        grid=(bsz, nq),
        in_specs=[pl.BlockSpec((None, QB, 256), lambda b, i: (b, i, P_CQ2 // 256)),
                  pl.BlockSpec((None, seq, 256), lambda b, i: (b, 0, P_CK // 256)),
                  pl.BlockSpec((None, seq, 256), lambda b, i: (b, 0, P_CV // 256)),
                  pl.BlockSpec((None, QB, 256), lambda b, i: (b, i, P_CQI // 256)),
                  pl.BlockSpec((None, seq, LANES), lambda b, i: (b, 0, P_CKI // LANES)),
                  pl.BlockSpec((None, QB, LANES), lambda b, i: (b, i, P_CWI // LANES)),
                  pl.BlockSpec((None, QB, 256), lambda b, i: (b, i, P_CG // 256)),
                  pl.BlockSpec((4, nq + 1, QB, QB), lambda b, i: (0, 0, 0, 0))],
        out_specs=pl.BlockSpec((None, QB, BRANCH_WIDTH), lambda b, i: (b, i, 0)),
        out_shape=jax.ShapeDtypeStruct((bsz, seq, BRANCH_WIDTH), BF16),
        scratch_shapes=[pltpu.VMEM((seq, QB), F32), pltpu.VMEM((seq, QB), F32),
                        pltpu.VMEM((BRANCH_WIDTH, seq), BF16)],
        compiler_params=_params(("arbitrary", "arbitrary")),
        name="sparse_attn",
    )(p32, p16, p32, p16, p16, p32, p32, bias)


def _dilated_kernel(q0_ref, q1_ref, k0_ref, k1_ref, v0_ref, v1_ref, g_ref, bias_ref, o_ref,
                    m_ref, l_ref, acc_ref):
    seq = q0_ref.shape[0]
    q_refs, k_refs, v_refs = (q0_ref, q1_ref), (k0_ref, k1_ref), (v0_ref, v1_ref)
    first = lax.broadcasted_iota(jnp.int32, (QB, LANES), 1) < HEAD_DIM
    qmasks = [_lane_mask(HEAD_DIM, hh, F32) * (HEAD_DIM ** -0.5 * LOG2E) for hh in range(2)]
    for p, (window, d) in enumerate(DILATED_PATTERNS):
        nb = seq // d // QB
        with_prev = nb > 1

        def group(n0, carry, p=p, d=d, with_prev=with_prev):
            work = []
            for u in range(DIL_UNROLL):
                n = n0 * DIL_UNROLL + u
                r = n % d
                j = n // d
                start = r + d * QB * j
                cur_rows = pl.ds(start, QB, stride=d)
                prev_rows = pl.ds(jnp.maximum(start - d * QB, r), QB, stride=d)
                for g in range(2):
                    q = q_refs[g][cur_rows, :]
                    kcat = k_refs[g][cur_rows, :]
                    vcat = v_refs[g][cur_rows, :]
                    if with_prev:
                        kcat = jnp.concatenate([k_refs[g][prev_rows, :], kcat], axis=0)
                        vcat = jnp.concatenate([v_refs[g][prev_rows, :], vcat], axis=0)
                    kcat = kcat.astype(BF16)
                    scores = []
                    for hh in range(2):
                        if with_prev:
                            bias = bias_ref[2 * g + hh, 2 * p + jnp.minimum(j, 1)]
                        else:
                            bias = bias_ref[2 * g + hh, 2 * p, :, QB:]
                        scores.append(_dot_nt((q * qmasks[hh]).astype(BF16), kcat) + bias)
                    work.append((cur_rows, g, vcat.astype(BF16), scores))
            for cur_rows, g, vcat, scores in work:
                ms, ls, accs = [], [], []
                for s in scores:
                    m = jnp.max(s, axis=-1, keepdims=True)
                    e = jnp.exp2(s - m)
                    ms.append(m)
                    ls.append(jnp.sum(e, axis=-1, keepdims=True))
                    accs.append(_dot(e.astype(BF16), vcat))
                m_new = jnp.where(first, ms[0], ms[1])
                l_new = jnp.where(first, ls[0], ls[1])
                a_new = jnp.where(first, accs[0], accs[1])
                if p == 0:
                    m_ref[g, cur_rows, :] = m_new
                    l_ref[g, cur_rows, :] = l_new
                    acc_ref[g, cur_rows, :] = a_new
                else:
                    m_old = m_ref[g, cur_rows, :]
                    m_tot = jnp.maximum(m_old, m_new)
                    w_old = jnp.exp2(m_old - m_tot)
                    w_new = jnp.exp2(m_new - m_tot)
                    m_ref[g, cur_rows, :] = m_tot
                    l_ref[g, cur_rows, :] = w_old * l_ref[g, cur_rows, :] + w_new * l_new
                    acc_ref[g, cur_rows, :] = w_old * acc_ref[g, cur_rows, :] + w_new * a_new
            return carry

        lax.fori_loop(0, d * nb // DIL_UNROLL, group, 0)
    gate = _silu(g_ref[...])
    for g in range(2):
        sl = slice(g * LANES, (g + 1) * LANES)
        o_ref[:, sl] = (acc_ref[g] / l_ref[g] * gate[:, sl]).astype(BF16)


def _dilated_attn(p32, bias):
    bsz, seq, _ = p32.shape

    def slab(off):
        return pl.BlockSpec((None, seq, LANES), lambda b, off=off: (b, 0, off // LANES))

    return pl.pallas_call(
        _dilated_kernel,
        grid=(bsz,),
        in_specs=[slab(P_BQ), slab(P_BQ + LANES), slab(P_BK), slab(P_BK + LANES),
                  slab(P_BV), slab(P_BV + LANES),
                  pl.BlockSpec((None, seq, 256), lambda b: (b, 0, P_BG // 256)),
                  pl.BlockSpec((4, 2 * len(DILATED_PATTERNS), QB, 2 * QB), lambda b: (0, 0, 0, 0))],
        out_specs=pl.BlockSpec((None, seq, BRANCH_WIDTH), lambda b: (b, 0, 0)),
        out_shape=jax.ShapeDtypeStruct((bsz, seq, BRANCH_WIDTH), BF16),
        scratch_shapes=[pltpu.VMEM((2, seq, LANES), F32)] * 3,
        compiler_params=_params(("arbitrary",)),
        name="dilated_attn",
    )(p32, p32, p32, p32, p32, p32, p32, bias)


def _out_proj_kernel(ya_ref, yb_ref, yc_ref, yd_ref, w_ref, h_ref, g_ref, o_ref):
    y = None
    for n, ref in enumerate((ya_ref, yb_ref, yc_ref, yd_ref)):
        t = _dot(ref[...], w_ref[n * BRANCH_WIDTH:(n + 1) * BRANCH_WIDTH, :])
        y = t if y is None else y + t
    ms = jnp.mean(y * y, axis=-1, keepdims=True)
    o_ref[...] = h_ref[...] + y * lax.rsqrt(ms + EPS) * g_ref[...]


def _out_proj(ya, yb, yc, yd, w, h, g, tm=512):
    bsz, seq, _ = h.shape
    yspec = pl.BlockSpec((None, tm, BRANCH_WIDTH), lambda b, i: (b, i, 0))
    return pl.pallas_call(
        _out_proj_kernel,
        grid=(bsz, seq // tm),
        in_specs=[yspec, yspec, yspec, yspec,
                  pl.BlockSpec((4 * BRANCH_WIDTH, D_MODEL), lambda b, i: (0, 0)),
                  pl.BlockSpec((None, tm, D_MODEL), lambda b, i: (b, i, 0)),
                  pl.BlockSpec((1, D_MODEL), lambda b, i: (0, 0))],
        out_specs=pl.BlockSpec((None, tm, D_MODEL), lambda b, i: (b, i, 0)),
        out_shape=jax.ShapeDtypeStruct(h.shape, F32),
        compiler_params=_params(("arbitrary", "arbitrary")),
        name="out_proj",
    )(ya, yb, yc, yd, w, h, g)


def kernel(x, w_in, w_out, norm_pre, norm_post, mla_q_norm, mla_kv_norm, mla_w_uq, mla_w_ukv,
           diff_lambda, diff_subln, rel_bias):
    bsz, seq, _ = x.shape
    depth = w_in.shape[0]
    nq = seq // QB
    k_top = min(IDX_TOPK_MAX, seq // 4)
    cos_t, sin_t = _rope_tables(seq)
    bias_b = _bias_expand(rel_bias, jnp.asarray(_bucket_tiles_dilated()), 0, 4, LOG2E)
    causal_buckets_t = jnp.asarray(np.swapaxes(_bucket_tiles_causal(nq), 1, 2))
    bias_c = _bias_expand(rel_bias, causal_buckets_t, 4, 4, LOG2E)
    bias_d = _bias_expand(rel_bias, causal_buckets_t, 8, 4, LOG2E)
    mask_tiles = jnp.asarray(_mask_tiles())
    h = x
    for layer in range(depth):
        w_arr = _arrange_w_in(w_in[layer]).astype(BF16)
        wq, wqrot, wk, wv = _arrange_mla(mla_w_uq[layer], mla_w_ukv[layer])
        gq = jnp.concatenate([mla_q_norm[layer], jnp.ones((256 - Q_LORA,), F32)])[None, :]
        gkv = mla_kv_norm[layer][None, :]
        p32, p16 = _in_proj(h, norm_pre[layer][None, :], w_arr)
        qa, ka, va = _mla_prep(p32, cos_t, sin_t, gq, gkv, wq, wqrot, wk, wv)
        y_a = _mla_attn(qa, ka, va, p32, mask_tiles)
        y_b = _dilated_attn(p32, bias_b)
        y_c = _sparse_attn(p32, p16, bias_c, k_top)
        lambda_init = 0.8 - 0.6 * math.exp(-0.3 * layer)
        subln = jnp.tile(diff_subln[layer], BRANCH_WIDTH // HEAD_DIM)[None, :]
        y_d = _diff_attn(p32, p16, bias_d, diff_lambda[layer], subln, lambda_init)
        h = _out_proj(y_a, y_b, y_c, y_d, w_out[layer].astype(BF16), h, norm_post[layer][None, :])
    return h
```

```python
import functools
import math

import jax
import jax.numpy as jnp
import numpy as np
from jax import lax
from jax.experimental import pallas as pl
from jax.experimental.pallas import tpu as pltpu

F32 = jnp.float32
BF16 = jnp.bfloat16

D_MODEL = 1024
A_HEADS, A_NOPE, A_ROPE, A_V = 4, 64, 32, 64
Q_LORA, KV_LORA = 192, 128
ROPE_THETA = 10000.0
HEAD_DIM = 64
DILATED_PATTERNS = ((128, 1), (512, 4), (2048, 16))
IDX_HEADS, IDX_DIM, IDX_TOPK_MAX = 8, 32, 256
D_QK = 32
BRANCH_WIDTH = 256
NUM_BUCKETS, MAX_DISTANCE = 32, 2048
NEG_INF = -1e30
EPS = 1e-6
LOG2E = math.log2(math.e)
KV_CHUNK = 512
COL_ACC_ROWS = 64
DIL_UNROLL = 4
LANES = 128
QB = 128
SQ = 2 * QB
VMEM_LIMIT = 56 * 1024 * 1024

_SPLIT = (Q_LORA, KV_LORA, A_ROPE, 256, 256, 256, 256, 256, 256, 256, 256, IDX_HEADS * IDX_DIM, IDX_DIM,
          IDX_HEADS, 256, 256, 256, 256, 256)
_OFF = np.concatenate([[0], np.cumsum(_SPLIT)]).tolist()
(_A_CQ, _A_CKV, _A_KR, _A_G, _B_Q, _B_K, _B_V, _B_G, _C_Q, _C_K, _C_V, _C_QI, _C_KI, _C_WI, _C_G,
 _D_Q, _D_K, _D_V, _D_G) = range(19)

P_CQ, P_AG, P_CKV, P_KR, P_KRROT, P_CKI = 0, 256, 512, 640, 768, 896
P_BQ, P_BK, P_BV, P_BG = 1024, 1280, 1536, 1792
P_CQ2, P_CK, P_CV, P_CQI, P_CG = 2048, 2304, 2560, 2816, 3072
P_DQ, P_DK, P_DV, P_DG = 3328, 3584, 3840, 4096
P_CWI = 4352
NCOL = 4480


def _dot(a, b):
    return jnp.dot(a, b, preferred_element_type=F32)


def _dot_nt(a, b):
    return lax.dot_general(a, b, (((1,), (1,)), ((), ())), preferred_element_type=F32)


def _params(sem):
    return pltpu.CompilerParams(dimension_semantics=sem, vmem_limit_bytes=VMEM_LIMIT)


def _rot_cols(w):
    half = w.shape[-1] // 2
    return jnp.concatenate([-w[..., half:], w[..., :half]], axis=-1)


def _arrange_w_in(w):
    def seg(i):
        return w[:, _OFF[i]:_OFF[i + 1]]

    def z(n):
        return jnp.zeros((w.shape[0], n), w.dtype)

    kr = seg(_A_KR)
    cols = [
        seg(_A_CQ), z(64),
        seg(_A_G),
        seg(_A_CKV),
        z(64), kr, z(32),
        z(64), _rot_cols(kr), z(32),
        seg(_C_KI), seg(_C_KI), seg(_C_KI), seg(_C_KI),
        seg(_B_Q), seg(_B_K), seg(_B_V), seg(_B_G),
        seg(_C_Q), seg(_C_K), seg(_C_V), seg(_C_QI), seg(_C_G),
        seg(_D_Q), seg(_D_K), seg(_D_V), seg(_D_G),
        seg(_C_WI), z(LANES - IDX_HEADS),
    ]
    out = jnp.concatenate(cols, axis=1)
    assert out.shape[1] == NCOL
    return out


def _arrange_mla(w_uq, w_ukv):
    wq = w_uq.reshape(Q_LORA, A_HEADS, A_NOPE + A_ROPE)
    nope, rope = wq[..., :A_NOPE], wq[..., A_NOPE:]
    zq = jnp.zeros((Q_LORA, A_HEADS, LANES - A_NOPE - A_ROPE), w_uq.dtype)
    wq_main = jnp.concatenate([nope, rope, zq], axis=-1).reshape(Q_LORA, A_HEADS * LANES)
    wq_rot = jnp.concatenate([jnp.zeros_like(nope), _rot_cols(rope), zq], axis=-1).reshape(Q_LORA, A_HEADS * LANES)
    pad = jnp.zeros((256 - Q_LORA, A_HEADS * LANES), w_uq.dtype)
    wq_main = jnp.concatenate([wq_main, pad], axis=0)
    wq_rot = jnp.concatenate([wq_rot, pad], axis=0)
    wkv = w_ukv.reshape(KV_LORA, A_HEADS, A_NOPE + A_V)
    knope, v = wkv[..., :A_NOPE], wkv[..., A_NOPE:]
    wk = jnp.concatenate([knope, jnp.zeros_like(knope)], axis=-1).reshape(KV_LORA, A_HEADS * LANES)
    wv_t = v.reshape(KV_LORA, A_HEADS * A_V).T
    return wq_main.astype(BF16), wq_rot.astype(BF16), wk.astype(BF16), wv_t.astype(BF16)


def _rope_tables(seq):
    inv = ROPE_THETA ** (-jnp.arange(0, A_ROPE, 2, dtype=F32) / A_ROPE)
    ang = jnp.arange(seq, dtype=F32)[:, None] * inv[None, :]
    cos, sin = jnp.cos(ang), jnp.sin(ang)
    one = jnp.ones((seq, A_NOPE), F32)
    zero = jnp.zeros((seq, LANES - A_NOPE - A_ROPE), F32)
    cos_t = jnp.concatenate([one, cos, cos, zero], axis=1)
    sin_t = jnp.concatenate([jnp.zeros_like(one), sin, sin, zero], axis=1)
    return cos_t, sin_t


def _t5_bucket_np(rel):
    n = np.maximum(rel, 0)
    max_exact = NUM_BUCKETS // 2
    nf = np.maximum(n, max_exact).astype(np.float64)
    large = max_exact + (np.log(nf / max_exact) / math.log(MAX_DISTANCE / max_exact)
                         * (NUM_BUCKETS - max_exact)).astype(np.int32)
    large = np.minimum(large, NUM_BUCKETS - 1)
    return np.where(n < max_exact, n, large).astype(np.int32)


MASKED_BUCKET = NUM_BUCKETS


def _bucket_tiles_causal(nq):
    q = np.arange(QB)[:, None]
    k = np.arange(QB)[None, :]
    tiles = [np.full((QB, QB), MASKED_BUCKET, np.int32)]
    for d in range(nq):
        rel = QB * d + q - k
        tiles.append(np.where(rel >= 0, _t5_bucket_np(rel), MASKED_BUCKET).astype(np.int32))
    return np.stack(tiles)


def _mask_tiles():
    k = np.arange(QB)[:, None]
    q = np.arange(QB)[None, :]
    diag = np.where(k <= q, 0.0, NEG_INF)
    return np.stack([np.full((QB, QB), NEG_INF), diag, np.zeros((QB, QB))]).astype(np.float32)


def _bucket_tiles_dilated():
    q = np.arange(QB)[:, None]
    k = np.arange(2 * QB)[None, :]
    rel = q + QB - k
    tiles = []
    for (window, d) in DILATED_PATTERNS:
        in_band = (rel >= 0) & (rel <= window // d)
        for has_prev in (False, True):
            ok = in_band & (has_prev | (k >= QB))
            tiles.append(np.where(ok, _t5_bucket_np(rel * d), MASKED_BUCKET).astype(np.int32))
    return np.stack(tiles)


def _bias_expand_kernel(table_ref, bucket_ref, out_ref, *, head0, scale):
    h = pl.program_id(0) + head0
    bk = bucket_ref[...]
    acc = jnp.where(bk == MASKED_BUCKET, NEG_INF, 0.0)
    for b in range(NUM_BUCKETS):
        acc = jnp.where(bk == b, table_ref[b, h] * scale, acc)
    out_ref[...] = acc


def _bias_expand(table, buckets, head0, nheads, scale=1.0):
    n, r, c = buckets.shape
    return pl.pallas_call(
        functools.partial(_bias_expand_kernel, head0=head0, scale=scale),
        grid=(nheads,),
        in_specs=[pl.BlockSpec(memory_space=pltpu.SMEM),
                  pl.BlockSpec((n, r, c), lambda h: (0, 0, 0))],
        out_specs=pl.BlockSpec((None, n, r, c), lambda h: (h, 0, 0, 0)),
        out_shape=jax.ShapeDtypeStruct((nheads, n, r, c), F32),
        compiler_params=_params(("arbitrary",)),
        name="bias_expand",
    )(table, buckets)


def _in_proj_kernel(x_ref, g_ref, w_ref, o32_ref, o16_ref):
    x = x_ref[...]
    ms = jnp.mean(x * x, axis=-1, keepdims=True)
    xn = x * lax.rsqrt(ms + EPS) * g_ref[...]
    p = _dot(xn.astype(BF16), w_ref[...])
    o32_ref[...] = p
    o16_ref[...] = p.astype(BF16)


def _in_proj(h, g, w, tm=256):
    bsz, seq, _ = h.shape
    return pl.pallas_call(
        _in_proj_kernel,
        grid=(bsz, seq // tm),
        in_specs=[pl.BlockSpec((None, tm, D_MODEL), lambda b, i: (b, i, 0)),
                  pl.BlockSpec((1, D_MODEL), lambda b, i: (0, 0)),
                  pl.BlockSpec((D_MODEL, NCOL), lambda b, i: (0, 0))],
        out_specs=[pl.BlockSpec((None, tm, NCOL), lambda b, i: (b, i, 0)),
                   pl.BlockSpec((None, tm, NCOL), lambda b, i: (b, i, 0))],
        out_shape=[jax.ShapeDtypeStruct((bsz, seq, NCOL), F32),
                   jax.ShapeDtypeStruct((bsz, seq, NCOL), BF16)],
        compiler_params=_params(("arbitrary", "arbitrary")),
        name="in_proj",
    )(h, g, w)


def _mla_prep_kernel(cq_ref, ckv_ref, kr_ref, krrot_ref, cos_ref, sin_ref, gq_ref, gkv_ref,
                     wq_ref, wqrot_ref, wk_ref, wv_ref, q_ref, k_ref, v_ref):
    cos = cos_ref[...]
    sin = sin_ref[...]
    cos4 = jnp.concatenate([cos] * A_HEADS, axis=1)
    sin4 = jnp.concatenate([sin] * A_HEADS, axis=1)
    cq = cq_ref[...]
    ms = jnp.sum(cq * cq, axis=-1, keepdims=True) * (1.0 / Q_LORA)
    nq = (cq * lax.rsqrt(ms + EPS) * gq_ref[...]).astype(BF16)
    q = _dot(nq, wq_ref[...]) * cos4 + _dot(nq, wqrot_ref[...]) * sin4
    q_ref[...] = (q * ((A_NOPE + A_ROPE) ** -0.5 * LOG2E)).astype(BF16)
    ckv = ckv_ref[...]
    ms = jnp.mean(ckv * ckv, axis=-1, keepdims=True)
    nkv = (ckv * lax.rsqrt(ms + EPS) * gkv_ref[...]).astype(BF16)
    kr = kr_ref[...] * cos + krrot_ref[...] * sin
    k = _dot(nkv, wk_ref[...]) + jnp.concatenate([kr] * A_HEADS, axis=1)
    k_ref[...] = k.astype(BF16)
    v_ref[...] = _dot_nt(wv_ref[...], nkv).astype(BF16)


def _mla_prep(p32, cos_t, sin_t, gq, gkv, wq, wqrot, wk, wv_t, tm=512):
    bsz, seq, _ = p32.shape
    w4 = A_HEADS * LANES
    row = lambda b, i: (b, i, 0)
    const = lambda b, i: (0, 0)
    return pl.pallas_call(
        _mla_prep_kernel,
        grid=(bsz, seq // tm),
        in_specs=[pl.BlockSpec((None, tm, 256), lambda b, i: (b, i, P_CQ // 256)),
                  pl.BlockSpec((None, tm, LANES), lambda b, i: (b, i, P_CKV // LANES)),
                  pl.BlockSpec((None, tm, LANES), lambda b, i: (b, i, P_KR // LANES)),
                  pl.BlockSpec((None, tm, LANES), lambda b, i: (b, i, P_KRROT // LANES)),
                  pl.BlockSpec((tm, LANES), lambda b, i: (i, 0)),
                  pl.BlockSpec((tm, LANES), lambda b, i: (i, 0)),
                  pl.BlockSpec((1, 256), const),
                  pl.BlockSpec((1, KV_LORA), const),
                  pl.BlockSpec((256, w4), const),
                  pl.BlockSpec((256, w4), const),
                  pl.BlockSpec((KV_LORA, w4), const),
                  pl.BlockSpec((A_HEADS * A_V, KV_LORA), const)],
        out_specs=[pl.BlockSpec((None, tm, w4), row), pl.BlockSpec((None, tm, w4), row),
                   pl.BlockSpec((None, A_HEADS * A_V, tm), lambda b, i: (b, 0, i))],
        out_shape=[jax.ShapeDtypeStruct((bsz, seq, w4), BF16), jax.ShapeDtypeStruct((bsz, seq, w4), BF16),
                   jax.ShapeDtypeStruct((bsz, A_HEADS * A_V, seq), BF16)],
        compiler_params=_params(("arbitrary", "arbitrary")),
        name="mla_prep",
    )(p32, p32, p32, p32, cos_t, sin_t, gq, gkv, wq, wqrot, wk, wv_t)


def _silu(g):
    return g * (1.0 / (1.0 + jnp.exp(-g)))


def _by_causal_width(i, seq, body, qrows=QB):
    per = KV_CHUNK // qrows
    for wb in range(seq // KV_CHUNK):
        pl.when(i // per == wb)(functools.partial(body, (wb + 1) * KV_CHUNK))


def _mask_tail_t(mask_ref, i, width):
    first = (width - KV_CHUNK) // QB
    return jnp.concatenate([mask_ref[jnp.clip(i - j, -1, 1) + 1] for j in range(first, width // QB)], axis=0)


def _col_reduce(x, op):
    rows, lanes = x.shape
    part = op(x.reshape(rows // COL_ACC_ROWS, COL_ACC_ROWS, lanes), axis=0)
    return op(part, axis=0, keepdims=True)


def _bias_col(bias_ref, h, i, width):
    return jnp.concatenate([bias_ref[h, jnp.maximum(i - j, -1) + 1] for j in range(width // QB)], axis=0)


def _lane_mask(width, seg, dtype):
    lane = lax.broadcasted_iota(jnp.int32, (1, LANES), 1)
    return jnp.where((lane >= seg * width) & (lane < (seg + 1) * width), 1.0, 0.0).astype(dtype)


def _mla_attn_kernel(q_ref, k_ref, vt_ref, g_ref, mask_ref, o_ref):
    i = pl.program_id(1)
    seq = k_ref.shape[0]

    def body(width):
        head_w = width - KV_CHUNK
        zero = jnp.zeros((QB, LANES), BF16)
        scores = []
        for g in range(A_HEADS // 2):
            q0 = q_ref[:, 2 * g * LANES:(2 * g + 1) * LANES]
            q1 = q_ref[:, (2 * g + 1) * LANES:(2 * g + 2) * LANES]
            qbd = jnp.concatenate([jnp.concatenate([q0, zero], axis=1),
                                   jnp.concatenate([zero, q1], axis=1)], axis=0)
            scores.append(_dot_nt(k_ref[:width, 2 * g * LANES:(2 * g + 2) * LANES], qbd))
        tail = _mask_tail_t(mask_ref, i, width)
        outs = []
        for g in range(A_HEADS // 2):
            es, ls = [], []
            for hh in range(2):
                s = scores[g][:, hh * QB:(hh + 1) * QB]
                s = jnp.concatenate([s[:head_w], s[head_w:] + tail], axis=0) if head_w else s + tail
                e = jnp.exp2(s - _col_reduce(s, jnp.max))
                ls.append(_col_reduce(e, jnp.sum))
                es.append(e.astype(BF16))
            ot = _dot(vt_ref[g * LANES:(g + 1) * LANES, :width], jnp.concatenate(es, axis=1))
            outs.append(ot[:A_V, :QB] / ls[0])
            outs.append(ot[A_V:, QB:] / ls[1])
        y_t = jnp.concatenate(outs, axis=0)
        y = jnp.concatenate([y_t[:LANES].T, y_t[LANES:].T], axis=1)
        o_ref[...] = (y * _silu(g_ref[...])).astype(BF16)

    _by_causal_width(i, seq, body)


def _mla_attn(q, k, v_t, p32, mask_tiles):
    bsz, seq, w4 = q.shape
    return pl.pallas_call(
        _mla_attn_kernel,
        grid=(bsz, seq // QB),
        in_specs=[pl.BlockSpec((None, QB, w4), lambda b, i: (b, i, 0)),
                  pl.BlockSpec((None, seq, w4), lambda b, i: (b, 0, 0)),
                  pl.BlockSpec((None, A_HEADS * A_V, seq), lambda b, i: (b, 0, 0)),
                  pl.BlockSpec((None, QB, 256), lambda b, i: (b, i, P_AG // 256)),
                  pl.BlockSpec((3, QB, QB), lambda b, i: (0, 0, 0))],
        out_specs=pl.BlockSpec((None, QB, BRANCH_WIDTH), lambda b, i: (b, i, 0)),
        out_shape=jax.ShapeDtypeStruct((bsz, seq, BRANCH_WIDTH), BF16),
        compiler_params=_params(("arbitrary", "arbitrary")),
        name="mla_attn",
    )(q, k, v_t, p32, mask_tiles)


def _transpose_values(v_ref, vt_ref):
    for j in range(v_ref.shape[0] // QB):
        for g in range(v_ref.shape[1] // LANES):
            tile = v_ref[j * QB:(j + 1) * QB, g * LANES:(g + 1) * LANES]
            vt_ref[g * LANES:(g + 1) * LANES, j * QB:(j + 1) * QB] = tile.T.astype(BF16)


def _diff_attn_kernel(q_ref, k_ref, v_ref, g_ref, bias_ref, lam_ref, subln_ref, o_ref, vt_ref, *, lambda_init):
    i = pl.program_id(1)
    seq = k_ref.shape[0]

    @pl.when(i == 0)
    def _():
        _transpose_values(v_ref, vt_ref)

    def body(width):
        lp = lam_ref[...]
        lam = (jnp.exp(jnp.sum(lp[0:1] * lp[1:2], axis=-1, keepdims=True))
               - jnp.exp(jnp.sum(lp[2:3] * lp[3:4], axis=-1, keepdims=True)) + lambda_init)
        scores = []
        for h in range(4):
            sl = slice((h // 2) * LANES, (h // 2 + 1) * LANES)
            qg = q_ref[:, sl] * (D_QK ** -0.5 * LOG2E)
            qcat = jnp.concatenate([(qg * _lane_mask(D_QK, 2 * (h % 2) + mm, F32)).astype(BF16)
                                    for mm in range(2)], axis=0)
            scores.append(_dot_nt(k_ref[:width, sl], qcat))
        outs = []
        for h in range(4):
            bias = _bias_col(bias_ref, h, i, width)
            es, ls = [], []
            for mm in range(2):
                s = scores[h][:, mm * QB:(mm + 1) * QB] + bias
                e = jnp.exp2(s - _col_reduce(s, jnp.max))
                ls.append(_col_reduce(e, jnp.sum))
                es.append(e.astype(BF16))
            ot = _dot(vt_ref[h * HEAD_DIM:(h + 1) * HEAD_DIM, :width], jnp.concatenate(es, axis=1))
            a = ot[:, :QB] / ls[0] - lam * (ot[:, QB:] / ls[1])
            ms = jnp.sum(a * a, axis=0, keepdims=True) * (1.0 / HEAD_DIM)
            outs.append(a * lax.rsqrt(ms + EPS))
        y_t = jnp.concatenate(outs, axis=0)
        y = jnp.concatenate([y_t[:LANES].T, y_t[LANES:].T], axis=1)
        o_ref[...] = (y * (subln_ref[...] * (1.0 - lambda_init)) * _silu(g_ref[...])).astype(BF16)

    _by_causal_width(i, seq, body)


def _diff_attn(p32, p16, bias, lam_params, subln, lambda_init):
    bsz, seq, _ = p32.shape
    nq = seq // QB
    return pl.pallas_call(
        functools.partial(_diff_attn_kernel, lambda_init=lambda_init),
        grid=(bsz, nq),
        in_specs=[pl.BlockSpec((None, QB, 256), lambda b, i: (b, i, P_DQ // 256)),
                  pl.BlockSpec((None, seq, 256), lambda b, i: (b, 0, P_DK // 256)),
                  pl.BlockSpec((None, seq, 256), lambda b, i: (b, 0, P_DV // 256)),
                  pl.BlockSpec((None, QB, 256), lambda b, i: (b, i, P_DG // 256)),
                  pl.BlockSpec((4, nq + 1, QB, QB), lambda b, i: (0, 0, 0, 0)),
                  pl.BlockSpec((4, D_QK), lambda b, i: (0, 0)),
                  pl.BlockSpec((1, BRANCH_WIDTH), lambda b, i: (0, 0))],
        out_specs=pl.BlockSpec((None, QB, BRANCH_WIDTH), lambda b, i: (b, i, 0)),
        out_shape=jax.ShapeDtypeStruct((bsz, seq, BRANCH_WIDTH), BF16),
        scratch_shapes=[pltpu.VMEM((BRANCH_WIDTH, seq), BF16)],
        compiler_params=_params(("arbitrary", "arbitrary")),
        name="diff_attn",
    )(p32, p16, p32, p32, bias, lam_params, subln)


def _sortable_to_float(key):
    return pltpu.bitcast(jnp.where(key < 0, key ^ jnp.int32(0x7FFFFFFF), key), F32)


def _kth_largest(score_ref, width, k_top):
    def count_ge(key):
        thr = _sortable_to_float(key)
        return _col_reduce(jnp.where(score_ref[:width, :] >= thr, 1.0, 0.0), jnp.sum)

    int_min = jnp.full((1, score_ref.shape[1]), -2 ** 31, jnp.int32)
    zero = jnp.zeros((1, score_ref.shape[1]), jnp.int32)
    t = jnp.where(count_ge(zero) >= k_top, zero, int_min)
    for bit in range(30, -1, -1):
        cand = t + jnp.int32(1 << bit)
        t = jnp.where(count_ge(cand) >= k_top, cand, t)
    return _sortable_to_float(t)


def _first_ties(eq, need, row):
    eqf = jnp.where(eq, 1.0, 0.0)
    nbits = int(eq.shape[0]).bit_length()

    def body(it, j):
        cand = j + (jnp.int32(1) << (nbits - 1 - it))
        cnt = _col_reduce(jnp.where(row < cand, eqf, 0.0), jnp.sum)
        return jnp.where(cnt <= need, cand, j)

    j = lax.fori_loop(0, nbits, body, jnp.zeros((1, eq.shape[1]), jnp.int32))
    return eq & (row < j)


def _sparse_attn_kernel(q_ref, k_ref, v_ref, qi_ref, ki_ref, wi_ref, g_ref, bias_ref, o_ref,
                        score_ref, neg_ref, vt_ref, *, k_top):
    i = pl.program_id(1)
    seq = k_ref.shape[0]

    @pl.when(i == 0)
    def _():
        _transpose_values(v_ref, vt_ref)

    def body(width):
        head_w = width - KV_CHUNK
        ki = ki_ref[:width, :]
        w_t = jnp.concatenate([wi_ref[u * QB:(u + 1) * QB, :].T for u in range(SQ // QB)], axis=1)
        w_t = w_t * (IDX_DIM ** -0.5 * IDX_HEADS ** -0.5)
        score = None
        for h in range(0, IDX_HEADS, 2):
            qg = qi_ref[:, (h // 4) * LANES:(h // 4 + 1) * LANES]
            qcat = jnp.concatenate([qg * _lane_mask(IDX_DIM, h % 4, BF16),
                                    qg * _lane_mask(IDX_DIM, h % 4 + 1, BF16)], axis=0)
            logit = _dot_nt(ki, qcat)
            term = (jnp.maximum(logit[:, :SQ], 0.0) * w_t[h:h + 1]
                    + jnp.maximum(logit[:, SQ:], 0.0) * w_t[h + 1:h + 2])
            score = term if score is None else score + term
        s_idx = head_w + lax.broadcasted_iota(jnp.int32, (KV_CHUNK, SQ), 0)
        tail_ok = s_idx <= i * SQ + lax.broadcasted_iota(jnp.int32, (KV_CHUNK, SQ), 1)
        if head_w:
            score_ref[:head_w, :] = score[:head_w]
        score_ref[head_w:width, :] = jnp.where(tail_ok, score[head_w:], NEG_INF)
        s2s = []
        for g in range(2):
            sl = slice(g * LANES, (g + 1) * LANES)
            qg = q_ref[:, sl] * (HEAD_DIM ** -0.5 * LOG2E)
            qcat = jnp.concatenate([(qg * _lane_mask(HEAD_DIM, hh, F32)).astype(BF16) for hh in range(2)], axis=0)
            s2s.append(_dot_nt(k_ref[:width, sl], qcat))
        thr = _kth_largest(score_ref, width, k_top)
        keep_tail = (score_ref[head_w:width, :] >= thr) & tail_ok
        cnt = _col_reduce(jnp.where(keep_tail, 1.0, 0.0), jnp.sum)
        neg_ref[head_w:width, :] = jnp.where(keep_tail, 0.0, NEG_INF)
        if head_w:
            keep_head = score_ref[:head_w, :] >= thr
            cnt = cnt + _col_reduce(jnp.where(keep_head, 1.0, 0.0), jnp.sum)
            neg_ref[:head_w, :] = jnp.where(keep_head, 0.0, NEG_INF)

        @pl.when(jnp.max(cnt) > k_top)
        def _():
            x = score_ref[:width, :]
            gt = x > thr
            need = k_top - _col_reduce(jnp.where(gt, 1.0, 0.0), jnp.sum)
            row = lax.broadcasted_iota(jnp.int32, (width, SQ), 0)
            neg_ref[:width, :] = jnp.where(gt | _first_ties(x == thr, need, row), 0.0, NEG_INF)

        outs = []
        for g in range(2):
            sl = slice(g * LANES, (g + 1) * LANES)
            s2 = s2s[g]
            es, ls = [], []
            for hh in range(2):
                bias = jnp.concatenate([_bias_col(bias_ref, 2 * g + hh, (SQ // QB) * i + u, width)
                                        for u in range(SQ // QB)], axis=1)
                s = s2[:, hh * SQ:(hh + 1) * SQ] + (bias + neg_ref[:width, :])
                e = jnp.exp2(s - _col_reduce(s, jnp.max))
                ls.append(_col_reduce(e, jnp.sum))
                es.append(e.astype(BF16))
            ot = _dot(vt_ref[sl, :width], jnp.concatenate(es, axis=1))
            outs.append(ot[:HEAD_DIM, :SQ] / ls[0])
            outs.append(ot[HEAD_DIM:, SQ:] / ls[1])
        y_t = jnp.concatenate(outs, axis=0)
        for u in range(SQ // QB):
            y = jnp.concatenate([y_t[g * LANES:(g + 1) * LANES, u * QB:(u + 1) * QB].T for g in range(2)], axis=1)
            rows = slice(u * QB, (u + 1) * QB)
            o_ref[rows, :] = (y * _silu(g_ref[rows, :])).astype(BF16)

    _by_causal_width(i, seq, body, SQ)


def _sparse_attn(p32, p16, bias, k_top):
    bsz, seq, _ = p32.shape
    nq = seq // QB
    return pl.pallas_call(
        functools.partial(_sparse_attn_kernel, k_top=k_top),
        grid=(bsz, seq // SQ),
        in_specs=[pl.BlockSpec((None, SQ, 256), lambda b, i: (b, i, P_CQ2 // 256)),
                  pl.BlockSpec((None, seq, 256), lambda b, i: (b, 0, P_CK // 256)),
                  pl.BlockSpec((None, seq, 256), lambda b, i: (b, 0, P_CV // 256)),
                  pl.BlockSpec((None, SQ, 256), lambda b, i: (b, i, P_CQI // 256)),
                  pl.BlockSpec((None, seq, LANES), lambda b, i: (b, 0, P_CKI // LANES)),
                  pl.BlockSpec((None, SQ, LANES), lambda b, i: (b, i, P_CWI // LANES)),
                  pl.BlockSpec((None, SQ, 256), lambda b, i: (b, i, P_CG // 256)),
                  pl.BlockSpec((4, nq + 1, QB, QB), lambda b, i: (0, 0, 0, 0))],
        out_specs=pl.BlockSpec((None, SQ, BRANCH_WIDTH), lambda b, i: (b, i, 0)),
        out_shape=jax.ShapeDtypeStruct((bsz, seq, BRANCH_WIDTH), BF16),
        scratch_shapes=[pltpu.VMEM((seq, SQ), F32), pltpu.VMEM((seq, SQ), F32),
                        pltpu.VMEM((BRANCH_WIDTH, seq), BF16)],
        compiler_params=_params(("arbitrary", "arbitrary")),
        name="sparse_attn",
    )(p32, p16, p32, p16, p16, p32, p32, bias)


def _dilated_kernel(q0_ref, q1_ref, k0_ref, k1_ref, v0_ref, v1_ref, g_ref, bias_ref, o_ref,
                    m_ref, l_ref, acc_ref):
    seq = q0_ref.shape[0]
    q_refs, k_refs, v_refs = (q0_ref, q1_ref), (k0_ref, k1_ref), (v0_ref, v1_ref)
    first = lax.broadcasted_iota(jnp.int32, (QB, LANES), 1) < HEAD_DIM
    qmasks = [_lane_mask(HEAD_DIM, hh, F32) * (HEAD_DIM ** -0.5 * LOG2E) for hh in range(2)]
    for p, (window, d) in enumerate(DILATED_PATTERNS):
        nb = seq // d // QB
        with_prev = nb > 1

        def group(n0, carry, p=p, d=d, with_prev=with_prev):
            work = []
            for u in range(DIL_UNROLL):
                n = n0 * DIL_UNROLL + u
                r = n % d
                j = n // d
                start = r + d * QB * j
                cur_rows = pl.ds(start, QB, stride=d)
                prev_rows = pl.ds(jnp.maximum(start - d * QB, r), QB, stride=d)
                for g in range(2):
                    q = q_refs[g][cur_rows, :]
                    kcat = k_refs[g][cur_rows, :]
                    vcat = v_refs[g][cur_rows, :]
                    if with_prev:
                        kcat = jnp.concatenate([k_refs[g][prev_rows, :], kcat], axis=0)
                        vcat = jnp.concatenate([v_refs[g][prev_rows, :], vcat], axis=0)
                    kcat = kcat.astype(BF16)
                    scores = []
                    for hh in range(2):
                        if with_prev:
                            bias = bias_ref[2 * g + hh, 2 * p + jnp.minimum(j, 1)]
                        else:
                            bias = bias_ref[2 * g + hh, 2 * p, :, QB:]
                        scores.append(_dot_nt((q * qmasks[hh]).astype(BF16), kcat) + bias)
                    work.append((cur_rows, g, vcat.astype(BF16), scores))
            for cur_rows, g, vcat, scores in work:
                ms, ls, accs = [], [], []
                for s in scores:
                    m = jnp.max(s, axis=-1, keepdims=True)
                    e = jnp.exp2(s - m)
                    ms.append(m)
                    ls.append(jnp.sum(e, axis=-1, keepdims=True))
                    accs.append(_dot(e.astype(BF16), vcat))
                m_new = jnp.where(first, ms[0], ms[1])
                l_new = jnp.where(first, ls[0], ls[1])
                a_new = jnp.where(first, accs[0], accs[1])
                if p == 0:
                    m_ref[g, cur_rows, :] = m_new
                    l_ref[g, cur_rows, :] = l_new
                    acc_ref[g, cur_rows, :] = a_new
                else:
                    m_old = m_ref[g, cur_rows, :]
                    m_tot = jnp.maximum(m_old, m_new)
                    w_old = jnp.exp2(m_old - m_tot)
                    w_new = jnp.exp2(m_new - m_tot)
                    m_ref[g, cur_rows, :] = m_tot
                    l_ref[g, cur_rows, :] = w_old * l_ref[g, cur_rows, :] + w_new * l_new
                    acc_ref[g, cur_rows, :] = w_old * acc_ref[g, cur_rows, :] + w_new * a_new
            return carry

        lax.fori_loop(0, d * nb // DIL_UNROLL, group, 0)
    gate = _silu(g_ref[...])
    for g in range(2):
        sl = slice(g * LANES, (g + 1) * LANES)
        o_ref[:, sl] = (acc_ref[g] / l_ref[g] * gate[:, sl]).astype(BF16)


def _dilated_attn(p32, bias):
    bsz, seq, _ = p32.shape

    def slab(off):
        return pl.BlockSpec((None, seq, LANES), lambda b, off=off: (b, 0, off // LANES))

    return pl.pallas_call(
        _dilated_kernel,
        grid=(bsz,),
        in_specs=[slab(P_BQ), slab(P_BQ + LANES), slab(P_BK), slab(P_BK + LANES),
                  slab(P_BV), slab(P_BV + LANES),
                  pl.BlockSpec((None, seq, 256), lambda b: (b, 0, P_BG // 256)),
                  pl.BlockSpec((4, 2 * len(DILATED_PATTERNS), QB, 2 * QB), lambda b: (0, 0, 0, 0))],
        out_specs=pl.BlockSpec((None, seq, BRANCH_WIDTH), lambda b: (b, 0, 0)),
        out_shape=jax.ShapeDtypeStruct((bsz, seq, BRANCH_WIDTH), BF16),
        scratch_shapes=[pltpu.VMEM((2, seq, LANES), F32)] * 3,
        compiler_params=_params(("arbitrary",)),
        name="dilated_attn",
    )(p32, p32, p32, p32, p32, p32, p32, bias)


def _out_proj_kernel(ya_ref, yb_ref, yc_ref, yd_ref, w_ref, h_ref, g_ref, o_ref):
    y = None
    for n, ref in enumerate((ya_ref, yb_ref, yc_ref, yd_ref)):
        t = _dot(ref[...], w_ref[n * BRANCH_WIDTH:(n + 1) * BRANCH_WIDTH, :])
        y = t if y is None else y + t
    ms = jnp.mean(y * y, axis=-1, keepdims=True)
    o_ref[...] = h_ref[...] + y * lax.rsqrt(ms + EPS) * g_ref[...]


def _out_proj(ya, yb, yc, yd, w, h, g, tm=512):
    bsz, seq, _ = h.shape
    yspec = pl.BlockSpec((None, tm, BRANCH_WIDTH), lambda b, i: (b, i, 0))
    return pl.pallas_call(
        _out_proj_kernel,
        grid=(bsz, seq // tm),
        in_specs=[yspec, yspec, yspec, yspec,
                  pl.BlockSpec((4 * BRANCH_WIDTH, D_MODEL), lambda b, i: (0, 0)),
                  pl.BlockSpec((None, tm, D_MODEL), lambda b, i: (b, i, 0)),
                  pl.BlockSpec((1, D_MODEL), lambda b, i: (0, 0))],
        out_specs=pl.BlockSpec((None, tm, D_MODEL), lambda b, i: (b, i, 0)),
        out_shape=jax.ShapeDtypeStruct(h.shape, F32),
        compiler_params=_params(("arbitrary", "arbitrary")),
        name="out_proj",
    )(ya, yb, yc, yd, w, h, g)


def kernel(x, w_in, w_out, norm_pre, norm_post, mla_q_norm, mla_kv_norm, mla_w_uq, mla_w_ukv,
           diff_lambda, diff_subln, rel_bias):
    bsz, seq, _ = x.shape
    depth = w_in.shape[0]
    nq = seq // QB
    k_top = min(IDX_TOPK_MAX, seq // 4)
    cos_t, sin_t = _rope_tables(seq)
    bias_b = _bias_expand(rel_bias, jnp.asarray(_bucket_tiles_dilated()), 0, 4, LOG2E)
    causal_buckets_t = jnp.asarray(np.swapaxes(_bucket_tiles_causal(nq), 1, 2))
    bias_c = _bias_expand(rel_bias, causal_buckets_t, 4, 4, LOG2E)
    bias_d = _bias_expand(rel_bias, causal_buckets_t, 8, 4, LOG2E)
    mask_tiles = jnp.asarray(_mask_tiles())
    h = x
    for layer in range(depth):
        w_arr = _arrange_w_in(w_in[layer]).astype(BF16)
        wq, wqrot, wk, wv = _arrange_mla(mla_w_uq[layer], mla_w_ukv[layer])
        gq = jnp.concatenate([mla_q_norm[layer], jnp.ones((256 - Q_LORA,), F32)])[None, :]
        gkv = mla_kv_norm[layer][None, :]
        p32, p16 = _in_proj(h, norm_pre[layer][None, :], w_arr)
        qa, ka, va = _mla_prep(p32, cos_t, sin_t, gq, gkv, wq, wqrot, wk, wv)
        y_a = _mla_attn(qa, ka, va, p32, mask_tiles)
        y_b = _dilated_attn(p32, bias_b)
        y_c = _sparse_attn(p32, p16, bias_c, k_top)
        lambda_init = 0.8 - 0.6 * math.exp(-0.3 * layer)
        subln = jnp.tile(diff_subln[layer], BRANCH_WIDTH // HEAD_DIM)[None, :]
        y_d = _diff_attn(p32, p16, bias_d, diff_lambda[layer], subln, lambda_init)
        h = _out_proj(y_a, y_b, y_c, y_d, w_out[layer].astype(BF16), h, norm_post[layer][None, :])
    return h
```

```python
import functools
import math

import jax
import jax.numpy as jnp
import numpy as np
from jax import lax
from jax.experimental import pallas as pl
from jax.experimental.pallas import tpu as pltpu

F32 = jnp.float32
BF16 = jnp.bfloat16

D_MODEL = 1024
A_HEADS, A_NOPE, A_ROPE, A_V = 4, 64, 32, 64
Q_LORA, KV_LORA = 192, 128
ROPE_THETA = 10000.0
HEAD_DIM = 64
DILATED_PATTERNS = ((128, 1), (512, 4), (2048, 16))
IDX_HEADS, IDX_DIM, IDX_TOPK_MAX = 8, 32, 256
D_QK = 32
BRANCH_WIDTH = 256
NUM_BUCKETS, MAX_DISTANCE = 32, 2048
NEG_INF = -1e30
EPS = 1e-6
LOG2E = math.log2(math.e)
KV_CHUNK = 256
COL_ACC_ROWS = 64
DIL_UNROLL = 4
LANES = 128
QB = 128
VMEM_LIMIT = 56 * 1024 * 1024

_SPLIT = (Q_LORA, KV_LORA, A_ROPE, 256, 256, 256, 256, 256, 256, 256, 256, IDX_HEADS * IDX_DIM, IDX_DIM,
          IDX_HEADS, 256, 256, 256, 256, 256)
_OFF = np.concatenate([[0], np.cumsum(_SPLIT)]).tolist()
(_A_CQ, _A_CKV, _A_KR, _A_G, _B_Q, _B_K, _B_V, _B_G, _C_Q, _C_K, _C_V, _C_QI, _C_KI, _C_WI, _C_G,
 _D_Q, _D_K, _D_V, _D_G) = range(19)

P_CQ, P_AG, P_CKV, P_KR, P_KRROT, P_CKI = 0, 256, 512, 640, 768, 896
P_BQ, P_BK, P_BV, P_BG = 1024, 1280, 1536, 1792
P_CQ2, P_CK, P_CV, P_CQI, P_CG = 2048, 2304, 2560, 2816, 3072
P_DQ, P_DK, P_DV, P_DG = 3328, 3584, 3840, 4096
P_CWI = 4352
NCOL = 4480


def _dot(a, b):
    return jnp.dot(a, b, preferred_element_type=F32)


def _dot_nt(a, b):
    return lax.dot_general(a, b, (((1,), (1,)), ((), ())), preferred_element_type=F32)


def _params(sem):
    return pltpu.CompilerParams(dimension_semantics=sem, vmem_limit_bytes=VMEM_LIMIT)


def _rot_cols(w):
    half = w.shape[-1] // 2
    return jnp.concatenate([-w[..., half:], w[..., :half]], axis=-1)


def _arrange_w_in(w):
    def seg(i):
        return w[:, _OFF[i]:_OFF[i + 1]]

    def z(n):
        return jnp.zeros((w.shape[0], n), w.dtype)

    kr = seg(_A_KR)
    cols = [
        seg(_A_CQ), z(64),
        seg(_A_G),
        seg(_A_CKV),
        z(64), kr, z(32),
        z(64), _rot_cols(kr), z(32),
        seg(_C_KI), seg(_C_KI), seg(_C_KI), seg(_C_KI),
        seg(_B_Q), seg(_B_K), seg(_B_V), seg(_B_G),
        seg(_C_Q), seg(_C_K), seg(_C_V), seg(_C_QI), seg(_C_G),
        seg(_D_Q), seg(_D_K), seg(_D_V), seg(_D_G),
        seg(_C_WI), z(LANES - IDX_HEADS),
    ]
    out = jnp.concatenate(cols, axis=1)
    assert out.shape[1] == NCOL
    return out


def _arrange_mla(w_uq, w_ukv):
    wq = w_uq.reshape(Q_LORA, A_HEADS, A_NOPE + A_ROPE)
    nope, rope = wq[..., :A_NOPE], wq[..., A_NOPE:]
    zq = jnp.zeros((Q_LORA, A_HEADS, LANES - A_NOPE - A_ROPE), w_uq.dtype)
    wq_main = jnp.concatenate([nope, rope, zq], axis=-1).reshape(Q_LORA, A_HEADS * LANES)
    wq_rot = jnp.concatenate([jnp.zeros_like(nope), _rot_cols(rope), zq], axis=-1).reshape(Q_LORA, A_HEADS * LANES)
    pad = jnp.zeros((256 - Q_LORA, A_HEADS * LANES), w_uq.dtype)
    wq_main = jnp.concatenate([wq_main, pad], axis=0)
    wq_rot = jnp.concatenate([wq_rot, pad], axis=0)
    wkv = w_ukv.reshape(KV_LORA, A_HEADS, A_NOPE + A_V)
    knope, v = wkv[..., :A_NOPE], wkv[..., A_NOPE:]
    wk = jnp.concatenate([knope, jnp.zeros_like(knope)], axis=-1).reshape(KV_LORA, A_HEADS * LANES)
    wv_t = v.reshape(KV_LORA, A_HEADS * A_V).T
    return wq_main.astype(BF16), wq_rot.astype(BF16), wk.astype(BF16), wv_t.astype(BF16)


def _rope_tables(seq):
    inv = ROPE_THETA ** (-jnp.arange(0, A_ROPE, 2, dtype=F32) / A_ROPE)
    ang = jnp.arange(seq, dtype=F32)[:, None] * inv[None, :]
    cos, sin = jnp.cos(ang), jnp.sin(ang)
    one = jnp.ones((seq, A_NOPE), F32)
    zero = jnp.zeros((seq, LANES - A_NOPE - A_ROPE), F32)
    cos_t = jnp.concatenate([one, cos, cos, zero], axis=1)
    sin_t = jnp.concatenate([jnp.zeros_like(one), sin, sin, zero], axis=1)
    return cos_t, sin_t


def _t5_bucket_np(rel):
    n = np.maximum(rel, 0)
    max_exact = NUM_BUCKETS // 2
    nf = np.maximum(n, max_exact).astype(np.float64)
    large = max_exact + (np.log(nf / max_exact) / math.log(MAX_DISTANCE / max_exact)
                         * (NUM_BUCKETS - max_exact)).astype(np.int32)
    large = np.minimum(large, NUM_BUCKETS - 1)
    return np.where(n < max_exact, n, large).astype(np.int32)


MASKED_BUCKET = NUM_BUCKETS


def _bucket_tiles_causal(nq):
    q = np.arange(QB)[:, None]
    k = np.arange(QB)[None, :]
    tiles = [np.full((QB, QB), MASKED_BUCKET, np.int32)]
    for d in range(nq):
        rel = QB * d + q - k
        tiles.append(np.where(rel >= 0, _t5_bucket_np(rel), MASKED_BUCKET).astype(np.int32))
    return np.stack(tiles)


def _mask_tiles():
    k = np.arange(QB)[:, None]
    q = np.arange(QB)[None, :]
    diag = np.where(k <= q, 0.0, NEG_INF)
    return np.stack([np.full((QB, QB), NEG_INF), diag, np.zeros((QB, QB))]).astype(np.float32)


def _bucket_tiles_dilated():
    q = np.arange(QB)[:, None]
    k = np.arange(2 * QB)[None, :]
    rel = q + QB - k
    tiles = []
    for (window, d) in DILATED_PATTERNS:
        in_band = (rel >= 0) & (rel <= window // d)
        for has_prev in (False, True):
            ok = in_band & (has_prev | (k >= QB))
            tiles.append(np.where(ok, _t5_bucket_np(rel * d), MASKED_BUCKET).astype(np.int32))
    return np.stack(tiles)


def _bias_expand_kernel(table_ref, bucket_ref, out_ref, *, head0, scale):
    h = pl.program_id(0) + head0
    bk = bucket_ref[...]
    acc = jnp.where(bk == MASKED_BUCKET, NEG_INF, 0.0)
    for b in range(NUM_BUCKETS):
        acc = jnp.where(bk == b, table_ref[b, h] * scale, acc)
    out_ref[...] = acc


def _bias_expand(table, buckets, head0, nheads, scale=1.0):
    n, r, c = buckets.shape
    return pl.pallas_call(
        functools.partial(_bias_expand_kernel, head0=head0, scale=scale),
        grid=(nheads,),
        in_specs=[pl.BlockSpec(memory_space=pltpu.SMEM),
                  pl.BlockSpec((n, r, c), lambda h: (0, 0, 0))],
        out_specs=pl.BlockSpec((None, n, r, c), lambda h: (h, 0, 0, 0)),
        out_shape=jax.ShapeDtypeStruct((nheads, n, r, c), F32),
        compiler_params=_params(("arbitrary",)),
        name="bias_expand",
    )(table, buckets)


def _in_proj_kernel(x_ref, g_ref, w_ref, o32_ref, o16_ref):
    x = x_ref[...]
    ms = jnp.mean(x * x, axis=-1, keepdims=True)
    xn = x * lax.rsqrt(ms + EPS) * g_ref[...]
    p = _dot(xn.astype(BF16), w_ref[...])
    o32_ref[...] = p
    o16_ref[...] = p.astype(BF16)


def _in_proj(h, g, w, tm=256):
    bsz, seq, _ = h.shape
    return pl.pallas_call(
        _in_proj_kernel,
        grid=(bsz, seq // tm),
        in_specs=[pl.BlockSpec((None, tm, D_MODEL), lambda b, i: (b, i, 0)),
                  pl.BlockSpec((1, D_MODEL), lambda b, i: (0, 0)),
                  pl.BlockSpec((D_MODEL, NCOL), lambda b, i: (0, 0))],
        out_specs=[pl.BlockSpec((None, tm, NCOL), lambda b, i: (b, i, 0)),
                   pl.BlockSpec((None, tm, NCOL), lambda b, i: (b, i, 0))],
        out_shape=[jax.ShapeDtypeStruct((bsz, seq, NCOL), F32),
                   jax.ShapeDtypeStruct((bsz, seq, NCOL), BF16)],
        compiler_params=_params(("arbitrary", "arbitrary")),
        name="in_proj",
    )(h, g, w)


def _mla_prep_kernel(cq_ref, ckv_ref, kr_ref, krrot_ref, cos_ref, sin_ref, gq_ref, gkv_ref,
                     wq_ref, wqrot_ref, wk_ref, wv_ref, q_ref, k_ref, v_ref):
    cos = cos_ref[...]
    sin = sin_ref[...]
    cos4 = jnp.concatenate([cos] * A_HEADS, axis=1)
    sin4 = jnp.concatenate([sin] * A_HEADS, axis=1)
    cq = cq_ref[...]
    ms = jnp.sum(cq * cq, axis=-1, keepdims=True) * (1.0 / Q_LORA)
    nq = (cq * lax.rsqrt(ms + EPS) * gq_ref[...]).astype(BF16)
    q = _dot(nq, wq_ref[...]) * cos4 + _dot(nq, wqrot_ref[...]) * sin4
    q_ref[...] = (q * ((A_NOPE + A_ROPE) ** -0.5 * LOG2E)).astype(BF16)
    ckv = ckv_ref[...]
    ms = jnp.mean(ckv * ckv, axis=-1, keepdims=True)
    nkv = (ckv * lax.rsqrt(ms + EPS) * gkv_ref[...]).astype(BF16)
    kr = kr_ref[...] * cos + krrot_ref[...] * sin
    k = _dot(nkv, wk_ref[...]) + jnp.concatenate([kr] * A_HEADS, axis=1)
    k_ref[...] = k.astype(BF16)
    v_ref[...] = _dot_nt(wv_ref[...], nkv).astype(BF16)


def _mla_prep(p32, cos_t, sin_t, gq, gkv, wq, wqrot, wk, wv_t, tm=512):
    bsz, seq, _ = p32.shape
    w4 = A_HEADS * LANES
    row = lambda b, i: (b, i, 0)
    const = lambda b, i: (0, 0)
    return pl.pallas_call(
        _mla_prep_kernel,
        grid=(bsz, seq // tm),
        in_specs=[pl.BlockSpec((None, tm, 256), lambda b, i: (b, i, P_CQ // 256)),
                  pl.BlockSpec((None, tm, LANES), lambda b, i: (b, i, P_CKV // LANES)),
                  pl.BlockSpec((None, tm, LANES), lambda b, i: (b, i, P_KR // LANES)),
                  pl.BlockSpec((None, tm, LANES), lambda b, i: (b, i, P_KRROT // LANES)),
                  pl.BlockSpec((tm, LANES), lambda b, i: (i, 0)),
                  pl.BlockSpec((tm, LANES), lambda b, i: (i, 0)),
                  pl.BlockSpec((1, 256), const),
                  pl.BlockSpec((1, KV_LORA), const),
                  pl.BlockSpec((256, w4), const),
                  pl.BlockSpec((256, w4), const),
                  pl.BlockSpec((KV_LORA, w4), const),
                  pl.BlockSpec((A_HEADS * A_V, KV_LORA), const)],
        out_specs=[pl.BlockSpec((None, tm, w4), row), pl.BlockSpec((None, tm, w4), row),
                   pl.BlockSpec((None, A_HEADS * A_V, tm), lambda b, i: (b, 0, i))],
        out_shape=[jax.ShapeDtypeStruct((bsz, seq, w4), BF16), jax.ShapeDtypeStruct((bsz, seq, w4), BF16),
                   jax.ShapeDtypeStruct((bsz, A_HEADS * A_V, seq), BF16)],
        compiler_params=_params(("arbitrary", "arbitrary")),
        name="mla_prep",
    )(p32, p32, p32, p32, cos_t, sin_t, gq, gkv, wq, wqrot, wk, wv_t)


def _silu(g):
    return g * (1.0 / (1.0 + jnp.exp(-g)))


def _by_causal_width(i, seq, body):
    per = KV_CHUNK // QB
    for wb in range(seq // KV_CHUNK):
        pl.when(i // per == wb)(functools.partial(body, (wb + 1) * KV_CHUNK))


def _mask_tail_t(mask_ref, i, width):
    first = (width - KV_CHUNK) // QB
    return jnp.concatenate([mask_ref[jnp.clip(i - j, -1, 1) + 1] for j in range(first, width // QB)], axis=0)


def _col_reduce(x, op):
    rows, lanes = x.shape
    part = op(x.reshape(rows // COL_ACC_ROWS, COL_ACC_ROWS, lanes), axis=0)
    return op(part, axis=0, keepdims=True)


def _bias_col(bias_ref, h, i, width):
    return jnp.concatenate([bias_ref[h, jnp.maximum(i - j, -1) + 1] for j in range(width // QB)], axis=0)


def _lane_mask(width, seg, dtype):
    lane = lax.broadcasted_iota(jnp.int32, (1, LANES), 1)
    return jnp.where((lane >= seg * width) & (lane < (seg + 1) * width), 1.0, 0.0).astype(dtype)


def _mla_attn_kernel(q_ref, k_ref, vt_ref, g_ref, mask_ref, o_ref):
    i = pl.program_id(1)
    seq = k_ref.shape[0]

    def body(width):
        head_w = width - KV_CHUNK
        zero = jnp.zeros((QB, LANES), BF16)
        scores = []
        for g in range(A_HEADS // 2):
            q0 = q_ref[:, 2 * g * LANES:(2 * g + 1) * LANES]
            q1 = q_ref[:, (2 * g + 1) * LANES:(2 * g + 2) * LANES]
            qbd = jnp.concatenate([jnp.concatenate([q0, zero], axis=1),
                                   jnp.concatenate([zero, q1], axis=1)], axis=0)
            scores.append(_dot_nt(k_ref[:width, 2 * g * LANES:(2 * g + 2) * LANES], qbd))
        tail = _mask_tail_t(mask_ref, i, width)
        outs = []
        for g in range(A_HEADS // 2):
            es, ls = [], []
            for hh in range(2):
                s = scores[g][:, hh * QB:(hh + 1) * QB]
                s = jnp.concatenate([s[:head_w], s[head_w:] + tail], axis=0) if head_w else s + tail
                e = jnp.exp2(s - _col_reduce(s, jnp.max))
                ls.append(_col_reduce(e, jnp.sum))
                es.append(e.astype(BF16))
            ot = _dot(vt_ref[g * LANES:(g + 1) * LANES, :width], jnp.concatenate(es, axis=1))
            outs.append(ot[:A_V, :QB] / ls[0])
            outs.append(ot[A_V:, QB:] / ls[1])
        y_t = jnp.concatenate(outs, axis=0)
        y = jnp.concatenate([y_t[:LANES].T, y_t[LANES:].T], axis=1)
        o_ref[...] = (y * _silu(g_ref[...])).astype(BF16)

    _by_causal_width(i, seq, body)


def _mla_attn(q, k, v_t, p32, mask_tiles):
    bsz, seq, w4 = q.shape
    return pl.pallas_call(
        _mla_attn_kernel,
        grid=(bsz, seq // QB),
        in_specs=[pl.BlockSpec((None, QB, w4), lambda b, i: (b, i, 0)),
                  pl.BlockSpec((None, seq, w4), lambda b, i: (b, 0, 0)),
                  pl.BlockSpec((None, A_HEADS * A_V, seq), lambda b, i: (b, 0, 0)),
                  pl.BlockSpec((None, QB, 256), lambda b, i: (b, i, P_AG // 256)),
                  pl.BlockSpec((3, QB, QB), lambda b, i: (0, 0, 0))],
        out_specs=pl.BlockSpec((None, QB, BRANCH_WIDTH), lambda b, i: (b, i, 0)),
        out_shape=jax.ShapeDtypeStruct((bsz, seq, BRANCH_WIDTH), BF16),
        compiler_params=_params(("arbitrary", "arbitrary")),
        name="mla_attn",
    )(q, k, v_t, p32, mask_tiles)


def _transpose_values(v_ref, vt_ref):
    for j in range(v_ref.shape[0] // QB):
        for g in range(v_ref.shape[1] // LANES):
            tile = v_ref[j * QB:(j + 1) * QB, g * LANES:(g + 1) * LANES]
            vt_ref[g * LANES:(g + 1) * LANES, j * QB:(j + 1) * QB] = tile.T.astype(BF16)


def _diff_attn_kernel(q_ref, k_ref, v_ref, g_ref, bias_ref, lam_ref, subln_ref, o_ref, vt_ref, *, lambda_init):
    i = pl.program_id(1)
    seq = k_ref.shape[0]

    @pl.when(i == 0)
    def _():
        _transpose_values(v_ref, vt_ref)

    def body(width):
        lp = lam_ref[...]
        lam = (jnp.exp(jnp.sum(lp[0:1] * lp[1:2], axis=-1, keepdims=True))
               - jnp.exp(jnp.sum(lp[2:3] * lp[3:4], axis=-1, keepdims=True)) + lambda_init)
        scores = []
        for h in range(4):
            sl = slice((h // 2) * LANES, (h // 2 + 1) * LANES)
            qg = q_ref[:, sl] * (D_QK ** -0.5 * LOG2E)
            qcat = jnp.concatenate([(qg * _lane_mask(D_QK, 2 * (h % 2) + mm, F32)).astype(BF16)
                                    for mm in range(2)], axis=0)
            scores.append(_dot_nt(k_ref[:width, sl], qcat))
        outs = []
        for h in range(4):
            bias = _bias_col(bias_ref, h, i, width)
            es, ls = [], []
            for mm in range(2):
                s = scores[h][:, mm * QB:(mm + 1) * QB] + bias
                e = jnp.exp2(s - _col_reduce(s, jnp.max))
                ls.append(_col_reduce(e, jnp.sum))
                es.append(e.astype(BF16))
            ot = _dot(vt_ref[h * HEAD_DIM:(h + 1) * HEAD_DIM, :width], jnp.concatenate(es, axis=1))
            a = ot[:, :QB] / ls[0] - lam * (ot[:, QB:] / ls[1])
            ms = jnp.sum(a * a, axis=0, keepdims=True) * (1.0 / HEAD_DIM)
            outs.append(a * lax.rsqrt(ms + EPS))
        y_t = jnp.concatenate(outs, axis=0)
        y = jnp.concatenate([y_t[:LANES].T, y_t[LANES:].T], axis=1)
        o_ref[...] = (y * (subln_ref[...] * (1.0 - lambda_init)) * _silu(g_ref[...])).astype(BF16)

    _by_causal_width(i, seq, body)


def _diff_attn(p32, p16, bias, lam_params, subln, lambda_init):
    bsz, seq, _ = p32.shape
    nq = seq // QB
    return pl.pallas_call(
        functools.partial(_diff_attn_kernel, lambda_init=lambda_init),
        grid=(bsz, nq),
        in_specs=[pl.BlockSpec((None, QB, 256), lambda b, i: (b, i, P_DQ // 256)),
                  pl.BlockSpec((None, seq, 256), lambda b, i: (b, 0, P_DK // 256)),
                  pl.BlockSpec((None, seq, 256), lambda b, i: (b, 0, P_DV // 256)),
                  pl.BlockSpec((None, QB, 256), lambda b, i: (b, i, P_DG // 256)),
                  pl.BlockSpec((4, nq + 1, QB, QB), lambda b, i: (0, 0, 0, 0)),
                  pl.BlockSpec((4, D_QK), lambda b, i: (0, 0)),
                  pl.BlockSpec((1, BRANCH_WIDTH), lambda b, i: (0, 0))],
        out_specs=pl.BlockSpec((None, QB, BRANCH_WIDTH), lambda b, i: (b, i, 0)),
        out_shape=jax.ShapeDtypeStruct((bsz, seq, BRANCH_WIDTH), BF16),
        scratch_shapes=[pltpu.VMEM((BRANCH_WIDTH, seq), BF16)],
        compiler_params=_params(("arbitrary", "arbitrary")),
        name="diff_attn",
    )(p32, p16, p32, p32, bias, lam_params, subln)


def _sortable_to_float(key):
    return pltpu.bitcast(jnp.where(key < 0, key ^ jnp.int32(0x7FFFFFFF), key), F32)


def _kth_largest(score_ref, width, k_top):
    def count_ge(key):
        thr = _sortable_to_float(key)
        return _col_reduce(jnp.where(score_ref[:width, :] >= thr, 1.0, 0.0), jnp.sum)

    int_min = jnp.full((1, QB), -2 ** 31, jnp.int32)
    zero = jnp.zeros((1, QB), jnp.int32)
    t = jnp.where(count_ge(zero) >= k_top, zero, int_min)

    def step(it, t):
        cand = t + (jnp.int32(1) << (30 - it))
        return jnp.where(count_ge(cand) >= k_top, cand, t)

    return _sortable_to_float(lax.fori_loop(0, 31, step, t))


def _first_ties(eq, need, row):
    eqf = jnp.where(eq, 1.0, 0.0)
    nbits = int(eq.shape[0]).bit_length()

    def body(it, j):
        cand = j + (jnp.int32(1) << (nbits - 1 - it))
        cnt = _col_reduce(jnp.where(row < cand, eqf, 0.0), jnp.sum)
        return jnp.where(cnt <= need, cand, j)

    j = lax.fori_loop(0, nbits, body, jnp.zeros((1, eq.shape[1]), jnp.int32))
    return eq & (row < j)


def _sparse_attn_kernel(q_ref, k_ref, v_ref, qi_ref, ki_ref, wi_ref, g_ref, bias_ref, o_ref,
                        score_ref, neg_ref, vt_ref, *, k_top):
    i = pl.program_id(1)
    seq = k_ref.shape[0]

    @pl.when(i == 0)
    def _():
        _transpose_values(v_ref, vt_ref)

    def body(width):
        head_w = width - KV_CHUNK
        ki = ki_ref[:width, :]
        w_t = wi_ref[...].T * (IDX_DIM ** -0.5 * IDX_HEADS ** -0.5)
        score = None
        for h in range(0, IDX_HEADS, 2):
            qg = qi_ref[:, (h // 4) * LANES:(h // 4 + 1) * LANES]
            qcat = jnp.concatenate([qg * _lane_mask(IDX_DIM, h % 4, BF16),
                                    qg * _lane_mask(IDX_DIM, h % 4 + 1, BF16)], axis=0)
            logit = _dot_nt(ki, qcat)
            term = (jnp.maximum(logit[:, :QB], 0.0) * w_t[h:h + 1]
                    + jnp.maximum(logit[:, QB:], 0.0) * w_t[h + 1:h + 2])
            score = term if score is None else score + term
        s_idx = head_w + lax.broadcasted_iota(jnp.int32, (KV_CHUNK, QB), 0)
        tail_ok = s_idx <= i * QB + lax.broadcasted_iota(jnp.int32, (KV_CHUNK, QB), 1)
        if head_w:
            score_ref[:head_w, :] = score[:head_w]
        score_ref[head_w:width, :] = jnp.where(tail_ok, score[head_w:], NEG_INF)
        s2s = []
        for g in range(2):
            sl = slice(g * LANES, (g + 1) * LANES)
            qg = q_ref[:, sl] * (HEAD_DIM ** -0.5 * LOG2E)
            qcat = jnp.concatenate([(qg * _lane_mask(HEAD_DIM, hh, F32)).astype(BF16) for hh in range(2)], axis=0)
            s2s.append(_dot_nt(k_ref[:width, sl], qcat))
        thr = _kth_largest(score_ref, width, k_top)
        keep_tail = (score_ref[head_w:width, :] >= thr) & tail_ok
        cnt = _col_reduce(jnp.where(keep_tail, 1.0, 0.0), jnp.sum)
        neg_ref[head_w:width, :] = jnp.where(keep_tail, 0.0, NEG_INF)
        if head_w:
            keep_head = score_ref[:head_w, :] >= thr
            cnt = cnt + _col_reduce(jnp.where(keep_head, 1.0, 0.0), jnp.sum)
            neg_ref[:head_w, :] = jnp.where(keep_head, 0.0, NEG_INF)

        @pl.when(jnp.max(cnt) > k_top)
        def _():
            x = score_ref[:width, :]
            gt = x > thr
            need = k_top - _col_reduce(jnp.where(gt, 1.0, 0.0), jnp.sum)
            row = lax.broadcasted_iota(jnp.int32, (width, QB), 0)
            neg_ref[:width, :] = jnp.where(gt | _first_ties(x == thr, need, row), 0.0, NEG_INF)

        outs = []
        for g in range(2):
            sl = slice(g * LANES, (g + 1) * LANES)
            s2 = s2s[g]
            es, ls = [], []
            for hh in range(2):
                s = s2[:, hh * QB:(hh + 1) * QB] + (_bias_col(bias_ref, 2 * g + hh, i, width) + neg_ref[:width, :])
                e = jnp.exp2(s - _col_reduce(s, jnp.max))
                ls.append(_col_reduce(e, jnp.sum))
                es.append(e.astype(BF16))
            ot = _dot(vt_ref[sl, :width], jnp.concatenate(es, axis=1))
            outs.append(ot[:HEAD_DIM, :QB] / ls[0])
            outs.append(ot[HEAD_DIM:, QB:] / ls[1])
        y_t = jnp.concatenate(outs, axis=0)
        y = jnp.concatenate([y_t[:LANES].T, y_t[LANES:].T], axis=1)
        o_ref[...] = (y * _silu(g_ref[...])).astype(BF16)

    _by_causal_width(i, seq, body)


def _sparse_attn(p32, p16, bias, k_top):
    bsz, seq, _ = p32.shape
    nq = seq // QB
    return pl.pallas_call(
        functools.partial(_sparse_attn_kernel, k_top=k_top),
        grid=(bsz, nq),
        in_specs=[pl.BlockSpec((None, QB, 256), lambda b, i: (b, i, P_CQ2 // 256)),
                  pl.BlockSpec((None, seq, 256), lambda b, i: (b, 0, P_CK // 256)),
                  pl.BlockSpec((None, seq, 256), lambda b, i: (b, 0, P_CV // 256)),
                  pl.BlockSpec((None, QB, 256), lambda b, i: (b, i, P_CQI // 256)),
                  pl.BlockSpec((None, seq, LANES), lambda b, i: (b, 0, P_CKI // LANES)),
                  pl.BlockSpec((None, QB, LANES), lambda b, i: (b, i, P_CWI // LANES)),
                  pl.BlockSpec((None, QB, 256), lambda b, i: (b, i, P_CG // 256)),
                  pl.BlockSpec((4, nq + 1, QB, QB), lambda b, i: (0, 0, 0, 0))],
        out_specs=pl.BlockSpec((None, QB, BRANCH_WIDTH), lambda b, i: (b, i, 0)),
        out_shape=jax.ShapeDtypeStruct((bsz, seq, BRANCH_WIDTH), BF16),
        scratch_shapes=[pltpu.VMEM((seq, QB), F32), pltpu.VMEM((seq, QB), F32),
                        pltpu.VMEM((BRANCH_WIDTH, seq), BF16)],
        compiler_params=_params(("arbitrary", "arbitrary")),
        name="sparse_attn",
    )(p32, p16, p32, p16, p16, p32, p32, bias)


def _dilated_kernel(q0_ref, q1_ref, k0_ref, k1_ref, v0_ref, v1_ref, g_ref, bias_ref, o_ref,
                    m_ref, l_ref, acc_ref):
    seq = q0_ref.shape[0]
    q_refs, k_refs, v_refs = (q0_ref, q1_ref), (k0_ref, k1_ref), (v0_ref, v1_ref)
    first = lax.broadcasted_iota(jnp.int32, (QB, LANES), 1) < HEAD_DIM
    qmasks = [_lane_mask(HEAD_DIM, hh, F32) * (HEAD_DIM ** -0.5 * LOG2E) for hh in range(2)]
    for p, (window, d) in enumerate(DILATED_PATTERNS):
        nb = seq // d // QB
        with_prev = nb > 1

        def group(n0, carry, p=p, d=d, with_prev=with_prev):
            work = []
            for u in range(DIL_UNROLL):
                n = n0 * DIL_UNROLL + u
                r = n % d
                j = n // d
                start = r + d * QB * j
                cur_rows = pl.ds(start, QB, stride=d)
                prev_rows = pl.ds(jnp.maximum(start - d * QB, r), QB, stride=d)
                for g in range(2):
                    q = q_refs[g][cur_rows, :]
                    kcat = k_refs[g][cur_rows, :]
                    vcat = v_refs[g][cur_rows, :]
                    if with_prev:
                        kcat = jnp.concatenate([k_refs[g][prev_rows, :], kcat], axis=0)
                        vcat = jnp.concatenate([v_refs[g][prev_rows, :], vcat], axis=0)
                    kcat = kcat.astype(BF16)
                    scores = []
                    for hh in range(2):
                        if with_prev:
                            bias = bias_ref[2 * g + hh, 2 * p + jnp.minimum(j, 1)]
                        else:
                            bias = bias_ref[2 * g + hh, 2 * p, :, QB:]
                        scores.append(_dot_nt((q * qmasks[hh]).astype(BF16), kcat) + bias)
                    work.append((cur_rows, g, vcat.astype(BF16), scores))
            for cur_rows, g, vcat, scores in work:
                ms, ls, accs = [], [], []
                for s in scores:
                    m = jnp.max(s, axis=-1, keepdims=True)
                    e = jnp.exp2(s - m)
                    ms.append(m)
                    ls.append(jnp.sum(e, axis=-1, keepdims=True))
                    accs.append(_dot(e.astype(BF16), vcat))
                m_new = jnp.where(first, ms[0], ms[1])
                l_new = jnp.where(first, ls[0], ls[1])
                a_new = jnp.where(first, accs[0], accs[1])
                if p == 0:
                    m_ref[g, cur_rows, :] = m_new
                    l_ref[g, cur_rows, :] = l_new
                    acc_ref[g, cur_rows, :] = a_new
                else:
                    m_old = m_ref[g, cur_rows, :]
                    m_tot = jnp.maximum(m_old, m_new)
                    w_old = jnp.exp2(m_old - m_tot)
                    w_new = jnp.exp2(m_new - m_tot)
                    m_ref[g, cur_rows, :] = m_tot
                    l_ref[g, cur_rows, :] = w_old * l_ref[g, cur_rows, :] + w_new * l_new
                    acc_ref[g, cur_rows, :] = w_old * acc_ref[g, cur_rows, :] + w_new * a_new
            return carry

        lax.fori_loop(0, d * nb // DIL_UNROLL, group, 0)
    gate = _silu(g_ref[...])
    for g in range(2):
        sl = slice(g * LANES, (g + 1) * LANES)
        o_ref[:, sl] = (acc_ref[g] / l_ref[g] * gate[:, sl]).astype(BF16)


def _dilated_attn(p32, bias):
    bsz, seq, _ = p32.shape

    def slab(off):
        return pl.BlockSpec((None, seq, LANES), lambda b, off=off: (b, 0, off // LANES))

    return pl.pallas_call(
        _dilated_kernel,
        grid=(bsz,),
        in_specs=[slab(P_BQ), slab(P_BQ + LANES), slab(P_BK), slab(P_BK + LANES),
                  slab(P_BV), slab(P_BV + LANES),
                  pl.BlockSpec((None, seq, 256), lambda b: (b, 0, P_BG // 256)),
                  pl.BlockSpec((4, 2 * len(DILATED_PATTERNS), QB, 2 * QB), lambda b: (0, 0, 0, 0))],
        out_specs=pl.BlockSpec((None, seq, BRANCH_WIDTH), lambda b: (b, 0, 0)),
        out_shape=jax.ShapeDtypeStruct((bsz, seq, BRANCH_WIDTH), BF16),
        scratch_shapes=[pltpu.VMEM((2, seq, LANES), F32)] * 3,
        compiler_params=_params(("arbitrary",)),
        name="dilated_attn",
    )(p32, p32, p32, p32, p32, p32, p32, bias)


def _out_proj_kernel(ya_ref, yb_ref, yc_ref, yd_ref, w_ref, h_ref, g_ref, o_ref):
    y = None
    for n, ref in enumerate((ya_ref, yb_ref, yc_ref, yd_ref)):
        t = _dot(ref[...], w_ref[n * BRANCH_WIDTH:(n + 1) * BRANCH_WIDTH, :])
        y = t if y is None else y + t
    ms = jnp.mean(y * y, axis=-1, keepdims=True)
    o_ref[...] = h_ref[...] + y * lax.rsqrt(ms + EPS) * g_ref[...]


def _out_proj(ya, yb, yc, yd, w, h, g, tm=512):
    bsz, seq, _ = h.shape
    yspec = pl.BlockSpec((None, tm, BRANCH_WIDTH), lambda b, i: (b, i, 0))
    return pl.pallas_call(
        _out_proj_kernel,
        grid=(bsz, seq // tm),
        in_specs=[yspec, yspec, yspec, yspec,
                  pl.BlockSpec((4 * BRANCH_WIDTH, D_MODEL), lambda b, i: (0, 0)),
                  pl.BlockSpec((None, tm, D_MODEL), lambda b, i: (b, i, 0)),
                  pl.BlockSpec((1, D_MODEL), lambda b, i: (0, 0))],
        out_specs=pl.BlockSpec((None, tm, D_MODEL), lambda b, i: (b, i, 0)),
        out_shape=jax.ShapeDtypeStruct(h.shape, F32),
        compiler_params=_params(("arbitrary", "arbitrary")),
        name="out_proj",
    )(ya, yb, yc, yd, w, h, g)


def kernel(x, w_in, w_out, norm_pre, norm_post, mla_q_norm, mla_kv_norm, mla_w_uq, mla_w_ukv,
           diff_lambda, diff_subln, rel_bias):
    bsz, seq, _ = x.shape
    depth = w_in.shape[0]
    nq = seq // QB
    k_top = min(IDX_TOPK_MAX, seq // 4)
    cos_t, sin_t = _rope_tables(seq)
    bias_b = _bias_expand(rel_bias, jnp.asarray(_bucket_tiles_dilated()), 0, 4, LOG2E)
    causal_buckets_t = jnp.asarray(np.swapaxes(_bucket_tiles_causal(nq), 1, 2))
    bias_c = _bias_expand(rel_bias, causal_buckets_t, 4, 4, LOG2E)
    bias_d = _bias_expand(rel_bias, causal_buckets_t, 8, 4, LOG2E)
    mask_tiles = jnp.asarray(_mask_tiles())
    h = x
    for layer in range(depth):
        w_arr = _arrange_w_in(w_in[layer]).astype(BF16)
        wq, wqrot, wk, wv = _arrange_mla(mla_w_uq[layer], mla_w_ukv[layer])
        gq = jnp.concatenate([mla_q_norm[layer], jnp.ones((256 - Q_LORA,), F32)])[None, :]
        gkv = mla_kv_norm[layer][None, :]
        p32, p16 = _in_proj(h, norm_pre[layer][None, :], w_arr)
        qa, ka, va = _mla_prep(p32, cos_t, sin_t, gq, gkv, wq, wqrot, wk, wv)
        y_a = _mla_attn(qa, ka, va, p32, mask_tiles)
        y_b = _dilated_attn(p32, bias_b)
        y_c = _sparse_attn(p32, p16, bias_c, k_top)
        lambda_init = 0.8 - 0.6 * math.exp(-0.3 * layer)
        subln = jnp.tile(diff_subln[layer], BRANCH_WIDTH // HEAD_DIM)[None, :]
        y_d = _diff_attn(p32, p16, bias_d, diff_lambda[layer], subln, lambda_init)
        h = _out_proj(y_a, y_b, y_c, y_d, w_out[layer].astype(BF16), h, norm_post[layer][None, :])
    return h
```

```python
import functools
import math

import jax
import jax.numpy as jnp
import numpy as np
from jax import lax
from jax.experimental import pallas as pl
from jax.experimental.pallas import tpu as pltpu

F32 = jnp.float32
BF16 = jnp.bfloat16

D_MODEL = 1024
A_HEADS, A_NOPE, A_ROPE, A_V = 4, 64, 32, 64
Q_LORA, KV_LORA = 192, 128
ROPE_THETA = 10000.0
HEAD_DIM = 64
DILATED_PATTERNS = ((128, 1), (512, 4), (2048, 16))
IDX_HEADS, IDX_DIM, IDX_TOPK_MAX = 8, 32, 256
D_QK = 32
BRANCH_WIDTH = 256
NUM_BUCKETS, MAX_DISTANCE = 32, 2048
NEG_INF = -1e30
EPS = 1e-6
LOG2E = math.log2(math.e)
KV_CHUNK = 256
COL_ACC_ROWS = 64
DIL_UNROLL = 4
LANES = 128
QB = 128
VMEM_LIMIT = 56 * 1024 * 1024

_SPLIT = (Q_LORA, KV_LORA, A_ROPE, 256, 256, 256, 256, 256, 256, 256, 256, IDX_HEADS * IDX_DIM, IDX_DIM,
          IDX_HEADS, 256, 256, 256, 256, 256)
_OFF = np.concatenate([[0], np.cumsum(_SPLIT)]).tolist()
(_A_CQ, _A_CKV, _A_KR, _A_G, _B_Q, _B_K, _B_V, _B_G, _C_Q, _C_K, _C_V, _C_QI, _C_KI, _C_WI, _C_G,
 _D_Q, _D_K, _D_V, _D_G) = range(19)

P_CQ, P_AG, P_CKV, P_KR, P_KRROT, P_CWI = 0, 256, 512, 640, 768, 896
P_BG, P_CQ2, P_CV, P_CG, P_DQ, P_DV, P_DG = 1024, 1280, 1536, 1792, 2048, 2304, 2560
NCOL32 = 2816
Q_CK, Q_CQI, Q_DK, Q_CKI = 0, 256, 512, 768
NCOL16 = 896
B_SLABS = 6
NCOL = NCOL32 + NCOL16 + B_SLABS * LANES


def _dot(a, b):
    return jnp.dot(a, b, preferred_element_type=F32)


def _dot_nt(a, b):
    return lax.dot_general(a, b, (((1,), (1,)), ((), ())), preferred_element_type=F32)


def _params(sem):
    return pltpu.CompilerParams(dimension_semantics=sem, vmem_limit_bytes=VMEM_LIMIT)


def _rot_cols(w):
    half = w.shape[-1] // 2
    return jnp.concatenate([-w[..., half:], w[..., :half]], axis=-1)


def _arrange_w_in(w):
    def seg(i):
        return w[:, _OFF[i]:_OFF[i + 1]]

    def z(n):
        return jnp.zeros((w.shape[0], n), w.dtype)

    kr = seg(_A_KR)
    cols = [
        seg(_A_CQ), z(64),
        seg(_A_G),
        seg(_A_CKV),
        z(64), kr, z(32),
        z(64), _rot_cols(kr), z(32),
        seg(_C_WI), z(LANES - IDX_HEADS),
        seg(_B_G), seg(_C_Q), seg(_C_V), seg(_C_G), seg(_D_Q), seg(_D_V), seg(_D_G),
        seg(_C_K), seg(_C_QI), seg(_D_K),
        seg(_C_KI), seg(_C_KI), seg(_C_KI), seg(_C_KI),
        seg(_B_Q) * (HEAD_DIM ** -0.5 * LOG2E), seg(_B_K), seg(_B_V),
    ]
    out = jnp.concatenate(cols, axis=1)
    assert out.shape[1] == NCOL
    return out


def _arrange_mla(w_uq, w_ukv):
    wq = w_uq.reshape(Q_LORA, A_HEADS, A_NOPE + A_ROPE)
    nope, rope = wq[..., :A_NOPE], wq[..., A_NOPE:]
    zq = jnp.zeros((Q_LORA, A_HEADS, LANES - A_NOPE - A_ROPE), w_uq.dtype)
    wq_main = jnp.concatenate([nope, rope, zq], axis=-1).reshape(Q_LORA, A_HEADS * LANES)
    wq_rot = jnp.concatenate([jnp.zeros_like(nope), _rot_cols(rope), zq], axis=-1).reshape(Q_LORA, A_HEADS * LANES)
    pad = jnp.zeros((256 - Q_LORA, A_HEADS * LANES), w_uq.dtype)
    wq_main = jnp.concatenate([wq_main, pad], axis=0)
    wq_rot = jnp.concatenate([wq_rot, pad], axis=0)
    wkv = w_ukv.reshape(KV_LORA, A_HEADS, A_NOPE + A_V)
    knope, v = wkv[..., :A_NOPE], wkv[..., A_NOPE:]
    wk = jnp.concatenate([knope, jnp.zeros_like(knope)], axis=-1).reshape(KV_LORA, A_HEADS * LANES)
    wv_t = v.reshape(KV_LORA, A_HEADS * A_V).T
    return wq_main.astype(BF16), wq_rot.astype(BF16), wk.astype(BF16), wv_t.astype(BF16)


def _rope_tables(seq):
    inv = ROPE_THETA ** (-jnp.arange(0, A_ROPE, 2, dtype=F32) / A_ROPE)
    ang = jnp.arange(seq, dtype=F32)[:, None] * inv[None, :]
    cos, sin = jnp.cos(ang), jnp.sin(ang)
    one = jnp.ones((seq, A_NOPE), F32)
    zero = jnp.zeros((seq, LANES - A_NOPE - A_ROPE), F32)
    cos_t = jnp.concatenate([one, cos, cos, zero], axis=1)
    sin_t = jnp.concatenate([jnp.zeros_like(one), sin, sin, zero], axis=1)
    return cos_t, sin_t


def _t5_bucket_np(rel):
    n = np.maximum(rel, 0)
    max_exact = NUM_BUCKETS // 2
    nf = np.maximum(n, max_exact).astype(np.float64)
    large = max_exact + (np.log(nf / max_exact) / math.log(MAX_DISTANCE / max_exact)
                         * (NUM_BUCKETS - max_exact)).astype(np.int32)
    large = np.minimum(large, NUM_BUCKETS - 1)
    return np.where(n < max_exact, n, large).astype(np.int32)


MASKED_BUCKET = NUM_BUCKETS


def _bucket_tiles_causal(nq):
    q = np.arange(QB)[:, None]
    k = np.arange(QB)[None, :]
    tiles = [np.full((QB, QB), MASKED_BUCKET, np.int32)]
    for d in range(nq):
        rel = QB * d + q - k
        tiles.append(np.where(rel >= 0, _t5_bucket_np(rel), MASKED_BUCKET).astype(np.int32))
    return np.stack(tiles)


def _mask_tiles():
    k = np.arange(QB)[:, None]
    q = np.arange(QB)[None, :]
    diag = np.where(k <= q, 0.0, NEG_INF)
    return np.stack([np.full((QB, QB), NEG_INF), diag, np.zeros((QB, QB))]).astype(np.float32)


def _bucket_tiles_dilated():
    q = np.arange(QB)[:, None]
    k = np.arange(2 * QB)[None, :]
    rel = q + QB - k
    tiles = []
    for (window, d) in DILATED_PATTERNS:
        in_band = (rel >= 0) & (rel <= window // d)
        for has_prev in (False, True):
            ok = in_band & (has_prev | (k >= QB))
            tiles.append(np.where(ok, _t5_bucket_np(rel * d), MASKED_BUCKET).astype(np.int32))
    return np.stack(tiles)


def _bias_expand_kernel(table_ref, bucket_ref, out_ref, *, head0, scale):
    h = pl.program_id(0) + head0
    bk = bucket_ref[...]
    acc = jnp.where(bk == MASKED_BUCKET, NEG_INF, 0.0)
    for b in range(NUM_BUCKETS):
        acc = jnp.where(bk == b, table_ref[b, h] * scale, acc)
    out_ref[...] = acc


def _bias_expand(table, buckets, head0, nheads, scale=1.0):
    n, r, c = buckets.shape
    return pl.pallas_call(
        functools.partial(_bias_expand_kernel, head0=head0, scale=scale),
        grid=(nheads,),
        in_specs=[pl.BlockSpec(memory_space=pltpu.SMEM),
                  pl.BlockSpec((n, r, c), lambda h: (0, 0, 0))],
        out_specs=pl.BlockSpec((None, n, r, c), lambda h: (h, 0, 0, 0)),
        out_shape=jax.ShapeDtypeStruct((nheads, n, r, c), F32),
        compiler_params=_params(("arbitrary",)),
        name="bias_expand",
    )(table, buckets)


def _in_proj_kernel(x_ref, g_ref, w_ref, o32_ref, o16_ref, ob_ref):
    x = x_ref[...]
    ms = jnp.mean(x * x, axis=-1, keepdims=True)
    xn = x * lax.rsqrt(ms + EPS) * g_ref[...]
    p = _dot(xn.astype(BF16), w_ref[...])
    o32_ref[...] = p[:, :NCOL32]
    o16_ref[...] = p[:, NCOL32:NCOL32 + NCOL16].astype(BF16)
    for s in range(B_SLABS):
        lo = NCOL32 + NCOL16 + s * LANES
        ob_ref[s] = p[:, lo:lo + LANES].astype(BF16)


def _in_proj(h, g, w, tm=256):
    bsz, seq, _ = h.shape
    return pl.pallas_call(
        _in_proj_kernel,
        grid=(bsz, seq // tm),
        in_specs=[pl.BlockSpec((None, tm, D_MODEL), lambda b, i: (b, i, 0)),
                  pl.BlockSpec((1, D_MODEL), lambda b, i: (0, 0)),
                  pl.BlockSpec((D_MODEL, NCOL), lambda b, i: (0, 0))],
        out_specs=[pl.BlockSpec((None, tm, NCOL32), lambda b, i: (b, i, 0)),
                   pl.BlockSpec((None, tm, NCOL16), lambda b, i: (b, i, 0)),
                   pl.BlockSpec((None, B_SLABS, tm, LANES), lambda b, i: (b, 0, i, 0))],
        out_shape=[jax.ShapeDtypeStruct((bsz, seq, NCOL32), F32),
                   jax.ShapeDtypeStruct((bsz, seq, NCOL16), BF16),
                   jax.ShapeDtypeStruct((bsz, B_SLABS, seq, LANES), BF16)],
        compiler_params=_params(("arbitrary", "arbitrary")),
        name="in_proj",
    )(h, g, w)


def _mla_prep_kernel(cq_ref, ckv_ref, kr_ref, krrot_ref, cos_ref, sin_ref, gq_ref, gkv_ref,
                     wq_ref, wqrot_ref, wk_ref, wv_ref, q_ref, k_ref, v_ref):
    cos = cos_ref[...]
    sin = sin_ref[...]
    cos4 = jnp.concatenate([cos] * A_HEADS, axis=1)
    sin4 = jnp.concatenate([sin] * A_HEADS, axis=1)
    cq = cq_ref[...]
    ms = jnp.sum(cq * cq, axis=-1, keepdims=True) * (1.0 / Q_LORA)
    nq = (cq * lax.rsqrt(ms + EPS) * gq_ref[...]).astype(BF16)
    q = _dot(nq, wq_ref[...]) * cos4 + _dot(nq, wqrot_ref[...]) * sin4
    q_ref[...] = (q * ((A_NOPE + A_ROPE) ** -0.5 * LOG2E)).astype(BF16)
    ckv = ckv_ref[...]
    ms = jnp.mean(ckv * ckv, axis=-1, keepdims=True)
    nkv = (ckv * lax.rsqrt(ms + EPS) * gkv_ref[...]).astype(BF16)
    kr = kr_ref[...] * cos + krrot_ref[...] * sin
    k = _dot(nkv, wk_ref[...]) + jnp.concatenate([kr] * A_HEADS, axis=1)
    k_ref[...] = k.astype(BF16)
    v_ref[...] = _dot_nt(wv_ref[...], nkv).astype(BF16)


def _mla_prep(p32, cos_t, sin_t, gq, gkv, wq, wqrot, wk, wv_t, tm=512):
    bsz, seq, _ = p32.shape
    w4 = A_HEADS * LANES
    row = lambda b, i: (b, i, 0)
    const = lambda b, i: (0, 0)
    return pl.pallas_call(
        _mla_prep_kernel,
        grid=(bsz, seq // tm),
        in_specs=[pl.BlockSpec((None, tm, 256), lambda b, i: (b, i, P_CQ // 256)),
                  pl.BlockSpec((None, tm, LANES), lambda b, i: (b, i, P_CKV // LANES)),
                  pl.BlockSpec((None, tm, LANES), lambda b, i: (b, i, P_KR // LANES)),
                  pl.BlockSpec((None, tm, LANES), lambda b, i: (b, i, P_KRROT // LANES)),
                  pl.BlockSpec((tm, LANES), lambda b, i: (i, 0)),
                  pl.BlockSpec((tm, LANES), lambda b, i: (i, 0)),
                  pl.BlockSpec((1, 256), const),
                  pl.BlockSpec((1, KV_LORA), const),
                  pl.BlockSpec((256, w4), const),
                  pl.BlockSpec((256, w4), const),
                  pl.BlockSpec((KV_LORA, w4), const),
                  pl.BlockSpec((A_HEADS * A_V, KV_LORA), const)],
        out_specs=[pl.BlockSpec((None, tm, w4), row), pl.BlockSpec((None, tm, w4), row),
                   pl.BlockSpec((None, A_HEADS * A_V, tm), lambda b, i: (b, 0, i))],
        out_shape=[jax.ShapeDtypeStruct((bsz, seq, w4), BF16), jax.ShapeDtypeStruct((bsz, seq, w4), BF16),
                   jax.ShapeDtypeStruct((bsz, A_HEADS * A_V, seq), BF16)],
        compiler_params=_params(("arbitrary", "arbitrary")),
        name="mla_prep",
    )(p32, p32, p32, p32, cos_t, sin_t, gq, gkv, wq, wqrot, wk, wv_t)


def _silu(g):
    return g * (1.0 / (1.0 + jnp.exp(-g)))


def _by_causal_width(i, seq, body):
    per = KV_CHUNK // QB
    for wb in range(seq // KV_CHUNK):
        pl.when(i // per == wb)(functools.partial(body, (wb + 1) * KV_CHUNK))


def _mask_tail_t(mask_ref, i, width):
    first = (width - KV_CHUNK) // QB
    return jnp.concatenate([mask_ref[jnp.clip(i - j, -1, 1) + 1] for j in range(first, width // QB)], axis=0)


def _col_reduce(x, op):
    rows, lanes = x.shape
    part = op(x.reshape(rows // COL_ACC_ROWS, COL_ACC_ROWS, lanes), axis=0)
    return op(part, axis=0, keepdims=True)


def _bias_col(bias_ref, h, i, width):
    return jnp.concatenate([bias_ref[h, jnp.maximum(i - j, -1) + 1] for j in range(width // QB)], axis=0)


def _lane_mask(width, seg, dtype):
    lane = lax.broadcasted_iota(jnp.int32, (1, LANES), 1)
    return jnp.where((lane >= seg * width) & (lane < (seg + 1) * width), 1.0, 0.0).astype(dtype)


def _mla_attn_kernel(q_ref, k_ref, vt_ref, g_ref, mask_ref, o_ref):
    i = pl.program_id(1)
    seq = k_ref.shape[0]

    def body(width):
        head_w = width - KV_CHUNK
        zero = jnp.zeros((QB, LANES), BF16)
        scores = []
        for g in range(A_HEADS // 2):
            q0 = q_ref[:, 2 * g * LANES:(2 * g + 1) * LANES]
            q1 = q_ref[:, (2 * g + 1) * LANES:(2 * g + 2) * LANES]
            qbd = jnp.concatenate([jnp.concatenate([q0, zero], axis=1),
                                   jnp.concatenate([zero, q1], axis=1)], axis=0)
            scores.append(_dot_nt(k_ref[:width, 2 * g * LANES:(2 * g + 2) * LANES], qbd))
        tail = _mask_tail_t(mask_ref, i, width)
        outs = []
        for g in range(A_HEADS // 2):
            es, ls = [], []
            for hh in range(2):
                s = scores[g][:, hh * QB:(hh + 1) * QB]
                s = jnp.concatenate([s[:head_w], s[head_w:] + tail], axis=0) if head_w else s + tail
                e = jnp.exp2(s - _col_reduce(s, jnp.max))
                ls.append(_col_reduce(e, jnp.sum))
                es.append(e.astype(BF16))
            ot = _dot(vt_ref[g * LANES:(g + 1) * LANES, :width], jnp.concatenate(es, axis=1))
            outs.append(ot[:A_V, :QB] / ls[0])
            outs.append(ot[A_V:, QB:] / ls[1])
        y_t = jnp.concatenate(outs, axis=0)
        y = jnp.concatenate([y_t[:LANES].T, y_t[LANES:].T], axis=1)
        o_ref[...] = (y * _silu(g_ref[...])).astype(BF16)

    _by_causal_width(i, seq, body)


def _mla_attn(q, k, v_t, p32, mask_tiles):
    bsz, seq, w4 = q.shape
    return pl.pallas_call(
        _mla_attn_kernel,
        grid=(bsz, seq // QB),
        in_specs=[pl.BlockSpec((None, QB, w4), lambda b, i: (b, i, 0)),
                  pl.BlockSpec((None, seq, w4), lambda b, i: (b, 0, 0)),
                  pl.BlockSpec((None, A_HEADS * A_V, seq), lambda b, i: (b, 0, 0)),
                  pl.BlockSpec((None, QB, 256), lambda b, i: (b, i, P_AG // 256)),
                  pl.BlockSpec((3, QB, QB), lambda b, i: (0, 0, 0))],
        out_specs=pl.BlockSpec((None, QB, BRANCH_WIDTH), lambda b, i: (b, i, 0)),
        out_shape=jax.ShapeDtypeStruct((bsz, seq, BRANCH_WIDTH), BF16),
        compiler_params=_params(("arbitrary", "arbitrary")),
        name="mla_attn",
    )(q, k, v_t, p32, mask_tiles)


def _transpose_values(v_ref, vt_ref):
    for j in range(v_ref.shape[0] // QB):
        for g in range(v_ref.shape[1] // LANES):
            tile = v_ref[j * QB:(j + 1) * QB, g * LANES:(g + 1) * LANES]
            vt_ref[g * LANES:(g + 1) * LANES, j * QB:(j + 1) * QB] = tile.T.astype(BF16)


def _diff_attn_kernel(q_ref, k_ref, v_ref, g_ref, bias_ref, lam_ref, subln_ref, o_ref, vt_ref, *, lambda_init):
    i = pl.program_id(1)
    seq = k_ref.shape[0]

    @pl.when(i == 0)
    def _():
        _transpose_values(v_ref, vt_ref)

    def body(width):
        lp = lam_ref[...]
        lam = (jnp.exp(jnp.sum(lp[0:1] * lp[1:2], axis=-1, keepdims=True))
               - jnp.exp(jnp.sum(lp[2:3] * lp[3:4], axis=-1, keepdims=True)) + lambda_init)
        scores = []
        for h in range(4):
            sl = slice((h // 2) * LANES, (h // 2 + 1) * LANES)
            qg = q_ref[:, sl] * (D_QK ** -0.5 * LOG2E)
            qcat = jnp.concatenate([(qg * _lane_mask(D_QK, 2 * (h % 2) + mm, F32)).astype(BF16)
                                    for mm in range(2)], axis=0)
            scores.append(_dot_nt(k_ref[:width, sl], qcat))
        outs = []
        for h in range(4):
            bias = _bias_col(bias_ref, h, i, width)
            es, ls = [], []
            for mm in range(2):
                s = scores[h][:, mm * QB:(mm + 1) * QB] + bias
                e = jnp.exp2(s - _col_reduce(s, jnp.max))
                ls.append(_col_reduce(e, jnp.sum))
                es.append(e.astype(BF16))
            ot = _dot(vt_ref[h * HEAD_DIM:(h + 1) * HEAD_DIM, :width], jnp.concatenate(es, axis=1))
            a = ot[:, :QB] / ls[0] - lam * (ot[:, QB:] / ls[1])
            ms = jnp.sum(a * a, axis=0, keepdims=True) * (1.0 / HEAD_DIM)
            outs.append(a * lax.rsqrt(ms + EPS))
        y_t = jnp.concatenate(outs, axis=0)
        y = jnp.concatenate([y_t[:LANES].T, y_t[LANES:].T], axis=1)
        o_ref[...] = (y * (subln_ref[...] * (1.0 - lambda_init)) * _silu(g_ref[...])).astype(BF16)

    _by_causal_width(i, seq, body)


def _diff_attn(p32, p16, bias, lam_params, subln, lambda_init):
    bsz, seq, _ = p32.shape
    nq = seq // QB
    return pl.pallas_call(
        functools.partial(_diff_attn_kernel, lambda_init=lambda_init),
        grid=(bsz, nq),
        in_specs=[pl.BlockSpec((None, QB, 256), lambda b, i: (b, i, P_DQ // 256)),
                  pl.BlockSpec((None, seq, 256), lambda b, i: (b, 0, Q_DK // 256)),
                  pl.BlockSpec((None, seq, 256), lambda b, i: (b, 0, P_DV // 256)),
                  pl.BlockSpec((None, QB, 256), lambda b, i: (b, i, P_DG // 256)),
                  pl.BlockSpec((4, nq + 1, QB, QB), lambda b, i: (0, 0, 0, 0)),
                  pl.BlockSpec((4, D_QK), lambda b, i: (0, 0)),
                  pl.BlockSpec((1, BRANCH_WIDTH), lambda b, i: (0, 0))],
        out_specs=pl.BlockSpec((None, QB, BRANCH_WIDTH), lambda b, i: (b, i, 0)),
        out_shape=jax.ShapeDtypeStruct((bsz, seq, BRANCH_WIDTH), BF16),
        scratch_shapes=[pltpu.VMEM((BRANCH_WIDTH, seq), BF16)],
        compiler_params=_params(("arbitrary", "arbitrary")),
        name="diff_attn",
    )(p32, p16, p32, p32, bias, lam_params, subln)


def _sortable_to_float(key):
    return pltpu.bitcast(jnp.where(key < 0, key ^ jnp.int32(0x7FFFFFFF), key), F32)


def _kth_largest(score_ref, width, k_top):
    def count_ge(key):
        thr = _sortable_to_float(key)
        return _col_reduce(jnp.where(score_ref[:width, :] >= thr, 1.0, 0.0), jnp.sum)

    int_min = jnp.full((1, QB), -2 ** 31, jnp.int32)
    zero = jnp.zeros((1, QB), jnp.int32)
    t = jnp.where(count_ge(zero) >= k_top, zero, int_min)

    def step(it, t):
        cand = t + (jnp.int32(1) << (30 - it))
        return jnp.where(count_ge(cand) >= k_top, cand, t)

    return _sortable_to_float(lax.fori_loop(0, 31, step, t))


def _first_ties(eq, need, row):
    eqf = jnp.where(eq, 1.0, 0.0)
    nbits = int(eq.shape[0]).bit_length()

    def body(it, j):
        cand = j + (jnp.int32(1) << (nbits - 1 - it))
        cnt = _col_reduce(jnp.where(row < cand, eqf, 0.0), jnp.sum)
        return jnp.where(cnt <= need, cand, j)

    j = lax.fori_loop(0, nbits, body, jnp.zeros((1, eq.shape[1]), jnp.int32))
    return eq & (row < j)


def _sparse_attn_kernel(q_ref, k_ref, v_ref, qi_ref, ki_ref, wi_ref, g_ref, bias_ref, o_ref,
                        score_ref, neg_ref, vt_ref, *, k_top):
    i = pl.program_id(1)
    seq = k_ref.shape[0]

    @pl.when(i == 0)
    def _():
        _transpose_values(v_ref, vt_ref)

    def body(width):
        head_w = width - KV_CHUNK
        ki = ki_ref[:width, :]
        w_t = wi_ref[...].T * (IDX_DIM ** -0.5 * IDX_HEADS ** -0.5)
        score = None
        for h in range(0, IDX_HEADS, 2):
            qg = qi_ref[:, (h // 4) * LANES:(h // 4 + 1) * LANES]
            qcat = jnp.concatenate([qg * _lane_mask(IDX_DIM, h % 4, BF16),
                                    qg * _lane_mask(IDX_DIM, h % 4 + 1, BF16)], axis=0)
            logit = _dot_nt(ki, qcat)
            term = (jnp.maximum(logit[:, :QB], 0.0) * w_t[h:h + 1]
                    + jnp.maximum(logit[:, QB:], 0.0) * w_t[h + 1:h + 2])
            score = term if score is None else score + term
        s_idx = head_w + lax.broadcasted_iota(jnp.int32, (KV_CHUNK, QB), 0)
        tail_ok = s_idx <= i * QB + lax.broadcasted_iota(jnp.int32, (KV_CHUNK, QB), 1)
        if head_w:
            score_ref[:head_w, :] = score[:head_w]
        score_ref[head_w:width, :] = jnp.where(tail_ok, score[head_w:], NEG_INF)
        s2s = []
        for g in range(2):
            sl = slice(g * LANES, (g + 1) * LANES)
            qg = q_ref[:, sl] * (HEAD_DIM ** -0.5 * LOG2E)
            qcat = jnp.concatenate([(qg * _lane_mask(HEAD_DIM, hh, F32)).astype(BF16) for hh in range(2)], axis=0)
            s2s.append(_dot_nt(k_ref[:width, sl], qcat))
        thr = _kth_largest(score_ref, width, k_top)
        keep_tail = (score_ref[head_w:width, :] >= thr) & tail_ok
        cnt = _col_reduce(jnp.where(keep_tail, 1.0, 0.0), jnp.sum)
        neg_ref[head_w:width, :] = jnp.where(keep_tail, 0.0, NEG_INF)
        if head_w:
            keep_head = score_ref[:head_w, :] >= thr
            cnt = cnt + _col_reduce(jnp.where(keep_head, 1.0, 0.0), jnp.sum)
            neg_ref[:head_w, :] = jnp.where(keep_head, 0.0, NEG_INF)

        @pl.when(jnp.max(cnt) > k_top)
        def _():
            x = score_ref[:width, :]
            gt = x > thr
            need = k_top - _col_reduce(jnp.where(gt, 1.0, 0.0), jnp.sum)
            row = lax.broadcasted_iota(jnp.int32, (width, QB), 0)
            neg_ref[:width, :] = jnp.where(gt | _first_ties(x == thr, need, row), 0.0, NEG_INF)

        outs = []
        for g in range(2):
            sl = slice(g * LANES, (g + 1) * LANES)
            s2 = s2s[g]
            es, ls = [], []
            for hh in range(2):
                s = s2[:, hh * QB:(hh + 1) * QB] + (_bias_col(bias_ref, 2 * g + hh, i, width) + neg_ref[:width, :])
                e = jnp.exp2(s - _col_reduce(s, jnp.max))
                ls.append(_col_reduce(e, jnp.sum))
                es.append(e.astype(BF16))
            ot = _dot(vt_ref[sl, :width], jnp.concatenate(es, axis=1))
            outs.append(ot[:HEAD_DIM, :QB] / ls[0])
            outs.append(ot[HEAD_DIM:, QB:] / ls[1])
        y_t = jnp.concatenate(outs, axis=0)
        y = jnp.concatenate([y_t[:LANES].T, y_t[LANES:].T], axis=1)
        o_ref[...] = (y * _silu(g_ref[...])).astype(BF16)

    _by_causal_width(i, seq, body)


def _sparse_attn(p32, p16, bias, k_top):
    bsz, seq, _ = p32.shape
    nq = seq // QB
    return pl.pallas_call(
        functools.partial(_sparse_attn_kernel, k_top=k_top),
        grid=(bsz, nq),
        in_specs=[pl.BlockSpec((None, QB, 256), lambda b, i: (b, i, P_CQ2 // 256)),
                  pl.BlockSpec((None, seq, 256), lambda b, i: (b, 0, Q_CK // 256)),
                  pl.BlockSpec((None, seq, 256), lambda b, i: (b, 0, P_CV // 256)),
                  pl.BlockSpec((None, QB, 256), lambda b, i: (b, i, Q_CQI // 256)),
                  pl.BlockSpec((None, seq, LANES), lambda b, i: (b, 0, Q_CKI // LANES)),
                  pl.BlockSpec((None, QB, LANES), lambda b, i: (b, i, P_CWI // LANES)),
                  pl.BlockSpec((None, QB, 256), lambda b, i: (b, i, P_CG // 256)),
                  pl.BlockSpec((4, nq + 1, QB, QB), lambda b, i: (0, 0, 0, 0))],
        out_specs=pl.BlockSpec((None, QB, BRANCH_WIDTH), lambda b, i: (b, i, 0)),
        out_shape=jax.ShapeDtypeStruct((bsz, seq, BRANCH_WIDTH), BF16),
        scratch_shapes=[pltpu.VMEM((seq, QB), F32), pltpu.VMEM((seq, QB), F32),
                        pltpu.VMEM((BRANCH_WIDTH, seq), BF16)],
        compiler_params=_params(("arbitrary", "arbitrary")),
        name="sparse_attn",
    )(p32, p16, p32, p16, p16, p32, p32, bias)


def _dilated_kernel(x1_ref, x4_ref, x16_ref, g_ref, bias_ref, o_ref, m_ref, l_ref, acc_ref):
    x_refs = {1: x1_ref, 4: x4_ref, 16: x16_ref}
    first = lax.broadcasted_iota(jnp.int32, (QB, LANES), 1) < HEAD_DIM
    hmask = [_lane_mask(HEAD_DIM, hh, BF16) for hh in range(2)]

    def run(p, d, tiles, sink):
        x = x_refs[d]
        work = []
        for cur, prev, r, variant in tiles:
            lanes = slice(r * LANES, (r + 1) * LANES)
            for g in range(2):
                q = x[g, cur, lanes]
                kcat = x[2 + g, cur, lanes]
                vcat = x[4 + g, cur, lanes]
                if prev is not None:
                    kcat = jnp.concatenate([x[2 + g, prev, lanes], kcat], axis=0)
                    vcat = jnp.concatenate([x[4 + g, prev, lanes], vcat], axis=0)
                scores = []
                for hh in range(2):
                    if prev is not None:
                        bias = bias_ref[2 * g + hh, 2 * p + variant]
                    else:
                        bias = bias_ref[2 * g + hh, 2 * p, :, QB:]
                    scores.append(_dot_nt(q * hmask[hh], kcat) + bias)
                work.append((vcat, scores))
        for n, (vcat, scores) in enumerate(work):
            ms, ls, accs = [], [], []
            for s in scores:
                m = jnp.max(s, axis=-1, keepdims=True)
                e = jnp.exp2(s - m)
                ms.append(m)
                ls.append(jnp.sum(e, axis=-1, keepdims=True))
                accs.append(_dot(e.astype(BF16), vcat))
            sink(n // 2, n % 2, jnp.where(first, ms[0], ms[1]), jnp.where(first, ls[0], ls[1]),
                 jnp.where(first, accs[0], accs[1]))

    def store_stats(slot, d, starts):
        def sink(t, g, m, l, a):
            rows = pl.ds(starts[t], QB, stride=d)
            m_ref[slot, g, rows, :] = m
            l_ref[slot, g, rows, :] = l
            acc_ref[slot, g, rows, :] = a
        return sink

    rows0 = slice(0, QB)
    for r0 in range(0, 16, DIL_UNROLL):
        run(2, 16, [(rows0, None, r, 0) for r in range(r0, r0 + DIL_UNROLL)],
            store_stats(1, 16, list(range(r0, r0 + DIL_UNROLL))))

    def step4(j, carry):
        cur = pl.ds(pl.multiple_of(j * QB, QB), QB)
        prev = pl.ds(pl.multiple_of(jnp.maximum(j - 1, 0) * QB, QB), QB)
        run(1, 4, [(cur, prev, r, jnp.minimum(j, 1)) for r in range(4)],
            store_stats(0, 4, [r + 4 * QB * j for r in range(4)]))
        return carry

    lax.fori_loop(0, 4, step4, 0)

    def step1(n0, carry):
        blocks = [n0 * DIL_UNROLL + u for u in range(DIL_UNROLL)]

        def sink(t, g, m, l, a):
            rows = pl.ds(pl.multiple_of(blocks[t] * QB, QB), QB)
            lanes = slice(g * LANES, (g + 1) * LANES)
            m4, m16 = m_ref[0, g, rows, :], m_ref[1, g, rows, :]
            m_tot = jnp.maximum(m, jnp.maximum(m4, m16))
            w1, w4, w16 = jnp.exp2(m - m_tot), jnp.exp2(m4 - m_tot), jnp.exp2(m16 - m_tot)
            num = w1 * a + w4 * acc_ref[0, g, rows, :] + w16 * acc_ref[1, g, rows, :]
            den = w1 * l + w4 * l_ref[0, g, rows, :] + w16 * l_ref[1, g, rows, :]
            o_ref[rows, lanes] = (num / den * _silu(g_ref[rows, lanes])).astype(BF16)

        tiles = []
        for j in blocks:
            cur = pl.ds(pl.multiple_of(j * QB, QB), QB)
            prev = pl.ds(pl.multiple_of(jnp.maximum(j - 1, 0) * QB, QB), QB)
            tiles.append((cur, prev, 0, jnp.minimum(j, 1)))
        run(0, 1, tiles, sink)
        return carry

    lax.fori_loop(0, g_ref.shape[0] // QB // DIL_UNROLL, step1, 0)


def _dilated_attn(pb, p32, bias):
    bsz, _, seq, _ = pb.shape

    def view(d):
        return pl.BlockSpec((None, B_SLABS, seq // d, d * LANES), lambda b: (b, 0, 0, 0))

    stats = pltpu.VMEM((2, 2, seq, LANES), F32)
    return pl.pallas_call(
        _dilated_kernel,
        grid=(bsz,),
        in_specs=[view(1), view(4), view(16),
                  pl.BlockSpec((None, seq, 256), lambda b: (b, 0, P_BG // 256)),
                  pl.BlockSpec((4, 2 * len(DILATED_PATTERNS), QB, 2 * QB), lambda b: (0, 0, 0, 0))],
        out_specs=pl.BlockSpec((None, seq, BRANCH_WIDTH), lambda b: (b, 0, 0)),
        out_shape=jax.ShapeDtypeStruct((bsz, seq, BRANCH_WIDTH), BF16),
        scratch_shapes=[stats, stats, stats],
        compiler_params=_params(("arbitrary",)),
        name="dilated_attn",
    )(pb, pb.reshape(bsz, B_SLABS, seq // 4, 4 * LANES), pb.reshape(bsz, B_SLABS, seq // 16, 16 * LANES),
      p32, bias)


def _out_proj_kernel(ya_ref, yb_ref, yc_ref, yd_ref, w_ref, h_ref, g_ref, o_ref):
    y = None
    for n, ref in enumerate((ya_ref, yb_ref, yc_ref, yd_ref)):
        t = _dot(ref[...], w_ref[n * BRANCH_WIDTH:(n + 1) * BRANCH_WIDTH, :])
        y = t if y is None else y + t
    ms = jnp.mean(y * y, axis=-1, keepdims=True)
    o_ref[...] = h_ref[...] + y * lax.rsqrt(ms + EPS) * g_ref[...]


def _out_proj(ya, yb, yc, yd, w, h, g, tm=512):
    bsz, seq, _ = h.shape
    yspec = pl.BlockSpec((None, tm, BRANCH_WIDTH), lambda b, i: (b, i, 0))
    return pl.pallas_call(
        _out_proj_kernel,
        grid=(bsz, seq // tm),
        in_specs=[yspec, yspec, yspec, yspec,
                  pl.BlockSpec((4 * BRANCH_WIDTH, D_MODEL), lambda b, i: (0, 0)),
                  pl.BlockSpec((None, tm, D_MODEL), lambda b, i: (b, i, 0)),
                  pl.BlockSpec((1, D_MODEL), lambda b, i: (0, 0))],
        out_specs=pl.BlockSpec((None, tm, D_MODEL), lambda b, i: (b, i, 0)),
        out_shape=jax.ShapeDtypeStruct(h.shape, F32),
        compiler_params=_params(("arbitrary", "arbitrary")),
        name="out_proj",
    )(ya, yb, yc, yd, w, h, g)


def kernel(x, w_in, w_out, norm_pre, norm_post, mla_q_norm, mla_kv_norm, mla_w_uq, mla_w_ukv,
           diff_lambda, diff_subln, rel_bias):
    bsz, seq, _ = x.shape
    depth = w_in.shape[0]
    nq = seq // QB
    k_top = min(IDX_TOPK_MAX, seq // 4)
    cos_t, sin_t = _rope_tables(seq)
    bias_b = _bias_expand(rel_bias, jnp.asarray(_bucket_tiles_dilated()), 0, 4, LOG2E)
    causal_buckets_t = jnp.asarray(np.swapaxes(_bucket_tiles_causal(nq), 1, 2))
    bias_c = _bias_expand(rel_bias, causal_buckets_t, 4, 4, LOG2E)
    bias_d = _bias_expand(rel_bias, causal_buckets_t, 8, 4, LOG2E)
    mask_tiles = jnp.asarray(_mask_tiles())
    h = x
    for layer in range(depth):
        w_arr = _arrange_w_in(w_in[layer]).astype(BF16)
        wq, wqrot, wk, wv = _arrange_mla(mla_w_uq[layer], mla_w_ukv[layer])
        gq = jnp.concatenate([mla_q_norm[layer], jnp.ones((256 - Q_LORA,), F32)])[None, :]
        gkv = mla_kv_norm[layer][None, :]
        p32, p16, pb = _in_proj(h, norm_pre[layer][None, :], w_arr)
        qa, ka, va = _mla_prep(p32, cos_t, sin_t, gq, gkv, wq, wqrot, wk, wv)
        y_a = _mla_attn(qa, ka, va, p32, mask_tiles)
        y_b = _dilated_attn(pb, p32, bias_b)
        y_c = _sparse_attn(p32, p16, bias_c, k_top)
        lambda_init = 0.8 - 0.6 * math.exp(-0.3 * layer)
        subln = jnp.tile(diff_subln[layer], BRANCH_WIDTH // HEAD_DIM)[None, :]
        y_d = _diff_attn(p32, p16, bias_d, diff_lambda[layer], subln, lambda_init)
        h = _out_proj(y_a, y_b, y_c, y_d, w_out[layer].astype(BF16), h, norm_post[layer][None, :])
    return h
```

```python
import functools
import math

import jax
import jax.numpy as jnp
import numpy as np
from jax import lax
from jax.experimental import pallas as pl
from jax.experimental.pallas import tpu as pltpu

F32 = jnp.float32
BF16 = jnp.bfloat16

D_MODEL = 1024
A_HEADS, A_NOPE, A_ROPE, A_V = 4, 64, 32, 64
Q_LORA, KV_LORA = 192, 128
ROPE_THETA = 10000.0
HEAD_DIM = 64
DILATED_PATTERNS = ((128, 1), (512, 4), (2048, 16))
IDX_HEADS, IDX_DIM, IDX_TOPK_MAX = 8, 32, 256
D_QK = 32
BRANCH_WIDTH = 256
NUM_BUCKETS, MAX_DISTANCE = 32, 2048
NEG_INF = -1e30
EPS = 1e-6
LOG2E = math.log2(math.e)
KV_CHUNK = 256
COL_ACC_ROWS = 64
DIL_UNROLL = 4
LANES = 128
QB = 128
VMEM_LIMIT = 56 * 1024 * 1024

_SPLIT = (Q_LORA, KV_LORA, A_ROPE, 256, 256, 256, 256, 256, 256, 256, 256, IDX_HEADS * IDX_DIM, IDX_DIM,
          IDX_HEADS, 256, 256, 256, 256, 256)
_OFF = np.concatenate([[0], np.cumsum(_SPLIT)]).tolist()
(_A_CQ, _A_CKV, _A_KR, _A_G, _B_Q, _B_K, _B_V, _B_G, _C_Q, _C_K, _C_V, _C_QI, _C_KI, _C_WI, _C_G,
 _D_Q, _D_K, _D_V, _D_G) = range(19)

P_CQ, P_AG, P_CKV, P_KR, P_KRROT, P_CWI = 0, 256, 512, 640, 768, 896
P_BG, P_CQ2, P_CV, P_CG, P_DQ, P_DV, P_DG = 1024, 1280, 1536, 1792, 2048, 2304, 2560
NCOL32 = 2816
Q_CK, Q_CQI, Q_DK, Q_CKI = 0, 256, 512, 768
NCOL16 = 896
B_SLABS = 6
NCOL = NCOL32 + NCOL16 + B_SLABS * LANES


def _dot(a, b):
    return jnp.dot(a, b, preferred_element_type=F32)


def _dot_nt(a, b):
    return lax.dot_general(a, b, (((1,), (1,)), ((), ())), preferred_element_type=F32)


def _params(sem):
    return pltpu.CompilerParams(dimension_semantics=sem, vmem_limit_bytes=VMEM_LIMIT)


def _rot_cols(w):
    half = w.shape[-1] // 2
    return jnp.concatenate([-w[..., half:], w[..., :half]], axis=-1)


def _arrange_w_in(w):
    def seg(i):
        return w[:, _OFF[i]:_OFF[i + 1]]

    def z(n):
        return jnp.zeros((w.shape[0], n), w.dtype)

    kr = seg(_A_KR)
    cols = [
        seg(_A_CQ), z(64),
        seg(_A_G),
        seg(_A_CKV),
        z(64), kr, z(32),
        z(64), _rot_cols(kr), z(32),
        seg(_C_WI), z(LANES - IDX_HEADS),
        seg(_B_G), seg(_C_Q), seg(_C_V), seg(_C_G), seg(_D_Q), seg(_D_V), seg(_D_G),
        seg(_C_K), seg(_C_QI), seg(_D_K),
        seg(_C_KI), seg(_C_KI), seg(_C_KI), seg(_C_KI),
        seg(_B_Q) * (HEAD_DIM ** -0.5 * LOG2E), seg(_B_K), seg(_B_V),
    ]
    out = jnp.concatenate(cols, axis=1)
    assert out.shape[1] == NCOL
    return out


def _arrange_mla(w_uq, w_ukv):
    wq = w_uq.reshape(Q_LORA, A_HEADS, A_NOPE + A_ROPE)
    nope, rope = wq[..., :A_NOPE], wq[..., A_NOPE:]
    zq = jnp.zeros((Q_LORA, A_HEADS, LANES - A_NOPE - A_ROPE), w_uq.dtype)
    wq_main = jnp.concatenate([nope, rope, zq], axis=-1).reshape(Q_LORA, A_HEADS * LANES)
    wq_rot = jnp.concatenate([jnp.zeros_like(nope), _rot_cols(rope), zq], axis=-1).reshape(Q_LORA, A_HEADS * LANES)
    pad = jnp.zeros((256 - Q_LORA, A_HEADS * LANES), w_uq.dtype)
    wq_main = jnp.concatenate([wq_main, pad], axis=0)
    wq_rot = jnp.concatenate([wq_rot, pad], axis=0)
    wkv = w_ukv.reshape(KV_LORA, A_HEADS, A_NOPE + A_V)
    knope, v = wkv[..., :A_NOPE], wkv[..., A_NOPE:]
    wk = jnp.concatenate([knope, jnp.zeros_like(knope)], axis=-1).reshape(KV_LORA, A_HEADS * LANES)
    wv_t = v.reshape(KV_LORA, A_HEADS * A_V).T
    return wq_main.astype(BF16), wq_rot.astype(BF16), wk.astype(BF16), wv_t.astype(BF16)


def _rope_tables(seq):
    inv = ROPE_THETA ** (-jnp.arange(0, A_ROPE, 2, dtype=F32) / A_ROPE)
    ang = jnp.arange(seq, dtype=F32)[:, None] * inv[None, :]
    cos, sin = jnp.cos(ang), jnp.sin(ang)
    one = jnp.ones((seq, A_NOPE), F32)
    zero = jnp.zeros((seq, LANES - A_NOPE - A_ROPE), F32)
    cos_t = jnp.concatenate([one, cos, cos, zero], axis=1)
    sin_t = jnp.concatenate([jnp.zeros_like(one), sin, sin, zero], axis=1)
    return cos_t, sin_t


def _t5_bucket_np(rel):
    n = np.maximum(rel, 0)
    max_exact = NUM_BUCKETS // 2
    nf = np.maximum(n, max_exact).astype(np.float64)
    large = max_exact + (np.log(nf / max_exact) / math.log(MAX_DISTANCE / max_exact)
                         * (NUM_BUCKETS - max_exact)).astype(np.int32)
    large = np.minimum(large, NUM_BUCKETS - 1)
    return np.where(n < max_exact, n, large).astype(np.int32)


MASKED_BUCKET = NUM_BUCKETS


def _bucket_tiles_causal(nq):
    q = np.arange(QB)[:, None]
    k = np.arange(QB)[None, :]
    tiles = [np.full((QB, QB), MASKED_BUCKET, np.int32)]
    for d in range(nq):
        rel = QB * d + q - k
        tiles.append(np.where(rel >= 0, _t5_bucket_np(rel), MASKED_BUCKET).astype(np.int32))
    return np.stack(tiles)


def _mask_tiles():
    k = np.arange(QB)[:, None]
    q = np.arange(QB)[None, :]
    diag = np.where(k <= q, 0.0, NEG_INF)
    return np.stack([np.full((QB, QB), NEG_INF), diag, np.zeros((QB, QB))]).astype(np.float32)


def _bucket_tiles_dilated():
    q = np.arange(QB)[:, None]
    k = np.arange(2 * QB)[None, :]
    rel = q + QB - k
    tiles = []
    for (window, d) in DILATED_PATTERNS:
        in_band = (rel >= 0) & (rel <= window // d)
        for has_prev in (False, True):
            ok = in_band & (has_prev | (k >= QB))
            tiles.append(np.where(ok, _t5_bucket_np(rel * d), MASKED_BUCKET).astype(np.int32))
    return np.stack(tiles)


def _bias_expand_kernel(table_ref, bucket_ref, out_ref, *, head0, scale):
    h = pl.program_id(0) + head0
    bk = bucket_ref[...]
    acc = jnp.where(bk == MASKED_BUCKET, NEG_INF, 0.0)
    for b in range(NUM_BUCKETS):
        acc = jnp.where(bk == b, table_ref[b, h] * scale, acc)
    out_ref[...] = acc


def _bias_expand(table, buckets, head0, nheads, scale=1.0):
    n, r, c = buckets.shape
    return pl.pallas_call(
        functools.partial(_bias_expand_kernel, head0=head0, scale=scale),
        grid=(nheads,),
        in_specs=[pl.BlockSpec(memory_space=pltpu.SMEM),
                  pl.BlockSpec((n, r, c), lambda h: (0, 0, 0))],
        out_specs=pl.BlockSpec((None, n, r, c), lambda h: (h, 0, 0, 0)),
        out_shape=jax.ShapeDtypeStruct((nheads, n, r, c), F32),
        compiler_params=_params(("arbitrary",)),
        name="bias_expand",
    )(table, buckets)


def _in_proj_kernel(x_ref, g_ref, w_ref, o32_ref, o16_ref, ob_ref):
    x = x_ref[...]
    ms = jnp.mean(x * x, axis=-1, keepdims=True)
    xn = x * lax.rsqrt(ms + EPS) * g_ref[...]
    p = _dot(xn.astype(BF16), w_ref[...])
    o32_ref[...] = p[:, :NCOL32]
    o16_ref[...] = p[:, NCOL32:NCOL32 + NCOL16].astype(BF16)
    for s in range(B_SLABS):
        lo = NCOL32 + NCOL16 + s * LANES
        ob_ref[s] = p[:, lo:lo + LANES]


def _in_proj(h, g, w, tm=256):
    bsz, seq, _ = h.shape
    return pl.pallas_call(
        _in_proj_kernel,
        grid=(bsz, seq // tm),
        in_specs=[pl.BlockSpec((None, tm, D_MODEL), lambda b, i: (b, i, 0)),
                  pl.BlockSpec((1, D_MODEL), lambda b, i: (0, 0)),
                  pl.BlockSpec((D_MODEL, NCOL), lambda b, i: (0, 0))],
        out_specs=[pl.BlockSpec((None, tm, NCOL32), lambda b, i: (b, i, 0)),
                   pl.BlockSpec((None, tm, NCOL16), lambda b, i: (b, i, 0)),
                   pl.BlockSpec((None, B_SLABS, tm, LANES), lambda b, i: (b, 0, i, 0))],
        out_shape=[jax.ShapeDtypeStruct((bsz, seq, NCOL32), F32),
                   jax.ShapeDtypeStruct((bsz, seq, NCOL16), BF16),
                   jax.ShapeDtypeStruct((bsz, B_SLABS, seq, LANES), F32)],
        compiler_params=_params(("arbitrary", "arbitrary")),
        name="in_proj",
    )(h, g, w)


def _mla_prep_kernel(cq_ref, ckv_ref, kr_ref, krrot_ref, cos_ref, sin_ref, gq_ref, gkv_ref,
                     wq_ref, wqrot_ref, wk_ref, wv_ref, q_ref, k_ref, v_ref):
    cos = cos_ref[...]
    sin = sin_ref[...]
    cos4 = jnp.concatenate([cos] * A_HEADS, axis=1)
    sin4 = jnp.concatenate([sin] * A_HEADS, axis=1)
    cq = cq_ref[...]
    ms = jnp.sum(cq * cq, axis=-1, keepdims=True) * (1.0 / Q_LORA)
    nq = (cq * lax.rsqrt(ms + EPS) * gq_ref[...]).astype(BF16)
    q = _dot(nq, wq_ref[...]) * cos4 + _dot(nq, wqrot_ref[...]) * sin4
    q_ref[...] = (q * ((A_NOPE + A_ROPE) ** -0.5 * LOG2E)).astype(BF16)
    ckv = ckv_ref[...]
    ms = jnp.mean(ckv * ckv, axis=-1, keepdims=True)
    nkv = (ckv * lax.rsqrt(ms + EPS) * gkv_ref[...]).astype(BF16)
    kr = kr_ref[...] * cos + krrot_ref[...] * sin
    k = _dot(nkv, wk_ref[...]) + jnp.concatenate([kr] * A_HEADS, axis=1)
    k_ref[...] = k.astype(BF16)
    v_ref[...] = _dot_nt(wv_ref[...], nkv).astype(BF16)


def _mla_prep(p32, cos_t, sin_t, gq, gkv, wq, wqrot, wk, wv_t, tm=512):
    bsz, seq, _ = p32.shape
    w4 = A_HEADS * LANES
    row = lambda b, i: (b, i, 0)
    const = lambda b, i: (0, 0)
    return pl.pallas_call(
        _mla_prep_kernel,
        grid=(bsz, seq // tm),
        in_specs=[pl.BlockSpec((None, tm, 256), lambda b, i: (b, i, P_CQ // 256)),
                  pl.BlockSpec((None, tm, LANES), lambda b, i: (b, i, P_CKV // LANES)),
                  pl.BlockSpec((None, tm, LANES), lambda b, i: (b, i, P_KR // LANES)),
                  pl.BlockSpec((None, tm, LANES), lambda b, i: (b, i, P_KRROT // LANES)),
                  pl.BlockSpec((tm, LANES), lambda b, i: (i, 0)),
                  pl.BlockSpec((tm, LANES), lambda b, i: (i, 0)),
                  pl.BlockSpec((1, 256), const),
                  pl.BlockSpec((1, KV_LORA), const),
                  pl.BlockSpec((256, w4), const),
                  pl.BlockSpec((256, w4), const),
                  pl.BlockSpec((KV_LORA, w4), const),
                  pl.BlockSpec((A_HEADS * A_V, KV_LORA), const)],
        out_specs=[pl.BlockSpec((None, tm, w4), row), pl.BlockSpec((None, tm, w4), row),
                   pl.BlockSpec((None, A_HEADS * A_V, tm), lambda b, i: (b, 0, i))],
        out_shape=[jax.ShapeDtypeStruct((bsz, seq, w4), BF16), jax.ShapeDtypeStruct((bsz, seq, w4), BF16),
                   jax.ShapeDtypeStruct((bsz, A_HEADS * A_V, seq), BF16)],
        compiler_params=_params(("arbitrary", "arbitrary")),
        name="mla_prep",
    )(p32, p32, p32, p32, cos_t, sin_t, gq, gkv, wq, wqrot, wk, wv_t)


def _silu(g):
    return g * (1.0 / (1.0 + jnp.exp(-g)))


def _by_causal_width(i, seq, body):
    per = KV_CHUNK // QB
    for wb in range(seq // KV_CHUNK):
        pl.when(i // per == wb)(functools.partial(body, (wb + 1) * KV_CHUNK))


def _mask_tail_t(mask_ref, i, width):
    first = (width - KV_CHUNK) // QB
    return jnp.concatenate([mask_ref[jnp.clip(i - j, -1, 1) + 1] for j in range(first, width // QB)], axis=0)


def _col_reduce(x, op):
    rows, lanes = x.shape
    part = op(x.reshape(rows // COL_ACC_ROWS, COL_ACC_ROWS, lanes), axis=0)
    return op(part, axis=0, keepdims=True)


def _bias_col(bias_ref, h, i, width):
    return jnp.concatenate([bias_ref[h, jnp.maximum(i - j, -1) + 1] for j in range(width // QB)], axis=0)


def _lane_mask(width, seg, dtype):
    lane = lax.broadcasted_iota(jnp.int32, (1, LANES), 1)
    return jnp.where((lane >= seg * width) & (lane < (seg + 1) * width), 1.0, 0.0).astype(dtype)


def _mla_attn_kernel(q_ref, k_ref, vt_ref, g_ref, mask_ref, o_ref):
    i = pl.program_id(1)
    seq = k_ref.shape[0]

    def body(width):
        head_w = width - KV_CHUNK
        zero = jnp.zeros((QB, LANES), BF16)
        scores = []
        for g in range(A_HEADS // 2):
            q0 = q_ref[:, 2 * g * LANES:(2 * g + 1) * LANES]
            q1 = q_ref[:, (2 * g + 1) * LANES:(2 * g + 2) * LANES]
            qbd = jnp.concatenate([jnp.concatenate([q0, zero], axis=1),
                                   jnp.concatenate([zero, q1], axis=1)], axis=0)
            scores.append(_dot_nt(k_ref[:width, 2 * g * LANES:(2 * g + 2) * LANES], qbd))
        tail = _mask_tail_t(mask_ref, i, width)
        outs = []
        for g in range(A_HEADS // 2):
            es, ls = [], []
            for hh in range(2):
                s = scores[g][:, hh * QB:(hh + 1) * QB]
                s = jnp.concatenate([s[:head_w], s[head_w:] + tail], axis=0) if head_w else s + tail
                e = jnp.exp2(s - _col_reduce(s, jnp.max))
                ls.append(_col_reduce(e, jnp.sum))
                es.append(e.astype(BF16))
            ot = _dot(vt_ref[g * LANES:(g + 1) * LANES, :width], jnp.concatenate(es, axis=1))
            outs.append(ot[:A_V, :QB] / ls[0])
            outs.append(ot[A_V:, QB:] / ls[1])
        y_t = jnp.concatenate(outs, axis=0)
        y = jnp.concatenate([y_t[:LANES].T, y_t[LANES:].T], axis=1)
        o_ref[...] = (y * _silu(g_ref[...])).astype(BF16)

    _by_causal_width(i, seq, body)


def _mla_attn(q, k, v_t, p32, mask_tiles):
    bsz, seq, w4 = q.shape
    return pl.pallas_call(
        _mla_attn_kernel,
        grid=(bsz, seq // QB),
        in_specs=[pl.BlockSpec((None, QB, w4), lambda b, i: (b, i, 0)),
                  pl.BlockSpec((None, seq, w4), lambda b, i: (b, 0, 0)),
                  pl.BlockSpec((None, A_HEADS * A_V, seq), lambda b, i: (b, 0, 0)),
                  pl.BlockSpec((None, QB, 256), lambda b, i: (b, i, P_AG // 256)),
                  pl.BlockSpec((3, QB, QB), lambda b, i: (0, 0, 0))],
        out_specs=pl.BlockSpec((None, QB, BRANCH_WIDTH), lambda b, i: (b, i, 0)),
        out_shape=jax.ShapeDtypeStruct((bsz, seq, BRANCH_WIDTH), BF16),
        compiler_params=_params(("arbitrary", "arbitrary")),
        name="mla_attn",
    )(q, k, v_t, p32, mask_tiles)


def _transpose_values(v_ref, vt_ref):
    for j in range(v_ref.shape[0] // QB):
        for g in range(v_ref.shape[1] // LANES):
            tile = v_ref[j * QB:(j + 1) * QB, g * LANES:(g + 1) * LANES]
            vt_ref[g * LANES:(g + 1) * LANES, j * QB:(j + 1) * QB] = tile.T.astype(BF16)


def _diff_attn_kernel(q_ref, k_ref, v_ref, g_ref, bias_ref, lam_ref, subln_ref, o_ref, vt_ref, *, lambda_init):
    i = pl.program_id(1)
    seq = k_ref.shape[0]

    @pl.when(i == 0)
    def _():
        _transpose_values(v_ref, vt_ref)

    def body(width):
        lp = lam_ref[...]
        lam = (jnp.exp(jnp.sum(lp[0:1] * lp[1:2], axis=-1, keepdims=True))
               - jnp.exp(jnp.sum(lp[2:3] * lp[3:4], axis=-1, keepdims=True)) + lambda_init)
        scores = []
        for h in range(4):
            sl = slice((h // 2) * LANES, (h // 2 + 1) * LANES)
            qg = q_ref[:, sl] * (D_QK ** -0.5 * LOG2E)
            qcat = jnp.concatenate([(qg * _lane_mask(D_QK, 2 * (h % 2) + mm, F32)).astype(BF16)
                                    for mm in range(2)], axis=0)
            scores.append(_dot_nt(k_ref[:width, sl], qcat))
        outs = []
        for h in range(4):
            bias = _bias_col(bias_ref, h, i, width)
            es, ls = [], []
            for mm in range(2):
                s = scores[h][:, mm * QB:(mm + 1) * QB] + bias
                e = jnp.exp2(s - _col_reduce(s, jnp.max))
                ls.append(_col_reduce(e, jnp.sum))
                es.append(e.astype(BF16))
            ot = _dot(vt_ref[h * HEAD_DIM:(h + 1) * HEAD_DIM, :width], jnp.concatenate(es, axis=1))
            a = ot[:, :QB] / ls[0] - lam * (ot[:, QB:] / ls[1])
            ms = jnp.sum(a * a, axis=0, keepdims=True) * (1.0 / HEAD_DIM)
            outs.append(a * lax.rsqrt(ms + EPS))
        y_t = jnp.concatenate(outs, axis=0)
        y = jnp.concatenate([y_t[:LANES].T, y_t[LANES:].T], axis=1)
        o_ref[...] = (y * (subln_ref[...] * (1.0 - lambda_init)) * _silu(g_ref[...])).astype(BF16)

    _by_causal_width(i, seq, body)


def _diff_attn(p32, p16, bias, lam_params, subln, lambda_init):
    bsz, seq, _ = p32.shape
    nq = seq // QB
    return pl.pallas_call(
        functools.partial(_diff_attn_kernel, lambda_init=lambda_init),
        grid=(bsz, nq),
        in_specs=[pl.BlockSpec((None, QB, 256), lambda b, i: (b, i, P_DQ // 256)),
                  pl.BlockSpec((None, seq, 256), lambda b, i: (b, 0, Q_DK // 256)),
                  pl.BlockSpec((None, seq, 256), lambda b, i: (b, 0, P_DV // 256)),
                  pl.BlockSpec((None, QB, 256), lambda b, i: (b, i, P_DG // 256)),
                  pl.BlockSpec((4, nq + 1, QB, QB), lambda b, i: (0, 0, 0, 0)),
                  pl.BlockSpec((4, D_QK), lambda b, i: (0, 0)),
                  pl.BlockSpec((1, BRANCH_WIDTH), lambda b, i: (0, 0))],
        out_specs=pl.BlockSpec((None, QB, BRANCH_WIDTH), lambda b, i: (b, i, 0)),
        out_shape=jax.ShapeDtypeStruct((bsz, seq, BRANCH_WIDTH), BF16),
        scratch_shapes=[pltpu.VMEM((BRANCH_WIDTH, seq), BF16)],
        compiler_params=_params(("arbitrary", "arbitrary")),
        name="diff_attn",
    )(p32, p16, p32, p32, bias, lam_params, subln)


def _sortable_to_float(key):
    return pltpu.bitcast(jnp.where(key < 0, key ^ jnp.int32(0x7FFFFFFF), key), F32)


def _kth_largest(score_ref, width, k_top):
    def count_ge(key):
        thr = _sortable_to_float(key)
        return _col_reduce(jnp.where(score_ref[:width, :] >= thr, 1.0, 0.0), jnp.sum)

    int_min = jnp.full((1, QB), -2 ** 31, jnp.int32)
    zero = jnp.zeros((1, QB), jnp.int32)
    t = jnp.where(count_ge(zero) >= k_top, zero, int_min)

    def step(it, t):
        cand = t + (jnp.int32(1) << (30 - it))
        return jnp.where(count_ge(cand) >= k_top, cand, t)

    return _sortable_to_float(lax.fori_loop(0, 31, step, t))


def _first_ties(eq, need, row):
    eqf = jnp.where(eq, 1.0, 0.0)
    nbits = int(eq.shape[0]).bit_length()

    def body(it, j):
        cand = j + (jnp.int32(1) << (nbits - 1 - it))
        cnt = _col_reduce(jnp.where(row < cand, eqf, 0.0), jnp.sum)
        return jnp.where(cnt <= need, cand, j)

    j = lax.fori_loop(0, nbits, body, jnp.zeros((1, eq.shape[1]), jnp.int32))
    return eq & (row < j)


def _sparse_attn_kernel(q_ref, k_ref, v_ref, qi_ref, ki_ref, wi_ref, g_ref, bias_ref, o_ref,
                        score_ref, neg_ref, vt_ref, *, k_top):
    i = pl.program_id(1)
    seq = k_ref.shape[0]

    @pl.when(i == 0)
    def _():
        _transpose_values(v_ref, vt_ref)

    def body(width):
        head_w = width - KV_CHUNK
        ki = ki_ref[:width, :]
        w_t = wi_ref[...].T * (IDX_DIM ** -0.5 * IDX_HEADS ** -0.5)
        score = None
        for h in range(0, IDX_HEADS, 2):
            qg = qi_ref[:, (h // 4) * LANES:(h // 4 + 1) * LANES]
            qcat = jnp.concatenate([qg * _lane_mask(IDX_DIM, h % 4, BF16),
                                    qg * _lane_mask(IDX_DIM, h % 4 + 1, BF16)], axis=0)
            logit = _dot_nt(ki, qcat)
            term = (jnp.maximum(logit[:, :QB], 0.0) * w_t[h:h + 1]
                    + jnp.maximum(logit[:, QB:], 0.0) * w_t[h + 1:h + 2])
            score = term if score is None else score + term
        s_idx = head_w + lax.broadcasted_iota(jnp.int32, (KV_CHUNK, QB), 0)
        tail_ok = s_idx <= i * QB + lax.broadcasted_iota(jnp.int32, (KV_CHUNK, QB), 1)
        if head_w:
            score_ref[:head_w, :] = score[:head_w]
        score_ref[head_w:width, :] = jnp.where(tail_ok, score[head_w:], NEG_INF)
        s2s = []
        for g in range(2):
            sl = slice(g * LANES, (g + 1) * LANES)
            qg = q_ref[:, sl] * (HEAD_DIM ** -0.5 * LOG2E)
            qcat = jnp.concatenate([(qg * _lane_mask(HEAD_DIM, hh, F32)).astype(BF16) for hh in range(2)], axis=0)
            s2s.append(_dot_nt(k_ref[:width, sl], qcat))
        thr = _kth_largest(score_ref, width, k_top)
        keep_tail = (score_ref[head_w:width, :] >= thr) & tail_ok
        cnt = _col_reduce(jnp.where(keep_tail, 1.0, 0.0), jnp.sum)
        neg_ref[head_w:width, :] = jnp.where(keep_tail, 0.0, NEG_INF)
        if head_w:
            keep_head = score_ref[:head_w, :] >= thr
            cnt = cnt + _col_reduce(jnp.where(keep_head, 1.0, 0.0), jnp.sum)
            neg_ref[:head_w, :] = jnp.where(keep_head, 0.0, NEG_INF)

        @pl.when(jnp.max(cnt) > k_top)
        def _():
            x = score_ref[:width, :]
            gt = x > thr
            need = k_top - _col_reduce(jnp.where(gt, 1.0, 0.0), jnp.sum)
            row = lax.broadcasted_iota(jnp.int32, (width, QB), 0)
            neg_ref[:width, :] = jnp.where(gt | _first_ties(x == thr, need, row), 0.0, NEG_INF)

        outs = []
        for g in range(2):
            sl = slice(g * LANES, (g + 1) * LANES)
            s2 = s2s[g]
            es, ls = [], []
            for hh in range(2):
                s = s2[:, hh * QB:(hh + 1) * QB] + (_bias_col(bias_ref, 2 * g + hh, i, width) + neg_ref[:width, :])
                e = jnp.exp2(s - _col_reduce(s, jnp.max))
                ls.append(_col_reduce(e, jnp.sum))
                es.append(e.astype(BF16))
            ot = _dot(vt_ref[sl, :width], jnp.concatenate(es, axis=1))
            outs.append(ot[:HEAD_DIM, :QB] / ls[0])
            outs.append(ot[HEAD_DIM:, QB:] / ls[1])
        y_t = jnp.concatenate(outs, axis=0)
        y = jnp.concatenate([y_t[:LANES].T, y_t[LANES:].T], axis=1)
        o_ref[...] = (y * _silu(g_ref[...])).astype(BF16)

    _by_causal_width(i, seq, body)


def _sparse_attn(p32, p16, bias, k_top):
    bsz, seq, _ = p32.shape
    nq = seq // QB
    return pl.pallas_call(
        functools.partial(_sparse_attn_kernel, k_top=k_top),
        grid=(bsz, nq),
        in_specs=[pl.BlockSpec((None, QB, 256), lambda b, i: (b, i, P_CQ2 // 256)),
                  pl.BlockSpec((None, seq, 256), lambda b, i: (b, 0, Q_CK // 256)),
                  pl.BlockSpec((None, seq, 256), lambda b, i: (b, 0, P_CV // 256)),
                  pl.BlockSpec((None, QB, 256), lambda b, i: (b, i, Q_CQI // 256)),
                  pl.BlockSpec((None, seq, LANES), lambda b, i: (b, 0, Q_CKI // LANES)),
                  pl.BlockSpec((None, QB, LANES), lambda b, i: (b, i, P_CWI // LANES)),
                  pl.BlockSpec((None, QB, 256), lambda b, i: (b, i, P_CG // 256)),
                  pl.BlockSpec((4, nq + 1, QB, QB), lambda b, i: (0, 0, 0, 0))],
        out_specs=pl.BlockSpec((None, QB, BRANCH_WIDTH), lambda b, i: (b, i, 0)),
        out_shape=jax.ShapeDtypeStruct((bsz, seq, BRANCH_WIDTH), BF16),
        scratch_shapes=[pltpu.VMEM((seq, QB), F32), pltpu.VMEM((seq, QB), F32),
                        pltpu.VMEM((BRANCH_WIDTH, seq), BF16)],
        compiler_params=_params(("arbitrary", "arbitrary")),
        name="sparse_attn",
    )(p32, p16, p32, p16, p16, p32, p32, bias)


def _dilated_kernel(x_ref, g_ref, bias_ref, o_ref, m_ref, l_ref, acc_ref):
    first = lax.broadcasted_iota(jnp.int32, (QB, LANES), 1) < HEAD_DIM
    hmask = [_lane_mask(HEAD_DIM, hh, F32) for hh in range(2)]

    def run(p, tiles, sink):
        work = []
        for cur, prev, variant in tiles:
            for g in range(2):
                q = x_ref[g, cur, :]
                kcat = x_ref[2 + g, cur, :]
                vcat = x_ref[4 + g, cur, :]
                if prev is not None:
                    kcat = jnp.concatenate([x_ref[2 + g, prev, :], kcat], axis=0)
                    vcat = jnp.concatenate([x_ref[4 + g, prev, :], vcat], axis=0)
                kcat = kcat.astype(BF16)
                scores = []
                for hh in range(2):
                    if prev is not None:
                        bias = bias_ref[2 * g + hh, 2 * p + variant]
                    else:
                        bias = bias_ref[2 * g + hh, 2 * p, :, QB:]
                    scores.append(_dot_nt((q * hmask[hh]).astype(BF16), kcat) + bias)
                work.append((vcat.astype(BF16), scores))
        for n, (vcat, scores) in enumerate(work):
            ms, ls, accs = [], [], []
            for s in scores:
                m = jnp.max(s, axis=-1, keepdims=True)
                e = jnp.exp2(s - m)
                ms.append(m)
                ls.append(jnp.sum(e, axis=-1, keepdims=True))
                accs.append(_dot(e.astype(BF16), vcat))
            sink(n // 2, n % 2, jnp.where(first, ms[0], ms[1]), jnp.where(first, ls[0], ls[1]),
                 jnp.where(first, accs[0], accs[1]))

    def store_stats(slot, d, starts):
        def sink(t, g, m, l, a):
            rows = pl.ds(starts[t], QB, stride=d)
            m_ref[slot, g, rows, :] = m
            l_ref[slot, g, rows, :] = l
            acc_ref[slot, g, rows, :] = a
        return sink

    def step16(n0, carry):
        starts = [n0 * DIL_UNROLL + u for u in range(DIL_UNROLL)]
        run(2, [(pl.ds(r, QB, stride=16), None, 0) for r in starts], store_stats(1, 16, starts))
        return carry

    lax.fori_loop(0, 16 // DIL_UNROLL, step16, 0)

    def step4(j, carry):
        starts = [r + 4 * QB * j for r in range(4)]
        tiles = [(pl.ds(s, QB, stride=4), pl.ds(jnp.maximum(s - 4 * QB, r), QB, stride=4), jnp.minimum(j, 1))
                 for r, s in enumerate(starts)]
        run(1, tiles, store_stats(0, 4, starts))
        return carry

    lax.fori_loop(0, 4, step4, 0)

    def step1(n0, carry):
        blocks = [n0 * DIL_UNROLL + u for u in range(DIL_UNROLL)]

        def sink(t, g, m, l, a):
            rows = pl.ds(pl.multiple_of(blocks[t] * QB, QB), QB)
            lanes = slice(g * LANES, (g + 1) * LANES)
            m4, m16 = m_ref[0, g, rows, :], m_ref[1, g, rows, :]
            m_tot = jnp.maximum(m, jnp.maximum(m4, m16))
            w1, w4, w16 = jnp.exp2(m - m_tot), jnp.exp2(m4 - m_tot), jnp.exp2(m16 - m_tot)
            num = w1 * a + w4 * acc_ref[0, g, rows, :] + w16 * acc_ref[1, g, rows, :]
            den = w1 * l + w4 * l_ref[0, g, rows, :] + w16 * l_ref[1, g, rows, :]
            o_ref[rows, lanes] = (num / den * _silu(g_ref[rows, lanes])).astype(BF16)

        tiles = []
        for j in blocks:
            cur = pl.ds(pl.multiple_of(j * QB, QB), QB)
            prev = pl.ds(pl.multiple_of(jnp.maximum(j - 1, 0) * QB, QB), QB)
            tiles.append((cur, prev, jnp.minimum(j, 1)))
        run(0, tiles, sink)
        return carry

    lax.fori_loop(0, g_ref.shape[0] // QB // DIL_UNROLL, step1, 0)


def _dilated_attn(pb, p32, bias):
    bsz, _, seq, _ = pb.shape
    stats = pltpu.VMEM((2, 2, seq, LANES), F32)
    return pl.pallas_call(
        _dilated_kernel,
        grid=(bsz,),
        in_specs=[pl.BlockSpec((None, B_SLABS, seq, LANES), lambda b: (b, 0, 0, 0)),
                  pl.BlockSpec((None, seq, 256), lambda b: (b, 0, P_BG // 256)),
                  pl.BlockSpec((4, 2 * len(DILATED_PATTERNS), QB, 2 * QB), lambda b: (0, 0, 0, 0))],
        out_specs=pl.BlockSpec((None, seq, BRANCH_WIDTH), lambda b: (b, 0, 0)),
        out_shape=jax.ShapeDtypeStruct((bsz, seq, BRANCH_WIDTH), BF16),
        scratch_shapes=[stats, stats, stats],
        compiler_params=_params(("arbitrary",)),
        name="dilated_attn",
    )(pb, p32, bias)


def _out_proj_kernel(ya_ref, yb_ref, yc_ref, yd_ref, w_ref, h_ref, g_ref, o_ref):
    y = None
    for n, ref in enumerate((ya_ref, yb_ref, yc_ref, yd_ref)):
        t = _dot(ref[...], w_ref[n * BRANCH_WIDTH:(n + 1) * BRANCH_WIDTH, :])
        y = t if y is None else y + t
    ms = jnp.mean(y * y, axis=-1, keepdims=True)
    o_ref[...] = h_ref[...] + y * lax.rsqrt(ms + EPS) * g_ref[...]


def _out_proj(ya, yb, yc, yd, w, h, g, tm=512):
    bsz, seq, _ = h.shape
    yspec = pl.BlockSpec((None, tm, BRANCH_WIDTH), lambda b, i: (b, i, 0))
    return pl.pallas_call(
        _out_proj_kernel,
        grid=(bsz, seq // tm),
        in_specs=[yspec, yspec, yspec, yspec,
                  pl.BlockSpec((4 * BRANCH_WIDTH, D_MODEL), lambda b, i: (0, 0)),
                  pl.BlockSpec((None, tm, D_MODEL), lambda b, i: (b, i, 0)),
                  pl.BlockSpec((1, D_MODEL), lambda b, i: (0, 0))],
        out_specs=pl.BlockSpec((None, tm, D_MODEL), lambda b, i: (b, i, 0)),
        out_shape=jax.ShapeDtypeStruct(h.shape, F32),
        compiler_params=_params(("arbitrary", "arbitrary")),
        name="out_proj",
    )(ya, yb, yc, yd, w, h, g)


def kernel(x, w_in, w_out, norm_pre, norm_post, mla_q_norm, mla_kv_norm, mla_w_uq, mla_w_ukv,
           diff_lambda, diff_subln, rel_bias):
    bsz, seq, _ = x.shape
    depth = w_in.shape[0]
    nq = seq // QB
    k_top = min(IDX_TOPK_MAX, seq // 4)
    cos_t, sin_t = _rope_tables(seq)
    bias_b = _bias_expand(rel_bias, jnp.asarray(_bucket_tiles_dilated()), 0, 4, LOG2E)
    causal_buckets_t = jnp.asarray(np.swapaxes(_bucket_tiles_causal(nq), 1, 2))
    bias_c = _bias_expand(rel_bias, causal_buckets_t, 4, 4, LOG2E)
    bias_d = _bias_expand(rel_bias, causal_buckets_t, 8, 4, LOG2E)
    mask_tiles = jnp.asarray(_mask_tiles())
    h = x
    for layer in range(depth):
        w_arr = _arrange_w_in(w_in[layer]).astype(BF16)
        wq, wqrot, wk, wv = _arrange_mla(mla_w_uq[layer], mla_w_ukv[layer])
        gq = jnp.concatenate([mla_q_norm[layer], jnp.ones((256 - Q_LORA,), F32)])[None, :]
        gkv = mla_kv_norm[layer][None, :]
        p32, p16, pb = _in_proj(h, norm_pre[layer][None, :], w_arr)
        qa, ka, va = _mla_prep(p32, cos_t, sin_t, gq, gkv, wq, wqrot, wk, wv)
        y_a = _mla_attn(qa, ka, va, p32, mask_tiles)
        y_b = _dilated_attn(pb, p32, bias_b)
        y_c = _sparse_attn(p32, p16, bias_c, k_top)
        lambda_init = 0.8 - 0.6 * math.exp(-0.3 * layer)
        subln = jnp.tile(diff_subln[layer], BRANCH_WIDTH // HEAD_DIM)[None, :]
        y_d = _diff_attn(p32, p16, bias_d, diff_lambda[layer], subln, lambda_init)
        h = _out_proj(y_a, y_b, y_c, y_d, w_out[layer].astype(BF16), h, norm_post[layer][None, :])
    return h
```

```python
import functools
import math

import jax
import jax.numpy as jnp
import numpy as np
from jax import lax
from jax.experimental import pallas as pl
from jax.experimental.pallas import tpu as pltpu

F32 = jnp.float32
BF16 = jnp.bfloat16

D_MODEL = 1024
A_HEADS, A_NOPE, A_ROPE, A_V = 4, 64, 32, 64
Q_LORA, KV_LORA = 192, 128
ROPE_THETA = 10000.0
HEAD_DIM = 64
DILATED_PATTERNS = ((128, 1), (512, 4), (2048, 16))
IDX_HEADS, IDX_DIM, IDX_TOPK_MAX = 8, 32, 256
D_QK = 32
BRANCH_WIDTH = 256
NUM_BUCKETS, MAX_DISTANCE = 32, 2048
NEG_INF = -1e30
EPS = 1e-6
LOG2E = math.log2(math.e)
KV_CHUNK = 256
COL_ACC_ROWS = 64
DIL_UNROLL = 4
LANES = 128
QB = 128
VMEM_LIMIT = 56 * 1024 * 1024

_SPLIT = (Q_LORA, KV_LORA, A_ROPE, 256, 256, 256, 256, 256, 256, 256, 256, IDX_HEADS * IDX_DIM, IDX_DIM,
          IDX_HEADS, 256, 256, 256, 256, 256)
_OFF = np.concatenate([[0], np.cumsum(_SPLIT)]).tolist()
(_A_CQ, _A_CKV, _A_KR, _A_G, _B_Q, _B_K, _B_V, _B_G, _C_Q, _C_K, _C_V, _C_QI, _C_KI, _C_WI, _C_G,
 _D_Q, _D_K, _D_V, _D_G) = range(19)

P_CQ, P_AG, P_CKV, P_KR, P_KRROT, P_CWI = 0, 256, 512, 640, 768, 896
P_BG, P_CQ2, P_CV, P_CG, P_DQ, P_DV, P_DG = 1024, 1280, 1536, 1792, 2048, 2304, 2560
NCOL32 = 2816
Q_CK, Q_CQI, Q_DK, Q_CKI = 0, 256, 512, 768
NCOL16 = 896
B_SLABS = 6
NCOL = NCOL32 + NCOL16 + B_SLABS * LANES


def _dot(a, b):
    return jnp.dot(a, b, preferred_element_type=F32)


def _dot_nt(a, b):
    return lax.dot_general(a, b, (((1,), (1,)), ((), ())), preferred_element_type=F32)


def _params(sem):
    return pltpu.CompilerParams(dimension_semantics=sem, vmem_limit_bytes=VMEM_LIMIT)


def _rot_cols(w):
    half = w.shape[-1] // 2
    return jnp.concatenate([-w[..., half:], w[..., :half]], axis=-1)


def _arrange_w_in(w):
    depth, rows, in_cols = w.shape
    tm = 256
    return pl.pallas_call(
        _arrange_w_in_kernel,
        grid=(depth, rows // tm),
        in_specs=[pl.BlockSpec((None, tm, in_cols), lambda l, i: (l, i, 0))],
        out_specs=pl.BlockSpec((None, tm, NCOL), lambda l, i: (l, i, 0)),
        out_shape=jax.ShapeDtypeStruct((depth, rows, NCOL), BF16),
        compiler_params=_params(("arbitrary", "arbitrary")),
        name="arrange_w_in",
    )(w)


def _w_in_pieces():
    pieces = []

    def put(dst, seg, lo=0, hi=None, scale=1.0):
        hi = _SPLIT[seg] if hi is None else hi
        pieces.append((dst, _OFF[seg] + lo, hi - lo, scale))
        return dst + hi - lo

    put(P_CQ, _A_CQ)
    put(P_AG, _A_G)
    put(P_CKV, _A_CKV)
    put(P_KR + A_NOPE, _A_KR)
    half = A_ROPE // 2
    put(P_KRROT + A_NOPE, _A_KR, half, A_ROPE, scale=-1.0)
    put(P_KRROT + A_NOPE + half, _A_KR, 0, half)
    put(P_CWI, _C_WI)
    for dst, seg in ((P_BG, _B_G), (P_CQ2, _C_Q), (P_CV, _C_V), (P_CG, _C_G), (P_DQ, _D_Q), (P_DV, _D_V),
                     (P_DG, _D_G), (NCOL32 + Q_CK, _C_K), (NCOL32 + Q_CQI, _C_QI), (NCOL32 + Q_DK, _D_K)):
        put(dst, seg)
    for copy in range(LANES // IDX_DIM):
        put(NCOL32 + Q_CKI + copy * IDX_DIM, _C_KI)
    base = NCOL32 + NCOL16
    base = put(base, _B_Q, scale=HEAD_DIM ** -0.5 * LOG2E)
    base = put(base, _B_K)
    base = put(base, _B_V)
    assert base == NCOL
    return pieces


def _arrange_w_in_kernel(w_ref, o_ref):
    o_ref[...] = jnp.zeros(o_ref.shape, o_ref.dtype)
    for dst, src, n, scale in _w_in_pieces():
        lo = src // LANES * LANES
        hi = min(-(-(src + n) // LANES) * LANES, w_ref.shape[1])
        v = w_ref[:, lo:hi][:, src - lo:src - lo + n]
        o_ref[:, dst:dst + n] = (v * scale if scale != 1.0 else v).astype(o_ref.dtype)


def _arrange_mla(w_uq, w_ukv):
    wq = w_uq.reshape(Q_LORA, A_HEADS, A_NOPE + A_ROPE)
    nope, rope = wq[..., :A_NOPE], wq[..., A_NOPE:]
    zq = jnp.zeros((Q_LORA, A_HEADS, LANES - A_NOPE - A_ROPE), w_uq.dtype)
    wq_main = jnp.concatenate([nope, rope, zq], axis=-1).reshape(Q_LORA, A_HEADS * LANES)
    wq_rot = jnp.concatenate([jnp.zeros_like(nope), _rot_cols(rope), zq], axis=-1).reshape(Q_LORA, A_HEADS * LANES)
    pad = jnp.zeros((256 - Q_LORA, A_HEADS * LANES), w_uq.dtype)
    wq_main = jnp.concatenate([wq_main, pad], axis=0)
    wq_rot = jnp.concatenate([wq_rot, pad], axis=0)
    wkv = w_ukv.reshape(KV_LORA, A_HEADS, A_NOPE + A_V)
    knope, v = wkv[..., :A_NOPE], wkv[..., A_NOPE:]
    wk = jnp.concatenate([knope, jnp.zeros_like(knope)], axis=-1).reshape(KV_LORA, A_HEADS * LANES)
    wv_t = v.reshape(KV_LORA, A_HEADS * A_V).T
    return wq_main.astype(BF16), wq_rot.astype(BF16), wk.astype(BF16), wv_t.astype(BF16)


def _rope_tables(seq):
    inv = ROPE_THETA ** (-jnp.arange(0, A_ROPE, 2, dtype=F32) / A_ROPE)
    ang = jnp.arange(seq, dtype=F32)[:, None] * inv[None, :]
    cos, sin = jnp.cos(ang), jnp.sin(ang)
    one = jnp.ones((seq, A_NOPE), F32)
    zero = jnp.zeros((seq, LANES - A_NOPE - A_ROPE), F32)
    cos_t = jnp.concatenate([one, cos, cos, zero], axis=1)
    sin_t = jnp.concatenate([jnp.zeros_like(one), sin, sin, zero], axis=1)
    return cos_t, sin_t


def _t5_bucket_np(rel):
    n = np.maximum(rel, 0)
    max_exact = NUM_BUCKETS // 2
    nf = np.maximum(n, max_exact).astype(np.float64)
    large = max_exact + (np.log(nf / max_exact) / math.log(MAX_DISTANCE / max_exact)
                         * (NUM_BUCKETS - max_exact)).astype(np.int32)
    large = np.minimum(large, NUM_BUCKETS - 1)
    return np.where(n < max_exact, n, large).astype(np.int32)


MASKED_BUCKET = NUM_BUCKETS


def _bucket_tiles_causal(nq):
    q = np.arange(QB)[:, None]
    k = np.arange(QB)[None, :]
    tiles = [np.full((QB, QB), MASKED_BUCKET, np.int32)]
    for d in range(nq):
        rel = QB * d + q - k
        tiles.append(np.where(rel >= 0, _t5_bucket_np(rel), MASKED_BUCKET).astype(np.int32))
    return np.stack(tiles)


def _mask_tiles():
    k = np.arange(QB)[:, None]
    q = np.arange(QB)[None, :]
    diag = np.where(k <= q, 0.0, NEG_INF)
    return np.stack([np.full((QB, QB), NEG_INF), diag, np.zeros((QB, QB))]).astype(np.float32)


def _bucket_tiles_dilated():
    q = np.arange(QB)[:, None]
    k = np.arange(2 * QB)[None, :]
    rel = q + QB - k
    tiles = []
    for (window, d) in DILATED_PATTERNS:
        in_band = (rel >= 0) & (rel <= window // d)
        for has_prev in (False, True):
            ok = in_band & (has_prev | (k >= QB))
            tiles.append(np.where(ok, _t5_bucket_np(rel * d), MASKED_BUCKET).astype(np.int32))
    return np.stack(tiles)


def _bias_expand_kernel(table_ref, bucket_ref, out_ref, *, head0, scale):
    h = pl.program_id(0) + head0
    bk = bucket_ref[...]
    acc = jnp.where(bk == MASKED_BUCKET, NEG_INF, 0.0)
    for b in range(NUM_BUCKETS):
        acc = jnp.where(bk == b, table_ref[b, h] * scale, acc)
    out_ref[...] = acc


def _bias_expand(table, buckets, head0, nheads, scale=1.0):
    n, r, c = buckets.shape
    return pl.pallas_call(
        functools.partial(_bias_expand_kernel, head0=head0, scale=scale),
        grid=(nheads,),
        in_specs=[pl.BlockSpec(memory_space=pltpu.SMEM),
                  pl.BlockSpec((n, r, c), lambda h: (0, 0, 0))],
        out_specs=pl.BlockSpec((None, n, r, c), lambda h: (h, 0, 0, 0)),
        out_shape=jax.ShapeDtypeStruct((nheads, n, r, c), F32),
        compiler_params=_params(("arbitrary",)),
        name="bias_expand",
    )(table, buckets)


def _in_proj_kernel(x_ref, g_ref, w_ref, o32_ref, o16_ref, ob_ref):
    x = x_ref[...]
    ms = jnp.mean(x * x, axis=-1, keepdims=True)
    xn = x * lax.rsqrt(ms + EPS) * g_ref[...]
    p = _dot(xn.astype(BF16), w_ref[...])
    o32_ref[...] = p[:, :NCOL32]
    o16_ref[...] = p[:, NCOL32:NCOL32 + NCOL16].astype(BF16)
    for s in range(B_SLABS):
        lo = NCOL32 + NCOL16 + s * LANES
        ob_ref[s] = p[:, lo:lo + LANES]


def _in_proj(h, g, w, layer, tm=256):
    bsz, seq, _ = h.shape
    return pl.pallas_call(
        _in_proj_kernel,
        grid=(bsz, seq // tm),
        in_specs=[pl.BlockSpec((None, tm, D_MODEL), lambda b, i: (b, i, 0)),
                  pl.BlockSpec((1, D_MODEL), lambda b, i: (0, 0)),
                  pl.BlockSpec((None, D_MODEL, NCOL), lambda b, i: (layer, 0, 0))],
        out_specs=[pl.BlockSpec((None, tm, NCOL32), lambda b, i: (b, i, 0)),
                   pl.BlockSpec((None, tm, NCOL16), lambda b, i: (b, i, 0)),
                   pl.BlockSpec((None, B_SLABS, tm, LANES), lambda b, i: (b, 0, i, 0))],
        out_shape=[jax.ShapeDtypeStruct((bsz, seq, NCOL32), F32),
                   jax.ShapeDtypeStruct((bsz, seq, NCOL16), BF16),
                   jax.ShapeDtypeStruct((bsz, B_SLABS, seq, LANES), F32)],
        compiler_params=_params(("arbitrary", "arbitrary")),
        name="in_proj",
    )(h, g, w)


def _mla_prep_kernel(cq_ref, ckv_ref, kr_ref, krrot_ref, cos_ref, sin_ref, gq_ref, gkv_ref,
                     wq_ref, wqrot_ref, wk_ref, wv_ref, q_ref, k_ref, v_ref):
    cos = cos_ref[...]
    sin = sin_ref[...]
    cos4 = jnp.concatenate([cos] * A_HEADS, axis=1)
    sin4 = jnp.concatenate([sin] * A_HEADS, axis=1)
    cq = cq_ref[...]
    ms = jnp.sum(cq * cq, axis=-1, keepdims=True) * (1.0 / Q_LORA)
    nq = (cq * lax.rsqrt(ms + EPS) * gq_ref[...]).astype(BF16)
    q = _dot(nq, wq_ref[...]) * cos4 + _dot(nq, wqrot_ref[...]) * sin4
    q_ref[...] = (q * ((A_NOPE + A_ROPE) ** -0.5 * LOG2E)).astype(BF16)
    ckv = ckv_ref[...]
    ms = jnp.mean(ckv * ckv, axis=-1, keepdims=True)
    nkv = (ckv * lax.rsqrt(ms + EPS) * gkv_ref[...]).astype(BF16)
    kr = kr_ref[...] * cos + krrot_ref[...] * sin
    k = _dot(nkv, wk_ref[...]) + jnp.concatenate([kr] * A_HEADS, axis=1)
    k_ref[...] = k.astype(BF16)
    v_ref[...] = _dot_nt(wv_ref[...], nkv).astype(BF16)


def _mla_prep(p32, cos_t, sin_t, gq, gkv, wq, wqrot, wk, wv_t, tm=512):
    bsz, seq, _ = p32.shape
    w4 = A_HEADS * LANES
    row = lambda b, i: (b, i, 0)
    const = lambda b, i: (0, 0)
    return pl.pallas_call(
        _mla_prep_kernel,
        grid=(bsz, seq // tm),
        in_specs=[pl.BlockSpec((None, tm, 256), lambda b, i: (b, i, P_CQ // 256)),
                  pl.BlockSpec((None, tm, LANES), lambda b, i: (b, i, P_CKV // LANES)),
                  pl.BlockSpec((None, tm, LANES), lambda b, i: (b, i, P_KR // LANES)),
                  pl.BlockSpec((None, tm, LANES), lambda b, i: (b, i, P_KRROT // LANES)),
                  pl.BlockSpec((tm, LANES), lambda b, i: (i, 0)),
                  pl.BlockSpec((tm, LANES), lambda b, i: (i, 0)),
                  pl.BlockSpec((1, 256), const),
                  pl.BlockSpec((1, KV_LORA), const),
                  pl.BlockSpec((256, w4), const),
                  pl.BlockSpec((256, w4), const),
                  pl.BlockSpec((KV_LORA, w4), const),
                  pl.BlockSpec((A_HEADS * A_V, KV_LORA), const)],
        out_specs=[pl.BlockSpec((None, tm, w4), row), pl.BlockSpec((None, tm, w4), row),
                   pl.BlockSpec((None, A_HEADS * A_V, tm), lambda b, i: (b, 0, i))],
        out_shape=[jax.ShapeDtypeStruct((bsz, seq, w4), BF16), jax.ShapeDtypeStruct((bsz, seq, w4), BF16),
                   jax.ShapeDtypeStruct((bsz, A_HEADS * A_V, seq), BF16)],
        compiler_params=_params(("arbitrary", "arbitrary")),
        name="mla_prep",
    )(p32, p32, p32, p32, cos_t, sin_t, gq, gkv, wq, wqrot, wk, wv_t)


def _silu(g):
    return g * (1.0 / (1.0 + jnp.exp(-g)))


def _by_causal_width(i, seq, body):
    per = KV_CHUNK // QB
    for wb in range(seq // KV_CHUNK):
        pl.when(i // per == wb)(functools.partial(body, (wb + 1) * KV_CHUNK))


def _mask_tail_t(mask_ref, i, width):
    first = (width - KV_CHUNK) // QB
    return jnp.concatenate([mask_ref[jnp.clip(i - j, -1, 1) + 1] for j in range(first, width // QB)], axis=0)


def _col_reduce(x, op):
    rows, lanes = x.shape
    part = op(x.reshape(rows // COL_ACC_ROWS, COL_ACC_ROWS, lanes), axis=0)
    return op(part, axis=0, keepdims=True)


def _bias_col(bias_ref, h, i, width):
    return jnp.concatenate([bias_ref[h, jnp.maximum(i - j, -1) + 1] for j in range(width // QB)], axis=0)


def _lane_mask(width, seg, dtype):
    lane = lax.broadcasted_iota(jnp.int32, (1, LANES), 1)
    return jnp.where((lane >= seg * width) & (lane < (seg + 1) * width), 1.0, 0.0).astype(dtype)


def _mla_attn_kernel(q_ref, k_ref, vt_ref, g_ref, mask_ref, o_ref):
    i = pl.program_id(1)
    seq = k_ref.shape[0]

    def body(width):
        head_w = width - KV_CHUNK
        zero = jnp.zeros((QB, LANES), BF16)
        scores = []
        for g in range(A_HEADS // 2):
            q0 = q_ref[:, 2 * g * LANES:(2 * g + 1) * LANES]
            q1 = q_ref[:, (2 * g + 1) * LANES:(2 * g + 2) * LANES]
            qbd = jnp.concatenate([jnp.concatenate([q0, zero], axis=1),
                                   jnp.concatenate([zero, q1], axis=1)], axis=0)
            scores.append(_dot_nt(k_ref[:width, 2 * g * LANES:(2 * g + 2) * LANES], qbd))
        tail = _mask_tail_t(mask_ref, i, width)
        outs = []
        for g in range(A_HEADS // 2):
            es, ls = [], []
            for hh in range(2):
                s = scores[g][:, hh * QB:(hh + 1) * QB]
                s = jnp.concatenate([s[:head_w], s[head_w:] + tail], axis=0) if head_w else s + tail
                e = jnp.exp2(s - _col_reduce(s, jnp.max))
                ls.append(_col_reduce(e, jnp.sum))
                es.append(e.astype(BF16))
            ot = _dot(vt_ref[g * LANES:(g + 1) * LANES, :width], jnp.concatenate(es, axis=1))
            outs.append(ot[:A_V, :QB] / ls[0])
            outs.append(ot[A_V:, QB:] / ls[1])
        y_t = jnp.concatenate(outs, axis=0)
        y = jnp.concatenate([y_t[:LANES].T, y_t[LANES:].T], axis=1)
        o_ref[...] = (y * _silu(g_ref[...])).astype(BF16)

    _by_causal_width(i, seq, body)


def _mla_attn(q, k, v_t, p32, mask_tiles):
    bsz, seq, w4 = q.shape
    return pl.pallas_call(
        _mla_attn_kernel,
        grid=(bsz, seq // QB),
        in_specs=[pl.BlockSpec((None, QB, w4), lambda b, i: (b, i, 0)),
                  pl.BlockSpec((None, seq, w4), lambda b, i: (b, 0, 0)),
                  pl.BlockSpec((None, A_HEADS * A_V, seq), lambda b, i: (b, 0, 0)),
                  pl.BlockSpec((None, QB, 256), lambda b, i: (b, i, P_AG // 256)),
                  pl.BlockSpec((3, QB, QB), lambda b, i: (0, 0, 0))],
        out_specs=pl.BlockSpec((None, QB, BRANCH_WIDTH), lambda b, i: (b, i, 0)),
        out_shape=jax.ShapeDtypeStruct((bsz, seq, BRANCH_WIDTH), BF16),
        compiler_params=_params(("arbitrary", "arbitrary")),
        name="mla_attn",
    )(q, k, v_t, p32, mask_tiles)


def _transpose_values(v_ref, vt_ref):
    for j in range(v_ref.shape[0] // QB):
        for g in range(v_ref.shape[1] // LANES):
            tile = v_ref[j * QB:(j + 1) * QB, g * LANES:(g + 1) * LANES]
            vt_ref[g * LANES:(g + 1) * LANES, j * QB:(j + 1) * QB] = tile.T.astype(BF16)


def _diff_attn_kernel(q_ref, k_ref, v_ref, g_ref, bias_ref, lam_ref, subln_ref, o_ref, vt_ref, *, lambda_init):
    i = pl.program_id(1)
    seq = k_ref.shape[0]

    @pl.when(i == 0)
    def _():
        _transpose_values(v_ref, vt_ref)

    def body(width):
        lp = lam_ref[...]
        lam = (jnp.exp(jnp.sum(lp[0:1] * lp[1:2], axis=-1, keepdims=True))
               - jnp.exp(jnp.sum(lp[2:3] * lp[3:4], axis=-1, keepdims=True)) + lambda_init)
        scores = []
        for h in range(4):
            sl = slice((h // 2) * LANES, (h // 2 + 1) * LANES)
            qg = q_ref[:, sl] * (D_QK ** -0.5 * LOG2E)
            qcat = jnp.concatenate([(qg * _lane_mask(D_QK, 2 * (h % 2) + mm, F32)).astype(BF16)
                                    for mm in range(2)], axis=0)
            scores.append(_dot_nt(k_ref[:width, sl], qcat))
        outs = []
        for h in range(4):
            bias = _bias_col(bias_ref, h, i, width)
            es, ls = [], []
            for mm in range(2):
                s = scores[h][:, mm * QB:(mm + 1) * QB] + bias
                e = jnp.exp2(s - _col_reduce(s, jnp.max))
                ls.append(_col_reduce(e, jnp.sum))
                es.append(e.astype(BF16))
            ot = _dot(vt_ref[h * HEAD_DIM:(h + 1) * HEAD_DIM, :width], jnp.concatenate(es, axis=1))
            a = ot[:, :QB] / ls[0] - lam * (ot[:, QB:] / ls[1])
            ms = jnp.sum(a * a, axis=0, keepdims=True) * (1.0 / HEAD_DIM)
            outs.append(a * lax.rsqrt(ms + EPS))
        y_t = jnp.concatenate(outs, axis=0)
        y = jnp.concatenate([y_t[:LANES].T, y_t[LANES:].T], axis=1)
        o_ref[...] = (y * (subln_ref[...] * (1.0 - lambda_init)) * _silu(g_ref[...])).astype(BF16)

    _by_causal_width(i, seq, body)


def _diff_attn(p32, p16, bias, lam_params, subln, lambda_init):
    bsz, seq, _ = p32.shape
    nq = seq // QB
    return pl.pallas_call(
        functools.partial(_diff_attn_kernel, lambda_init=lambda_init),
        grid=(bsz, nq),
        in_specs=[pl.BlockSpec((None, QB, 256), lambda b, i: (b, i, P_DQ // 256)),
                  pl.BlockSpec((None, seq, 256), lambda b, i: (b, 0, Q_DK // 256)),
                  pl.BlockSpec((None, seq, 256), lambda b, i: (b, 0, P_DV // 256)),
                  pl.BlockSpec((None, QB, 256), lambda b, i: (b, i, P_DG // 256)),
                  pl.BlockSpec((4, nq + 1, QB, QB), lambda b, i: (0, 0, 0, 0)),
                  pl.BlockSpec((4, D_QK), lambda b, i: (0, 0)),
                  pl.BlockSpec((1, BRANCH_WIDTH), lambda b, i: (0, 0))],
        out_specs=pl.BlockSpec((None, QB, BRANCH_WIDTH), lambda b, i: (b, i, 0)),
        out_shape=jax.ShapeDtypeStruct((bsz, seq, BRANCH_WIDTH), BF16),
        scratch_shapes=[pltpu.VMEM((BRANCH_WIDTH, seq), BF16)],
        compiler_params=_params(("arbitrary", "arbitrary")),
        name="diff_attn",
    )(p32, p16, p32, p32, bias, lam_params, subln)


def _sortable_to_float(key):
    return pltpu.bitcast(jnp.where(key < 0, key ^ jnp.int32(0x7FFFFFFF), key), F32)


def _kth_largest(score_ref, width, k_top):
    def count_ge(key):
        thr = _sortable_to_float(key)
        return _col_reduce(jnp.where(score_ref[:width, :] >= thr, 1.0, 0.0), jnp.sum)

    int_min = jnp.full((1, QB), -2 ** 31, jnp.int32)
    zero = jnp.zeros((1, QB), jnp.int32)
    t = jnp.where(count_ge(zero) >= k_top, zero, int_min)

    def step(it, t):
        cand = t + (jnp.int32(1) << (30 - it))
        return jnp.where(count_ge(cand) >= k_top, cand, t)

    return _sortable_to_float(lax.fori_loop(0, 31, step, t))


def _first_ties(eq, need, row):
    eqf = jnp.where(eq, 1.0, 0.0)
    nbits = int(eq.shape[0]).bit_length()

    def body(it, j):
        cand = j + (jnp.int32(1) << (nbits - 1 - it))
        cnt = _col_reduce(jnp.where(row < cand, eqf, 0.0), jnp.sum)
        return jnp.where(cnt <= need, cand, j)

    j = lax.fori_loop(0, nbits, body, jnp.zeros((1, eq.shape[1]), jnp.int32))
    return eq & (row < j)


def _sparse_attn_kernel(q_ref, k_ref, v_ref, qi_ref, ki_ref, wi_ref, g_ref, bias_ref, o_ref,
                        score_ref, neg_ref, vt_ref, *, k_top):
    i = pl.program_id(1)
    seq = k_ref.shape[0]

    @pl.when(i == 0)
    def _():
        _transpose_values(v_ref, vt_ref)

    def body(width):
        head_w = width - KV_CHUNK
        ki = ki_ref[:width, :]
        w_t = wi_ref[...].T * (IDX_DIM ** -0.5 * IDX_HEADS ** -0.5)
        score = None
        for h in range(0, IDX_HEADS, 2):
            qg = qi_ref[:, (h // 4) * LANES:(h // 4 + 1) * LANES]
            qcat = jnp.concatenate([qg * _lane_mask(IDX_DIM, h % 4, BF16),
                                    qg * _lane_mask(IDX_DIM, h % 4 + 1, BF16)], axis=0)
            logit = _dot_nt(ki, qcat)
            term = (jnp.maximum(logit[:, :QB], 0.0) * w_t[h:h + 1]
                    + jnp.maximum(logit[:, QB:], 0.0) * w_t[h + 1:h + 2])
            score = term if score is None else score + term
        s_idx = head_w + lax.broadcasted_iota(jnp.int32, (KV_CHUNK, QB), 0)
        tail_ok = s_idx <= i * QB + lax.broadcasted_iota(jnp.int32, (KV_CHUNK, QB), 1)
        if head_w:
            score_ref[:head_w, :] = score[:head_w]
        score_ref[head_w:width, :] = jnp.where(tail_ok, score[head_w:], NEG_INF)
        s2s = []
        for g in range(2):
            sl = slice(g * LANES, (g + 1) * LANES)
            qg = q_ref[:, sl] * (HEAD_DIM ** -0.5 * LOG2E)
            qcat = jnp.concatenate([(qg * _lane_mask(HEAD_DIM, hh, F32)).astype(BF16) for hh in range(2)], axis=0)
            s2 = _dot_nt(k_ref[:width, sl], qcat)
            s2s.append([s2[:, hh * QB:(hh + 1) * QB] + _bias_col(bias_ref, 2 * g + hh, i, width) for hh in range(2)])
        thr = _kth_largest(score_ref, width, k_top)
        keep_tail = (score_ref[head_w:width, :] >= thr) & tail_ok
        cnt = _col_reduce(jnp.where(keep_tail, 1.0, 0.0), jnp.sum)
        neg_ref[head_w:width, :] = jnp.where(keep_tail, 0.0, NEG_INF)
        if head_w:
            keep_head = score_ref[:head_w, :] >= thr
            cnt = cnt + _col_reduce(jnp.where(keep_head, 1.0, 0.0), jnp.sum)
            neg_ref[:head_w, :] = jnp.where(keep_head, 0.0, NEG_INF)

        @pl.when(jnp.max(cnt) > k_top)
        def _():
            x = score_ref[:width, :]
            gt = x > thr
            need = k_top - _col_reduce(jnp.where(gt, 1.0, 0.0), jnp.sum)
            row = lax.broadcasted_iota(jnp.int32, (width, QB), 0)
            neg_ref[:width, :] = jnp.where(gt | _first_ties(x == thr, need, row), 0.0, NEG_INF)

        outs = []
        for g in range(2):
            sl = slice(g * LANES, (g + 1) * LANES)
            es, ls = [], []
            for hh in range(2):
                s = s2s[g][hh] + neg_ref[:width, :]
                e = jnp.exp2(s - _col_reduce(s, jnp.max))
                ls.append(_col_reduce(e, jnp.sum))
                es.append(e.astype(BF16))
            ot = _dot(vt_ref[sl, :width], jnp.concatenate(es, axis=1))
            outs.append(ot[:HEAD_DIM, :QB] / ls[0])
            outs.append(ot[HEAD_DIM:, QB:] / ls[1])
        y_t = jnp.concatenate(outs, axis=0)
        y = jnp.concatenate([y_t[:LANES].T, y_t[LANES:].T], axis=1)
        o_ref[...] = (y * _silu(g_ref[...])).astype(BF16)

    _by_causal_width(i, seq, body)


def _sparse_attn(p32, p16, bias, k_top):
    bsz, seq, _ = p32.shape
    nq = seq // QB
    return pl.pallas_call(
        functools.partial(_sparse_attn_kernel, k_top=k_top),
        grid=(bsz, nq),
        in_specs=[pl.BlockSpec((None, QB, 256), lambda b, i: (b, i, P_CQ2 // 256)),
                  pl.BlockSpec((None, seq, 256), lambda b, i: (b, 0, Q_CK // 256)),
                  pl.BlockSpec((None, seq, 256), lambda b, i: (b, 0, P_CV // 256)),
                  pl.BlockSpec((None, QB, 256), lambda b, i: (b, i, Q_CQI // 256)),
                  pl.BlockSpec((None, seq, LANES), lambda b, i: (b, 0, Q_CKI // LANES)),
                  pl.BlockSpec((None, QB, LANES), lambda b, i: (b, i, P_CWI // LANES)),
                  pl.BlockSpec((None, QB, 256), lambda b, i: (b, i, P_CG // 256)),
                  pl.BlockSpec((4, nq + 1, QB, QB), lambda b, i: (0, 0, 0, 0))],
        out_specs=pl.BlockSpec((None, QB, BRANCH_WIDTH), lambda b, i: (b, i, 0)),
        out_shape=jax.ShapeDtypeStruct((bsz, seq, BRANCH_WIDTH), BF16),
        scratch_shapes=[pltpu.VMEM((seq, QB), F32), pltpu.VMEM((seq, QB), F32),
                        pltpu.VMEM((BRANCH_WIDTH, seq), BF16)],
        compiler_params=_params(("arbitrary", "arbitrary")),
        name="sparse_attn",
    )(p32, p16, p32, p16, p16, p32, p32, bias)


def _dilated_kernel(x_ref, g_ref, bias_ref, o_ref, m_ref, l_ref, acc_ref):
    first = lax.broadcasted_iota(jnp.int32, (QB, LANES), 1) < HEAD_DIM
    hmask = [_lane_mask(HEAD_DIM, hh, F32) for hh in range(2)]

    def run(p, tiles, sink):
        work = []
        for cur, prev, variant in tiles:
            for g in range(2):
                q = x_ref[g, cur, :]
                kcat = x_ref[2 + g, cur, :]
                vcat = x_ref[4 + g, cur, :]
                if prev is not None:
                    kcat = jnp.concatenate([x_ref[2 + g, prev, :], kcat], axis=0)
                    vcat = jnp.concatenate([x_ref[4 + g, prev, :], vcat], axis=0)
                kcat = kcat.astype(BF16)
                scores = []
                for hh in range(2):
                    if prev is not None:
                        bias = bias_ref[2 * g + hh, 2 * p + variant]
                    else:
                        bias = bias_ref[2 * g + hh, 2 * p, :, QB:]
                    scores.append(_dot_nt((q * hmask[hh]).astype(BF16), kcat) + bias)
                work.append((vcat.astype(BF16), scores))
        for n, (vcat, scores) in enumerate(work):
            ms, ls, accs = [], [], []
            for s in scores:
                m = jnp.max(s, axis=-1, keepdims=True)
                e = jnp.exp2(s - m)
                ms.append(m)
                ls.append(jnp.sum(e, axis=-1, keepdims=True))
                accs.append(_dot(e.astype(BF16), vcat))
            sink(n // 2, n % 2, jnp.where(first, ms[0], ms[1]), jnp.where(first, ls[0], ls[1]),
                 jnp.where(first, accs[0], accs[1]))

    def store_stats(slot, d, starts):
        def sink(t, g, m, l, a):
            rows = pl.ds(starts[t], QB, stride=d)
            m_ref[slot, g, rows, :] = m
            l_ref[slot, g, rows, :] = l
            acc_ref[slot, g, rows, :] = a
        return sink

    def step16(n0, carry):
        starts = [n0 * DIL_UNROLL + u for u in range(DIL_UNROLL)]
        run(2, [(pl.ds(r, QB, stride=16), None, 0) for r in starts], store_stats(1, 16, starts))
        return carry

    lax.fori_loop(0, 16 // DIL_UNROLL, step16, 0)

    def step4(j, carry):
        starts = [r + 4 * QB * j for r in range(4)]
        tiles = [(pl.ds(s, QB, stride=4), pl.ds(jnp.maximum(s - 4 * QB, r), QB, stride=4), jnp.minimum(j, 1))
                 for r, s in enumerate(starts)]
        run(1, tiles, store_stats(0, 4, starts))
        return carry

    lax.fori_loop(0, 4, step4, 0)

    def step1(n0, carry):
        blocks = [n0 * DIL_UNROLL + u for u in range(DIL_UNROLL)]

        def sink(t, g, m, l, a):
            rows = pl.ds(pl.multiple_of(blocks[t] * QB, QB), QB)
            lanes = slice(g * LANES, (g + 1) * LANES)
            m4, m16 = m_ref[0, g, rows, :], m_ref[1, g, rows, :]
            m_tot = jnp.maximum(m, jnp.maximum(m4, m16))
            w1, w4, w16 = jnp.exp2(m - m_tot), jnp.exp2(m4 - m_tot), jnp.exp2(m16 - m_tot)
            num = w1 * a + w4 * acc_ref[0, g, rows, :] + w16 * acc_ref[1, g, rows, :]
            den = w1 * l + w4 * l_ref[0, g, rows, :] + w16 * l_ref[1, g, rows, :]
            o_ref[rows, lanes] = (num / den * _silu(g_ref[rows, lanes])).astype(BF16)

        tiles = []
        for j in blocks:
            cur = pl.ds(pl.multiple_of(j * QB, QB), QB)
            prev = pl.ds(pl.multiple_of(jnp.maximum(j - 1, 0) * QB, QB), QB)
            tiles.append((cur, prev, jnp.minimum(j, 1)))
        run(0, tiles, sink)
        return carry

    lax.fori_loop(0, g_ref.shape[0] // QB // DIL_UNROLL, step1, 0)


def _dilated_attn(pb, p32, bias):
    bsz, _, seq, _ = pb.shape
    stats = pltpu.VMEM((2, 2, seq, LANES), F32)
    return pl.pallas_call(
        _dilated_kernel,
        grid=(bsz,),
        in_specs=[pl.BlockSpec((None, B_SLABS, seq, LANES), lambda b: (b, 0, 0, 0)),
                  pl.BlockSpec((None, seq, 256), lambda b: (b, 0, P_BG // 256)),
                  pl.BlockSpec((4, 2 * len(DILATED_PATTERNS), QB, 2 * QB), lambda b: (0, 0, 0, 0))],
        out_specs=pl.BlockSpec((None, seq, BRANCH_WIDTH), lambda b: (b, 0, 0)),
        out_shape=jax.ShapeDtypeStruct((bsz, seq, BRANCH_WIDTH), BF16),
        scratch_shapes=[stats, stats, stats],
        compiler_params=_params(("arbitrary",)),
        name="dilated_attn",
    )(pb, p32, bias)


def _out_proj_kernel(ya_ref, yb_ref, yc_ref, yd_ref, w_ref, h_ref, g_ref, o_ref):
    y = None
    for n, ref in enumerate((ya_ref, yb_ref, yc_ref, yd_ref)):
        t = _dot(ref[...], w_ref[n * BRANCH_WIDTH:(n + 1) * BRANCH_WIDTH, :])
        y = t if y is None else y + t
    ms = jnp.mean(y * y, axis=-1, keepdims=True)
    o_ref[...] = h_ref[...] + y * lax.rsqrt(ms + EPS) * g_ref[...]


def _out_proj(ya, yb, yc, yd, w, layer, h, g, tm=512):
    bsz, seq, _ = h.shape
    yspec = pl.BlockSpec((None, tm, BRANCH_WIDTH), lambda b, i: (b, i, 0))
    return pl.pallas_call(
        _out_proj_kernel,
        grid=(bsz, seq // tm),
        in_specs=[yspec, yspec, yspec, yspec,
                  pl.BlockSpec((None, 4 * BRANCH_WIDTH, D_MODEL), lambda b, i: (layer, 0, 0)),
                  pl.BlockSpec((None, tm, D_MODEL), lambda b, i: (b, i, 0)),
                  pl.BlockSpec((1, D_MODEL), lambda b, i: (0, 0))],
        out_specs=pl.BlockSpec((None, tm, D_MODEL), lambda b, i: (b, i, 0)),
        out_shape=jax.ShapeDtypeStruct(h.shape, F32),
        compiler_params=_params(("arbitrary", "arbitrary")),
        name="out_proj",
    )(ya, yb, yc, yd, w, h, g)


def kernel(x, w_in, w_out, norm_pre, norm_post, mla_q_norm, mla_kv_norm, mla_w_uq, mla_w_ukv,
           diff_lambda, diff_subln, rel_bias):
    bsz, seq, _ = x.shape
    depth = w_in.shape[0]
    nq = seq // QB
    k_top = min(IDX_TOPK_MAX, seq // 4)
    cos_t, sin_t = _rope_tables(seq)
    bias_b = _bias_expand(rel_bias, jnp.asarray(_bucket_tiles_dilated()), 0, 4, LOG2E)
    causal_buckets_t = jnp.asarray(np.swapaxes(_bucket_tiles_causal(nq), 1, 2))
    bias_c = _bias_expand(rel_bias, causal_buckets_t, 4, 4, LOG2E)
    bias_d = _bias_expand(rel_bias, causal_buckets_t, 8, 4, LOG2E)
    mask_tiles = jnp.asarray(_mask_tiles())
    w_arr_all = _arrange_w_in(w_in)
    w_out16 = w_out.astype(BF16)
    h = x
    for layer in range(depth):
        wq, wqrot, wk, wv = _arrange_mla(mla_w_uq[layer], mla_w_ukv[layer])
        gq = jnp.concatenate([mla_q_norm[layer], jnp.ones((256 - Q_LORA,), F32)])[None, :]
        gkv = mla_kv_norm[layer][None, :]
        p32, p16, pb = _in_proj(h, norm_pre[layer][None, :], w_arr_all, layer)
        qa, ka, va = _mla_prep(p32, cos_t, sin_t, gq, gkv, wq, wqrot, wk, wv)
        y_a = _mla_attn(qa, ka, va, p32, mask_tiles)
        y_b = _dilated_attn(pb, p32, bias_b)
        y_c = _sparse_attn(p32, p16, bias_c, k_top)
        lambda_init = 0.8 - 0.6 * math.exp(-0.3 * layer)
        subln = jnp.tile(diff_subln[layer], BRANCH_WIDTH // HEAD_DIM)[None, :]
        y_d = _diff_attn(p32, p16, bias_d, diff_lambda[layer], subln, lambda_init)
        h = _out_proj(y_a, y_b, y_c, y_d, w_out16, layer, h, norm_post[layer][None, :])
    return h
```

```python
import functools
import math

import jax
import jax.numpy as jnp
import numpy as np
from jax import lax
from jax.experimental import pallas as pl
from jax.experimental.pallas import tpu as pltpu

F32 = jnp.float32
BF16 = jnp.bfloat16

D_MODEL = 1024
A_HEADS, A_NOPE, A_ROPE, A_V = 4, 64, 32, 64
Q_LORA, KV_LORA = 192, 128
ROPE_THETA = 10000.0
HEAD_DIM = 64
DILATED_PATTERNS = ((128, 1), (512, 4), (2048, 16))
IDX_HEADS, IDX_DIM, IDX_TOPK_MAX = 8, 32, 256
D_QK = 32
BRANCH_WIDTH = 256
NUM_BUCKETS, MAX_DISTANCE = 32, 2048
NEG_INF = -1e30
EPS = 1e-6
LOG2E = math.log2(math.e)
KV_CHUNK = 256
COL_ACC_ROWS = 64
DIL_UNROLL = 4
LANES = 128
QB = 128
VMEM_LIMIT = 56 * 1024 * 1024

_SPLIT = (Q_LORA, KV_LORA, A_ROPE, 256, 256, 256, 256, 256, 256, 256, 256, IDX_HEADS * IDX_DIM, IDX_DIM,
          IDX_HEADS, 256, 256, 256, 256, 256)
_OFF = np.concatenate([[0], np.cumsum(_SPLIT)]).tolist()
(_A_CQ, _A_CKV, _A_KR, _A_G, _B_Q, _B_K, _B_V, _B_G, _C_Q, _C_K, _C_V, _C_QI, _C_KI, _C_WI, _C_G,
 _D_Q, _D_K, _D_V, _D_G) = range(19)

L_CQ, L_CKV, L_KR, L_KRROT = 0, 256, 384, 512
P_START = 512
P_CWI, P_AG, P_BG, P_CQ2, P_CV, P_CG, P_DQ, P_DV, P_DG = 128, 256, 512, 768, 1024, 1280, 1536, 1792, 2048
NCOL32 = 2304
Q_START = P_START + NCOL32
Q_CK, Q_CQI, Q_DK, Q_CKI = 0, 256, 512, 768
NCOL16 = 896
B_START = Q_START + NCOL16
B_SLABS = 6
NCOL = B_START + B_SLABS * LANES


def _dot(a, b):
    return jnp.dot(a, b, preferred_element_type=F32)


def _dot_nt(a, b):
    return lax.dot_general(a, b, (((1,), (1,)), ((), ())), preferred_element_type=F32)


def _params(sem):
    return pltpu.CompilerParams(dimension_semantics=sem, vmem_limit_bytes=VMEM_LIMIT)


def _rot_cols(w):
    half = w.shape[-1] // 2
    return jnp.concatenate([-w[..., half:], w[..., :half]], axis=-1)


def _arrange_w_in(w):
    depth, rows, in_cols = w.shape
    tm = 256
    return pl.pallas_call(
        _arrange_w_in_kernel,
        grid=(depth, rows // tm),
        in_specs=[pl.BlockSpec((None, tm, in_cols), lambda l, i: (l, i, 0))],
        out_specs=pl.BlockSpec((None, tm, NCOL), lambda l, i: (l, i, 0)),
        out_shape=jax.ShapeDtypeStruct((depth, rows, NCOL), BF16),
        compiler_params=_params(("arbitrary", "arbitrary")),
        name="arrange_w_in",
    )(w)


def _w_in_pieces():
    pieces = []

    def put(dst, seg, lo=0, hi=None, scale=1.0):
        hi = _SPLIT[seg] if hi is None else hi
        pieces.append((dst, _OFF[seg] + lo, hi - lo, scale))
        return dst + hi - lo

    put(L_CQ, _A_CQ)
    put(L_CKV, _A_CKV)
    put(L_KR + A_NOPE, _A_KR)
    half = A_ROPE // 2
    put(L_KRROT + A_NOPE, _A_KR, half, A_ROPE, scale=-1.0)
    put(L_KRROT + A_NOPE + half, _A_KR, 0, half)
    for dst, seg in ((P_CWI, _C_WI), (P_AG, _A_G), (P_BG, _B_G), (P_CQ2, _C_Q), (P_CV, _C_V), (P_CG, _C_G),
                     (P_DQ, _D_Q), (P_DV, _D_V), (P_DG, _D_G)):
        put(P_START + dst, seg)
    for dst, seg in ((Q_CK, _C_K), (Q_CQI, _C_QI), (Q_DK, _D_K)):
        put(Q_START + dst, seg)
    for copy in range(LANES // IDX_DIM):
        put(Q_START + Q_CKI + copy * IDX_DIM, _C_KI)
    base = put(B_START, _B_Q, scale=HEAD_DIM ** -0.5 * LOG2E)
    base = put(base, _B_K)
    base = put(base, _B_V)
    assert base == NCOL
    return pieces


def _arrange_w_in_kernel(w_ref, o_ref):
    o_ref[...] = jnp.zeros(o_ref.shape, o_ref.dtype)
    for dst, src, n, scale in _w_in_pieces():
        lo = src // LANES * LANES
        hi = min(-(-(src + n) // LANES) * LANES, w_ref.shape[1])
        v = w_ref[:, lo:hi][:, src - lo:src - lo + n]
        o_ref[:, dst:dst + n] = (v * scale if scale != 1.0 else v).astype(o_ref.dtype)


def _arrange_mla(w_uq, w_ukv):
    wq = w_uq.reshape(Q_LORA, A_HEADS, A_NOPE + A_ROPE)
    nope, rope = wq[..., :A_NOPE], wq[..., A_NOPE:]
    zq = jnp.zeros((Q_LORA, A_HEADS, LANES - A_NOPE - A_ROPE), w_uq.dtype)
    wq_main = jnp.concatenate([nope, rope, zq], axis=-1).reshape(Q_LORA, A_HEADS * LANES)
    wq_rot = jnp.concatenate([jnp.zeros_like(nope), _rot_cols(rope), zq], axis=-1).reshape(Q_LORA, A_HEADS * LANES)
    pad = jnp.zeros((256 - Q_LORA, A_HEADS * LANES), w_uq.dtype)
    wq_main = jnp.concatenate([wq_main, pad], axis=0)
    wq_rot = jnp.concatenate([wq_rot, pad], axis=0)
    wkv = w_ukv.reshape(KV_LORA, A_HEADS, A_NOPE + A_V)
    knope, v = wkv[..., :A_NOPE], wkv[..., A_NOPE:]
    wk = jnp.concatenate([knope, jnp.zeros_like(knope)], axis=-1).reshape(KV_LORA, A_HEADS * LANES)
    wv_t = v.reshape(KV_LORA, A_HEADS * A_V).T
    return wq_main.astype(BF16), wq_rot.astype(BF16), wk.astype(BF16), wv_t.astype(BF16)


def _rope_tables(seq):
    inv = ROPE_THETA ** (-jnp.arange(0, A_ROPE, 2, dtype=F32) / A_ROPE)
    ang = jnp.arange(seq, dtype=F32)[:, None] * inv[None, :]
    cos, sin = jnp.cos(ang), jnp.sin(ang)
    one = jnp.ones((seq, A_NOPE), F32)
    zero = jnp.zeros((seq, LANES - A_NOPE - A_ROPE), F32)
    cos_t = jnp.concatenate([one, cos, cos, zero], axis=1)
    sin_t = jnp.concatenate([jnp.zeros_like(one), sin, sin, zero], axis=1)
    return cos_t, sin_t


def _t5_bucket_np(rel):
    n = np.maximum(rel, 0)
    max_exact = NUM_BUCKETS // 2
    nf = np.maximum(n, max_exact).astype(np.float64)
    large = max_exact + (np.log(nf / max_exact) / math.log(MAX_DISTANCE / max_exact)
                         * (NUM_BUCKETS - max_exact)).astype(np.int32)
    large = np.minimum(large, NUM_BUCKETS - 1)
    return np.where(n < max_exact, n, large).astype(np.int32)


MASKED_BUCKET = NUM_BUCKETS


def _bucket_tiles_causal(nq):
    q = np.arange(QB)[:, None]
    k = np.arange(QB)[None, :]
    tiles = [np.full((QB, QB), MASKED_BUCKET, np.int32)]
    for d in range(nq):
        rel = QB * d + q - k
        tiles.append(np.where(rel >= 0, _t5_bucket_np(rel), MASKED_BUCKET).astype(np.int32))
    return np.stack(tiles)


def _mask_tiles():
    k = np.arange(QB)[:, None]
    q = np.arange(QB)[None, :]
    diag = np.where(k <= q, 0.0, NEG_INF)
    return np.stack([np.full((QB, QB), NEG_INF), diag, np.zeros((QB, QB))]).astype(np.float32)


def _bucket_tiles_dilated():
    q = np.arange(QB)[:, None]
    k = np.arange(2 * QB)[None, :]
    rel = q + QB - k
    tiles = []
    for (window, d) in DILATED_PATTERNS:
        in_band = (rel >= 0) & (rel <= window // d)
        for has_prev in (False, True):
            ok = in_band & (has_prev | (k >= QB))
            tiles.append(np.where(ok, _t5_bucket_np(rel * d), MASKED_BUCKET).astype(np.int32))
    return np.stack(tiles)


def _bias_expand_kernel(table_ref, bucket_ref, out_ref, *, head0, scale):
    h = pl.program_id(0) + head0
    bk = bucket_ref[...]
    acc = jnp.where(bk == MASKED_BUCKET, NEG_INF, 0.0)
    for b in range(NUM_BUCKETS):
        acc = jnp.where(bk == b, table_ref[b, h] * scale, acc)
    out_ref[...] = acc


def _bias_expand(table, buckets, head0, nheads, scale=1.0):
    n, r, c = buckets.shape
    return pl.pallas_call(
        functools.partial(_bias_expand_kernel, head0=head0, scale=scale),
        grid=(nheads,),
        in_specs=[pl.BlockSpec(memory_space=pltpu.SMEM),
                  pl.BlockSpec((n, r, c), lambda h: (0, 0, 0))],
        out_specs=pl.BlockSpec((None, n, r, c), lambda h: (h, 0, 0, 0)),
        out_shape=jax.ShapeDtypeStruct((nheads, n, r, c), F32),
        compiler_params=_params(("arbitrary",)),
        name="bias_expand",
    )(table, buckets)


def _in_proj_kernel(x_ref, g_ref, w_ref, cos_ref, sin_ref, gq_ref, gkv_ref, wq_ref, wqrot_ref, wk_ref, wv_ref,
                    o32_ref, o16_ref, ob_ref, q_ref, k_ref, v_ref):
    x = x_ref[...]
    ms = jnp.mean(x * x, axis=-1, keepdims=True)
    xn = x * lax.rsqrt(ms + EPS) * g_ref[...]
    p = _dot(xn.astype(BF16), w_ref[...])
    o32_ref[...] = p[:, P_START:Q_START]
    o16_ref[...] = p[:, Q_START:B_START].astype(BF16)
    for s in range(B_SLABS):
        ob_ref[s] = p[:, B_START + s * LANES:B_START + (s + 1) * LANES]
    cos = cos_ref[...]
    sin = sin_ref[...]
    cos4 = jnp.concatenate([cos] * A_HEADS, axis=1)
    sin4 = jnp.concatenate([sin] * A_HEADS, axis=1)
    cq = p[:, L_CQ:L_CQ + 256]
    ms = jnp.sum(cq * cq, axis=-1, keepdims=True) * (1.0 / Q_LORA)
    nq = (cq * lax.rsqrt(ms + EPS) * gq_ref[...]).astype(BF16)
    q = _dot(nq, wq_ref[...]) * cos4 + _dot(nq, wqrot_ref[...]) * sin4
    q_ref[...] = (q * ((A_NOPE + A_ROPE) ** -0.5 * LOG2E)).astype(BF16)
    ckv = p[:, L_CKV:L_CKV + KV_LORA]
    ms = jnp.mean(ckv * ckv, axis=-1, keepdims=True)
    nkv = (ckv * lax.rsqrt(ms + EPS) * gkv_ref[...]).astype(BF16)
    kr = p[:, L_KR:L_KR + LANES] * cos + p[:, L_KRROT:L_KRROT + LANES] * sin
    k = _dot(nkv, wk_ref[...]) + jnp.concatenate([kr] * A_HEADS, axis=1)
    k_ref[...] = k.astype(BF16)
    v_ref[...] = _dot_nt(wv_ref[...], nkv).astype(BF16)


def _in_proj(h, g, w, layer, cos_t, sin_t, gq, gkv, wq, wqrot, wk, wv_t, tm=256):
    bsz, seq, _ = h.shape
    w4 = A_HEADS * LANES
    row = lambda b, i: (b, i, 0)
    const = lambda b, i: (0, 0)
    return pl.pallas_call(
        _in_proj_kernel,
        grid=(bsz, seq // tm),
        in_specs=[pl.BlockSpec((None, tm, D_MODEL), row),
                  pl.BlockSpec((1, D_MODEL), const),
                  pl.BlockSpec((None, D_MODEL, NCOL), lambda b, i: (layer, 0, 0)),
                  pl.BlockSpec((tm, LANES), lambda b, i: (i, 0)),
                  pl.BlockSpec((tm, LANES), lambda b, i: (i, 0)),
                  pl.BlockSpec((1, 256), const),
                  pl.BlockSpec((1, KV_LORA), const),
                  pl.BlockSpec((256, w4), const),
                  pl.BlockSpec((256, w4), const),
                  pl.BlockSpec((KV_LORA, w4), const),
                  pl.BlockSpec((A_HEADS * A_V, KV_LORA), const)],
        out_specs=[pl.BlockSpec((None, tm, NCOL32), row),
                   pl.BlockSpec((None, tm, NCOL16), row),
                   pl.BlockSpec((None, B_SLABS, tm, LANES), lambda b, i: (b, 0, i, 0)),
                   pl.BlockSpec((None, tm, w4), row), pl.BlockSpec((None, tm, w4), row),
                   pl.BlockSpec((None, A_HEADS * A_V, tm), lambda b, i: (b, 0, i))],
        out_shape=[jax.ShapeDtypeStruct((bsz, seq, NCOL32), F32),
                   jax.ShapeDtypeStruct((bsz, seq, NCOL16), BF16),
                   jax.ShapeDtypeStruct((bsz, B_SLABS, seq, LANES), F32),
                   jax.ShapeDtypeStruct((bsz, seq, w4), BF16), jax.ShapeDtypeStruct((bsz, seq, w4), BF16),
                   jax.ShapeDtypeStruct((bsz, A_HEADS * A_V, seq), BF16)],
        compiler_params=_params(("arbitrary", "arbitrary")),
        name="in_proj",
    )(h, g, w, cos_t, sin_t, gq, gkv, wq, wqrot, wk, wv_t)


def _silu(g):
    return g * (1.0 / (1.0 + jnp.exp(-g)))


def _by_causal_width(i, seq, body):
    per = KV_CHUNK // QB
    for wb in range(seq // KV_CHUNK):
        pl.when(i // per == wb)(functools.partial(body, (wb + 1) * KV_CHUNK))


def _mask_tail_t(mask_ref, i, width):
    first = (width - KV_CHUNK) // QB
    return jnp.concatenate([mask_ref[jnp.clip(i - j, -1, 1) + 1] for j in range(first, width // QB)], axis=0)


def _col_reduce(x, op):
    rows, lanes = x.shape
    part = op(x.reshape(rows // COL_ACC_ROWS, COL_ACC_ROWS, lanes), axis=0)
    return op(part, axis=0, keepdims=True)


def _bias_col(bias_ref, h, i, width):
    return jnp.concatenate([bias_ref[h, jnp.maximum(i - j, -1) + 1] for j in range(width // QB)], axis=0)


def _lane_mask(width, seg, dtype):
    lane = lax.broadcasted_iota(jnp.int32, (1, LANES), 1)
    return jnp.where((lane >= seg * width) & (lane < (seg + 1) * width), 1.0, 0.0).astype(dtype)


def _mla_attn_kernel(q_ref, k_ref, vt_ref, g_ref, mask_ref, o_ref):
    i = pl.program_id(1)
    seq = k_ref.shape[0]

    def body(width):
        head_w = width - KV_CHUNK
        zero = jnp.zeros((QB, LANES), BF16)
        scores = []
        for g in range(A_HEADS // 2):
            q0 = q_ref[:, 2 * g * LANES:(2 * g + 1) * LANES]
            q1 = q_ref[:, (2 * g + 1) * LANES:(2 * g + 2) * LANES]
            qbd = jnp.concatenate([jnp.concatenate([q0, zero], axis=1),
                                   jnp.concatenate([zero, q1], axis=1)], axis=0)
            scores.append(_dot_nt(k_ref[:width, 2 * g * LANES:(2 * g + 2) * LANES], qbd))
        tail = _mask_tail_t(mask_ref, i, width)
        outs = []
        for g in range(A_HEADS // 2):
            es, ls = [], []
            for hh in range(2):
                s = scores[g][:, hh * QB:(hh + 1) * QB]
                s = jnp.concatenate([s[:head_w], s[head_w:] + tail], axis=0) if head_w else s + tail
                e = jnp.exp2(s - _col_reduce(s, jnp.max))
                ls.append(_col_reduce(e, jnp.sum))
                es.append(e.astype(BF16))
            ot = _dot(vt_ref[g * LANES:(g + 1) * LANES, :width], jnp.concatenate(es, axis=1))
            outs.append(ot[:A_V, :QB] / ls[0])
            outs.append(ot[A_V:, QB:] / ls[1])
        y_t = jnp.concatenate(outs, axis=0)
        y = jnp.concatenate([y_t[:LANES].T, y_t[LANES:].T], axis=1)
        o_ref[...] = (y * _silu(g_ref[...])).astype(BF16)

    _by_causal_width(i, seq, body)


def _mla_attn(q, k, v_t, p32, mask_tiles):
    bsz, seq, w4 = q.shape
    return pl.pallas_call(
        _mla_attn_kernel,
        grid=(bsz, seq // QB),
        in_specs=[pl.BlockSpec((None, QB, w4), lambda b, i: (b, i, 0)),
                  pl.BlockSpec((None, seq, w4), lambda b, i: (b, 0, 0)),
                  pl.BlockSpec((None, A_HEADS * A_V, seq), lambda b, i: (b, 0, 0)),
                  pl.BlockSpec((None, QB, 256), lambda b, i: (b, i, P_AG // 256)),
                  pl.BlockSpec((3, QB, QB), lambda b, i: (0, 0, 0))],
        out_specs=pl.BlockSpec((None, QB, BRANCH_WIDTH), lambda b, i: (b, i, 0)),
        out_shape=jax.ShapeDtypeStruct((bsz, seq, BRANCH_WIDTH), BF16),
        compiler_params=_params(("arbitrary", "arbitrary")),
        name="mla_attn",
    )(q, k, v_t, p32, mask_tiles)


def _transpose_values(v_ref, vt_ref):
    for j in range(v_ref.shape[0] // QB):
        for g in range(v_ref.shape[1] // LANES):
            tile = v_ref[j * QB:(j + 1) * QB, g * LANES:(g + 1) * LANES]
            vt_ref[g * LANES:(g + 1) * LANES, j * QB:(j + 1) * QB] = tile.T.astype(BF16)


def _diff_attn_kernel(q_ref, k_ref, v_ref, g_ref, bias_ref, lam_ref, subln_ref, o_ref, vt_ref, *, lambda_init):
    i = pl.program_id(1)
    seq = k_ref.shape[0]

    @pl.when(i == 0)
    def _():
        _transpose_values(v_ref, vt_ref)

    def body(width):
        lp = lam_ref[...]
        lam = (jnp.exp(jnp.sum(lp[0:1] * lp[1:2], axis=-1, keepdims=True))
               - jnp.exp(jnp.sum(lp[2:3] * lp[3:4], axis=-1, keepdims=True)) + lambda_init)
        scores = []
        for h in range(4):
            sl = slice((h // 2) * LANES, (h // 2 + 1) * LANES)
            qg = q_ref[:, sl] * (D_QK ** -0.5 * LOG2E)
            qcat = jnp.concatenate([(qg * _lane_mask(D_QK, 2 * (h % 2) + mm, F32)).astype(BF16)
                                    for mm in range(2)], axis=0)
            scores.append(_dot_nt(k_ref[:width, sl], qcat))
        outs = []
        for h in range(4):
            bias = _bias_col(bias_ref, h, i, width)
            es, ls = [], []
            for mm in range(2):
                s = scores[h][:, mm * QB:(mm + 1) * QB] + bias
                e = jnp.exp2(s - _col_reduce(s, jnp.max))
                ls.append(_col_reduce(e, jnp.sum))
                es.append(e.astype(BF16))
            ot = _dot(vt_ref[h * HEAD_DIM:(h + 1) * HEAD_DIM, :width], jnp.concatenate(es, axis=1))
            a = ot[:, :QB] / ls[0] - lam * (ot[:, QB:] / ls[1])
            ms = jnp.sum(a * a, axis=0, keepdims=True) * (1.0 / HEAD_DIM)
            outs.append(a * lax.rsqrt(ms + EPS))
        y_t = jnp.concatenate(outs, axis=0)
        y = jnp.concatenate([y_t[:LANES].T, y_t[LANES:].T], axis=1)
        o_ref[...] = (y * (subln_ref[...] * (1.0 - lambda_init)) * _silu(g_ref[...])).astype(BF16)

    _by_causal_width(i, seq, body)


def _diff_attn(p32, p16, bias, lam_params, subln, lambda_init):
    bsz, seq, _ = p32.shape
    nq = seq // QB
    return pl.pallas_call(
        functools.partial(_diff_attn_kernel, lambda_init=lambda_init),
        grid=(bsz, nq),
        in_specs=[pl.BlockSpec((None, QB, 256), lambda b, i: (b, i, P_DQ // 256)),
                  pl.BlockSpec((None, seq, 256), lambda b, i: (b, 0, Q_DK // 256)),
                  pl.BlockSpec((None, seq, 256), lambda b, i: (b, 0, P_DV // 256)),
                  pl.BlockSpec((None, QB, 256), lambda b, i: (b, i, P_DG // 256)),
                  pl.BlockSpec((4, nq + 1, QB, QB), lambda b, i: (0, 0, 0, 0)),
                  pl.BlockSpec((4, D_QK), lambda b, i: (0, 0)),
                  pl.BlockSpec((1, BRANCH_WIDTH), lambda b, i: (0, 0))],
        out_specs=pl.BlockSpec((None, QB, BRANCH_WIDTH), lambda b, i: (b, i, 0)),
        out_shape=jax.ShapeDtypeStruct((bsz, seq, BRANCH_WIDTH), BF16),
        scratch_shapes=[pltpu.VMEM((BRANCH_WIDTH, seq), BF16)],
        compiler_params=_params(("arbitrary", "arbitrary")),
        name="diff_attn",
    )(p32, p16, p32, p32, bias, lam_params, subln)


def _sortable_to_float(key):
    return pltpu.bitcast(jnp.where(key < 0, key ^ jnp.int32(0x7FFFFFFF), key), F32)


def _kth_largest(score_ref, width, k_top):
    def count_ge(key):
        thr = _sortable_to_float(key)
        return _col_reduce(jnp.where(score_ref[:width, :] >= thr, 1.0, 0.0), jnp.sum)

    int_min = jnp.full((1, QB), -2 ** 31, jnp.int32)
    zero = jnp.zeros((1, QB), jnp.int32)
    t = jnp.where(count_ge(zero) >= k_top, zero, int_min)

    def step(it, t):
        cand = t + (jnp.int32(1) << (30 - it))
        return jnp.where(count_ge(cand) >= k_top, cand, t)

    return _sortable_to_float(lax.fori_loop(0, 31, step, t))


def _first_ties(eq, need, row):
    eqf = jnp.where(eq, 1.0, 0.0)
    nbits = int(eq.shape[0]).bit_length()

    def body(it, j):
        cand = j + (jnp.int32(1) << (nbits - 1 - it))
        cnt = _col_reduce(jnp.where(row < cand, eqf, 0.0), jnp.sum)
        return jnp.where(cnt <= need, cand, j)

    j = lax.fori_loop(0, nbits, body, jnp.zeros((1, eq.shape[1]), jnp.int32))
    return eq & (row < j)


def _sparse_attn_kernel(q_ref, k_ref, v_ref, qi_ref, ki_ref, wi_ref, g_ref, bias_ref, o_ref,
                        score_ref, neg_ref, vt_ref, *, k_top):
    i = pl.program_id(1)
    seq = k_ref.shape[0]

    @pl.when(i == 0)
    def _():
        _transpose_values(v_ref, vt_ref)

    def body(width):
        head_w = width - KV_CHUNK
        ki = ki_ref[:width, :]
        w_t = wi_ref[...].T * (IDX_DIM ** -0.5 * IDX_HEADS ** -0.5)
        score = None
        for h in range(0, IDX_HEADS, 2):
            qg = qi_ref[:, (h // 4) * LANES:(h // 4 + 1) * LANES]
            qcat = jnp.concatenate([qg * _lane_mask(IDX_DIM, h % 4, BF16),
                                    qg * _lane_mask(IDX_DIM, h % 4 + 1, BF16)], axis=0)
            logit = _dot_nt(ki, qcat)
            term = (jnp.maximum(logit[:, :QB], 0.0) * w_t[h:h + 1]
                    + jnp.maximum(logit[:, QB:], 0.0) * w_t[h + 1:h + 2])
            score = term if score is None else score + term
        s_idx = head_w + lax.broadcasted_iota(jnp.int32, (KV_CHUNK, QB), 0)
        tail_ok = s_idx <= i * QB + lax.broadcasted_iota(jnp.int32, (KV_CHUNK, QB), 1)
        if head_w:
            score_ref[:head_w, :] = score[:head_w]
        score_ref[head_w:width, :] = jnp.where(tail_ok, score[head_w:], NEG_INF)
        s2s = []
        for g in range(2):
            sl = slice(g * LANES, (g + 1) * LANES)
            qg = q_ref[:, sl] * (HEAD_DIM ** -0.5 * LOG2E)
            qcat = jnp.concatenate([(qg * _lane_mask(HEAD_DIM, hh, F32)).astype(BF16) for hh in range(2)], axis=0)
            s2 = _dot_nt(k_ref[:width, sl], qcat)
            s2s.append([s2[:, hh * QB:(hh + 1) * QB] + _bias_col(bias_ref, 2 * g + hh, i, width) for hh in range(2)])
        thr = _kth_largest(score_ref, width, k_top)
        keep_tail = (score_ref[head_w:width, :] >= thr) & tail_ok
        cnt = _col_reduce(jnp.where(keep_tail, 1.0, 0.0), jnp.sum)
        neg_ref[head_w:width, :] = jnp.where(keep_tail, 0.0, NEG_INF)
        if head_w:
            keep_head = score_ref[:head_w, :] >= thr
            cnt = cnt + _col_reduce(jnp.where(keep_head, 1.0, 0.0), jnp.sum)
            neg_ref[:head_w, :] = jnp.where(keep_head, 0.0, NEG_INF)

        @pl.when(jnp.max(cnt) > k_top)
        def _():
            x = score_ref[:width, :]
            gt = x > thr
            need = k_top - _col_reduce(jnp.where(gt, 1.0, 0.0), jnp.sum)
            row = lax.broadcasted_iota(jnp.int32, (width, QB), 0)
            neg_ref[:width, :] = jnp.where(gt | _first_ties(x == thr, need, row), 0.0, NEG_INF)

        outs = []
        for g in range(2):
            sl = slice(g * LANES, (g + 1) * LANES)
            es, ls = [], []
            for hh in range(2):
                s = s2s[g][hh] + neg_ref[:width, :]
                e = jnp.exp2(s - _col_reduce(s, jnp.max))
                ls.append(_col_reduce(e, jnp.sum))
                es.append(e.astype(BF16))
            ot = _dot(vt_ref[sl, :width], jnp.concatenate(es, axis=1))
            outs.append(ot[:HEAD_DIM, :QB] / ls[0])
            outs.append(ot[HEAD_DIM:, QB:] / ls[1])
        y_t = jnp.concatenate(outs, axis=0)
        y = jnp.concatenate([y_t[:LANES].T, y_t[LANES:].T], axis=1)
        o_ref[...] = (y * _silu(g_ref[...])).astype(BF16)

    _by_causal_width(i, seq, body)


def _sparse_attn(p32, p16, bias, k_top):
    bsz, seq, _ = p32.shape
    nq = seq // QB
    return pl.pallas_call(
        functools.partial(_sparse_attn_kernel, k_top=k_top),
        grid=(bsz, nq),
        in_specs=[pl.BlockSpec((None, QB, 256), lambda b, i: (b, i, P_CQ2 // 256)),
                  pl.BlockSpec((None, seq, 256), lambda b, i: (b, 0, Q_CK // 256)),
                  pl.BlockSpec((None, seq, 256), lambda b, i: (b, 0, P_CV // 256)),
                  pl.BlockSpec((None, QB, 256), lambda b, i: (b, i, Q_CQI // 256)),
                  pl.BlockSpec((None, seq, LANES), lambda b, i: (b, 0, Q_CKI // LANES)),
                  pl.BlockSpec((None, QB, LANES), lambda b, i: (b, i, P_CWI // LANES)),
                  pl.BlockSpec((None, QB, 256), lambda b, i: (b, i, P_CG // 256)),
                  pl.BlockSpec((4, nq + 1, QB, QB), lambda b, i: (0, 0, 0, 0))],
        out_specs=pl.BlockSpec((None, QB, BRANCH_WIDTH), lambda b, i: (b, i, 0)),
        out_shape=jax.ShapeDtypeStruct((bsz, seq, BRANCH_WIDTH), BF16),
        scratch_shapes=[pltpu.VMEM((seq, QB), F32), pltpu.VMEM((seq, QB), F32),
                        pltpu.VMEM((BRANCH_WIDTH, seq), BF16)],
        compiler_params=_params(("arbitrary", "arbitrary")),
        name="sparse_attn",
    )(p32, p16, p32, p16, p16, p32, p32, bias)


def _dilated_kernel(x_ref, g_ref, bias_ref, o_ref, m_ref, l_ref, acc_ref):
    first = lax.broadcasted_iota(jnp.int32, (QB, LANES), 1) < HEAD_DIM
    hmask = [_lane_mask(HEAD_DIM, hh, F32) for hh in range(2)]

    def run(p, tiles, sink):
        work = []
        for cur, prev, variant in tiles:
            for g in range(2):
                q = x_ref[g, cur, :]
                kcat = x_ref[2 + g, cur, :]
                vcat = x_ref[4 + g, cur, :]
                if prev is not None:
                    kcat = jnp.concatenate([x_ref[2 + g, prev, :], kcat], axis=0)
                    vcat = jnp.concatenate([x_ref[4 + g, prev, :], vcat], axis=0)
                kcat = kcat.astype(BF16)
                scores = []
                for hh in range(2):
                    if prev is not None:
                        bias = bias_ref[2 * g + hh, 2 * p + variant]
                    else:
                        bias = bias_ref[2 * g + hh, 2 * p, :, QB:]
                    scores.append(_dot_nt((q * hmask[hh]).astype(BF16), kcat) + bias)
                work.append((vcat.astype(BF16), scores))
        for n, (vcat, scores) in enumerate(work):
            ms, ls, accs = [], [], []
            for s in scores:
                m = jnp.max(s, axis=-1, keepdims=True)
                e = jnp.exp2(s - m)
                ms.append(m)
                ls.append(jnp.sum(e, axis=-1, keepdims=True))
                accs.append(_dot(e.astype(BF16), vcat))
            sink(n // 2, n % 2, jnp.where(first, ms[0], ms[1]), jnp.where(first, ls[0], ls[1]),
                 jnp.where(first, accs[0], accs[1]))

    def store_stats(slot, d, starts):
        def sink(t, g, m, l, a):
            rows = pl.ds(starts[t], QB, stride=d)
            m_ref[slot, g, rows, :] = m
            l_ref[slot, g, rows, :] = l
            acc_ref[slot, g, rows, :] = a
        return sink

    def step16(n0, carry):
        starts = [n0 * DIL_UNROLL + u for u in range(DIL_UNROLL)]
        run(2, [(pl.ds(r, QB, stride=16), None, 0) for r in starts], store_stats(1, 16, starts))
        return carry

    lax.fori_loop(0, 16 // DIL_UNROLL, step16, 0)

    def step4(j, carry):
        starts = [r + 4 * QB * j for r in range(4)]
        tiles = [(pl.ds(s, QB, stride=4), pl.ds(jnp.maximum(s - 4 * QB, r), QB, stride=4), jnp.minimum(j, 1))
                 for r, s in enumerate(starts)]
        run(1, tiles, store_stats(0, 4, starts))
        return carry

    lax.fori_loop(0, 4, step4, 0)

    def step1(n0, carry):
        blocks = [n0 * DIL_UNROLL + u for u in range(DIL_UNROLL)]

        def sink(t, g, m, l, a):
            rows = pl.ds(pl.multiple_of(blocks[t] * QB, QB), QB)
            lanes = slice(g * LANES, (g + 1) * LANES)
            m4, m16 = m_ref[0, g, rows, :], m_ref[1, g, rows, :]
            m_tot = jnp.maximum(m, jnp.maximum(m4, m16))
            w1, w4, w16 = jnp.exp2(m - m_tot), jnp.exp2(m4 - m_tot), jnp.exp2(m16 - m_tot)
            num = w1 * a + w4 * acc_ref[0, g, rows, :] + w16 * acc_ref[1, g, rows, :]
            den = w1 * l + w4 * l_ref[0, g, rows, :] + w16 * l_ref[1, g, rows, :]
            o_ref[rows, lanes] = (num / den * _silu(g_ref[rows, lanes])).astype(BF16)

        tiles = []
        for j in blocks:
            cur = pl.ds(pl.multiple_of(j * QB, QB), QB)
            prev = pl.ds(pl.multiple_of(jnp.maximum(j - 1, 0) * QB, QB), QB)
            tiles.append((cur, prev, jnp.minimum(j, 1)))
        run(0, tiles, sink)
        return carry

    lax.fori_loop(0, g_ref.shape[0] // QB // DIL_UNROLL, step1, 0)


def _dilated_attn(pb, p32, bias):
    bsz, _, seq, _ = pb.shape
    stats = pltpu.VMEM((2, 2, seq, LANES), F32)
    return pl.pallas_call(
        _dilated_kernel,
        grid=(bsz,),
        in_specs=[pl.BlockSpec((None, B_SLABS, seq, LANES), lambda b: (b, 0, 0, 0)),
                  pl.BlockSpec((None, seq, 256), lambda b: (b, 0, P_BG // 256)),
                  pl.BlockSpec((4, 2 * len(DILATED_PATTERNS), QB, 2 * QB), lambda b: (0, 0, 0, 0))],
        out_specs=pl.BlockSpec((None, seq, BRANCH_WIDTH), lambda b: (b, 0, 0)),
        out_shape=jax.ShapeDtypeStruct((bsz, seq, BRANCH_WIDTH), BF16),
        scratch_shapes=[stats, stats, stats],
        compiler_params=_params(("arbitrary",)),
        name="dilated_attn",
    )(pb, p32, bias)


def _out_proj_kernel(ya_ref, yb_ref, yc_ref, yd_ref, w_ref, h_ref, g_ref, o_ref):
    y = None
    for n, ref in enumerate((ya_ref, yb_ref, yc_ref, yd_ref)):
        t = _dot(ref[...], w_ref[n * BRANCH_WIDTH:(n + 1) * BRANCH_WIDTH, :])
        y = t if y is None else y + t
    ms = jnp.mean(y * y, axis=-1, keepdims=True)
    o_ref[...] = h_ref[...] + y * lax.rsqrt(ms + EPS) * g_ref[...]


def _out_proj(ya, yb, yc, yd, w, layer, h, g, tm=512):
    bsz, seq, _ = h.shape
    yspec = pl.BlockSpec((None, tm, BRANCH_WIDTH), lambda b, i: (b, i, 0))
    return pl.pallas_call(
        _out_proj_kernel,
        grid=(bsz, seq // tm),
        in_specs=[yspec, yspec, yspec, yspec,
                  pl.BlockSpec((None, 4 * BRANCH_WIDTH, D_MODEL), lambda b, i: (layer, 0, 0)),
                  pl.BlockSpec((None, tm, D_MODEL), lambda b, i: (b, i, 0)),
                  pl.BlockSpec((1, D_MODEL), lambda b, i: (0, 0))],
        out_specs=pl.BlockSpec((None, tm, D_MODEL), lambda b, i: (b, i, 0)),
        out_shape=jax.ShapeDtypeStruct(h.shape, F32),
        compiler_params=_params(("arbitrary", "arbitrary")),
        name="out_proj",
    )(ya, yb, yc, yd, w, h, g)


def kernel(x, w_in, w_out, norm_pre, norm_post, mla_q_norm, mla_kv_norm, mla_w_uq, mla_w_ukv,
           diff_lambda, diff_subln, rel_bias):
    bsz, seq, _ = x.shape
    depth = w_in.shape[0]
    nq = seq // QB
    k_top = min(IDX_TOPK_MAX, seq // 4)
    cos_t, sin_t = _rope_tables(seq)
    bias_b = _bias_expand(rel_bias, jnp.asarray(_bucket_tiles_dilated()), 0, 4, LOG2E)
    causal_buckets_t = jnp.asarray(np.swapaxes(_bucket_tiles_causal(nq), 1, 2))
    bias_c = _bias_expand(rel_bias, causal_buckets_t, 4, 4, LOG2E)
    bias_d = _bias_expand(rel_bias, causal_buckets_t, 8, 4, LOG2E)
    mask_tiles = jnp.asarray(_mask_tiles())
    w_arr_all = _arrange_w_in(w_in)
    w_out16 = w_out.astype(BF16)
    h = x
    for layer in range(depth):
        wq, wqrot, wk, wv = _arrange_mla(mla_w_uq[layer], mla_w_ukv[layer])
        gq = jnp.concatenate([mla_q_norm[layer], jnp.ones((256 - Q_LORA,), F32)])[None, :]
        gkv = mla_kv_norm[layer][None, :]
        p32, p16, pb, qa, ka, va = _in_proj(h, norm_pre[layer][None, :], w_arr_all, layer,
                                            cos_t, sin_t, gq, gkv, wq, wqrot, wk, wv)
        y_a = _mla_attn(qa, ka, va, p32, mask_tiles)
        y_b = _dilated_attn(pb, p32, bias_b)
        y_c = _sparse_attn(p32, p16, bias_c, k_top)
        lambda_init = 0.8 - 0.6 * math.exp(-0.3 * layer)
        subln = jnp.tile(diff_subln[layer], BRANCH_WIDTH // HEAD_DIM)[None, :]
        y_d = _diff_attn(p32, p16, bias_d, diff_lambda[layer], subln, lambda_init)
        h = _out_proj(y_a, y_b, y_c, y_d, w_out16, layer, h, norm_post[layer][None, :])
    return h
```

```python
import functools
import math

import jax
import jax.numpy as jnp
import numpy as np
from jax import lax
from jax.experimental import pallas as pl
from jax.experimental.pallas import tpu as pltpu

F32 = jnp.float32
BF16 = jnp.bfloat16

D_MODEL = 1024
A_HEADS, A_NOPE, A_ROPE, A_V = 4, 64, 32, 64
Q_LORA, KV_LORA = 192, 128
ROPE_THETA = 10000.0
HEAD_DIM = 64
DILATED_PATTERNS = ((128, 1), (512, 4), (2048, 16))
IDX_HEADS, IDX_DIM, IDX_TOPK_MAX = 8, 32, 256
D_QK = 32
BRANCH_WIDTH = 256
NUM_BUCKETS, MAX_DISTANCE = 32, 2048
NEG_INF = -1e30
EPS = 1e-6
LOG2E = math.log2(math.e)
KV_CHUNK = 256
COL_ACC_ROWS = 64
DIL_UNROLL = 4
LANES = 128
QB = 128
VMEM_LIMIT = 56 * 1024 * 1024

_SPLIT = (Q_LORA, KV_LORA, A_ROPE, 256, 256, 256, 256, 256, 256, 256, 256, IDX_HEADS * IDX_DIM, IDX_DIM,
          IDX_HEADS, 256, 256, 256, 256, 256)
_OFF = np.concatenate([[0], np.cumsum(_SPLIT)]).tolist()
(_A_CQ, _A_CKV, _A_KR, _A_G, _B_Q, _B_K, _B_V, _B_G, _C_Q, _C_K, _C_V, _C_QI, _C_KI, _C_WI, _C_G,
 _D_Q, _D_K, _D_V, _D_G) = range(19)

L_CQ, L_CKV, L_KR, L_KRROT = 0, 256, 384, 512
P_START = 512
P_CWI, P_AG, P_BG, P_CQ2, P_CV, P_CG, P_DQ, P_DV, P_DG = 128, 256, 512, 768, 1024, 1280, 1536, 1792, 2048
NCOL32 = 2304
Q_START = P_START + NCOL32
Q_CK, Q_CQI, Q_DK, Q_CKI = 0, 256, 512, 768
NCOL16 = 896
B_START = Q_START + NCOL16
B_SLABS = 6
NCOL = B_START + B_SLABS * LANES


def _dot(a, b):
    return jnp.dot(a, b, preferred_element_type=F32)


def _dot_nt(a, b):
    return lax.dot_general(a, b, (((1,), (1,)), ((), ())), preferred_element_type=F32)


def _params(sem):
    return pltpu.CompilerParams(dimension_semantics=sem, vmem_limit_bytes=VMEM_LIMIT)


def _rot_cols(w):
    half = w.shape[-1] // 2
    return jnp.concatenate([-w[..., half:], w[..., :half]], axis=-1)


def _arrange_w_in(w):
    depth, rows, in_cols = w.shape
    tm = 256
    return pl.pallas_call(
        _arrange_w_in_kernel,
        grid=(depth, rows // tm),
        in_specs=[pl.BlockSpec((None, tm, in_cols), lambda l, i: (l, i, 0))],
        out_specs=pl.BlockSpec((None, tm, NCOL), lambda l, i: (l, i, 0)),
        out_shape=jax.ShapeDtypeStruct((depth, rows, NCOL), BF16),
        compiler_params=_params(("arbitrary", "arbitrary")),
        name="arrange_w_in",
    )(w)


def _w_in_pieces():
    pieces = []

    def put(dst, seg, lo=0, hi=None, scale=1.0):
        hi = _SPLIT[seg] if hi is None else hi
        pieces.append((dst, _OFF[seg] + lo, hi - lo, scale))
        return dst + hi - lo

    put(L_CQ, _A_CQ)
    put(L_CKV, _A_CKV)
    put(L_KR + A_NOPE, _A_KR)
    half = A_ROPE // 2
    put(L_KRROT + A_NOPE, _A_KR, half, A_ROPE, scale=-1.0)
    put(L_KRROT + A_NOPE + half, _A_KR, 0, half)
    for dst, seg in ((P_CWI, _C_WI), (P_AG, _A_G), (P_BG, _B_G), (P_CQ2, _C_Q), (P_CV, _C_V), (P_CG, _C_G),
                     (P_DQ, _D_Q), (P_DV, _D_V), (P_DG, _D_G)):
        put(P_START + dst, seg)
    for dst, seg in ((Q_CK, _C_K), (Q_CQI, _C_QI), (Q_DK, _D_K)):
        put(Q_START + dst, seg)
    for copy in range(LANES // IDX_DIM):
        put(Q_START + Q_CKI + copy * IDX_DIM, _C_KI)
    base = put(B_START, _B_Q, scale=HEAD_DIM ** -0.5 * LOG2E)
    base = put(base, _B_K)
    base = put(base, _B_V)
    assert base == NCOL
    return pieces


def _arrange_w_in_kernel(w_ref, o_ref):
    o_ref[...] = jnp.zeros(o_ref.shape, o_ref.dtype)
    for dst, src, n, scale in _w_in_pieces():
        lo = src // LANES * LANES
        hi = min(-(-(src + n) // LANES) * LANES, w_ref.shape[1])
        v = w_ref[:, lo:hi][:, src - lo:src - lo + n]
        o_ref[:, dst:dst + n] = (v * scale if scale != 1.0 else v).astype(o_ref.dtype)


def _arrange_mla(w_uq, w_ukv):
    wq = w_uq.reshape(Q_LORA, A_HEADS, A_NOPE + A_ROPE)
    nope, rope = wq[..., :A_NOPE], wq[..., A_NOPE:]
    zq = jnp.zeros((Q_LORA, A_HEADS, LANES - A_NOPE - A_ROPE), w_uq.dtype)
    wq_main = jnp.concatenate([nope, rope, zq], axis=-1).reshape(Q_LORA, A_HEADS * LANES)
    wq_rot = jnp.concatenate([jnp.zeros_like(nope), _rot_cols(rope), zq], axis=-1).reshape(Q_LORA, A_HEADS * LANES)
    pad = jnp.zeros((256 - Q_LORA, A_HEADS * LANES), w_uq.dtype)
    wq_main = jnp.concatenate([wq_main, pad], axis=0)
    wq_rot = jnp.concatenate([wq_rot, pad], axis=0)
    wkv = w_ukv.reshape(KV_LORA, A_HEADS, A_NOPE + A_V)
    knope, v = wkv[..., :A_NOPE], wkv[..., A_NOPE:]
    wk = jnp.concatenate([knope, jnp.zeros_like(knope)], axis=-1).reshape(KV_LORA, A_HEADS * LANES)
    wv_t = v.reshape(KV_LORA, A_HEADS * A_V).T
    return wq_main.astype(BF16), wq_rot.astype(BF16), wk.astype(BF16), wv_t.astype(BF16)


def _rope_tables(seq):
    inv = ROPE_THETA ** (-jnp.arange(0, A_ROPE, 2, dtype=F32) / A_ROPE)
    ang = jnp.arange(seq, dtype=F32)[:, None] * inv[None, :]
    cos, sin = jnp.cos(ang), jnp.sin(ang)
    one = jnp.ones((seq, A_NOPE), F32)
    zero = jnp.zeros((seq, LANES - A_NOPE - A_ROPE), F32)
    cos_t = jnp.concatenate([one, cos, cos, zero], axis=1)
    sin_t = jnp.concatenate([jnp.zeros_like(one), sin, sin, zero], axis=1)
    return cos_t, sin_t


def _t5_bucket_np(rel):
    n = np.maximum(rel, 0)
    max_exact = NUM_BUCKETS // 2
    nf = np.maximum(n, max_exact).astype(np.float64)
    large = max_exact + (np.log(nf / max_exact) / math.log(MAX_DISTANCE / max_exact)
                         * (NUM_BUCKETS - max_exact)).astype(np.int32)
    large = np.minimum(large, NUM_BUCKETS - 1)
    return np.where(n < max_exact, n, large).astype(np.int32)


MASKED_BUCKET = NUM_BUCKETS


def _bucket_tiles_causal(nq):
    q = np.arange(QB)[:, None]
    k = np.arange(QB)[None, :]
    tiles = [np.full((QB, QB), MASKED_BUCKET, np.int32)]
    for d in range(nq):
        rel = QB * d + q - k
        tiles.append(np.where(rel >= 0, _t5_bucket_np(rel), MASKED_BUCKET).astype(np.int32))
    return np.stack(tiles)


def _mask_tiles():
    k = np.arange(QB)[:, None]
    q = np.arange(QB)[None, :]
    diag = np.where(k <= q, 0.0, NEG_INF)
    return np.stack([np.full((QB, QB), NEG_INF), diag, np.zeros((QB, QB))]).astype(np.float32)


def _bucket_tiles_dilated():
    q = np.arange(QB)[:, None]
    k = np.arange(2 * QB)[None, :]
    rel = q + QB - k
    tiles = []
    for (window, d) in DILATED_PATTERNS:
        in_band = (rel >= 0) & (rel <= window // d)
        for has_prev in (False, True):
            ok = in_band & (has_prev | (k >= QB))
            tiles.append(np.where(ok, _t5_bucket_np(rel * d), MASKED_BUCKET).astype(np.int32))
    return np.stack(tiles)


def _bias_expand_kernel(table_ref, bucket_ref, out_ref, *, head0, scale):
    h = pl.program_id(0) + head0
    bk = bucket_ref[...]
    acc = jnp.where(bk == MASKED_BUCKET, NEG_INF, 0.0)
    for b in range(NUM_BUCKETS):
        acc = jnp.where(bk == b, table_ref[b, h] * scale, acc)
    out_ref[...] = acc


def _bias_expand(table, buckets, head0, nheads, scale=1.0):
    n, r, c = buckets.shape
    return pl.pallas_call(
        functools.partial(_bias_expand_kernel, head0=head0, scale=scale),
        grid=(nheads,),
        in_specs=[pl.BlockSpec(memory_space=pltpu.SMEM),
                  pl.BlockSpec((n, r, c), lambda h: (0, 0, 0))],
        out_specs=pl.BlockSpec((None, n, r, c), lambda h: (h, 0, 0, 0)),
        out_shape=jax.ShapeDtypeStruct((nheads, n, r, c), F32),
        compiler_params=_params(("arbitrary",)),
        name="bias_expand",
    )(table, buckets)


def _in_proj_kernel(x_ref, g_ref, w_ref, cos_ref, sin_ref, gq_ref, gkv_ref, wq_ref, wqrot_ref, wk_ref, wv_ref,
                    o32_ref, o16_ref, ob_ref, q_ref, k_ref, v_ref):
    x = x_ref[...]
    ms = jnp.mean(x * x, axis=-1, keepdims=True)
    xn = x * lax.rsqrt(ms + EPS) * g_ref[...]
    p = _dot(xn.astype(BF16), w_ref[...])
    o32_ref[...] = p[:, P_START:Q_START]
    o16_ref[...] = p[:, Q_START:B_START].astype(BF16)
    for s in range(B_SLABS):
        ob_ref[s] = p[:, B_START + s * LANES:B_START + (s + 1) * LANES]
    cos = cos_ref[...]
    sin = sin_ref[...]
    cos4 = jnp.concatenate([cos] * A_HEADS, axis=1)
    sin4 = jnp.concatenate([sin] * A_HEADS, axis=1)
    cq = p[:, L_CQ:L_CQ + 256]
    ms = jnp.sum(cq * cq, axis=-1, keepdims=True) * (1.0 / Q_LORA)
    nq = (cq * lax.rsqrt(ms + EPS) * gq_ref[...]).astype(BF16)
    q = _dot(nq, wq_ref[...]) * cos4 + _dot(nq, wqrot_ref[...]) * sin4
    q_ref[...] = (q * ((A_NOPE + A_ROPE) ** -0.5 * LOG2E)).astype(BF16)
    ckv = p[:, L_CKV:L_CKV + KV_LORA]
    ms = jnp.mean(ckv * ckv, axis=-1, keepdims=True)
    nkv = (ckv * lax.rsqrt(ms + EPS) * gkv_ref[...]).astype(BF16)
    kr = p[:, L_KR:L_KR + LANES] * cos + p[:, L_KRROT:L_KRROT + LANES] * sin
    k = _dot(nkv, wk_ref[...]) + jnp.concatenate([kr] * A_HEADS, axis=1)
    k_ref[...] = k.astype(BF16)
    v_ref[...] = _dot_nt(wv_ref[...], nkv).astype(BF16)


def _in_proj(h, g, w, layer, cos_t, sin_t, gq, gkv, wq, wqrot, wk, wv_t, tm=256):
    bsz, seq, _ = h.shape
    w4 = A_HEADS * LANES
    row = lambda b, i: (b, i, 0)
    const = lambda b, i: (0, 0)
    return pl.pallas_call(
        _in_proj_kernel,
        grid=(bsz, seq // tm),
        in_specs=[pl.BlockSpec((None, tm, D_MODEL), row),
                  pl.BlockSpec((1, D_MODEL), const),
                  pl.BlockSpec((None, D_MODEL, NCOL), lambda b, i: (layer, 0, 0)),
                  pl.BlockSpec((tm, LANES), lambda b, i: (i, 0)),
                  pl.BlockSpec((tm, LANES), lambda b, i: (i, 0)),
                  pl.BlockSpec((1, 256), const),
                  pl.BlockSpec((1, KV_LORA), const),
                  pl.BlockSpec((256, w4), const),
                  pl.BlockSpec((256, w4), const),
                  pl.BlockSpec((KV_LORA, w4), const),
                  pl.BlockSpec((A_HEADS * A_V, KV_LORA), const)],
        out_specs=[pl.BlockSpec((None, tm, NCOL32), row),
                   pl.BlockSpec((None, tm, NCOL16), row),
                   pl.BlockSpec((None, B_SLABS, tm, LANES), lambda b, i: (b, 0, i, 0)),
                   pl.BlockSpec((None, tm, w4), row), pl.BlockSpec((None, tm, w4), row),
                   pl.BlockSpec((None, A_HEADS * A_V, tm), lambda b, i: (b, 0, i))],
        out_shape=[jax.ShapeDtypeStruct((bsz, seq, NCOL32), F32),
                   jax.ShapeDtypeStruct((bsz, seq, NCOL16), BF16),
                   jax.ShapeDtypeStruct((bsz, B_SLABS, seq, LANES), F32),
                   jax.ShapeDtypeStruct((bsz, seq, w4), BF16), jax.ShapeDtypeStruct((bsz, seq, w4), BF16),
                   jax.ShapeDtypeStruct((bsz, A_HEADS * A_V, seq), BF16)],
        compiler_params=_params(("arbitrary", "arbitrary")),
        name="in_proj",
    )(h, g, w, cos_t, sin_t, gq, gkv, wq, wqrot, wk, wv_t)


def _silu(g):
    return g * (1.0 / (1.0 + jnp.exp(-g)))


def _for_step_blocks(i, seq, block_body):
    per = KV_CHUNK // QB
    for wb in range(seq // KV_CHUNK):
        @pl.when(i == wb)
        def _(width=(wb + 1) * KV_CHUNK):
            def one(u, carry):
                block_body(width, per * i + u, pl.ds(pl.multiple_of(u * QB, QB), QB))
                return carry

            lax.fori_loop(0, per, one, 0)


def _mask_tail_t(mask_ref, i, width):
    first = (width - KV_CHUNK) // QB
    return jnp.concatenate([mask_ref[jnp.clip(i - j, -1, 1) + 1] for j in range(first, width // QB)], axis=0)


def _col_reduce(x, op):
    rows, lanes = x.shape
    part = op(x.reshape(rows // COL_ACC_ROWS, COL_ACC_ROWS, lanes), axis=0)
    return op(part, axis=0, keepdims=True)


def _bias_col(bias_ref, h, i, width):
    return jnp.concatenate([bias_ref[h, jnp.maximum(i - j, -1) + 1] for j in range(width // QB)], axis=0)


def _lane_mask(width, seg, dtype):
    lane = lax.broadcasted_iota(jnp.int32, (1, LANES), 1)
    return jnp.where((lane >= seg * width) & (lane < (seg + 1) * width), 1.0, 0.0).astype(dtype)


def _mla_attn_kernel(q_ref, k_ref, vt_ref, g_ref, mask_ref, o_ref):
    i = pl.program_id(1)
    seq = k_ref.shape[0]

    def body(width, ib, rows):
        head_w = width - KV_CHUNK
        zero = jnp.zeros((QB, LANES), BF16)
        scores = []
        for g in range(A_HEADS // 2):
            q0 = q_ref[rows, 2 * g * LANES:(2 * g + 1) * LANES]
            q1 = q_ref[rows, (2 * g + 1) * LANES:(2 * g + 2) * LANES]
            qbd = jnp.concatenate([jnp.concatenate([q0, zero], axis=1),
                                   jnp.concatenate([zero, q1], axis=1)], axis=0)
            scores.append(_dot_nt(k_ref[:width, 2 * g * LANES:(2 * g + 2) * LANES], qbd))
        tail = _mask_tail_t(mask_ref, ib, width)
        outs = []
        for g in range(A_HEADS // 2):
            es, ls = [], []
            for hh in range(2):
                s = scores[g][:, hh * QB:(hh + 1) * QB]
                s = jnp.concatenate([s[:head_w], s[head_w:] + tail], axis=0) if head_w else s + tail
                e = jnp.exp2(s - _col_reduce(s, jnp.max))
                ls.append(_col_reduce(e, jnp.sum))
                es.append(e.astype(BF16))
            ot = _dot(vt_ref[g * LANES:(g + 1) * LANES, :width], jnp.concatenate(es, axis=1))
            outs.append(ot[:A_V, :QB] / ls[0])
            outs.append(ot[A_V:, QB:] / ls[1])
        y_t = jnp.concatenate(outs, axis=0)
        y = jnp.concatenate([y_t[:LANES].T, y_t[LANES:].T], axis=1)
        o_ref[rows, :] = (y * _silu(g_ref[rows, :])).astype(BF16)

    _for_step_blocks(i, seq, body)


def _mla_attn(q, k, v_t, p32, mask_tiles):
    bsz, seq, w4 = q.shape
    return pl.pallas_call(
        _mla_attn_kernel,
        grid=(bsz, seq // KV_CHUNK),
        in_specs=[pl.BlockSpec((None, KV_CHUNK, w4), lambda b, i: (b, i, 0)),
                  pl.BlockSpec((None, seq, w4), lambda b, i: (b, 0, 0)),
                  pl.BlockSpec((None, A_HEADS * A_V, seq), lambda b, i: (b, 0, 0)),
                  pl.BlockSpec((None, KV_CHUNK, 256), lambda b, i: (b, i, P_AG // 256)),
                  pl.BlockSpec((3, QB, QB), lambda b, i: (0, 0, 0))],
        out_specs=pl.BlockSpec((None, KV_CHUNK, BRANCH_WIDTH), lambda b, i: (b, i, 0)),
        out_shape=jax.ShapeDtypeStruct((bsz, seq, BRANCH_WIDTH), BF16),
        compiler_params=_params(("arbitrary", "arbitrary")),
        name="mla_attn",
    )(q, k, v_t, p32, mask_tiles)


def _transpose_values(v_ref, vt_ref):
    for j in range(v_ref.shape[0] // QB):
        for g in range(v_ref.shape[1] // LANES):
            tile = v_ref[j * QB:(j + 1) * QB, g * LANES:(g + 1) * LANES]
            vt_ref[g * LANES:(g + 1) * LANES, j * QB:(j + 1) * QB] = tile.T.astype(BF16)


def _diff_attn_kernel(q_ref, k_ref, v_ref, g_ref, bias_ref, lam_ref, subln_ref, o_ref, vt_ref, *, lambda_init):
    i = pl.program_id(1)
    seq = k_ref.shape[0]

    @pl.when(i == 0)
    def _():
        _transpose_values(v_ref, vt_ref)

    def body(width, ib, rows):
        lp = lam_ref[...]
        lam = (jnp.exp(jnp.sum(lp[0:1] * lp[1:2], axis=-1, keepdims=True))
               - jnp.exp(jnp.sum(lp[2:3] * lp[3:4], axis=-1, keepdims=True)) + lambda_init)
        scores = []
        for h in range(4):
            sl = slice((h // 2) * LANES, (h // 2 + 1) * LANES)
            qg = q_ref[rows, sl] * (D_QK ** -0.5 * LOG2E)
            qcat = jnp.concatenate([(qg * _lane_mask(D_QK, 2 * (h % 2) + mm, F32)).astype(BF16)
                                    for mm in range(2)], axis=0)
            scores.append(_dot_nt(k_ref[:width, sl], qcat))
        outs = []
        for h in range(4):
            bias = _bias_col(bias_ref, h, ib, width)
            es, ls = [], []
            for mm in range(2):
                s = scores[h][:, mm * QB:(mm + 1) * QB] + bias
                e = jnp.exp2(s - _col_reduce(s, jnp.max))
                ls.append(_col_reduce(e, jnp.sum))
                es.append(e.astype(BF16))
            ot = _dot(vt_ref[h * HEAD_DIM:(h + 1) * HEAD_DIM, :width], jnp.concatenate(es, axis=1))
            a = ot[:, :QB] / ls[0] - lam * (ot[:, QB:] / ls[1])
            ms = jnp.sum(a * a, axis=0, keepdims=True) * (1.0 / HEAD_DIM)
            outs.append(a * lax.rsqrt(ms + EPS))
        y_t = jnp.concatenate(outs, axis=0)
        y = jnp.concatenate([y_t[:LANES].T, y_t[LANES:].T], axis=1)
        o_ref[rows, :] = (y * (subln_ref[...] * (1.0 - lambda_init)) * _silu(g_ref[rows, :])).astype(BF16)

    _for_step_blocks(i, seq, body)


def _diff_attn(p32, p16, bias, lam_params, subln, lambda_init):
    bsz, seq, _ = p32.shape
    nq = seq // QB
    return pl.pallas_call(
        functools.partial(_diff_attn_kernel, lambda_init=lambda_init),
        grid=(bsz, seq // KV_CHUNK),
        in_specs=[pl.BlockSpec((None, KV_CHUNK, 256), lambda b, i: (b, i, P_DQ // 256)),
                  pl.BlockSpec((None, seq, 256), lambda b, i: (b, 0, Q_DK // 256)),
                  pl.BlockSpec((None, seq, 256), lambda b, i: (b, 0, P_DV // 256)),
                  pl.BlockSpec((None, KV_CHUNK, 256), lambda b, i: (b, i, P_DG // 256)),
                  pl.BlockSpec((4, nq + 1, QB, QB), lambda b, i: (0, 0, 0, 0)),
                  pl.BlockSpec((4, D_QK), lambda b, i: (0, 0)),
                  pl.BlockSpec((1, BRANCH_WIDTH), lambda b, i: (0, 0))],
        out_specs=pl.BlockSpec((None, KV_CHUNK, BRANCH_WIDTH), lambda b, i: (b, i, 0)),
        out_shape=jax.ShapeDtypeStruct((bsz, seq, BRANCH_WIDTH), BF16),
        scratch_shapes=[pltpu.VMEM((BRANCH_WIDTH, seq), BF16)],
        compiler_params=_params(("arbitrary", "arbitrary")),
        name="diff_attn",
    )(p32, p16, p32, p32, bias, lam_params, subln)


def _sortable_to_float(key):
    return pltpu.bitcast(jnp.where(key < 0, key ^ jnp.int32(0x7FFFFFFF), key), F32)


def _kth_largest(score_ref, width, k_top):
    def count_ge(key):
        thr = _sortable_to_float(key)
        return _col_reduce(jnp.where(score_ref[:width, :] >= thr, 1.0, 0.0), jnp.sum)

    int_min = jnp.full((1, QB), -2 ** 31, jnp.int32)
    zero = jnp.zeros((1, QB), jnp.int32)
    t = jnp.where(count_ge(zero) >= k_top, zero, int_min)

    def step(it, t):
        cand = t + (jnp.int32(1) << (30 - it))
        return jnp.where(count_ge(cand) >= k_top, cand, t)

    return _sortable_to_float(lax.fori_loop(0, 31, step, t))


def _first_ties(eq, need, row):
    eqf = jnp.where(eq, 1.0, 0.0)
    nbits = int(eq.shape[0]).bit_length()

    def body(it, j):
        cand = j + (jnp.int32(1) << (nbits - 1 - it))
        cnt = _col_reduce(jnp.where(row < cand, eqf, 0.0), jnp.sum)
        return jnp.where(cnt <= need, cand, j)

    j = lax.fori_loop(0, nbits, body, jnp.zeros((1, eq.shape[1]), jnp.int32))
    return eq & (row < j)


def _sparse_attn_kernel(q_ref, k_ref, v_ref, qi_ref, ki_ref, wi_ref, g_ref, bias_ref, o_ref,
                        score_ref, neg_ref, vt_ref, *, k_top):
    i = pl.program_id(1)
    seq = k_ref.shape[0]

    @pl.when(i == 0)
    def _():
        _transpose_values(v_ref, vt_ref)

    def body(width, ib, rows):
        head_w = width - KV_CHUNK
        ki = ki_ref[:width, :]
        w_t = wi_ref[rows, :].T * (IDX_DIM ** -0.5 * IDX_HEADS ** -0.5)
        score = None
        for h in range(0, IDX_HEADS, 2):
            qg = qi_ref[rows, (h // 4) * LANES:(h // 4 + 1) * LANES]
            qcat = jnp.concatenate([qg * _lane_mask(IDX_DIM, h % 4, BF16),
                                    qg * _lane_mask(IDX_DIM, h % 4 + 1, BF16)], axis=0)
            logit = _dot_nt(ki, qcat)
            term = (jnp.maximum(logit[:, :QB], 0.0) * w_t[h:h + 1]
                    + jnp.maximum(logit[:, QB:], 0.0) * w_t[h + 1:h + 2])
            score = term if score is None else score + term
        s_idx = head_w + lax.broadcasted_iota(jnp.int32, (KV_CHUNK, QB), 0)
        tail_ok = s_idx <= ib * QB + lax.broadcasted_iota(jnp.int32, (KV_CHUNK, QB), 1)
        if head_w:
            score_ref[:head_w, :] = score[:head_w]
        score_ref[head_w:width, :] = jnp.where(tail_ok, score[head_w:], NEG_INF)
        s2s = []
        for g in range(2):
            sl = slice(g * LANES, (g + 1) * LANES)
            qg = q_ref[rows, sl] * (HEAD_DIM ** -0.5 * LOG2E)
            qcat = jnp.concatenate([(qg * _lane_mask(HEAD_DIM, hh, F32)).astype(BF16) for hh in range(2)], axis=0)
            s2 = _dot_nt(k_ref[:width, sl], qcat)
            s2s.append([s2[:, hh * QB:(hh + 1) * QB] + _bias_col(bias_ref, 2 * g + hh, ib, width) for hh in range(2)])
        thr = _kth_largest(score_ref, width, k_top)
        keep_tail = (score_ref[head_w:width, :] >= thr) & tail_ok
        cnt = _col_reduce(jnp.where(keep_tail, 1.0, 0.0), jnp.sum)
        neg_ref[head_w:width, :] = jnp.where(keep_tail, 0.0, NEG_INF)
        if head_w:
            keep_head = score_ref[:head_w, :] >= thr
            cnt = cnt + _col_reduce(jnp.where(keep_head, 1.0, 0.0), jnp.sum)
            neg_ref[:head_w, :] = jnp.where(keep_head, 0.0, NEG_INF)

        @pl.when(jnp.max(cnt) > k_top)
        def _():
            x = score_ref[:width, :]
            gt = x > thr
            need = k_top - _col_reduce(jnp.where(gt, 1.0, 0.0), jnp.sum)
            row = lax.broadcasted_iota(jnp.int32, (width, QB), 0)
            neg_ref[:width, :] = jnp.where(gt | _first_ties(x == thr, need, row), 0.0, NEG_INF)

        outs = []
        for g in range(2):
            sl = slice(g * LANES, (g + 1) * LANES)
            es, ls = [], []
            for hh in range(2):
                s = s2s[g][hh] + neg_ref[:width, :]
                e = jnp.exp2(s - _col_reduce(s, jnp.max))
                ls.append(_col_reduce(e, jnp.sum))
                es.append(e.astype(BF16))
            ot = _dot(vt_ref[sl, :width], jnp.concatenate(es, axis=1))
            outs.append(ot[:HEAD_DIM, :QB] / ls[0])
            outs.append(ot[HEAD_DIM:, QB:] / ls[1])
        y_t = jnp.concatenate(outs, axis=0)
        y = jnp.concatenate([y_t[:LANES].T, y_t[LANES:].T], axis=1)
        o_ref[rows, :] = (y * _silu(g_ref[rows, :])).astype(BF16)

    _for_step_blocks(i, seq, body)


def _sparse_attn(p32, p16, bias, k_top):
    bsz, seq, _ = p32.shape
    nq = seq // QB
    return pl.pallas_call(
        functools.partial(_sparse_attn_kernel, k_top=k_top),
        grid=(bsz, seq // KV_CHUNK),
        in_specs=[pl.BlockSpec((None, KV_CHUNK, 256), lambda b, i: (b, i, P_CQ2 // 256)),
                  pl.BlockSpec((None, seq, 256), lambda b, i: (b, 0, Q_CK // 256)),
                  pl.BlockSpec((None, seq, 256), lambda b, i: (b, 0, P_CV // 256)),
                  pl.BlockSpec((None, KV_CHUNK, 256), lambda b, i: (b, i, Q_CQI // 256)),
                  pl.BlockSpec((None, seq, LANES), lambda b, i: (b, 0, Q_CKI // LANES)),
                  pl.BlockSpec((None, KV_CHUNK, LANES), lambda b, i: (b, i, P_CWI // LANES)),
                  pl.BlockSpec((None, KV_CHUNK, 256), lambda b, i: (b, i, P_CG // 256)),
                  pl.BlockSpec((4, nq + 1, QB, QB), lambda b, i: (0, 0, 0, 0))],
        out_specs=pl.BlockSpec((None, KV_CHUNK, BRANCH_WIDTH), lambda b, i: (b, i, 0)),
        out_shape=jax.ShapeDtypeStruct((bsz, seq, BRANCH_WIDTH), BF16),
        scratch_shapes=[pltpu.VMEM((seq, QB), F32), pltpu.VMEM((seq, QB), F32),
                        pltpu.VMEM((BRANCH_WIDTH, seq), BF16)],
        compiler_params=_params(("arbitrary", "arbitrary")),
        name="sparse_attn",
    )(p32, p16, p32, p16, p16, p32, p32, bias)


def _dilated_kernel(x_ref, g_ref, bias_ref, o_ref, m_ref, l_ref, acc_ref):
    first = lax.broadcasted_iota(jnp.int32, (QB, LANES), 1) < HEAD_DIM
    hmask = [_lane_mask(HEAD_DIM, hh, F32) for hh in range(2)]

    def run(p, tiles, sink):
        work = []
        for cur, prev, variant in tiles:
            for g in range(2):
                q = x_ref[g, cur, :]
                kcat = x_ref[2 + g, cur, :]
                vcat = x_ref[4 + g, cur, :]
                if prev is not None:
                    kcat = jnp.concatenate([x_ref[2 + g, prev, :], kcat], axis=0)
                    vcat = jnp.concatenate([x_ref[4 + g, prev, :], vcat], axis=0)
                kcat = kcat.astype(BF16)
                scores = []
                for hh in range(2):
                    if prev is not None:
                        bias = bias_ref[2 * g + hh, 2 * p + variant]
                    else:
                        bias = bias_ref[2 * g + hh, 2 * p, :, QB:]
                    scores.append(_dot_nt((q * hmask[hh]).astype(BF16), kcat) + bias)
                work.append((vcat.astype(BF16), scores))
        for n, (vcat, scores) in enumerate(work):
            ms, ls, accs = [], [], []
            for s in scores:
                m = jnp.max(s, axis=-1, keepdims=True)
                e = jnp.exp2(s - m)
                ms.append(m)
                ls.append(jnp.sum(e, axis=-1, keepdims=True))
                accs.append(_dot(e.astype(BF16), vcat))
            sink(n // 2, n % 2, jnp.where(first, ms[0], ms[1]), jnp.where(first, ls[0], ls[1]),
                 jnp.where(first, accs[0], accs[1]))

    def store_stats(slot, d, starts):
        def sink(t, g, m, l, a):
            rows = pl.ds(starts[t], QB, stride=d)
            m_ref[slot, g, rows, :] = m
            l_ref[slot, g, rows, :] = l
            acc_ref[slot, g, rows, :] = a
        return sink

    def step16(n0, carry):
        starts = [n0 * DIL_UNROLL + u for u in range(DIL_UNROLL)]
        run(2, [(pl.ds(r, QB, stride=16), None, 0) for r in starts], store_stats(1, 16, starts))
        return carry

    lax.fori_loop(0, 16 // DIL_UNROLL, step16, 0)

    def step4(j, carry):
        starts = [r + 4 * QB * j for r in range(4)]
        tiles = [(pl.ds(s, QB, stride=4), pl.ds(jnp.maximum(s - 4 * QB, r), QB, stride=4), jnp.minimum(j, 1))
                 for r, s in enumerate(starts)]
        run(1, tiles, store_stats(0, 4, starts))
        return carry

    lax.fori_loop(0, 4, step4, 0)

    def step1(n0, carry):
        blocks = [n0 * DIL_UNROLL + u for u in range(DIL_UNROLL)]

        def sink(t, g, m, l, a):
            rows = pl.ds(pl.multiple_of(blocks[t] * QB, QB), QB)
            lanes = slice(g * LANES, (g + 1) * LANES)
            m4, m16 = m_ref[0, g, rows, :], m_ref[1, g, rows, :]
            m_tot = jnp.maximum(m, jnp.maximum(m4, m16))
            w1, w4, w16 = jnp.exp2(m - m_tot), jnp.exp2(m4 - m_tot), jnp.exp2(m16 - m_tot)
            num = w1 * a + w4 * acc_ref[0, g, rows, :] + w16 * acc_ref[1, g, rows, :]
            den = w1 * l + w4 * l_ref[0, g, rows, :] + w16 * l_ref[1, g, rows, :]
            o_ref[rows, lanes] = (num / den * _silu(g_ref[rows, lanes])).astype(BF16)

        tiles = []
        for j in blocks:
            cur = pl.ds(pl.multiple_of(j * QB, QB), QB)
            prev = pl.ds(pl.multiple_of(jnp.maximum(j - 1, 0) * QB, QB), QB)
            tiles.append((cur, prev, jnp.minimum(j, 1)))
        run(0, tiles, sink)
        return carry

    lax.fori_loop(0, g_ref.shape[0] // QB // DIL_UNROLL, step1, 0)


def _dilated_attn(pb, p32, bias):
    bsz, _, seq, _ = pb.shape
    stats = pltpu.VMEM((2, 2, seq, LANES), F32)
    return pl.pallas_call(
        _dilated_kernel,
        grid=(bsz,),
        in_specs=[pl.BlockSpec((None, B_SLABS, seq, LANES), lambda b: (b, 0, 0, 0)),
                  pl.BlockSpec((None, seq, 256), lambda b: (b, 0, P_BG // 256)),
                  pl.BlockSpec((4, 2 * len(DILATED_PATTERNS), QB, 2 * QB), lambda b: (0, 0, 0, 0))],
        out_specs=pl.BlockSpec((None, seq, BRANCH_WIDTH), lambda b: (b, 0, 0)),
        out_shape=jax.ShapeDtypeStruct((bsz, seq, BRANCH_WIDTH), BF16),
        scratch_shapes=[stats, stats, stats],
        compiler_params=_params(("arbitrary",)),
        name="dilated_attn",
    )(pb, p32, bias)


def _out_proj_kernel(ya_ref, yb_ref, yc_ref, yd_ref, w_ref, h_ref, g_ref, o_ref):
    y = None
    for n, ref in enumerate((ya_ref, yb_ref, yc_ref, yd_ref)):
        t = _dot(ref[...], w_ref[n * BRANCH_WIDTH:(n + 1) * BRANCH_WIDTH, :])
        y = t if y is None else y + t
    ms = jnp.mean(y * y, axis=-1, keepdims=True)
    o_ref[...] = h_ref[...] + y * lax.rsqrt(ms + EPS) * g_ref[...]


def _out_proj(ya, yb, yc, yd, w, layer, h, g, tm=512):
    bsz, seq, _ = h.shape
    yspec = pl.BlockSpec((None, tm, BRANCH_WIDTH), lambda b, i: (b, i, 0))
    return pl.pallas_call(
        _out_proj_kernel,
        grid=(bsz, seq // tm),
        in_specs=[yspec, yspec, yspec, yspec,
                  pl.BlockSpec((None, 4 * BRANCH_WIDTH, D_MODEL), lambda b, i: (layer, 0, 0)),
                  pl.BlockSpec((None, tm, D_MODEL), lambda b, i: (b, i, 0)),
                  pl.BlockSpec((1, D_MODEL), lambda b, i: (0, 0))],
        out_specs=pl.BlockSpec((None, tm, D_MODEL), lambda b, i: (b, i, 0)),
        out_shape=jax.ShapeDtypeStruct(h.shape, F32),
        compiler_params=_params(("arbitrary", "arbitrary")),
        name="out_proj",
    )(ya, yb, yc, yd, w, h, g)


def kernel(x, w_in, w_out, norm_pre, norm_post, mla_q_norm, mla_kv_norm, mla_w_uq, mla_w_ukv,
           diff_lambda, diff_subln, rel_bias):
    bsz, seq, _ = x.shape
    depth = w_in.shape[0]
    nq = seq // QB
    k_top = min(IDX_TOPK_MAX, seq // 4)
    cos_t, sin_t = _rope_tables(seq)
    bias_b = _bias_expand(rel_bias, jnp.asarray(_bucket_tiles_dilated()), 0, 4, LOG2E)
    causal_buckets_t = jnp.asarray(np.swapaxes(_bucket_tiles_causal(nq), 1, 2))
    bias_c = _bias_expand(rel_bias, causal_buckets_t, 4, 4, LOG2E)
    bias_d = _bias_expand(rel_bias, causal_buckets_t, 8, 4, LOG2E)
    mask_tiles = jnp.asarray(_mask_tiles())
    w_arr_all = _arrange_w_in(w_in)
    w_out16 = w_out.astype(BF16)
    h = x
    for layer in range(depth):
        wq, wqrot, wk, wv = _arrange_mla(mla_w_uq[layer], mla_w_ukv[layer])
        gq = jnp.concatenate([mla_q_norm[layer], jnp.ones((256 - Q_LORA,), F32)])[None, :]
        gkv = mla_kv_norm[layer][None, :]
        p32, p16, pb, qa, ka, va = _in_proj(h, norm_pre[layer][None, :], w_arr_all, layer,
                                            cos_t, sin_t, gq, gkv, wq, wqrot, wk, wv)
        y_a = _mla_attn(qa, ka, va, p32, mask_tiles)
        y_b = _dilated_attn(pb, p32, bias_b)
        y_c = _sparse_attn(p32, p16, bias_c, k_top)
        lambda_init = 0.8 - 0.6 * math.exp(-0.3 * layer)
        subln = jnp.tile(diff_subln[layer], BRANCH_WIDTH // HEAD_DIM)[None, :]
        y_d = _diff_attn(p32, p16, bias_d, diff_lambda[layer], subln, lambda_init)
        h = _out_proj(y_a, y_b, y_c, y_d, w_out16, layer, h, norm_post[layer][None, :])
    return h
```

```python
import functools
import math

import jax
import jax.numpy as jnp
import numpy as np
from jax import lax
from jax.experimental import pallas as pl
from jax.experimental.pallas import tpu as pltpu

F32 = jnp.float32
BF16 = jnp.bfloat16

D_MODEL = 1024
A_HEADS, A_NOPE, A_ROPE, A_V = 4, 64, 32, 64
Q_LORA, KV_LORA = 192, 128
ROPE_THETA = 10000.0
HEAD_DIM = 64
DILATED_PATTERNS = ((128, 1), (512, 4), (2048, 16))
IDX_HEADS, IDX_DIM, IDX_TOPK_MAX = 8, 32, 256
D_QK = 32
BRANCH_WIDTH = 256
NUM_BUCKETS, MAX_DISTANCE = 32, 2048
NEG_INF = -1e30
EPS = 1e-6
LOG2E = math.log2(math.e)
KV_CHUNK = 256
COL_ACC_ROWS = 64
DIL_UNROLL = 4
LANES = 128
QB = 128
VMEM_LIMIT = 56 * 1024 * 1024

_SPLIT = (Q_LORA, KV_LORA, A_ROPE, 256, 256, 256, 256, 256, 256, 256, 256, IDX_HEADS * IDX_DIM, IDX_DIM,
          IDX_HEADS, 256, 256, 256, 256, 256)
_OFF = np.concatenate([[0], np.cumsum(_SPLIT)]).tolist()
(_A_CQ, _A_CKV, _A_KR, _A_G, _B_Q, _B_K, _B_V, _B_G, _C_Q, _C_K, _C_V, _C_QI, _C_KI, _C_WI, _C_G,
 _D_Q, _D_K, _D_V, _D_G) = range(19)

L_CQ, L_CKV, L_KR, L_KRROT = 0, 256, 384, 512
P_START = 512
P_CWI, P_AG, P_BG, P_CQ2, P_CV, P_CG, P_DQ, P_DV, P_DG = 128, 256, 512, 768, 1024, 1280, 1536, 1792, 2048
NCOL32 = 2304
Q_START = P_START + NCOL32
Q_CK, Q_CQI, Q_DK, Q_CKI = 0, 256, 512, 768
NCOL16 = 896
B_START = Q_START + NCOL16
B_SLABS = 6
NCOL = B_START + B_SLABS * LANES


def _dot(a, b):
    return jnp.dot(a, b, preferred_element_type=F32)


def _dot_nt(a, b):
    return lax.dot_general(a, b, (((1,), (1,)), ((), ())), preferred_element_type=F32)


def _params(sem):
    return pltpu.CompilerParams(dimension_semantics=sem, vmem_limit_bytes=VMEM_LIMIT)


def _rot_cols(w):
    half = w.shape[-1] // 2
    return jnp.concatenate([-w[..., half:], w[..., :half]], axis=-1)


def _arrange_w_in(w):
    depth, rows, in_cols = w.shape
    tm = 256
    return pl.pallas_call(
        _arrange_w_in_kernel,
        grid=(depth, rows // tm),
        in_specs=[pl.BlockSpec((None, tm, in_cols), lambda l, i: (l, i, 0))],
        out_specs=pl.BlockSpec((None, tm, NCOL), lambda l, i: (l, i, 0)),
        out_shape=jax.ShapeDtypeStruct((depth, rows, NCOL), BF16),
        compiler_params=_params(("arbitrary", "arbitrary")),
        name="arrange_w_in",
    )(w)


def _w_in_pieces():
    pieces = []

    def put(dst, seg, lo=0, hi=None, scale=1.0):
        hi = _SPLIT[seg] if hi is None else hi
        pieces.append((dst, _OFF[seg] + lo, hi - lo, scale))
        return dst + hi - lo

    put(L_CQ, _A_CQ)
    put(L_CKV, _A_CKV)
    put(L_KR + A_NOPE, _A_KR)
    half = A_ROPE // 2
    put(L_KRROT + A_NOPE, _A_KR, half, A_ROPE, scale=-1.0)
    put(L_KRROT + A_NOPE + half, _A_KR, 0, half)
    for dst, seg in ((P_CWI, _C_WI), (P_AG, _A_G), (P_BG, _B_G), (P_CQ2, _C_Q), (P_CV, _C_V), (P_CG, _C_G),
                     (P_DQ, _D_Q), (P_DV, _D_V), (P_DG, _D_G)):
        put(P_START + dst, seg)
    for dst, seg in ((Q_CK, _C_K), (Q_CQI, _C_QI), (Q_DK, _D_K)):
        put(Q_START + dst, seg)
    for copy in range(LANES // IDX_DIM):
        put(Q_START + Q_CKI + copy * IDX_DIM, _C_KI)
    base = put(B_START, _B_Q, scale=HEAD_DIM ** -0.5 * LOG2E)
    base = put(base, _B_K)
    base = put(base, _B_V)
    assert base == NCOL
    return pieces


def _arrange_w_in_kernel(w_ref, o_ref):
    o_ref[...] = jnp.zeros(o_ref.shape, o_ref.dtype)
    for dst, src, n, scale in _w_in_pieces():
        lo = src // LANES * LANES
        hi = min(-(-(src + n) // LANES) * LANES, w_ref.shape[1])
        v = w_ref[:, lo:hi][:, src - lo:src - lo + n]
        o_ref[:, dst:dst + n] = (v * scale if scale != 1.0 else v).astype(o_ref.dtype)


def _arrange_mla(w_uq, w_ukv):
    wq = w_uq.reshape(Q_LORA, A_HEADS, A_NOPE + A_ROPE)
    nope, rope = wq[..., :A_NOPE], wq[..., A_NOPE:]
    zq = jnp.zeros((Q_LORA, A_HEADS, LANES - A_NOPE - A_ROPE), w_uq.dtype)
    wq_main = jnp.concatenate([nope, rope, zq], axis=-1).reshape(Q_LORA, A_HEADS * LANES)
    wq_rot = jnp.concatenate([jnp.zeros_like(nope), _rot_cols(rope), zq], axis=-1).reshape(Q_LORA, A_HEADS * LANES)
    pad = jnp.zeros((256 - Q_LORA, A_HEADS * LANES), w_uq.dtype)
    wq_main = jnp.concatenate([wq_main, pad], axis=0)
    wq_rot = jnp.concatenate([wq_rot, pad], axis=0)
    wkv = w_ukv.reshape(KV_LORA, A_HEADS, A_NOPE + A_V)
    knope, v = wkv[..., :A_NOPE], wkv[..., A_NOPE:]
    wk = jnp.concatenate([knope, jnp.zeros_like(knope)], axis=-1).reshape(KV_LORA, A_HEADS * LANES)
    wv_t = v.reshape(KV_LORA, A_HEADS * A_V).T
    return wq_main.astype(BF16), wq_rot.astype(BF16), wk.astype(BF16), wv_t.astype(BF16)


def _rope_tables(seq):
    inv = ROPE_THETA ** (-jnp.arange(0, A_ROPE, 2, dtype=F32) / A_ROPE)
    ang = jnp.arange(seq, dtype=F32)[:, None] * inv[None, :]
    cos, sin = jnp.cos(ang), jnp.sin(ang)
    one = jnp.ones((seq, A_NOPE), F32)
    zero = jnp.zeros((seq, LANES - A_NOPE - A_ROPE), F32)
    cos_t = jnp.concatenate([one, cos, cos, zero], axis=1)
    sin_t = jnp.concatenate([jnp.zeros_like(one), sin, sin, zero], axis=1)
    return cos_t, sin_t


def _t5_bucket_np(rel):
    n = np.maximum(rel, 0)
    max_exact = NUM_BUCKETS // 2
    nf = np.maximum(n, max_exact).astype(np.float64)
    large = max_exact + (np.log(nf / max_exact) / math.log(MAX_DISTANCE / max_exact)
                         * (NUM_BUCKETS - max_exact)).astype(np.int32)
    large = np.minimum(large, NUM_BUCKETS - 1)
    return np.where(n < max_exact, n, large).astype(np.int32)


MASKED_BUCKET = NUM_BUCKETS


def _bucket_tiles_causal(nq):
    q = np.arange(QB)[:, None]
    k = np.arange(QB)[None, :]
    tiles = [np.full((QB, QB), MASKED_BUCKET, np.int32)]
    for d in range(nq):
        rel = QB * d + q - k
        tiles.append(np.where(rel >= 0, _t5_bucket_np(rel), MASKED_BUCKET).astype(np.int32))
    return np.stack(tiles)


def _mask_tiles():
    k = np.arange(QB)[:, None]
    q = np.arange(QB)[None, :]
    diag = np.where(k <= q, 0.0, NEG_INF)
    return np.stack([np.full((QB, QB), NEG_INF), diag, np.zeros((QB, QB))]).astype(np.float32)


def _bucket_tiles_dilated():
    q = np.arange(QB)[:, None]
    k = np.arange(2 * QB)[None, :]
    rel = q + QB - k
    tiles = []
    for (window, d) in DILATED_PATTERNS:
        in_band = (rel >= 0) & (rel <= window // d)
        for has_prev in (False, True):
            ok = in_band & (has_prev | (k >= QB))
            tiles.append(np.where(ok, _t5_bucket_np(rel * d), MASKED_BUCKET).astype(np.int32))
    return np.stack(tiles)


def _bias_expand_kernel(table_ref, bucket_ref, out_ref, *, head0, scale, present):
    h = pl.program_id(0) + head0
    for n, buckets_in_tile in enumerate(present):
        bk = bucket_ref[n]
        acc = jnp.where(bk == MASKED_BUCKET, NEG_INF, 0.0)
        for b in buckets_in_tile:
            acc = jnp.where(bk == b, table_ref[b, h] * scale, acc)
        out_ref[n] = acc


def _bias_expand(table, buckets, head0, nheads, scale=1.0):
    n, r, c = buckets.shape
    present = tuple(tuple(int(b) for b in np.unique(tile) if b != MASKED_BUCKET) for tile in buckets)
    buckets = jnp.asarray(buckets)
    return pl.pallas_call(
        functools.partial(_bias_expand_kernel, head0=head0, scale=scale, present=present),
        grid=(nheads,),
        in_specs=[pl.BlockSpec(memory_space=pltpu.SMEM),
                  pl.BlockSpec((n, r, c), lambda h: (0, 0, 0))],
        out_specs=pl.BlockSpec((None, n, r, c), lambda h: (h, 0, 0, 0)),
        out_shape=jax.ShapeDtypeStruct((nheads, n, r, c), F32),
        compiler_params=_params(("arbitrary",)),
        name="bias_expand",
    )(table, buckets)


def _in_proj_kernel(x_ref, g_ref, w_ref, cos_ref, sin_ref, gq_ref, gkv_ref, wq_ref, wqrot_ref, wk_ref, wv_ref,
                    o32_ref, o16_ref, ob_ref, q_ref, k_ref, v_ref):
    x = x_ref[...]
    ms = jnp.mean(x * x, axis=-1, keepdims=True)
    xn = x * lax.rsqrt(ms + EPS) * g_ref[...]
    p = _dot(xn.astype(BF16), w_ref[...])
    o32_ref[...] = p[:, P_START:Q_START]
    o16_ref[...] = p[:, Q_START:B_START].astype(BF16)
    for s in range(B_SLABS):
        ob_ref[s] = p[:, B_START + s * LANES:B_START + (s + 1) * LANES]
    cos = cos_ref[...]
    sin = sin_ref[...]
    cos4 = jnp.concatenate([cos] * A_HEADS, axis=1)
    sin4 = jnp.concatenate([sin] * A_HEADS, axis=1)
    cq = p[:, L_CQ:L_CQ + 256]
    ms = jnp.sum(cq * cq, axis=-1, keepdims=True) * (1.0 / Q_LORA)
    nq = (cq * lax.rsqrt(ms + EPS) * gq_ref[...]).astype(BF16)
    q = _dot(nq, wq_ref[...]) * cos4 + _dot(nq, wqrot_ref[...]) * sin4
    q_ref[...] = (q * ((A_NOPE + A_ROPE) ** -0.5 * LOG2E)).astype(BF16)
    ckv = p[:, L_CKV:L_CKV + KV_LORA]
    ms = jnp.mean(ckv * ckv, axis=-1, keepdims=True)
    nkv = (ckv * lax.rsqrt(ms + EPS) * gkv_ref[...]).astype(BF16)
    kr = p[:, L_KR:L_KR + LANES] * cos + p[:, L_KRROT:L_KRROT + LANES] * sin
    k = _dot(nkv, wk_ref[...]) + jnp.concatenate([kr] * A_HEADS, axis=1)
    k_ref[...] = k.astype(BF16)
    v_ref[...] = _dot_nt(wv_ref[...], nkv).astype(BF16)


def _in_proj(h, g, w, layer, cos_t, sin_t, gq, gkv, wq, wqrot, wk, wv_t, tm=256):
    bsz, seq, _ = h.shape
    w4 = A_HEADS * LANES
    row = lambda b, i: (b, i, 0)
    const = lambda b, i: (0, 0)
    return pl.pallas_call(
        _in_proj_kernel,
        grid=(bsz, seq // tm),
        in_specs=[pl.BlockSpec((None, tm, D_MODEL), row),
                  pl.BlockSpec((1, D_MODEL), const),
                  pl.BlockSpec((None, D_MODEL, NCOL), lambda b, i: (layer, 0, 0)),
                  pl.BlockSpec((tm, LANES), lambda b, i: (i, 0)),
                  pl.BlockSpec((tm, LANES), lambda b, i: (i, 0)),
                  pl.BlockSpec((1, 256), const),
                  pl.BlockSpec((1, KV_LORA), const),
                  pl.BlockSpec((256, w4), const),
                  pl.BlockSpec((256, w4), const),
                  pl.BlockSpec((KV_LORA, w4), const),
                  pl.BlockSpec((A_HEADS * A_V, KV_LORA), const)],
        out_specs=[pl.BlockSpec((None, tm, NCOL32), row),
                   pl.BlockSpec((None, tm, NCOL16), row),
                   pl.BlockSpec((None, B_SLABS, tm, LANES), lambda b, i: (b, 0, i, 0)),
                   pl.BlockSpec((None, tm, w4), row), pl.BlockSpec((None, tm, w4), row),
                   pl.BlockSpec((None, A_HEADS * A_V, tm), lambda b, i: (b, 0, i))],
        out_shape=[jax.ShapeDtypeStruct((bsz, seq, NCOL32), F32),
                   jax.ShapeDtypeStruct((bsz, seq, NCOL16), BF16),
                   jax.ShapeDtypeStruct((bsz, B_SLABS, seq, LANES), F32),
                   jax.ShapeDtypeStruct((bsz, seq, w4), BF16), jax.ShapeDtypeStruct((bsz, seq, w4), BF16),
                   jax.ShapeDtypeStruct((bsz, A_HEADS * A_V, seq), BF16)],
        compiler_params=_params(("arbitrary", "arbitrary")),
        name="in_proj",
    )(h, g, w, cos_t, sin_t, gq, gkv, wq, wqrot, wk, wv_t)


def _silu(g):
    return g * (1.0 / (1.0 + jnp.exp(-g)))


def _for_step_blocks(i, seq, block_body, unroll=False):
    per = KV_CHUNK // QB
    for wb in range(seq // KV_CHUNK):
        @pl.when(i == wb)
        def _(width=(wb + 1) * KV_CHUNK):
            if unroll:
                for u in range(per):
                    block_body(width, per * i + u, slice(u * QB, (u + 1) * QB))
                return

            def one(u, carry):
                block_body(width, per * i + u, pl.ds(pl.multiple_of(u * QB, QB), QB))
                return carry

            lax.fori_loop(0, per, one, 0)


def _mask_tail_t(mask_ref, i, width):
    first = (width - KV_CHUNK) // QB
    return jnp.concatenate([mask_ref[jnp.clip(i - j, -1, 1) + 1] for j in range(first, width // QB)], axis=0)


def _col_reduce(x, op):
    rows, lanes = x.shape
    part = op(x.reshape(rows // COL_ACC_ROWS, COL_ACC_ROWS, lanes), axis=0)
    return op(part, axis=0, keepdims=True)


def _bias_col(bias_ref, h, i, width):
    return jnp.concatenate([bias_ref[h, jnp.maximum(i - j, -1) + 1] for j in range(width // QB)], axis=0)


def _lane_mask(width, seg, dtype):
    lane = lax.broadcasted_iota(jnp.int32, (1, LANES), 1)
    return jnp.where((lane >= seg * width) & (lane < (seg + 1) * width), 1.0, 0.0).astype(dtype)


def _mla_attn_kernel(q_ref, k_ref, vt_ref, g_ref, mask_ref, o_ref):
    i = pl.program_id(1)
    seq = k_ref.shape[0]

    def body(width, ib, rows):
        head_w = width - KV_CHUNK
        zero = jnp.zeros((QB, LANES), BF16)
        scores = []
        for g in range(A_HEADS // 2):
            q0 = q_ref[rows, 2 * g * LANES:(2 * g + 1) * LANES]
            q1 = q_ref[rows, (2 * g + 1) * LANES:(2 * g + 2) * LANES]
            qbd = jnp.concatenate([jnp.concatenate([q0, zero], axis=1),
                                   jnp.concatenate([zero, q1], axis=1)], axis=0)
            scores.append(_dot_nt(k_ref[:width, 2 * g * LANES:(2 * g + 2) * LANES], qbd))
        tail = _mask_tail_t(mask_ref, ib, width)
        outs = []
        for g in range(A_HEADS // 2):
            es, ls = [], []
            for hh in range(2):
                s = scores[g][:, hh * QB:(hh + 1) * QB]
                s = jnp.concatenate([s[:head_w], s[head_w:] + tail], axis=0) if head_w else s + tail
                e = jnp.exp2(s - _col_reduce(s, jnp.max))
                ls.append(_col_reduce(e, jnp.sum))
                es.append(e.astype(BF16))
            ot = _dot(vt_ref[g * LANES:(g + 1) * LANES, :width], jnp.concatenate(es, axis=1))
            outs.append(ot[:A_V, :QB] / ls[0])
            outs.append(ot[A_V:, QB:] / ls[1])
        y_t = jnp.concatenate(outs, axis=0)
        y = jnp.concatenate([y_t[:LANES].T, y_t[LANES:].T], axis=1)
        o_ref[rows, :] = (y * _silu(g_ref[rows, :])).astype(BF16)

    _for_step_blocks(i, seq, body, unroll=True)


def _mla_attn(q, k, v_t, p32, mask_tiles):
    bsz, seq, w4 = q.shape
    return pl.pallas_call(
        _mla_attn_kernel,
        grid=(bsz, seq // KV_CHUNK),
        in_specs=[pl.BlockSpec((None, KV_CHUNK, w4), lambda b, i: (b, i, 0)),
                  pl.BlockSpec((None, seq, w4), lambda b, i: (b, 0, 0)),
                  pl.BlockSpec((None, A_HEADS * A_V, seq), lambda b, i: (b, 0, 0)),
                  pl.BlockSpec((None, KV_CHUNK, 256), lambda b, i: (b, i, P_AG // 256)),
                  pl.BlockSpec((3, QB, QB), lambda b, i: (0, 0, 0))],
        out_specs=pl.BlockSpec((None, KV_CHUNK, BRANCH_WIDTH), lambda b, i: (b, i, 0)),
        out_shape=jax.ShapeDtypeStruct((bsz, seq, BRANCH_WIDTH), BF16),
        compiler_params=_params(("arbitrary", "arbitrary")),
        name="mla_attn",
    )(q, k, v_t, p32, mask_tiles)


def _transpose_values(v_ref, vt_ref):
    for j in range(v_ref.shape[0] // QB):
        for g in range(v_ref.shape[1] // LANES):
            tile = v_ref[j * QB:(j + 1) * QB, g * LANES:(g + 1) * LANES]
            vt_ref[g * LANES:(g + 1) * LANES, j * QB:(j + 1) * QB] = tile.T.astype(BF16)


def _diff_attn_kernel(q_ref, k_ref, v_ref, g_ref, bias_ref, lam_ref, subln_ref, o_ref, vt_ref, *, lambda_init):
    i = pl.program_id(1)
    seq = k_ref.shape[0]

    @pl.when(i == 0)
    def _():
        _transpose_values(v_ref, vt_ref)

    def body(width, ib, rows):
        lp = lam_ref[...]
        lam = (jnp.exp(jnp.sum(lp[0:1] * lp[1:2], axis=-1, keepdims=True))
               - jnp.exp(jnp.sum(lp[2:3] * lp[3:4], axis=-1, keepdims=True)) + lambda_init)
        scores = []
        for h in range(4):
            sl = slice((h // 2) * LANES, (h // 2 + 1) * LANES)
            qg = q_ref[rows, sl] * (D_QK ** -0.5 * LOG2E)
            qcat = jnp.concatenate([(qg * _lane_mask(D_QK, 2 * (h % 2) + mm, F32)).astype(BF16)
                                    for mm in range(2)], axis=0)
            scores.append(_dot_nt(k_ref[:width, sl], qcat))
        outs = []
        for h in range(4):
            bias = _bias_col(bias_ref, h, ib, width)
            es, ls = [], []
            for mm in range(2):
                s = scores[h][:, mm * QB:(mm + 1) * QB] + bias
                e = jnp.exp2(s - _col_reduce(s, jnp.max))
                ls.append(_col_reduce(e, jnp.sum))
                es.append(e.astype(BF16))
            ot = _dot(vt_ref[h * HEAD_DIM:(h + 1) * HEAD_DIM, :width], jnp.concatenate(es, axis=1))
            a = ot[:, :QB] / ls[0] - lam * (ot[:, QB:] / ls[1])
            ms = jnp.sum(a * a, axis=0, keepdims=True) * (1.0 / HEAD_DIM)
            outs.append(a * lax.rsqrt(ms + EPS))
        y_t = jnp.concatenate(outs, axis=0)
        y = jnp.concatenate([y_t[:LANES].T, y_t[LANES:].T], axis=1)
        o_ref[rows, :] = (y * (subln_ref[...] * (1.0 - lambda_init)) * _silu(g_ref[rows, :])).astype(BF16)

    _for_step_blocks(i, seq, body, unroll=True)


def _diff_attn(p32, p16, bias, lam_params, subln, lambda_init):
    bsz, seq, _ = p32.shape
    nq = seq // QB
    return pl.pallas_call(
        functools.partial(_diff_attn_kernel, lambda_init=lambda_init),
        grid=(bsz, seq // KV_CHUNK),
        in_specs=[pl.BlockSpec((None, KV_CHUNK, 256), lambda b, i: (b, i, P_DQ // 256)),
                  pl.BlockSpec((None, seq, 256), lambda b, i: (b, 0, Q_DK // 256)),
                  pl.BlockSpec((None, seq, 256), lambda b, i: (b, 0, P_DV // 256)),
                  pl.BlockSpec((None, KV_CHUNK, 256), lambda b, i: (b, i, P_DG // 256)),
                  pl.BlockSpec((4, nq + 1, QB, QB), lambda b, i: (0, 0, 0, 0)),
                  pl.BlockSpec((4, D_QK), lambda b, i: (0, 0)),
                  pl.BlockSpec((1, BRANCH_WIDTH), lambda b, i: (0, 0))],
        out_specs=pl.BlockSpec((None, KV_CHUNK, BRANCH_WIDTH), lambda b, i: (b, i, 0)),
        out_shape=jax.ShapeDtypeStruct((bsz, seq, BRANCH_WIDTH), BF16),
        scratch_shapes=[pltpu.VMEM((BRANCH_WIDTH, seq), BF16)],
        compiler_params=_params(("arbitrary", "arbitrary")),
        name="diff_attn",
    )(p32, p16, p32, p32, bias, lam_params, subln)


def _sortable_to_float(key):
    return pltpu.bitcast(jnp.where(key < 0, key ^ jnp.int32(0x7FFFFFFF), key), F32)


def _kth_largest(score_ref, width, k_top):
    def count_ge(key):
        thr = _sortable_to_float(key)
        return _col_reduce(jnp.where(score_ref[:width, :] >= thr, 1.0, 0.0), jnp.sum)

    int_min = jnp.full((1, QB), -2 ** 31, jnp.int32)
    zero = jnp.zeros((1, QB), jnp.int32)
    t = jnp.where(count_ge(zero) >= k_top, zero, int_min)

    def step(it, t):
        cand = t + (jnp.int32(1) << (30 - it))
        return jnp.where(count_ge(cand) >= k_top, cand, t)

    return _sortable_to_float(lax.fori_loop(0, 31, step, t))


def _first_ties(eq, need, row):
    eqf = jnp.where(eq, 1.0, 0.0)
    nbits = int(eq.shape[0]).bit_length()

    def body(it, j):
        cand = j + (jnp.int32(1) << (nbits - 1 - it))
        cnt = _col_reduce(jnp.where(row < cand, eqf, 0.0), jnp.sum)
        return jnp.where(cnt <= need, cand, j)

    j = lax.fori_loop(0, nbits, body, jnp.zeros((1, eq.shape[1]), jnp.int32))
    return eq & (row < j)


def _sparse_attn_kernel(q_ref, k_ref, v_ref, qi_ref, ki_ref, wi_ref, g_ref, bias_ref, o_ref,
                        score_ref, neg_ref, vt_ref, *, k_top):
    i = pl.program_id(1)
    seq = k_ref.shape[0]

    @pl.when(i == 0)
    def _():
        _transpose_values(v_ref, vt_ref)

    def body(width, ib, rows):
        head_w = width - KV_CHUNK
        ki = ki_ref[:width, :]
        w_t = wi_ref[rows, :].T * (IDX_DIM ** -0.5 * IDX_HEADS ** -0.5)
        score = None
        for h in range(0, IDX_HEADS, 2):
            qg = qi_ref[rows, (h // 4) * LANES:(h // 4 + 1) * LANES]
            qcat = jnp.concatenate([qg * _lane_mask(IDX_DIM, h % 4, BF16),
                                    qg * _lane_mask(IDX_DIM, h % 4 + 1, BF16)], axis=0)
            logit = _dot_nt(ki, qcat)
            term = (jnp.maximum(logit[:, :QB], 0.0) * w_t[h:h + 1]
                    + jnp.maximum(logit[:, QB:], 0.0) * w_t[h + 1:h + 2])
            score = term if score is None else score + term
        s_idx = head_w + lax.broadcasted_iota(jnp.int32, (KV_CHUNK, QB), 0)
        tail_ok = s_idx <= ib * QB + lax.broadcasted_iota(jnp.int32, (KV_CHUNK, QB), 1)
        if head_w:
            score_ref[:head_w, :] = score[:head_w]
        score_ref[head_w:width, :] = jnp.where(tail_ok, score[head_w:], NEG_INF)
        s2s = []
        for g in range(2):
            sl = slice(g * LANES, (g + 1) * LANES)
            qg = q_ref[rows, sl] * (HEAD_DIM ** -0.5 * LOG2E)
            qcat = jnp.concatenate([(qg * _lane_mask(HEAD_DIM, hh, F32)).astype(BF16) for hh in range(2)], axis=0)
            s2 = _dot_nt(k_ref[:width, sl], qcat)
            s2s.append([s2[:, hh * QB:(hh + 1) * QB] + _bias_col(bias_ref, 2 * g + hh, ib, width) for hh in range(2)])
        thr = _kth_largest(score_ref, width, k_top)
        keep_tail = (score_ref[head_w:width, :] >= thr) & tail_ok
        cnt = _col_reduce(jnp.where(keep_tail, 1.0, 0.0), jnp.sum)
        neg_ref[head_w:width, :] = jnp.where(keep_tail, 0.0, NEG_INF)
        if head_w:
            keep_head = score_ref[:head_w, :] >= thr
            cnt = cnt + _col_reduce(jnp.where(keep_head, 1.0, 0.0), jnp.sum)
            neg_ref[:head_w, :] = jnp.where(keep_head, 0.0, NEG_INF)

        @pl.when(jnp.max(cnt) > k_top)
        def _():
            x = score_ref[:width, :]
            gt = x > thr
            need = k_top - _col_reduce(jnp.where(gt, 1.0, 0.0), jnp.sum)
            row = lax.broadcasted_iota(jnp.int32, (width, QB), 0)
            neg_ref[:width, :] = jnp.where(gt | _first_ties(x == thr, need, row), 0.0, NEG_INF)

        outs = []
        for g in range(2):
            sl = slice(g * LANES, (g + 1) * LANES)
            es, ls = [], []
            for hh in range(2):
                s = s2s[g][hh] + neg_ref[:width, :]
                e = jnp.exp2(s - _col_reduce(s, jnp.max))
                ls.append(_col_reduce(e, jnp.sum))
                es.append(e.astype(BF16))
            ot = _dot(vt_ref[sl, :width], jnp.concatenate(es, axis=1))
            outs.append(ot[:HEAD_DIM, :QB] / ls[0])
            outs.append(ot[HEAD_DIM:, QB:] / ls[1])
        y_t = jnp.concatenate(outs, axis=0)
        y = jnp.concatenate([y_t[:LANES].T, y_t[LANES:].T], axis=1)
        o_ref[rows, :] = (y * _silu(g_ref[rows, :])).astype(BF16)

    _for_step_blocks(i, seq, body)


def _sparse_attn(p32, p16, bias, k_top):
    bsz, seq, _ = p32.shape
    nq = seq // QB
    return pl.pallas_call(
        functools.partial(_sparse_attn_kernel, k_top=k_top),
        grid=(bsz, seq // KV_CHUNK),
        in_specs=[pl.BlockSpec((None, KV_CHUNK, 256), lambda b, i: (b, i, P_CQ2 // 256)),
                  pl.BlockSpec((None, seq, 256), lambda b, i: (b, 0, Q_CK // 256)),
                  pl.BlockSpec((None, seq, 256), lambda b, i: (b, 0, P_CV // 256)),
                  pl.BlockSpec((None, KV_CHUNK, 256), lambda b, i: (b, i, Q_CQI // 256)),
                  pl.BlockSpec((None, seq, LANES), lambda b, i: (b, 0, Q_CKI // LANES)),
                  pl.BlockSpec((None, KV_CHUNK, LANES), lambda b, i: (b, i, P_CWI // LANES)),
                  pl.BlockSpec((None, KV_CHUNK, 256), lambda b, i: (b, i, P_CG // 256)),
                  pl.BlockSpec((4, nq + 1, QB, QB), lambda b, i: (0, 0, 0, 0))],
        out_specs=pl.BlockSpec((None, KV_CHUNK, BRANCH_WIDTH), lambda b, i: (b, i, 0)),
        out_shape=jax.ShapeDtypeStruct((bsz, seq, BRANCH_WIDTH), BF16),
        scratch_shapes=[pltpu.VMEM((seq, QB), F32), pltpu.VMEM((seq, QB), F32),
                        pltpu.VMEM((BRANCH_WIDTH, seq), BF16)],
        compiler_params=_params(("arbitrary", "arbitrary")),
        name="sparse_attn",
    )(p32, p16, p32, p16, p16, p32, p32, bias)


def _dilated_kernel(x_ref, g_ref, bias_ref, o_ref, m_ref, l_ref, acc_ref):
    first = lax.broadcasted_iota(jnp.int32, (QB, LANES), 1) < HEAD_DIM
    hmask = [_lane_mask(HEAD_DIM, hh, F32) for hh in range(2)]

    def run(p, tiles, sink):
        work = []
        for cur, prev, variant in tiles:
            for g in range(2):
                q = x_ref[g, cur, :]
                kcat = x_ref[2 + g, cur, :]
                vcat = x_ref[4 + g, cur, :]
                if prev is not None:
                    kcat = jnp.concatenate([x_ref[2 + g, prev, :], kcat], axis=0)
                    vcat = jnp.concatenate([x_ref[4 + g, prev, :], vcat], axis=0)
                kcat = kcat.astype(BF16)
                scores = []
                for hh in range(2):
                    if prev is not None:
                        bias = bias_ref[2 * g + hh, 2 * p + variant]
                    else:
                        bias = bias_ref[2 * g + hh, 2 * p, :, QB:]
                    scores.append(_dot_nt((q * hmask[hh]).astype(BF16), kcat) + bias)
                work.append((vcat.astype(BF16), scores))
        for n, (vcat, scores) in enumerate(work):
            ms, ls, accs = [], [], []
            for s in scores:
                m = jnp.max(s, axis=-1, keepdims=True)
                e = jnp.exp2(s - m)
                ms.append(m)
                ls.append(jnp.sum(e, axis=-1, keepdims=True))
                accs.append(_dot(e.astype(BF16), vcat))
            sink(n // 2, n % 2, jnp.where(first, ms[0], ms[1]), jnp.where(first, ls[0], ls[1]),
                 jnp.where(first, accs[0], accs[1]))

    def store_stats(slot, d, starts):
        def sink(t, g, m, l, a):
            rows = pl.ds(starts[t], QB, stride=d)
            m_ref[slot, g, rows, :] = m
            l_ref[slot, g, rows, :] = l
            acc_ref[slot, g, rows, :] = a
        return sink

    def step16(n0, carry):
        starts = [n0 * DIL_UNROLL + u for u in range(DIL_UNROLL)]
        run(2, [(pl.ds(r, QB, stride=16), None, 0) for r in starts], store_stats(1, 16, starts))
        return carry

    lax.fori_loop(0, 16 // DIL_UNROLL, step16, 0)

    def step4(j, carry):
        starts = [r + 4 * QB * j for r in range(4)]
        tiles = [(pl.ds(s, QB, stride=4), pl.ds(jnp.maximum(s - 4 * QB, r), QB, stride=4), jnp.minimum(j, 1))
                 for r, s in enumerate(starts)]
        run(1, tiles, store_stats(0, 4, starts))
        return carry

    lax.fori_loop(0, 4, step4, 0)

    def step1(n0, carry):
        blocks = [n0 * DIL_UNROLL + u for u in range(DIL_UNROLL)]

        def sink(t, g, m, l, a):
            rows = pl.ds(pl.multiple_of(blocks[t] * QB, QB), QB)
            lanes = slice(g * LANES, (g + 1) * LANES)
            m4, m16 = m_ref[0, g, rows, :], m_ref[1, g, rows, :]
            m_tot = jnp.maximum(m, jnp.maximum(m4, m16))
            w1, w4, w16 = jnp.exp2(m - m_tot), jnp.exp2(m4 - m_tot), jnp.exp2(m16 - m_tot)
            num = w1 * a + w4 * acc_ref[0, g, rows, :] + w16 * acc_ref[1, g, rows, :]
            den = w1 * l + w4 * l_ref[0, g, rows, :] + w16 * l_ref[1, g, rows, :]
            o_ref[rows, lanes] = (num / den * _silu(g_ref[rows, lanes])).astype(BF16)

        tiles = []
        for j in blocks:
            cur = pl.ds(pl.multiple_of(j * QB, QB), QB)
            prev = pl.ds(pl.multiple_of(jnp.maximum(j - 1, 0) * QB, QB), QB)
            tiles.append((cur, prev, jnp.minimum(j, 1)))
        run(0, tiles, sink)
        return carry

    lax.fori_loop(0, g_ref.shape[0] // QB // DIL_UNROLL, step1, 0)


def _dilated_attn(pb, p32, bias):
    bsz, _, seq, _ = pb.shape
    stats = pltpu.VMEM((2, 2, seq, LANES), F32)
    return pl.pallas_call(
        _dilated_kernel,
        grid=(bsz,),
        in_specs=[pl.BlockSpec((None, B_SLABS, seq, LANES), lambda b: (b, 0, 0, 0)),
                  pl.BlockSpec((None, seq, 256), lambda b: (b, 0, P_BG // 256)),
                  pl.BlockSpec((4, 2 * len(DILATED_PATTERNS), QB, 2 * QB), lambda b: (0, 0, 0, 0))],
        out_specs=pl.BlockSpec((None, seq, BRANCH_WIDTH), lambda b: (b, 0, 0)),
        out_shape=jax.ShapeDtypeStruct((bsz, seq, BRANCH_WIDTH), BF16),
        scratch_shapes=[stats, stats, stats],
        compiler_params=_params(("arbitrary",)),
        name="dilated_attn",
    )(pb, p32, bias)


def _out_proj_kernel(ya_ref, yb_ref, yc_ref, yd_ref, w_ref, h_ref, g_ref, o_ref):
    y = _dot(jnp.concatenate([ya_ref[...], yb_ref[...], yc_ref[...], yd_ref[...]], axis=1), w_ref[...])
    ms = jnp.mean(y * y, axis=-1, keepdims=True)
    o_ref[...] = h_ref[...] + y * lax.rsqrt(ms + EPS) * g_ref[...]


def _out_proj(ya, yb, yc, yd, w, layer, h, g, tm=512):
    bsz, seq, _ = h.shape
    yspec = pl.BlockSpec((None, tm, BRANCH_WIDTH), lambda b, i: (b, i, 0))
    return pl.pallas_call(
        _out_proj_kernel,
        grid=(bsz, seq // tm),
        in_specs=[yspec, yspec, yspec, yspec,
                  pl.BlockSpec((None, 4 * BRANCH_WIDTH, D_MODEL), lambda b, i: (layer, 0, 0)),
                  pl.BlockSpec((None, tm, D_MODEL), lambda b, i: (b, i, 0)),
                  pl.BlockSpec((1, D_MODEL), lambda b, i: (0, 0))],
        out_specs=pl.BlockSpec((None, tm, D_MODEL), lambda b, i: (b, i, 0)),
        out_shape=jax.ShapeDtypeStruct(h.shape, F32),
        compiler_params=_params(("arbitrary", "arbitrary")),
        name="out_proj",
    )(ya, yb, yc, yd, w, h, g)


def kernel(x, w_in, w_out, norm_pre, norm_post, mla_q_norm, mla_kv_norm, mla_w_uq, mla_w_ukv,
           diff_lambda, diff_subln, rel_bias):
    bsz, seq, _ = x.shape
    depth = w_in.shape[0]
    nq = seq // QB
    k_top = min(IDX_TOPK_MAX, seq // 4)
    cos_t, sin_t = _rope_tables(seq)
    bias_b = _bias_expand(rel_bias, _bucket_tiles_dilated(), 0, 4, LOG2E)
    causal_buckets_t = np.ascontiguousarray(np.swapaxes(_bucket_tiles_causal(nq), 1, 2))
    bias_c = _bias_expand(rel_bias, causal_buckets_t, 4, 4, LOG2E)
    bias_d = _bias_expand(rel_bias, causal_buckets_t, 8, 4, LOG2E)
    mask_tiles = jnp.asarray(_mask_tiles())
    w_arr_all = _arrange_w_in(w_in)
    w_out16 = w_out.astype(BF16)
    h = x
    for layer in range(depth):
        wq, wqrot, wk, wv = _arrange_mla(mla_w_uq[layer], mla_w_ukv[layer])
        gq = jnp.concatenate([mla_q_norm[layer], jnp.ones((256 - Q_LORA,), F32)])[None, :]
        gkv = mla_kv_norm[layer][None, :]
        p32, p16, pb, qa, ka, va = _in_proj(h, norm_pre[layer][None, :], w_arr_all, layer,
                                            cos_t, sin_t, gq, gkv, wq, wqrot, wk, wv)
        y_a = _mla_attn(qa, ka, va, p32, mask_tiles)
        y_b = _dilated_attn(pb, p32, bias_b)
        y_c = _sparse_attn(p32, p16, bias_c, k_top)
        lambda_init = 0.8 - 0.6 * math.exp(-0.3 * layer)
        subln = jnp.tile(diff_subln[layer], BRANCH_WIDTH // HEAD_DIM)[None, :]
        y_d = _diff_attn(p32, p16, bias_d, diff_lambda[layer], subln, lambda_init)
        h = _out_proj(y_a, y_b, y_c, y_d, w_out16, layer, h, norm_post[layer][None, :])
    return h
```

```python
import functools
import math

import jax
import jax.numpy as jnp
import numpy as np
from jax import lax
from jax.experimental import pallas as pl
from jax.experimental.pallas import tpu as pltpu

F32 = jnp.float32
BF16 = jnp.bfloat16

D_MODEL = 1024
A_HEADS, A_NOPE, A_ROPE, A_V = 4, 64, 32, 64
Q_LORA, KV_LORA = 192, 128
ROPE_THETA = 10000.0
HEAD_DIM = 64
DILATED_PATTERNS = ((128, 1), (512, 4), (2048, 16))
IDX_HEADS, IDX_DIM, IDX_TOPK_MAX = 8, 32, 256
D_QK = 32
BRANCH_WIDTH = 256
NUM_BUCKETS, MAX_DISTANCE = 32, 2048
NEG_INF = -1e30
EPS = 1e-6
LOG2E = math.log2(math.e)
KV_CHUNK = 256
COL_ACC_ROWS = 64
DIL_UNROLL = 4
LANES = 128
QB = 128
VMEM_LIMIT = 56 * 1024 * 1024

_SPLIT = (Q_LORA, KV_LORA, A_ROPE, 256, 256, 256, 256, 256, 256, 256, 256, IDX_HEADS * IDX_DIM, IDX_DIM,
          IDX_HEADS, 256, 256, 256, 256, 256)
_OFF = np.concatenate([[0], np.cumsum(_SPLIT)]).tolist()
(_A_CQ, _A_CKV, _A_KR, _A_G, _B_Q, _B_K, _B_V, _B_G, _C_Q, _C_K, _C_V, _C_QI, _C_KI, _C_WI, _C_G,
 _D_Q, _D_K, _D_V, _D_G) = range(19)

L_CQ, L_CKV, L_KR, L_KRROT = 0, 256, 384, 512
P_START = 512
P_CWI, P_AG, P_BG, P_CQ2, P_CV, P_CG, P_DQ, P_DV, P_DG = 128, 256, 512, 768, 1024, 1280, 1536, 1792, 2048
NCOL32 = 2304
Q_START = P_START + NCOL32
Q_CK, Q_CQI, Q_DK, Q_CKI = 0, 256, 512, 768
NCOL16 = 896
B_START = Q_START + NCOL16
B_SLABS = 6
NCOL = B_START + B_SLABS * LANES


def _dot(a, b):
    return jnp.dot(a, b, preferred_element_type=F32)


def _dot_nt(a, b):
    return lax.dot_general(a, b, (((1,), (1,)), ((), ())), preferred_element_type=F32)


def _params(sem):
    return pltpu.CompilerParams(dimension_semantics=sem, vmem_limit_bytes=VMEM_LIMIT)


def _rot_cols(w):
    half = w.shape[-1] // 2
    return jnp.concatenate([-w[..., half:], w[..., :half]], axis=-1)


def _arrange_w_in(w):
    depth, rows, in_cols = w.shape
    tm = 256
    return pl.pallas_call(
        _arrange_w_in_kernel,
        grid=(depth, rows // tm),
        in_specs=[pl.BlockSpec((None, tm, in_cols), lambda l, i: (l, i, 0))],
        out_specs=pl.BlockSpec((None, tm, NCOL), lambda l, i: (l, i, 0)),
        out_shape=jax.ShapeDtypeStruct((depth, rows, NCOL), BF16),
        compiler_params=_params(("arbitrary", "arbitrary")),
        name="arrange_w_in",
    )(w)


def _w_in_pieces():
    pieces = []

    def put(dst, seg, lo=0, hi=None, scale=1.0):
        hi = _SPLIT[seg] if hi is None else hi
        pieces.append((dst, _OFF[seg] + lo, hi - lo, scale))
        return dst + hi - lo

    put(L_CQ, _A_CQ)
    put(L_CKV, _A_CKV)
    put(L_KR + A_NOPE, _A_KR)
    half = A_ROPE // 2
    put(L_KRROT + A_NOPE, _A_KR, half, A_ROPE, scale=-1.0)
    put(L_KRROT + A_NOPE + half, _A_KR, 0, half)
    for dst, seg in ((P_CWI, _C_WI), (P_AG, _A_G), (P_BG, _B_G), (P_CQ2, _C_Q), (P_CV, _C_V), (P_CG, _C_G),
                     (P_DQ, _D_Q), (P_DV, _D_V), (P_DG, _D_G)):
        put(P_START + dst, seg)
    for dst, seg in ((Q_CK, _C_K), (Q_CQI, _C_QI), (Q_DK, _D_K)):
        put(Q_START + dst, seg)
    for copy in range(LANES // IDX_DIM):
        put(Q_START + Q_CKI + copy * IDX_DIM, _C_KI)
    base = put(B_START, _B_Q, scale=HEAD_DIM ** -0.5 * LOG2E)
    base = put(base, _B_K)
    base = put(base, _B_V)
    assert base == NCOL
    return pieces


def _arrange_w_in_kernel(w_ref, o_ref):
    o_ref[...] = jnp.zeros(o_ref.shape, o_ref.dtype)
    for dst, src, n, scale in _w_in_pieces():
        lo = src // LANES * LANES
        hi = min(-(-(src + n) // LANES) * LANES, w_ref.shape[1])
        v = w_ref[:, lo:hi][:, src - lo:src - lo + n]
        o_ref[:, dst:dst + n] = (v * scale if scale != 1.0 else v).astype(o_ref.dtype)


def _arrange_mla(w_uq, w_ukv):
    wq = w_uq.reshape(Q_LORA, A_HEADS, A_NOPE + A_ROPE)
    nope, rope = wq[..., :A_NOPE], wq[..., A_NOPE:]
    zq = jnp.zeros((Q_LORA, A_HEADS, LANES - A_NOPE - A_ROPE), w_uq.dtype)
    wq_main = jnp.concatenate([nope, rope, zq], axis=-1).reshape(Q_LORA, A_HEADS * LANES)
    wq_rot = jnp.concatenate([jnp.zeros_like(nope), _rot_cols(rope), zq], axis=-1).reshape(Q_LORA, A_HEADS * LANES)
    pad = jnp.zeros((256 - Q_LORA, A_HEADS * LANES), w_uq.dtype)
    wq_main = jnp.concatenate([wq_main, pad], axis=0)
    wq_rot = jnp.concatenate([wq_rot, pad], axis=0)
    wkv = w_ukv.reshape(KV_LORA, A_HEADS, A_NOPE + A_V)
    knope, v = wkv[..., :A_NOPE], wkv[..., A_NOPE:]
    wk = jnp.concatenate([knope, jnp.zeros_like(knope)], axis=-1).reshape(KV_LORA, A_HEADS * LANES)
    wv_t = v.reshape(KV_LORA, A_HEADS * A_V).T
    return wq_main.astype(BF16), wq_rot.astype(BF16), wk.astype(BF16), wv_t.astype(BF16)


def _rope_tables(seq):
    inv = ROPE_THETA ** (-jnp.arange(0, A_ROPE, 2, dtype=F32) / A_ROPE)
    ang = jnp.arange(seq, dtype=F32)[:, None] * inv[None, :]
    cos, sin = jnp.cos(ang), jnp.sin(ang)
    one = jnp.ones((seq, A_NOPE), F32)
    zero = jnp.zeros((seq, LANES - A_NOPE - A_ROPE), F32)
    cos_t = jnp.concatenate([one, cos, cos, zero], axis=1)
    sin_t = jnp.concatenate([jnp.zeros_like(one), sin, sin, zero], axis=1)
    return cos_t, sin_t


def _t5_bucket_np(rel):
    n = np.maximum(rel, 0)
    max_exact = NUM_BUCKETS // 2
    nf = np.maximum(n, max_exact).astype(np.float64)
    large = max_exact + (np.log(nf / max_exact) / math.log(MAX_DISTANCE / max_exact)
                         * (NUM_BUCKETS - max_exact)).astype(np.int32)
    large = np.minimum(large, NUM_BUCKETS - 1)
    return np.where(n < max_exact, n, large).astype(np.int32)


MASKED_BUCKET = NUM_BUCKETS


def _bucket_tiles_causal(nq):
    q = np.arange(QB)[:, None]
    k = np.arange(QB)[None, :]
    tiles = [np.full((QB, QB), MASKED_BUCKET, np.int32)]
    for d in range(nq):
        rel = QB * d + q - k
        tiles.append(np.where(rel >= 0, _t5_bucket_np(rel), MASKED_BUCKET).astype(np.int32))
    return np.stack(tiles)


def _mask_tiles():
    k = np.arange(QB)[:, None]
    q = np.arange(QB)[None, :]
    diag = np.where(k <= q, 0.0, NEG_INF)
    return np.stack([np.full((QB, QB), NEG_INF), diag, np.zeros((QB, QB))]).astype(np.float32)


def _bucket_tiles_dilated():
    q = np.arange(QB)[:, None]
    k = np.arange(2 * QB)[None, :]
    rel = q + QB - k
    tiles = []
    for (window, d) in DILATED_PATTERNS:
        in_band = (rel >= 0) & (rel <= window // d)
        for has_prev in (False, True):
            ok = in_band & (has_prev | (k >= QB))
            tiles.append(np.where(ok, _t5_bucket_np(rel * d), MASKED_BUCKET).astype(np.int32))
    return np.stack(tiles)


def _bias_expand_kernel(table_ref, bucket_ref, out_ref, *, head0, scale, present):
    h = pl.program_id(0) + head0
    for n, buckets_in_tile in enumerate(present):
        bk = bucket_ref[n]
        acc = jnp.where(bk == MASKED_BUCKET, NEG_INF, 0.0)
        for b in buckets_in_tile:
            acc = jnp.where(bk == b, table_ref[b, h] * scale, acc)
        out_ref[n] = acc


def _bias_expand(table, buckets, head0, nheads, scale=1.0):
    n, r, c = buckets.shape
    present = tuple(tuple(int(b) for b in np.unique(tile) if b != MASKED_BUCKET) for tile in buckets)
    buckets = jnp.asarray(buckets)
    return pl.pallas_call(
        functools.partial(_bias_expand_kernel, head0=head0, scale=scale, present=present),
        grid=(nheads,),
        in_specs=[pl.BlockSpec(memory_space=pltpu.SMEM),
                  pl.BlockSpec((n, r, c), lambda h: (0, 0, 0))],
        out_specs=pl.BlockSpec((None, n, r, c), lambda h: (h, 0, 0, 0)),
        out_shape=jax.ShapeDtypeStruct((nheads, n, r, c), F32),
        compiler_params=_params(("arbitrary",)),
        name="bias_expand",
    )(table, buckets)


def _in_proj_kernel(x_ref, g_ref, w_ref, cos_ref, sin_ref, gq_ref, gkv_ref, wq_ref, wqrot_ref, wk_ref, wv_ref,
                    o32_ref, o16_ref, ob_ref, q_ref, k_ref, v_ref):
    x = x_ref[...]
    ms = jnp.mean(x * x, axis=-1, keepdims=True)
    xn = x * lax.rsqrt(ms + EPS) * g_ref[...]
    p = _dot(xn.astype(BF16), w_ref[...])
    o32_ref[...] = p[:, P_START:Q_START]
    o16_ref[...] = p[:, Q_START:B_START].astype(BF16)
    for s in range(B_SLABS):
        ob_ref[s] = p[:, B_START + s * LANES:B_START + (s + 1) * LANES]
    cos = cos_ref[...]
    sin = sin_ref[...]
    cos4 = jnp.concatenate([cos] * A_HEADS, axis=1)
    sin4 = jnp.concatenate([sin] * A_HEADS, axis=1)
    cq = p[:, L_CQ:L_CQ + 256]
    ms = jnp.sum(cq * cq, axis=-1, keepdims=True) * (1.0 / Q_LORA)
    nq = (cq * lax.rsqrt(ms + EPS) * gq_ref[...]).astype(BF16)
    q = _dot(nq, wq_ref[...]) * cos4 + _dot(nq, wqrot_ref[...]) * sin4
    q_ref[...] = (q * ((A_NOPE + A_ROPE) ** -0.5 * LOG2E)).astype(BF16)
    ckv = p[:, L_CKV:L_CKV + KV_LORA]
    ms = jnp.mean(ckv * ckv, axis=-1, keepdims=True)
    nkv = (ckv * lax.rsqrt(ms + EPS) * gkv_ref[...]).astype(BF16)
    kr = p[:, L_KR:L_KR + LANES] * cos + p[:, L_KRROT:L_KRROT + LANES] * sin
    k = _dot(nkv, wk_ref[...]) + jnp.concatenate([kr] * A_HEADS, axis=1)
    k_ref[...] = k.astype(BF16)
    v_ref[...] = _dot_nt(wv_ref[...], nkv).astype(BF16)


def _in_proj(h, g, w, layer, cos_t, sin_t, gq, gkv, wq, wqrot, wk, wv_t, tm=256):
    bsz, seq, _ = h.shape
    w4 = A_HEADS * LANES
    row = lambda b, i: (b, i, 0)
    const = lambda b, i: (0, 0)
    return pl.pallas_call(
        _in_proj_kernel,
        grid=(bsz, seq // tm),
        in_specs=[pl.BlockSpec((None, tm, D_MODEL), row),
                  pl.BlockSpec((1, D_MODEL), const),
                  pl.BlockSpec((None, D_MODEL, NCOL), lambda b, i: (layer, 0, 0)),
                  pl.BlockSpec((tm, LANES), lambda b, i: (i, 0)),
                  pl.BlockSpec((tm, LANES), lambda b, i: (i, 0)),
                  pl.BlockSpec((1, 256), const),
                  pl.BlockSpec((1, KV_LORA), const),
                  pl.BlockSpec((256, w4), const),
                  pl.BlockSpec((256, w4), const),
                  pl.BlockSpec((KV_LORA, w4), const),
                  pl.BlockSpec((A_HEADS * A_V, KV_LORA), const)],
        out_specs=[pl.BlockSpec((None, tm, NCOL32), row),
                   pl.BlockSpec((None, tm, NCOL16), row),
                   pl.BlockSpec((None, B_SLABS, tm, LANES), lambda b, i: (b, 0, i, 0)),
                   pl.BlockSpec((None, tm, w4), row), pl.BlockSpec((None, tm, w4), row),
                   pl.BlockSpec((None, A_HEADS * A_V, tm), lambda b, i: (b, 0, i))],
        out_shape=[jax.ShapeDtypeStruct((bsz, seq, NCOL32), F32),
                   jax.ShapeDtypeStruct((bsz, seq, NCOL16), BF16),
                   jax.ShapeDtypeStruct((bsz, B_SLABS, seq, LANES), F32),
                   jax.ShapeDtypeStruct((bsz, seq, w4), BF16), jax.ShapeDtypeStruct((bsz, seq, w4), BF16),
                   jax.ShapeDtypeStruct((bsz, A_HEADS * A_V, seq), BF16)],
        compiler_params=_params(("arbitrary", "arbitrary")),
        name="in_proj",
    )(h, g, w, cos_t, sin_t, gq, gkv, wq, wqrot, wk, wv_t)


def _silu(g):
    return g * (1.0 / (1.0 + jnp.exp(-g)))


def _for_step_blocks(i, seq, block_body, unroll=False):
    per = KV_CHUNK // QB
    for wb in range(seq // KV_CHUNK):
        @pl.when(i == wb)
        def _(width=(wb + 1) * KV_CHUNK):
            if unroll:
                for u in range(per):
                    block_body(width, per * i + u, slice(u * QB, (u + 1) * QB))
                return

            def one(u, carry):
                block_body(width, per * i + u, pl.ds(pl.multiple_of(u * QB, QB), QB))
                return carry

            lax.fori_loop(0, per, one, 0)


def _mask_tail_t(mask_ref, i, width):
    first = (width - KV_CHUNK) // QB
    return jnp.concatenate([mask_ref[jnp.clip(i - j, -1, 1) + 1] for j in range(first, width // QB)], axis=0)


def _col_reduce(x, op):
    rows, lanes = x.shape
    part = op(x.reshape(rows // COL_ACC_ROWS, COL_ACC_ROWS, lanes), axis=0)
    return op(part, axis=0, keepdims=True)


def _bias_col(bias_ref, h, i, width):
    return jnp.concatenate([bias_ref[h, jnp.maximum(i - j, -1) + 1] for j in range(width // QB)], axis=0)


def _lane_mask(width, seg, dtype):
    lane = lax.broadcasted_iota(jnp.int32, (1, LANES), 1)
    return jnp.where((lane >= seg * width) & (lane < (seg + 1) * width), 1.0, 0.0).astype(dtype)


def _mla_attn_kernel(q_ref, k_ref, vt_ref, g_ref, mask_ref, o_ref):
    i = pl.program_id(1)
    seq = k_ref.shape[0]

    def body(width, ib, rows):
        head_w = width - KV_CHUNK
        zero = jnp.zeros((QB, LANES), BF16)
        scores = []
        for g in range(A_HEADS // 2):
            q0 = q_ref[rows, 2 * g * LANES:(2 * g + 1) * LANES]
            q1 = q_ref[rows, (2 * g + 1) * LANES:(2 * g + 2) * LANES]
            qbd = jnp.concatenate([jnp.concatenate([q0, zero], axis=1),
                                   jnp.concatenate([zero, q1], axis=1)], axis=0)
            scores.append(_dot_nt(k_ref[:width, 2 * g * LANES:(2 * g + 2) * LANES], qbd))
        tail = _mask_tail_t(mask_ref, ib, width)
        outs = []
        for g in range(A_HEADS // 2):
            es, ls = [], []
            for hh in range(2):
                s = scores[g][:, hh * QB:(hh + 1) * QB]
                s = jnp.concatenate([s[:head_w], s[head_w:] + tail], axis=0) if head_w else s + tail
                e = jnp.exp2(s - _col_reduce(s, jnp.max))
                ls.append(_col_reduce(e, jnp.sum))
                es.append(e.astype(BF16))
            ot = _dot(vt_ref[g * LANES:(g + 1) * LANES, :width], jnp.concatenate(es, axis=1))
            outs.append(ot[:A_V, :QB] / ls[0])
            outs.append(ot[A_V:, QB:] / ls[1])
        y_t = jnp.concatenate(outs, axis=0)
        y = jnp.concatenate([y_t[:LANES].T, y_t[LANES:].T], axis=1)
        o_ref[rows, :] = (y * _silu(g_ref[rows, :])).astype(BF16)

    _for_step_blocks(i, seq, body, unroll=True)


def _mla_attn(q, k, v_t, p32, mask_tiles):
    bsz, seq, w4 = q.shape
    return pl.pallas_call(
        _mla_attn_kernel,
        grid=(bsz, seq // KV_CHUNK),
        in_specs=[pl.BlockSpec((None, KV_CHUNK, w4), lambda b, i: (b, i, 0)),
                  pl.BlockSpec((None, seq, w4), lambda b, i: (b, 0, 0)),
                  pl.BlockSpec((None, A_HEADS * A_V, seq), lambda b, i: (b, 0, 0)),
                  pl.BlockSpec((None, KV_CHUNK, 256), lambda b, i: (b, i, P_AG // 256)),
                  pl.BlockSpec((3, QB, QB), lambda b, i: (0, 0, 0))],
        out_specs=pl.BlockSpec((None, KV_CHUNK, BRANCH_WIDTH), lambda b, i: (b, i, 0)),
        out_shape=jax.ShapeDtypeStruct((bsz, seq, BRANCH_WIDTH), BF16),
        compiler_params=_params(("arbitrary", "arbitrary")),
        name="mla_attn",
    )(q, k, v_t, p32, mask_tiles)


def _transpose_values(v_ref, vt_ref):
    for j in range(v_ref.shape[0] // QB):
        for g in range(v_ref.shape[1] // LANES):
            tile = v_ref[j * QB:(j + 1) * QB, g * LANES:(g + 1) * LANES]
            vt_ref[g * LANES:(g + 1) * LANES, j * QB:(j + 1) * QB] = tile.T.astype(BF16)


def _diff_attn_kernel(q_ref, k_ref, v_ref, g_ref, bias_ref, lam_ref, subln_ref, o_ref, vt_ref, *, lambda_init):
    i = pl.program_id(1)
    seq = k_ref.shape[0]

    @pl.when(i == 0)
    def _():
        _transpose_values(v_ref, vt_ref)

    def body(width, ib, rows):
        lp = lam_ref[...]
        lam = (jnp.exp(jnp.sum(lp[0:1] * lp[1:2], axis=-1, keepdims=True))
               - jnp.exp(jnp.sum(lp[2:3] * lp[3:4], axis=-1, keepdims=True)) + lambda_init)
        scores = []
        for h in range(4):
            sl = slice((h // 2) * LANES, (h // 2 + 1) * LANES)
            qg = q_ref[rows, sl] * (D_QK ** -0.5 * LOG2E)
            qcat = jnp.concatenate([(qg * _lane_mask(D_QK, 2 * (h % 2) + mm, F32)).astype(BF16)
                                    for mm in range(2)], axis=0)
            scores.append(_dot_nt(k_ref[:width, sl], qcat))
        outs = []
        for h in range(4):
            bias = _bias_col(bias_ref, h, ib, width)
            es, ls = [], []
            for mm in range(2):
                s = scores[h][:, mm * QB:(mm + 1) * QB] + bias
                e = jnp.exp2(s - _col_reduce(s, jnp.max))
                ls.append(_col_reduce(e, jnp.sum))
                es.append(e.astype(BF16))
            ot = _dot(vt_ref[h * HEAD_DIM:(h + 1) * HEAD_DIM, :width], jnp.concatenate(es, axis=1))
            a = ot[:, :QB] / ls[0] - lam * (ot[:, QB:] / ls[1])
            ms = jnp.sum(a * a, axis=0, keepdims=True) * (1.0 / HEAD_DIM)
            outs.append(a * lax.rsqrt(ms + EPS))
        y_t = jnp.concatenate(outs, axis=0)
        y = jnp.concatenate([y_t[:LANES].T, y_t[LANES:].T], axis=1)
        o_ref[rows, :] = (y * (subln_ref[...] * (1.0 - lambda_init)) * _silu(g_ref[rows, :])).astype(BF16)

    _for_step_blocks(i, seq, body)


def _diff_attn(p32, p16, bias, lam_params, subln, lambda_init):
    bsz, seq, _ = p32.shape
    nq = seq // QB
    return pl.pallas_call(
        functools.partial(_diff_attn_kernel, lambda_init=lambda_init),
        grid=(bsz, seq // KV_CHUNK),
        in_specs=[pl.BlockSpec((None, KV_CHUNK, 256), lambda b, i: (b, i, P_DQ // 256)),
                  pl.BlockSpec((None, seq, 256), lambda b, i: (b, 0, Q_DK // 256)),
                  pl.BlockSpec((None, seq, 256), lambda b, i: (b, 0, P_DV // 256)),
                  pl.BlockSpec((None, KV_CHUNK, 256), lambda b, i: (b, i, P_DG // 256)),
                  pl.BlockSpec((4, nq + 1, QB, QB), lambda b, i: (0, 0, 0, 0)),
                  pl.BlockSpec((4, D_QK), lambda b, i: (0, 0)),
                  pl.BlockSpec((1, BRANCH_WIDTH), lambda b, i: (0, 0))],
        out_specs=pl.BlockSpec((None, KV_CHUNK, BRANCH_WIDTH), lambda b, i: (b, i, 0)),
        out_shape=jax.ShapeDtypeStruct((bsz, seq, BRANCH_WIDTH), BF16),
        scratch_shapes=[pltpu.VMEM((BRANCH_WIDTH, seq), BF16)],
        compiler_params=_params(("arbitrary", "arbitrary")),
        name="diff_attn",
    )(p32, p16, p32, p32, bias, lam_params, subln)


def _sortable_to_float(key):
    return pltpu.bitcast(jnp.where(key < 0, key ^ jnp.int32(0x7FFFFFFF), key), F32)


def _kth_largest(score_ref, width, k_top):
    def count_ge(key):
        thr = _sortable_to_float(key)
        return _col_reduce(jnp.where(score_ref[:width, :] >= thr, 1.0, 0.0), jnp.sum)

    int_min = jnp.full((1, QB), -2 ** 31, jnp.int32)
    zero = jnp.zeros((1, QB), jnp.int32)
    t = jnp.where(count_ge(zero) >= k_top, zero, int_min)

    def step(it, t):
        cand = t + (jnp.int32(1) << (30 - it))
        return jnp.where(count_ge(cand) >= k_top, cand, t)

    return _sortable_to_float(lax.fori_loop(0, 31, step, t))


def _first_ties(eq, need, row):
    eqf = jnp.where(eq, 1.0, 0.0)
    nbits = int(eq.shape[0]).bit_length()

    def body(it, j):
        cand = j + (jnp.int32(1) << (nbits - 1 - it))
        cnt = _col_reduce(jnp.where(row < cand, eqf, 0.0), jnp.sum)
        return jnp.where(cnt <= need, cand, j)

    j = lax.fori_loop(0, nbits, body, jnp.zeros((1, eq.shape[1]), jnp.int32))
    return eq & (row < j)


def _sparse_attn_kernel(q_ref, k_ref, v_ref, qi_ref, ki_ref, wi_ref, g_ref, bias_ref, o_ref,
                        score_ref, neg_ref, vt_ref, *, k_top):
    i = pl.program_id(1)
    seq = k_ref.shape[0]

    @pl.when(i == 0)
    def _():
        _transpose_values(v_ref, vt_ref)

    def body(width, ib, rows):
        head_w = width - KV_CHUNK
        ki = ki_ref[:width, :]
        w_t = wi_ref[rows, :].T * (IDX_DIM ** -0.5 * IDX_HEADS ** -0.5)
        score = None
        for h in range(0, IDX_HEADS, 2):
            qg = qi_ref[rows, (h // 4) * LANES:(h // 4 + 1) * LANES]
            qcat = jnp.concatenate([qg * _lane_mask(IDX_DIM, h % 4, BF16),
                                    qg * _lane_mask(IDX_DIM, h % 4 + 1, BF16)], axis=0)
            logit = _dot_nt(ki, qcat)
            term = (jnp.maximum(logit[:, :QB], 0.0) * w_t[h:h + 1]
                    + jnp.maximum(logit[:, QB:], 0.0) * w_t[h + 1:h + 2])
            score = term if score is None else score + term
        s_idx = head_w + lax.broadcasted_iota(jnp.int32, (KV_CHUNK, QB), 0)
        tail_ok = s_idx <= ib * QB + lax.broadcasted_iota(jnp.int32, (KV_CHUNK, QB), 1)
        if head_w:
            score_ref[:head_w, :] = score[:head_w]
        score_ref[head_w:width, :] = jnp.where(tail_ok, score[head_w:], NEG_INF)
        s2s = []
        for g in range(2):
            sl = slice(g * LANES, (g + 1) * LANES)
            qg = q_ref[rows, sl] * (HEAD_DIM ** -0.5 * LOG2E)
            qcat = jnp.concatenate([(qg * _lane_mask(HEAD_DIM, hh, F32)).astype(BF16) for hh in range(2)], axis=0)
            s2 = _dot_nt(k_ref[:width, sl], qcat)
            s2s.append([s2[:, hh * QB:(hh + 1) * QB] + _bias_col(bias_ref, 2 * g + hh, ib, width) for hh in range(2)])
        thr = _kth_largest(score_ref, width, k_top)
        keep_tail = (score_ref[head_w:width, :] >= thr) & tail_ok
        cnt = _col_reduce(jnp.where(keep_tail, 1.0, 0.0), jnp.sum)
        neg_ref[head_w:width, :] = jnp.where(keep_tail, 0.0, NEG_INF)
        if head_w:
            keep_head = score_ref[:head_w, :] >= thr
            cnt = cnt + _col_reduce(jnp.where(keep_head, 1.0, 0.0), jnp.sum)
            neg_ref[:head_w, :] = jnp.where(keep_head, 0.0, NEG_INF)

        @pl.when(jnp.max(cnt) > k_top)
        def _():
            x = score_ref[:width, :]
            gt = x > thr
            need = k_top - _col_reduce(jnp.where(gt, 1.0, 0.0), jnp.sum)
            row = lax.broadcasted_iota(jnp.int32, (width, QB), 0)
            neg_ref[:width, :] = jnp.where(gt | _first_ties(x == thr, need, row), 0.0, NEG_INF)

        outs = []
        for g in range(2):
            sl = slice(g * LANES, (g + 1) * LANES)
            es, ls = [], []
            for hh in range(2):
                s = s2s[g][hh] + neg_ref[:width, :]
                e = jnp.exp2(s - _col_reduce(s, jnp.max))
                ls.append(_col_reduce(e, jnp.sum))
                es.append(e.astype(BF16))
            ot = _dot(vt_ref[sl, :width], jnp.concatenate(es, axis=1))
            outs.append(ot[:HEAD_DIM, :QB] / ls[0])
            outs.append(ot[HEAD_DIM:, QB:] / ls[1])
        y_t = jnp.concatenate(outs, axis=0)
        y = jnp.concatenate([y_t[:LANES].T, y_t[LANES:].T], axis=1)
        o_ref[rows, :] = (y * _silu(g_ref[rows, :])).astype(BF16)

    _for_step_blocks(i, seq, body)


def _sparse_attn(p32, p16, bias, k_top):
    bsz, seq, _ = p32.shape
    nq = seq // QB
    return pl.pallas_call(
        functools.partial(_sparse_attn_kernel, k_top=k_top),
        grid=(bsz, seq // KV_CHUNK),
        in_specs=[pl.BlockSpec((None, KV_CHUNK, 256), lambda b, i: (b, i, P_CQ2 // 256)),
                  pl.BlockSpec((None, seq, 256), lambda b, i: (b, 0, Q_CK // 256)),
                  pl.BlockSpec((None, seq, 256), lambda b, i: (b, 0, P_CV // 256)),
                  pl.BlockSpec((None, KV_CHUNK, 256), lambda b, i: (b, i, Q_CQI // 256)),
                  pl.BlockSpec((None, seq, LANES), lambda b, i: (b, 0, Q_CKI // LANES)),
                  pl.BlockSpec((None, KV_CHUNK, LANES), lambda b, i: (b, i, P_CWI // LANES)),
                  pl.BlockSpec((None, KV_CHUNK, 256), lambda b, i: (b, i, P_CG // 256)),
                  pl.BlockSpec((4, nq + 1, QB, QB), lambda b, i: (0, 0, 0, 0))],
        out_specs=pl.BlockSpec((None, KV_CHUNK, BRANCH_WIDTH), lambda b, i: (b, i, 0)),
        out_shape=jax.ShapeDtypeStruct((bsz, seq, BRANCH_WIDTH), BF16),
        scratch_shapes=[pltpu.VMEM((seq, QB), F32), pltpu.VMEM((seq, QB), F32),
                        pltpu.VMEM((BRANCH_WIDTH, seq), BF16)],
        compiler_params=_params(("arbitrary", "arbitrary")),
        name="sparse_attn",
    )(p32, p16, p32, p16, p16, p32, p32, bias)


def _dilated_kernel(x_ref, g_ref, bias_ref, o_ref, m_ref, l_ref, acc_ref):
    first = lax.broadcasted_iota(jnp.int32, (QB, LANES), 1) < HEAD_DIM
    hmask = [_lane_mask(HEAD_DIM, hh, F32) for hh in range(2)]

    def run(p, tiles, sink):
        work = []
        for cur, prev, variant in tiles:
            for g in range(2):
                q = x_ref[g, cur, :]
                kcat = x_ref[2 + g, cur, :]
                vcat = x_ref[4 + g, cur, :]
                if prev is not None:
                    kcat = jnp.concatenate([x_ref[2 + g, prev, :], kcat], axis=0)
                    vcat = jnp.concatenate([x_ref[4 + g, prev, :], vcat], axis=0)
                kcat = kcat.astype(BF16)
                scores = []
                for hh in range(2):
                    if prev is not None:
                        bias = bias_ref[2 * g + hh, 2 * p + variant]
                    else:
                        bias = bias_ref[2 * g + hh, 2 * p, :, QB:]
                    scores.append(_dot_nt((q * hmask[hh]).astype(BF16), kcat) + bias)
                work.append((vcat.astype(BF16), scores))
        for n, (vcat, scores) in enumerate(work):
            ms, ls, accs = [], [], []
            for s in scores:
                m = jnp.max(s, axis=-1, keepdims=True)
                e = jnp.exp2(s - m)
                ms.append(m)
                ls.append(jnp.sum(e, axis=-1, keepdims=True))
                accs.append(_dot(e.astype(BF16), vcat))
            sink(n // 2, n % 2, jnp.where(first, ms[0], ms[1]), jnp.where(first, ls[0], ls[1]),
                 jnp.where(first, accs[0], accs[1]))

    def store_stats(slot, d, starts):
        def sink(t, g, m, l, a):
            rows = pl.ds(starts[t], QB, stride=d)
            m_ref[slot, g, rows, :] = m
            l_ref[slot, g, rows, :] = l
            acc_ref[slot, g, rows, :] = a
        return sink

    def step16(n0, carry):
        starts = [n0 * DIL_UNROLL + u for u in range(DIL_UNROLL)]
        run(2, [(pl.ds(r, QB, stride=16), None, 0) for r in starts], store_stats(1, 16, starts))
        return carry

    lax.fori_loop(0, 16 // DIL_UNROLL, step16, 0)

    def step4(j, carry):
        starts = [r + 4 * QB * j for r in range(4)]
        tiles = [(pl.ds(s, QB, stride=4), pl.ds(jnp.maximum(s - 4 * QB, r), QB, stride=4), jnp.minimum(j, 1))
                 for r, s in enumerate(starts)]
        run(1, tiles, store_stats(0, 4, starts))
        return carry

    lax.fori_loop(0, 4, step4, 0)

    def step1(n0, carry):
        blocks = [n0 * DIL_UNROLL + u for u in range(DIL_UNROLL)]

        def sink(t, g, m, l, a):
            rows = pl.ds(pl.multiple_of(blocks[t] * QB, QB), QB)
            lanes = slice(g * LANES, (g + 1) * LANES)
            m4, m16 = m_ref[0, g, rows, :], m_ref[1, g, rows, :]
            m_tot = jnp.maximum(m, jnp.maximum(m4, m16))
            w1, w4, w16 = jnp.exp2(m - m_tot), jnp.exp2(m4 - m_tot), jnp.exp2(m16 - m_tot)
            num = w1 * a + w4 * acc_ref[0, g, rows, :] + w16 * acc_ref[1, g, rows, :]
            den = w1 * l + w4 * l_ref[0, g, rows, :] + w16 * l_ref[1, g, rows, :]
            o_ref[rows, lanes] = (num / den * _silu(g_ref[rows, lanes])).astype(BF16)

        tiles = []
        for j in blocks:
            cur = pl.ds(pl.multiple_of(j * QB, QB), QB)
            prev = pl.ds(pl.multiple_of(jnp.maximum(j - 1, 0) * QB, QB), QB)
            tiles.append((cur, prev, jnp.minimum(j, 1)))
        run(0, tiles, sink)
        return carry

    lax.fori_loop(0, g_ref.shape[0] // QB // DIL_UNROLL, step1, 0)


def _dilated_attn(pb, p32, bias):
    bsz, _, seq, _ = pb.shape
    stats = pltpu.VMEM((2, 2, seq, LANES), F32)
    return pl.pallas_call(
        _dilated_kernel,
        grid=(bsz,),
        in_specs=[pl.BlockSpec((None, B_SLABS, seq, LANES), lambda b: (b, 0, 0, 0)),
                  pl.BlockSpec((None, seq, 256), lambda b: (b, 0, P_BG // 256)),
                  pl.BlockSpec((4, 2 * len(DILATED_PATTERNS), QB, 2 * QB), lambda b: (0, 0, 0, 0))],
        out_specs=pl.BlockSpec((None, seq, BRANCH_WIDTH), lambda b: (b, 0, 0)),
        out_shape=jax.ShapeDtypeStruct((bsz, seq, BRANCH_WIDTH), BF16),
        scratch_shapes=[stats, stats, stats],
        compiler_params=_params(("arbitrary",)),
        name="dilated_attn",
    )(pb, p32, bias)


def _out_proj_kernel(ya_ref, yb_ref, yc_ref, yd_ref, w_ref, h_ref, g_ref, o_ref):
    y = _dot(jnp.concatenate([ya_ref[...], yb_ref[...], yc_ref[...], yd_ref[...]], axis=1), w_ref[...])
    ms = jnp.mean(y * y, axis=-1, keepdims=True)
    o_ref[...] = h_ref[...] + y * lax.rsqrt(ms + EPS) * g_ref[...]


def _out_proj(ya, yb, yc, yd, w, layer, h, g, tm=1024):
    bsz, seq, _ = h.shape
    yspec = pl.BlockSpec((None, tm, BRANCH_WIDTH), lambda b, i: (b, i, 0))
    return pl.pallas_call(
        _out_proj_kernel,
        grid=(bsz, seq // tm),
        in_specs=[yspec, yspec, yspec, yspec,
                  pl.BlockSpec((None, 4 * BRANCH_WIDTH, D_MODEL), lambda b, i: (layer, 0, 0)),
                  pl.BlockSpec((None, tm, D_MODEL), lambda b, i: (b, i, 0)),
                  pl.BlockSpec((1, D_MODEL), lambda b, i: (0, 0))],
        out_specs=pl.BlockSpec((None, tm, D_MODEL), lambda b, i: (b, i, 0)),
        out_shape=jax.ShapeDtypeStruct(h.shape, F32),
        compiler_params=_params(("arbitrary", "arbitrary")),
        name="out_proj",
    )(ya, yb, yc, yd, w, h, g)


def kernel(x, w_in, w_out, norm_pre, norm_post, mla_q_norm, mla_kv_norm, mla_w_uq, mla_w_ukv,
           diff_lambda, diff_subln, rel_bias):
    bsz, seq, _ = x.shape
    depth = w_in.shape[0]
    nq = seq // QB
    k_top = min(IDX_TOPK_MAX, seq // 4)
    cos_t, sin_t = _rope_tables(seq)
    bias_b = _bias_expand(rel_bias, _bucket_tiles_dilated(), 0, 4, LOG2E)
    causal_buckets_t = np.ascontiguousarray(np.swapaxes(_bucket_tiles_causal(nq), 1, 2))
    bias_c = _bias_expand(rel_bias, causal_buckets_t, 4, 4, LOG2E)
    bias_d = _bias_expand(rel_bias, causal_buckets_t, 8, 4, LOG2E)
    mask_tiles = jnp.asarray(_mask_tiles())
    w_arr_all = _arrange_w_in(w_in)
    w_out16 = w_out.astype(BF16)
    h = x
    for layer in range(depth):
        wq, wqrot, wk, wv = _arrange_mla(mla_w_uq[layer], mla_w_ukv[layer])
        gq = jnp.concatenate([mla_q_norm[layer], jnp.ones((256 - Q_LORA,), F32)])[None, :]
        gkv = mla_kv_norm[layer][None, :]
        p32, p16, pb, qa, ka, va = _in_proj(h, norm_pre[layer][None, :], w_arr_all, layer,
                                            cos_t, sin_t, gq, gkv, wq, wqrot, wk, wv)
        y_a = _mla_attn(qa, ka, va, p32, mask_tiles)
        y_b = _dilated_attn(pb, p32, bias_b)
        y_c = _sparse_attn(p32, p16, bias_c, k_top)
        lambda_init = 0.8 - 0.6 * math.exp(-0.3 * layer)
        subln = jnp.tile(diff_subln[layer], BRANCH_WIDTH // HEAD_DIM)[None, :]
        y_d = _diff_attn(p32, p16, bias_d, diff_lambda[layer], subln, lambda_init)
        h = _out_proj(y_a, y_b, y_c, y_d, w_out16, layer, h, norm_post[layer][None, :])
    return h
```

```python
import functools
import math

import jax
import jax.numpy as jnp
import numpy as np
from jax import lax
from jax.experimental import pallas as pl
from jax.experimental.pallas import tpu as pltpu

F32 = jnp.float32
BF16 = jnp.bfloat16

D_MODEL = 1024
A_HEADS, A_NOPE, A_ROPE, A_V = 4, 64, 32, 64
Q_LORA, KV_LORA = 192, 128
ROPE_THETA = 10000.0
HEAD_DIM = 64
DILATED_PATTERNS = ((128, 1), (512, 4), (2048, 16))
IDX_HEADS, IDX_DIM, IDX_TOPK_MAX = 8, 32, 256
D_QK = 32
BRANCH_WIDTH = 256
NUM_BUCKETS, MAX_DISTANCE = 32, 2048
NEG_INF = -1e30
EPS = 1e-6
LOG2E = math.log2(math.e)
KV_CHUNK = 256
COL_ACC_ROWS = 64
DIL_UNROLL = 4
LANES = 128
QB = 128
VMEM_LIMIT = 56 * 1024 * 1024

_SPLIT = (Q_LORA, KV_LORA, A_ROPE, 256, 256, 256, 256, 256, 256, 256, 256, IDX_HEADS * IDX_DIM, IDX_DIM,
          IDX_HEADS, 256, 256, 256, 256, 256)
_OFF = np.concatenate([[0], np.cumsum(_SPLIT)]).tolist()
(_A_CQ, _A_CKV, _A_KR, _A_G, _B_Q, _B_K, _B_V, _B_G, _C_Q, _C_K, _C_V, _C_QI, _C_KI, _C_WI, _C_G,
 _D_Q, _D_K, _D_V, _D_G) = range(19)

L_CQ, L_CKV, L_KR, L_KRROT = 0, 256, 384, 512
P_START = 512
P_CWI, P_AG, P_BG, P_CQ2, P_CV, P_CG, P_DQ, P_DV, P_DG = 128, 256, 512, 768, 1024, 1280, 1536, 1792, 2048
NCOL32 = 2304
Q_START = P_START + NCOL32
Q_CK, Q_CQI, Q_DK, Q_CKI = 0, 256, 512, 768
NCOL16 = 896
B_START = Q_START + NCOL16
B_SLABS = 6
NCOL = B_START + B_SLABS * LANES


def _dot(a, b):
    return jnp.dot(a, b, preferred_element_type=F32)


def _dot_nt(a, b):
    return lax.dot_general(a, b, (((1,), (1,)), ((), ())), preferred_element_type=F32)


def _params(sem):
    return pltpu.CompilerParams(dimension_semantics=sem, vmem_limit_bytes=VMEM_LIMIT)


def _rot_cols(w):
    half = w.shape[-1] // 2
    return jnp.concatenate([-w[..., half:], w[..., :half]], axis=-1)


def _arrange_w_in(w):
    depth, rows, in_cols = w.shape
    tm = 256
    return pl.pallas_call(
        _arrange_w_in_kernel,
        grid=(depth, rows // tm),
        in_specs=[pl.BlockSpec((None, tm, in_cols), lambda l, i: (l, i, 0))],
        out_specs=pl.BlockSpec((None, tm, NCOL), lambda l, i: (l, i, 0)),
        out_shape=jax.ShapeDtypeStruct((depth, rows, NCOL), BF16),
        compiler_params=_params(("arbitrary", "arbitrary")),
        name="arrange_w_in",
    )(w)


def _w_in_pieces():
    pieces = []

    def put(dst, seg, lo=0, hi=None, scale=1.0):
        hi = _SPLIT[seg] if hi is None else hi
        pieces.append((dst, _OFF[seg] + lo, hi - lo, scale))
        return dst + hi - lo

    put(L_CQ, _A_CQ)
    put(L_CKV, _A_CKV)
    put(L_KR + A_NOPE, _A_KR)
    half = A_ROPE // 2
    put(L_KRROT + A_NOPE, _A_KR, half, A_ROPE, scale=-1.0)
    put(L_KRROT + A_NOPE + half, _A_KR, 0, half)
    for dst, seg in ((P_CWI, _C_WI), (P_AG, _A_G), (P_BG, _B_G), (P_CQ2, _C_Q), (P_CV, _C_V), (P_CG, _C_G),
                     (P_DQ, _D_Q), (P_DV, _D_V), (P_DG, _D_G)):
        put(P_START + dst, seg)
    for dst, seg in ((Q_CK, _C_K), (Q_CQI, _C_QI), (Q_DK, _D_K)):
        put(Q_START + dst, seg)
    for copy in range(LANES // IDX_DIM):
        put(Q_START + Q_CKI + copy * IDX_DIM, _C_KI)
    base = put(B_START, _B_Q, scale=HEAD_DIM ** -0.5 * LOG2E)
    base = put(base, _B_K)
    base = put(base, _B_V)
    assert base == NCOL
    return pieces


def _arrange_w_in_kernel(w_ref, o_ref):
    o_ref[...] = jnp.zeros(o_ref.shape, o_ref.dtype)
    for dst, src, n, scale in _w_in_pieces():
        lo = src // LANES * LANES
        hi = min(-(-(src + n) // LANES) * LANES, w_ref.shape[1])
        v = w_ref[:, lo:hi][:, src - lo:src - lo + n]
        o_ref[:, dst:dst + n] = (v * scale if scale != 1.0 else v).astype(o_ref.dtype)


def _arrange_mla(w_uq, w_ukv):
    wq = w_uq.reshape(Q_LORA, A_HEADS, A_NOPE + A_ROPE)
    nope, rope = wq[..., :A_NOPE], wq[..., A_NOPE:]
    zq = jnp.zeros((Q_LORA, A_HEADS, LANES - A_NOPE - A_ROPE), w_uq.dtype)
    wq_main = jnp.concatenate([nope, rope, zq], axis=-1).reshape(Q_LORA, A_HEADS * LANES)
    wq_rot = jnp.concatenate([jnp.zeros_like(nope), _rot_cols(rope), zq], axis=-1).reshape(Q_LORA, A_HEADS * LANES)
    pad = jnp.zeros((256 - Q_LORA, A_HEADS * LANES), w_uq.dtype)
    wq_main = jnp.concatenate([wq_main, pad], axis=0)
    wq_rot = jnp.concatenate([wq_rot, pad], axis=0)
    wkv = w_ukv.reshape(KV_LORA, A_HEADS, A_NOPE + A_V)
    knope, v = wkv[..., :A_NOPE], wkv[..., A_NOPE:]
    wk = jnp.concatenate([knope, jnp.zeros_like(knope)], axis=-1).reshape(KV_LORA, A_HEADS * LANES)
    wv_t = v.reshape(KV_LORA, A_HEADS * A_V).T
    return wq_main.astype(BF16), wq_rot.astype(BF16), wk.astype(BF16), wv_t.astype(BF16)


def _rope_tables(seq):
    inv = ROPE_THETA ** (-jnp.arange(0, A_ROPE, 2, dtype=F32) / A_ROPE)
    ang = jnp.arange(seq, dtype=F32)[:, None] * inv[None, :]
    cos, sin = jnp.cos(ang), jnp.sin(ang)
    one = jnp.ones((seq, A_NOPE), F32)
    zero = jnp.zeros((seq, LANES - A_NOPE - A_ROPE), F32)
    cos_t = jnp.concatenate([one, cos, cos, zero], axis=1)
    sin_t = jnp.concatenate([jnp.zeros_like(one), sin, sin, zero], axis=1)
    return cos_t, sin_t


def _t5_bucket_np(rel):
    n = np.maximum(rel, 0)
    max_exact = NUM_BUCKETS // 2
    nf = np.maximum(n, max_exact).astype(np.float64)
    large = max_exact + (np.log(nf / max_exact) / math.log(MAX_DISTANCE / max_exact)
                         * (NUM_BUCKETS - max_exact)).astype(np.int32)
    large = np.minimum(large, NUM_BUCKETS - 1)
    return np.where(n < max_exact, n, large).astype(np.int32)


MASKED_BUCKET = NUM_BUCKETS


def _bucket_tiles_causal(nq):
    q = np.arange(QB)[:, None]
    k = np.arange(QB)[None, :]
    tiles = [np.full((QB, QB), MASKED_BUCKET, np.int32)]
    for d in range(nq):
        rel = QB * d + q - k
        tiles.append(np.where(rel >= 0, _t5_bucket_np(rel), MASKED_BUCKET).astype(np.int32))
    return np.stack(tiles)


def _mask_tiles():
    k = np.arange(QB)[:, None]
    q = np.arange(QB)[None, :]
    diag = np.where(k <= q, 0.0, NEG_INF)
    return np.stack([np.full((QB, QB), NEG_INF), diag, np.zeros((QB, QB))]).astype(np.float32)


def _bucket_tiles_dilated():
    q = np.arange(QB)[:, None]
    k = np.arange(2 * QB)[None, :]
    rel = q + QB - k
    tiles = []
    for (window, d) in DILATED_PATTERNS:
        in_band = (rel >= 0) & (rel <= window // d)
        for has_prev in (False, True):
            ok = in_band & (has_prev | (k >= QB))
            tiles.append(np.where(ok, _t5_bucket_np(rel * d), MASKED_BUCKET).astype(np.int32))
    return np.stack(tiles)


def _bias_expand_kernel(table_ref, bucket_ref, out_ref, *, head0, scale, present):
    h = pl.program_id(0) + head0
    for n, buckets_in_tile in enumerate(present):
        bk = bucket_ref[n]
        acc = jnp.where(bk == MASKED_BUCKET, NEG_INF, 0.0)
        for b in buckets_in_tile:
            acc = jnp.where(bk == b, table_ref[b, h] * scale, acc)
        out_ref[n] = acc


def _bias_expand(table, buckets, head0, nheads, scale=1.0):
    n, r, c = buckets.shape
    present = tuple(tuple(int(b) for b in np.unique(tile) if b != MASKED_BUCKET) for tile in buckets)
    buckets = jnp.asarray(buckets)
    return pl.pallas_call(
        functools.partial(_bias_expand_kernel, head0=head0, scale=scale, present=present),
        grid=(nheads,),
        in_specs=[pl.BlockSpec(memory_space=pltpu.SMEM),
                  pl.BlockSpec((n, r, c), lambda h: (0, 0, 0))],
        out_specs=pl.BlockSpec((None, n, r, c), lambda h: (h, 0, 0, 0)),
        out_shape=jax.ShapeDtypeStruct((nheads, n, r, c), F32),
        compiler_params=_params(("arbitrary",)),
        name="bias_expand",
    )(table, buckets)


def _in_proj_kernel(x_ref, g_ref, w_ref, cos_ref, sin_ref, gq_ref, gkv_ref, wq_ref, wqrot_ref, wk_ref, wv_ref,
                    o32_ref, o16_ref, ob_ref, q_ref, k_ref, v_ref):
    x = x_ref[...]
    ms = jnp.mean(x * x, axis=-1, keepdims=True)
    xn = x * lax.rsqrt(ms + EPS) * g_ref[...]
    p = _dot(xn.astype(BF16), w_ref[...])
    o32_ref[...] = p[:, P_START:Q_START]
    o16_ref[...] = p[:, Q_START:B_START].astype(BF16)
    for s in range(B_SLABS):
        ob_ref[s] = p[:, B_START + s * LANES:B_START + (s + 1) * LANES]
    cos = cos_ref[...]
    sin = sin_ref[...]
    cos4 = jnp.concatenate([cos] * A_HEADS, axis=1)
    sin4 = jnp.concatenate([sin] * A_HEADS, axis=1)
    cq = p[:, L_CQ:L_CQ + 256]
    ms = jnp.sum(cq * cq, axis=-1, keepdims=True) * (1.0 / Q_LORA)
    nq = (cq * lax.rsqrt(ms + EPS) * gq_ref[...]).astype(BF16)
    q = _dot(nq, wq_ref[...]) * cos4 + _dot(nq, wqrot_ref[...]) * sin4
    q_ref[...] = (q * ((A_NOPE + A_ROPE) ** -0.5 * LOG2E)).astype(BF16)
    ckv = p[:, L_CKV:L_CKV + KV_LORA]
    ms = jnp.mean(ckv * ckv, axis=-1, keepdims=True)
    nkv = (ckv * lax.rsqrt(ms + EPS) * gkv_ref[...]).astype(BF16)
    kr = p[:, L_KR:L_KR + LANES] * cos + p[:, L_KRROT:L_KRROT + LANES] * sin
    k = _dot(nkv, wk_ref[...]) + jnp.concatenate([kr] * A_HEADS, axis=1)
    k_ref[...] = k.astype(BF16)
    v_ref[...] = _dot_nt(wv_ref[...], nkv).astype(BF16)


def _in_proj(h, g, w, layer, cos_t, sin_t, gq, gkv, wq, wqrot, wk, wv_t, tm=512):
    bsz, seq, _ = h.shape
    w4 = A_HEADS * LANES
    row = lambda b, i: (b, i, 0)
    const = lambda b, i: (0, 0)
    return pl.pallas_call(
        _in_proj_kernel,
        grid=(bsz, seq // tm),
        in_specs=[pl.BlockSpec((None, tm, D_MODEL), row),
                  pl.BlockSpec((1, D_MODEL), const),
                  pl.BlockSpec((None, D_MODEL, NCOL), lambda b, i: (layer, 0, 0)),
                  pl.BlockSpec((tm, LANES), lambda b, i: (i, 0)),
                  pl.BlockSpec((tm, LANES), lambda b, i: (i, 0)),
                  pl.BlockSpec((1, 256), const),
                  pl.BlockSpec((1, KV_LORA), const),
                  pl.BlockSpec((256, w4), const),
                  pl.BlockSpec((256, w4), const),
                  pl.BlockSpec((KV_LORA, w4), const),
                  pl.BlockSpec((A_HEADS * A_V, KV_LORA), const)],
        out_specs=[pl.BlockSpec((None, tm, NCOL32), row),
                   pl.BlockSpec((None, tm, NCOL16), row),
                   pl.BlockSpec((None, B_SLABS, tm, LANES), lambda b, i: (b, 0, i, 0)),
                   pl.BlockSpec((None, tm, w4), row), pl.BlockSpec((None, tm, w4), row),
                   pl.BlockSpec((None, A_HEADS * A_V, tm), lambda b, i: (b, 0, i))],
        out_shape=[jax.ShapeDtypeStruct((bsz, seq, NCOL32), F32),
                   jax.ShapeDtypeStruct((bsz, seq, NCOL16), BF16),
                   jax.ShapeDtypeStruct((bsz, B_SLABS, seq, LANES), F32),
                   jax.ShapeDtypeStruct((bsz, seq, w4), BF16), jax.ShapeDtypeStruct((bsz, seq, w4), BF16),
                   jax.ShapeDtypeStruct((bsz, A_HEADS * A_V, seq), BF16)],
        compiler_params=_params(("arbitrary", "arbitrary")),
        name="in_proj",
    )(h, g, w, cos_t, sin_t, gq, gkv, wq, wqrot, wk, wv_t)


def _silu(g):
    return g * (1.0 / (1.0 + jnp.exp(-g)))


def _for_step_blocks(i, seq, block_body, unroll_upto=0):
    per = KV_CHUNK // QB
    for wb in range(seq // KV_CHUNK):
        @pl.when(i == wb)
        def _(width=(wb + 1) * KV_CHUNK):
            if width <= unroll_upto:
                for u in range(per):
                    block_body(width, per * i + u, slice(u * QB, (u + 1) * QB))
                return

            def one(u, carry):
                block_body(width, per * i + u, pl.ds(pl.multiple_of(u * QB, QB), QB))
                return carry

            lax.fori_loop(0, per, one, 0)


def _mask_tail_t(mask_ref, i, width):
    first = (width - KV_CHUNK) // QB
    return jnp.concatenate([mask_ref[jnp.clip(i - j, -1, 1) + 1] for j in range(first, width // QB)], axis=0)


def _col_reduce(x, op):
    rows, lanes = x.shape
    part = op(x.reshape(rows // COL_ACC_ROWS, COL_ACC_ROWS, lanes), axis=0)
    return op(part, axis=0, keepdims=True)


def _bias_col(bias_ref, h, i, width):
    return jnp.concatenate([bias_ref[h, jnp.maximum(i - j, -1) + 1] for j in range(width // QB)], axis=0)


def _lane_mask(width, seg, dtype):
    lane = lax.broadcasted_iota(jnp.int32, (1, LANES), 1)
    return jnp.where((lane >= seg * width) & (lane < (seg + 1) * width), 1.0, 0.0).astype(dtype)


def _mla_attn_kernel(q_ref, k_ref, vt_ref, g_ref, mask_ref, o_ref):
    i = pl.program_id(1)
    seq = k_ref.shape[0]

    def body(width, ib, rows):
        head_w = width - KV_CHUNK
        zero = jnp.zeros((QB, LANES), BF16)
        scores = []
        for g in range(A_HEADS // 2):
            q0 = q_ref[rows, 2 * g * LANES:(2 * g + 1) * LANES]
            q1 = q_ref[rows, (2 * g + 1) * LANES:(2 * g + 2) * LANES]
            qbd = jnp.concatenate([jnp.concatenate([q0, zero], axis=1),
                                   jnp.concatenate([zero, q1], axis=1)], axis=0)
            scores.append(_dot_nt(k_ref[:width, 2 * g * LANES:(2 * g + 2) * LANES], qbd))
        tail = _mask_tail_t(mask_ref, ib, width)
        outs = []
        for g in range(A_HEADS // 2):
            es, ls = [], []
            for hh in range(2):
                s = scores[g][:, hh * QB:(hh + 1) * QB]
                s = jnp.concatenate([s[:head_w], s[head_w:] + tail], axis=0) if head_w else s + tail
                e = jnp.exp2(s - _col_reduce(s, jnp.max))
                ls.append(_col_reduce(e, jnp.sum))
                es.append(e.astype(BF16))
            ot = _dot(vt_ref[g * LANES:(g + 1) * LANES, :width], jnp.concatenate(es, axis=1))
            outs.append(ot[:A_V, :QB] / ls[0])
            outs.append(ot[A_V:, QB:] / ls[1])
        y_t = jnp.concatenate(outs, axis=0)
        y = jnp.concatenate([y_t[:LANES].T, y_t[LANES:].T], axis=1)
        o_ref[rows, :] = (y * _silu(g_ref[rows, :])).astype(BF16)

    _for_step_blocks(i, seq, body, unroll_upto=seq)


def _mla_attn(q, k, v_t, p32, mask_tiles):
    bsz, seq, w4 = q.shape
    return pl.pallas_call(
        _mla_attn_kernel,
        grid=(bsz, seq // KV_CHUNK),
        in_specs=[pl.BlockSpec((None, KV_CHUNK, w4), lambda b, i: (b, i, 0)),
                  pl.BlockSpec((None, seq, w4), lambda b, i: (b, 0, 0)),
                  pl.BlockSpec((None, A_HEADS * A_V, seq), lambda b, i: (b, 0, 0)),
                  pl.BlockSpec((None, KV_CHUNK, 256), lambda b, i: (b, i, P_AG // 256)),
                  pl.BlockSpec((3, QB, QB), lambda b, i: (0, 0, 0))],
        out_specs=pl.BlockSpec((None, KV_CHUNK, BRANCH_WIDTH), lambda b, i: (b, i, 0)),
        out_shape=jax.ShapeDtypeStruct((bsz, seq, BRANCH_WIDTH), BF16),
        compiler_params=_params(("arbitrary", "arbitrary")),
        name="mla_attn",
    )(q, k, v_t, p32, mask_tiles)


def _transpose_values(v_ref, vt_ref):
    for j in range(v_ref.shape[0] // QB):
        for g in range(v_ref.shape[1] // LANES):
            tile = v_ref[j * QB:(j + 1) * QB, g * LANES:(g + 1) * LANES]
            vt_ref[g * LANES:(g + 1) * LANES, j * QB:(j + 1) * QB] = tile.T.astype(BF16)


def _diff_attn_kernel(q_ref, k_ref, v_ref, g_ref, bias_ref, lam_ref, subln_ref, o_ref, vt_ref, *, lambda_init):
    i = pl.program_id(1)
    seq = k_ref.shape[0]

    @pl.when(i == 0)
    def _():
        _transpose_values(v_ref, vt_ref)

    def body(width, ib, rows):
        lp = lam_ref[...]
        lam = (jnp.exp(jnp.sum(lp[0:1] * lp[1:2], axis=-1, keepdims=True))
               - jnp.exp(jnp.sum(lp[2:3] * lp[3:4], axis=-1, keepdims=True)) + lambda_init)
        scores = []
        for h in range(4):
            sl = slice((h // 2) * LANES, (h // 2 + 1) * LANES)
            qg = q_ref[rows, sl] * (D_QK ** -0.5 * LOG2E)
            qcat = jnp.concatenate([(qg * _lane_mask(D_QK, 2 * (h % 2) + mm, F32)).astype(BF16)
                                    for mm in range(2)], axis=0)
            scores.append(_dot_nt(k_ref[:width, sl], qcat))
        outs = []
        for h in range(4):
            bias = _bias_col(bias_ref, h, ib, width)
            es, ls = [], []
            for mm in range(2):
                s = scores[h][:, mm * QB:(mm + 1) * QB] + bias
                e = jnp.exp2(s - _col_reduce(s, jnp.max))
                ls.append(_col_reduce(e, jnp.sum))
                es.append(e.astype(BF16))
            ot = _dot(vt_ref[h * HEAD_DIM:(h + 1) * HEAD_DIM, :width], jnp.concatenate(es, axis=1))
            a = ot[:, :QB] / ls[0] - lam * (ot[:, QB:] / ls[1])
            ms = jnp.sum(a * a, axis=0, keepdims=True) * (1.0 / HEAD_DIM)
            outs.append(a * lax.rsqrt(ms + EPS))
        y_t = jnp.concatenate(outs, axis=0)
        y = jnp.concatenate([y_t[:LANES].T, y_t[LANES:].T], axis=1)
        o_ref[rows, :] = (y * (subln_ref[...] * (1.0 - lambda_init)) * _silu(g_ref[rows, :])).astype(BF16)

    _for_step_blocks(i, seq, body, unroll_upto=seq // 2)


def _diff_attn(p32, p16, bias, lam_params, subln, lambda_init):
    bsz, seq, _ = p32.shape
    nq = seq // QB
    return pl.pallas_call(
        functools.partial(_diff_attn_kernel, lambda_init=lambda_init),
        grid=(bsz, seq // KV_CHUNK),
        in_specs=[pl.BlockSpec((None, KV_CHUNK, 256), lambda b, i: (b, i, P_DQ // 256)),
                  pl.BlockSpec((None, seq, 256), lambda b, i: (b, 0, Q_DK // 256)),
                  pl.BlockSpec((None, seq, 256), lambda b, i: (b, 0, P_DV // 256)),
                  pl.BlockSpec((None, KV_CHUNK, 256), lambda b, i: (b, i, P_DG // 256)),
                  pl.BlockSpec((4, nq + 1, QB, QB), lambda b, i: (0, 0, 0, 0)),
                  pl.BlockSpec((4, D_QK), lambda b, i: (0, 0)),
                  pl.BlockSpec((1, BRANCH_WIDTH), lambda b, i: (0, 0))],
        out_specs=pl.BlockSpec((None, KV_CHUNK, BRANCH_WIDTH), lambda b, i: (b, i, 0)),
        out_shape=jax.ShapeDtypeStruct((bsz, seq, BRANCH_WIDTH), BF16),
        scratch_shapes=[pltpu.VMEM((BRANCH_WIDTH, seq), BF16)],
        compiler_params=_params(("arbitrary", "arbitrary")),
        name="diff_attn",
    )(p32, p16, p32, p32, bias, lam_params, subln)


def _sortable_to_float(key):
    return pltpu.bitcast(jnp.where(key < 0, key ^ jnp.int32(0x7FFFFFFF), key), F32)


def _kth_largest(score_ref, width, k_top):
    def count_ge(key):
        thr = _sortable_to_float(key)
        return _col_reduce(jnp.where(score_ref[:width, :] >= thr, 1.0, 0.0), jnp.sum)

    int_min = jnp.full((1, QB), -2 ** 31, jnp.int32)
    zero = jnp.zeros((1, QB), jnp.int32)
    t = jnp.where(count_ge(zero) >= k_top, zero, int_min)

    def step(it, t):
        cand = t + (jnp.int32(1) << (30 - it))
        return jnp.where(count_ge(cand) >= k_top, cand, t)

    return _sortable_to_float(lax.fori_loop(0, 31, step, t))


def _first_ties(eq, need, row):
    eqf = jnp.where(eq, 1.0, 0.0)
    nbits = int(eq.shape[0]).bit_length()

    def body(it, j):
        cand = j + (jnp.int32(1) << (nbits - 1 - it))
        cnt = _col_reduce(jnp.where(row < cand, eqf, 0.0), jnp.sum)
        return jnp.where(cnt <= need, cand, j)

    j = lax.fori_loop(0, nbits, body, jnp.zeros((1, eq.shape[1]), jnp.int32))
    return eq & (row < j)


def _sparse_attn_kernel(q_ref, k_ref, v_ref, qi_ref, ki_ref, wi_ref, g_ref, bias_ref, o_ref,
                        score_ref, neg_ref, vt_ref, *, k_top):
    i = pl.program_id(1)
    seq = k_ref.shape[0]

    @pl.when(i == 0)
    def _():
        _transpose_values(v_ref, vt_ref)

    def body(width, ib, rows):
        head_w = width - KV_CHUNK
        ki = ki_ref[:width, :]
        w_t = wi_ref[rows, :].T * (IDX_DIM ** -0.5 * IDX_HEADS ** -0.5)
        score = None
        for h in range(0, IDX_HEADS, 2):
            qg = qi_ref[rows, (h // 4) * LANES:(h // 4 + 1) * LANES]
            qcat = jnp.concatenate([qg * _lane_mask(IDX_DIM, h % 4, BF16),
                                    qg * _lane_mask(IDX_DIM, h % 4 + 1, BF16)], axis=0)
            logit = _dot_nt(ki, qcat)
            term = (jnp.maximum(logit[:, :QB], 0.0) * w_t[h:h + 1]
                    + jnp.maximum(logit[:, QB:], 0.0) * w_t[h + 1:h + 2])
            score = term if score is None else score + term
        s_idx = head_w + lax.broadcasted_iota(jnp.int32, (KV_CHUNK, QB), 0)
        tail_ok = s_idx <= ib * QB + lax.broadcasted_iota(jnp.int32, (KV_CHUNK, QB), 1)
        if head_w:
            score_ref[:head_w, :] = score[:head_w]
        score_ref[head_w:width, :] = jnp.where(tail_ok, score[head_w:], NEG_INF)
        s2s = []
        for g in range(2):
            sl = slice(g * LANES, (g + 1) * LANES)
            qg = q_ref[rows, sl] * (HEAD_DIM ** -0.5 * LOG2E)
            qcat = jnp.concatenate([(qg * _lane_mask(HEAD_DIM, hh, F32)).astype(BF16) for hh in range(2)], axis=0)
            s2 = _dot_nt(k_ref[:width, sl], qcat)
            s2s.append([s2[:, hh * QB:(hh + 1) * QB] + _bias_col(bias_ref, 2 * g + hh, ib, width) for hh in range(2)])
        thr = _kth_largest(score_ref, width, k_top)
        keep_tail = (score_ref[head_w:width, :] >= thr) & tail_ok
        cnt = _col_reduce(jnp.where(keep_tail, 1.0, 0.0), jnp.sum)
        neg_ref[head_w:width, :] = jnp.where(keep_tail, 0.0, NEG_INF)
        if head_w:
            keep_head = score_ref[:head_w, :] >= thr
            cnt = cnt + _col_reduce(jnp.where(keep_head, 1.0, 0.0), jnp.sum)
            neg_ref[:head_w, :] = jnp.where(keep_head, 0.0, NEG_INF)

        @pl.when(jnp.max(cnt) > k_top)
        def _():
            x = score_ref[:width, :]
            gt = x > thr
            need = k_top - _col_reduce(jnp.where(gt, 1.0, 0.0), jnp.sum)
            row = lax.broadcasted_iota(jnp.int32, (width, QB), 0)
            neg_ref[:width, :] = jnp.where(gt | _first_ties(x == thr, need, row), 0.0, NEG_INF)

        outs = []
        for g in range(2):
            sl = slice(g * LANES, (g + 1) * LANES)
            es, ls = [], []
            for hh in range(2):
                s = s2s[g][hh] + neg_ref[:width, :]
                e = jnp.exp2(s - _col_reduce(s, jnp.max))
                ls.append(_col_reduce(e, jnp.sum))
                es.append(e.astype(BF16))
            ot = _dot(vt_ref[sl, :width], jnp.concatenate(es, axis=1))
            outs.append(ot[:HEAD_DIM, :QB] / ls[0])
            outs.append(ot[HEAD_DIM:, QB:] / ls[1])
        y_t = jnp.concatenate(outs, axis=0)
        y = jnp.concatenate([y_t[:LANES].T, y_t[LANES:].T], axis=1)
        o_ref[rows, :] = (y * _silu(g_ref[rows, :])).astype(BF16)

    _for_step_blocks(i, seq, body)


def _sparse_attn(p32, p16, bias, k_top):
    bsz, seq, _ = p32.shape
    nq = seq // QB
    return pl.pallas_call(
        functools.partial(_sparse_attn_kernel, k_top=k_top),
        grid=(bsz, seq // KV_CHUNK),
        in_specs=[pl.BlockSpec((None, KV_CHUNK, 256), lambda b, i: (b, i, P_CQ2 // 256)),
                  pl.BlockSpec((None, seq, 256), lambda b, i: (b, 0, Q_CK // 256)),
                  pl.BlockSpec((None, seq, 256), lambda b, i: (b, 0, P_CV // 256)),
                  pl.BlockSpec((None, KV_CHUNK, 256), lambda b, i: (b, i, Q_CQI // 256)),
                  pl.BlockSpec((None, seq, LANES), lambda b, i: (b, 0, Q_CKI // LANES)),
                  pl.BlockSpec((None, KV_CHUNK, LANES), lambda b, i: (b, i, P_CWI // LANES)),
                  pl.BlockSpec((None, KV_CHUNK, 256), lambda b, i: (b, i, P_CG // 256)),
                  pl.BlockSpec((4, nq + 1, QB, QB), lambda b, i: (0, 0, 0, 0))],
        out_specs=pl.BlockSpec((None, KV_CHUNK, BRANCH_WIDTH), lambda b, i: (b, i, 0)),
        out_shape=jax.ShapeDtypeStruct((bsz, seq, BRANCH_WIDTH), BF16),
        scratch_shapes=[pltpu.VMEM((seq, QB), F32), pltpu.VMEM((seq, QB), F32),
                        pltpu.VMEM((BRANCH_WIDTH, seq), BF16)],
        compiler_params=_params(("arbitrary", "arbitrary")),
        name="sparse_attn",
    )(p32, p16, p32, p16, p16, p32, p32, bias)


def _dilated_kernel(x_ref, g_ref, bias_ref, o_ref, m_ref, l_ref, acc_ref):
    first = lax.broadcasted_iota(jnp.int32, (QB, LANES), 1) < HEAD_DIM
    hmask = [_lane_mask(HEAD_DIM, hh, F32) for hh in range(2)]

    def run(p, tiles, sink):
        work = []
        for cur, prev, variant in tiles:
            for g in range(2):
                q = x_ref[g, cur, :]
                kcat = x_ref[2 + g, cur, :]
                vcat = x_ref[4 + g, cur, :]
                if prev is not None:
                    kcat = jnp.concatenate([x_ref[2 + g, prev, :], kcat], axis=0)
                    vcat = jnp.concatenate([x_ref[4 + g, prev, :], vcat], axis=0)
                kcat = kcat.astype(BF16)
                scores = []
                for hh in range(2):
                    if prev is not None:
                        bias = bias_ref[2 * g + hh, 2 * p + variant]
                    else:
                        bias = bias_ref[2 * g + hh, 2 * p, :, QB:]
                    scores.append(_dot_nt((q * hmask[hh]).astype(BF16), kcat) + bias)
                work.append((vcat.astype(BF16), scores))
        for n, (vcat, scores) in enumerate(work):
            ms, ls, accs = [], [], []
            for s in scores:
                m = jnp.max(s, axis=-1, keepdims=True)
                e = jnp.exp2(s - m)
                ms.append(m)
                ls.append(jnp.sum(e, axis=-1, keepdims=True))
                accs.append(_dot(e.astype(BF16), vcat))
            sink(n // 2, n % 2, jnp.where(first, ms[0], ms[1]), jnp.where(first, ls[0], ls[1]),
                 jnp.where(first, accs[0], accs[1]))

    def store_stats(slot, d, starts):
        def sink(t, g, m, l, a):
            rows = pl.ds(starts[t], QB, stride=d)
            m_ref[slot, g, rows, :] = m
            l_ref[slot, g, rows, :] = l
            acc_ref[slot, g, rows, :] = a
        return sink

    def step16(n0, carry):
        starts = [n0 * DIL_UNROLL + u for u in range(DIL_UNROLL)]
        run(2, [(pl.ds(r, QB, stride=16), None, 0) for r in starts], store_stats(1, 16, starts))
        return carry

    lax.fori_loop(0, 16 // DIL_UNROLL, step16, 0)

    def step4(j, carry):
        starts = [r + 4 * QB * j for r in range(4)]
        tiles = [(pl.ds(s, QB, stride=4), pl.ds(jnp.maximum(s - 4 * QB, r), QB, stride=4), jnp.minimum(j, 1))
                 for r, s in enumerate(starts)]
        run(1, tiles, store_stats(0, 4, starts))
        return carry

    lax.fori_loop(0, 4, step4, 0)

    def step1(n0, carry):
        blocks = [n0 * DIL_UNROLL + u for u in range(DIL_UNROLL)]

        def sink(t, g, m, l, a):
            rows = pl.ds(pl.multiple_of(blocks[t] * QB, QB), QB)
            lanes = slice(g * LANES, (g + 1) * LANES)
            m4, m16 = m_ref[0, g, rows, :], m_ref[1, g, rows, :]
            m_tot = jnp.maximum(m, jnp.maximum(m4, m16))
            w1, w4, w16 = jnp.exp2(m - m_tot), jnp.exp2(m4 - m_tot), jnp.exp2(m16 - m_tot)
            num = w1 * a + w4 * acc_ref[0, g, rows, :] + w16 * acc_ref[1, g, rows, :]
            den = w1 * l + w4 * l_ref[0, g, rows, :] + w16 * l_ref[1, g, rows, :]
            o_ref[rows, lanes] = (num / den * _silu(g_ref[rows, lanes])).astype(BF16)

        tiles = []
        for j in blocks:
            cur = pl.ds(pl.multiple_of(j * QB, QB), QB)
            prev = pl.ds(pl.multiple_of(jnp.maximum(j - 1, 0) * QB, QB), QB)
            tiles.append((cur, prev, jnp.minimum(j, 1)))
        run(0, tiles, sink)
        return carry

    lax.fori_loop(0, g_ref.shape[0] // QB // DIL_UNROLL, step1, 0)


def _dilated_attn(pb, p32, bias):
    bsz, _, seq, _ = pb.shape
    stats = pltpu.VMEM((2, 2, seq, LANES), F32)
    return pl.pallas_call(
        _dilated_kernel,
        grid=(bsz,),
        in_specs=[pl.BlockSpec((None, B_SLABS, seq, LANES), lambda b: (b, 0, 0, 0)),
                  pl.BlockSpec((None, seq, 256), lambda b: (b, 0, P_BG // 256)),
                  pl.BlockSpec((4, 2 * len(DILATED_PATTERNS), QB, 2 * QB), lambda b: (0, 0, 0, 0))],
        out_specs=pl.BlockSpec((None, seq, BRANCH_WIDTH), lambda b: (b, 0, 0)),
        out_shape=jax.ShapeDtypeStruct((bsz, seq, BRANCH_WIDTH), BF16),
        scratch_shapes=[stats, stats, stats],
        compiler_params=_params(("arbitrary",)),
        name="dilated_attn",
    )(pb, p32, bias)


def _out_proj_kernel(ya_ref, yb_ref, yc_ref, yd_ref, w_ref, h_ref, g_ref, o_ref):
    y = _dot(jnp.concatenate([ya_ref[...], yb_ref[...], yc_ref[...], yd_ref[...]], axis=1), w_ref[...])
    ms = jnp.mean(y * y, axis=-1, keepdims=True)
    o_ref[...] = h_ref[...] + y * lax.rsqrt(ms + EPS) * g_ref[...]


def _out_proj(ya, yb, yc, yd, w, layer, h, g, tm=1024):
    bsz, seq, _ = h.shape
    yspec = pl.BlockSpec((None, tm, BRANCH_WIDTH), lambda b, i: (b, i, 0))
    return pl.pallas_call(
        _out_proj_kernel,
        grid=(bsz, seq // tm),
        in_specs=[yspec, yspec, yspec, yspec,
                  pl.BlockSpec((None, 4 * BRANCH_WIDTH, D_MODEL), lambda b, i: (layer, 0, 0)),
                  pl.BlockSpec((None, tm, D_MODEL), lambda b, i: (b, i, 0)),
                  pl.BlockSpec((1, D_MODEL), lambda b, i: (0, 0))],
        out_specs=pl.BlockSpec((None, tm, D_MODEL), lambda b, i: (b, i, 0)),
        out_shape=jax.ShapeDtypeStruct(h.shape, F32),
        compiler_params=_params(("arbitrary", "arbitrary")),
        name="out_proj",
    )(ya, yb, yc, yd, w, h, g)


def kernel(x, w_in, w_out, norm_pre, norm_post, mla_q_norm, mla_kv_norm, mla_w_uq, mla_w_ukv,
           diff_lambda, diff_subln, rel_bias):
    bsz, seq, _ = x.shape
    depth = w_in.shape[0]
    nq = seq // QB
    k_top = min(IDX_TOPK_MAX, seq // 4)
    cos_t, sin_t = _rope_tables(seq)
    bias_b = _bias_expand(rel_bias, _bucket_tiles_dilated(), 0, 4, LOG2E)
    causal_buckets_t = np.ascontiguousarray(np.swapaxes(_bucket_tiles_causal(nq), 1, 2))
    bias_c = _bias_expand(rel_bias, causal_buckets_t, 4, 4, LOG2E)
    bias_d = _bias_expand(rel_bias, causal_buckets_t, 8, 4, LOG2E)
    mask_tiles = jnp.asarray(_mask_tiles())
    w_arr_all = _arrange_w_in(w_in)
    w_out16 = w_out.astype(BF16)
    h = x
    for layer in range(depth):
        wq, wqrot, wk, wv = _arrange_mla(mla_w_uq[layer], mla_w_ukv[layer])
        gq = jnp.concatenate([mla_q_norm[layer], jnp.ones((256 - Q_LORA,), F32)])[None, :]
        gkv = mla_kv_norm[layer][None, :]
        p32, p16, pb, qa, ka, va = _in_proj(h, norm_pre[layer][None, :], w_arr_all, layer,
                                            cos_t, sin_t, gq, gkv, wq, wqrot, wk, wv)
        y_a = _mla_attn(qa, ka, va, p32, mask_tiles)
        y_b = _dilated_attn(pb, p32, bias_b)
        y_c = _sparse_attn(p32, p16, bias_c, k_top)
        lambda_init = 0.8 - 0.6 * math.exp(-0.3 * layer)
        subln = jnp.tile(diff_subln[layer], BRANCH_WIDTH // HEAD_DIM)[None, :]
        y_d = _diff_attn(p32, p16, bias_d, diff_lambda[layer], subln, lambda_init)
        h = _out_proj(y_a, y_b, y_c, y_d, w_out16, layer, h, norm_post[layer][None, :])
    return h
```

```python
import functools
import math

import jax
import jax.numpy as jnp
import numpy as np
from jax import lax
from jax.experimental import pallas as pl
from jax.experimental.pallas import tpu as pltpu

F32 = jnp.float32
BF16 = jnp.bfloat16

D_MODEL = 1024
A_HEADS, A_NOPE, A_ROPE, A_V = 4, 64, 32, 64
Q_LORA, KV_LORA = 192, 128
ROPE_THETA = 10000.0
HEAD_DIM = 64
DILATED_PATTERNS = ((128, 1), (512, 4), (2048, 16))
IDX_HEADS, IDX_DIM, IDX_TOPK_MAX = 8, 32, 256
D_QK = 32
BRANCH_WIDTH = 256
NUM_BUCKETS, MAX_DISTANCE = 32, 2048
NEG_INF = -1e30
EPS = 1e-6
LOG2E = math.log2(math.e)
KV_CHUNK = 256
COL_ACC_ROWS = 64
DIL_UNROLL = 4
LANES = 128
QB = 128
VMEM_LIMIT = 56 * 1024 * 1024

_SPLIT = (Q_LORA, KV_LORA, A_ROPE, 256, 256, 256, 256, 256, 256, 256, 256, IDX_HEADS * IDX_DIM, IDX_DIM,
          IDX_HEADS, 256, 256, 256, 256, 256)
_OFF = np.concatenate([[0], np.cumsum(_SPLIT)]).tolist()
(_A_CQ, _A_CKV, _A_KR, _A_G, _B_Q, _B_K, _B_V, _B_G, _C_Q, _C_K, _C_V, _C_QI, _C_KI, _C_WI, _C_G,
 _D_Q, _D_K, _D_V, _D_G) = range(19)

L_CQ, L_CKV, L_KR, L_KRROT = 0, 256, 384, 512
P_START = 512
P_CWI, P_AG, P_BG, P_CQ2, P_CV, P_CG, P_DQ, P_DV, P_DG = 128, 256, 512, 768, 1024, 1280, 1536, 1792, 2048
NCOL32 = 2304
Q_START = P_START + NCOL32
Q_CK, Q_CQI, Q_DK, Q_CKI = 0, 256, 512, 768
NCOL16 = 896
B_START = Q_START + NCOL16
B_SLABS = 6
NCOL = B_START + B_SLABS * LANES


def _dot(a, b):
    return jnp.dot(a, b, preferred_element_type=F32)


def _dot_nt(a, b):
    return lax.dot_general(a, b, (((1,), (1,)), ((), ())), preferred_element_type=F32)


def _params(sem):
    return pltpu.CompilerParams(dimension_semantics=sem, vmem_limit_bytes=VMEM_LIMIT)


def _rot_cols(w):
    half = w.shape[-1] // 2
    return jnp.concatenate([-w[..., half:], w[..., :half]], axis=-1)


def _arrange_w_in(w):
    depth, rows, in_cols = w.shape
    tm = 256
    return pl.pallas_call(
        _arrange_w_in_kernel,
        grid=(depth, rows // tm),
        in_specs=[pl.BlockSpec((None, tm, in_cols), lambda l, i: (l, i, 0))],
        out_specs=pl.BlockSpec((None, tm, NCOL), lambda l, i: (l, i, 0)),
        out_shape=jax.ShapeDtypeStruct((depth, rows, NCOL), BF16),
        compiler_params=_params(("arbitrary", "arbitrary")),
        name="arrange_w_in",
    )(w)


def _w_in_pieces():
    pieces = []

    def put(dst, seg, lo=0, hi=None, scale=1.0):
        hi = _SPLIT[seg] if hi is None else hi
        pieces.append((dst, _OFF[seg] + lo, hi - lo, scale))
        return dst + hi - lo

    put(L_CQ, _A_CQ)
    put(L_CKV, _A_CKV)
    put(L_KR + A_NOPE, _A_KR)
    half = A_ROPE // 2
    put(L_KRROT + A_NOPE, _A_KR, half, A_ROPE, scale=-1.0)
    put(L_KRROT + A_NOPE + half, _A_KR, 0, half)
    for dst, seg in ((P_CWI, _C_WI), (P_AG, _A_G), (P_BG, _B_G), (P_CQ2, _C_Q), (P_CV, _C_V), (P_CG, _C_G),
                     (P_DQ, _D_Q), (P_DV, _D_V), (P_DG, _D_G)):
        put(P_START + dst, seg)
    for dst, seg in ((Q_CK, _C_K), (Q_CQI, _C_QI), (Q_DK, _D_K)):
        put(Q_START + dst, seg)
    for copy in range(LANES // IDX_DIM):
        put(Q_START + Q_CKI + copy * IDX_DIM, _C_KI)
    base = put(B_START, _B_Q, scale=HEAD_DIM ** -0.5 * LOG2E)
    base = put(base, _B_K)
    base = put(base, _B_V)
    assert base == NCOL
    return pieces


def _arrange_w_in_kernel(w_ref, o_ref):
    o_ref[...] = jnp.zeros(o_ref.shape, o_ref.dtype)
    for dst, src, n, scale in _w_in_pieces():
        lo = src // LANES * LANES
        hi = min(-(-(src + n) // LANES) * LANES, w_ref.shape[1])
        v = w_ref[:, lo:hi][:, src - lo:src - lo + n]
        o_ref[:, dst:dst + n] = (v * scale if scale != 1.0 else v).astype(o_ref.dtype)


def _arrange_mla(w_uq, w_ukv):
    wq = w_uq.reshape(Q_LORA, A_HEADS, A_NOPE + A_ROPE)
    nope, rope = wq[..., :A_NOPE], wq[..., A_NOPE:]
    zq = jnp.zeros((Q_LORA, A_HEADS, LANES - A_NOPE - A_ROPE), w_uq.dtype)
    wq_main = jnp.concatenate([nope, rope, zq], axis=-1).reshape(Q_LORA, A_HEADS * LANES)
    wq_rot = jnp.concatenate([jnp.zeros_like(nope), _rot_cols(rope), zq], axis=-1).reshape(Q_LORA, A_HEADS * LANES)
    pad = jnp.zeros((256 - Q_LORA, A_HEADS * LANES), w_uq.dtype)
    wq_main = jnp.concatenate([wq_main, pad], axis=0)
    wq_rot = jnp.concatenate([wq_rot, pad], axis=0)
    wkv = w_ukv.reshape(KV_LORA, A_HEADS, A_NOPE + A_V)
    knope, v = wkv[..., :A_NOPE], wkv[..., A_NOPE:]
    wk = jnp.concatenate([knope, jnp.zeros_like(knope)], axis=-1).reshape(KV_LORA, A_HEADS * LANES)
    wv_t = v.reshape(KV_LORA, A_HEADS * A_V).T
    return wq_main.astype(BF16), wq_rot.astype(BF16), wk.astype(BF16), wv_t.astype(BF16)


def _rope_tables(seq):
    inv = ROPE_THETA ** (-jnp.arange(0, A_ROPE, 2, dtype=F32) / A_ROPE)
    ang = jnp.arange(seq, dtype=F32)[:, None] * inv[None, :]
    cos, sin = jnp.cos(ang), jnp.sin(ang)
    one = jnp.ones((seq, A_NOPE), F32)
    zero = jnp.zeros((seq, LANES - A_NOPE - A_ROPE), F32)
    cos_t = jnp.concatenate([one, cos, cos, zero], axis=1)
    sin_t = jnp.concatenate([jnp.zeros_like(one), sin, sin, zero], axis=1)
    return cos_t, sin_t


def _t5_bucket_np(rel):
    n = np.maximum(rel, 0)
    max_exact = NUM_BUCKETS // 2
    nf = np.maximum(n, max_exact).astype(np.float64)
    large = max_exact + (np.log(nf / max_exact) / math.log(MAX_DISTANCE / max_exact)
                         * (NUM_BUCKETS - max_exact)).astype(np.int32)
    large = np.minimum(large, NUM_BUCKETS - 1)
    return np.where(n < max_exact, n, large).astype(np.int32)


MASKED_BUCKET = NUM_BUCKETS


def _bucket_tiles_causal(nq):
    q = np.arange(QB)[:, None]
    k = np.arange(QB)[None, :]
    tiles = [np.full((QB, QB), MASKED_BUCKET, np.int32)]
    for d in range(nq):
        rel = QB * d + q - k
        tiles.append(np.where(rel >= 0, _t5_bucket_np(rel), MASKED_BUCKET).astype(np.int32))
    return np.stack(tiles)


def _mask_tiles():
    k = np.arange(QB)[:, None]
    q = np.arange(QB)[None, :]
    diag = np.where(k <= q, 0.0, NEG_INF)
    return np.stack([np.full((QB, QB), NEG_INF), diag, np.zeros((QB, QB))]).astype(np.float32)


def _bucket_tiles_dilated():
    q = np.arange(QB)[:, None]
    k = np.arange(2 * QB)[None, :]
    rel = q + QB - k
    tiles = []
    for (window, d) in DILATED_PATTERNS:
        in_band = (rel >= 0) & (rel <= window // d)
        for has_prev in (False, True):
            ok = in_band & (has_prev | (k >= QB))
            tiles.append(np.where(ok, _t5_bucket_np(rel * d), MASKED_BUCKET).astype(np.int32))
    return np.stack(tiles)


def _bias_expand_kernel(table_ref, bucket_ref, out_ref, *, head0, scale, present):
    h = pl.program_id(0) + head0
    for n, buckets_in_tile in enumerate(present):
        bk = bucket_ref[n]
        acc = jnp.where(bk == MASKED_BUCKET, NEG_INF, 0.0)
        for b in buckets_in_tile:
            acc = jnp.where(bk == b, table_ref[b, h] * scale, acc)
        out_ref[n] = acc


def _bias_expand(table, buckets, head0, nheads, scale=1.0):
    n, r, c = buckets.shape
    present = tuple(tuple(int(b) for b in np.unique(tile) if b != MASKED_BUCKET) for tile in buckets)
    buckets = jnp.asarray(buckets)
    return pl.pallas_call(
        functools.partial(_bias_expand_kernel, head0=head0, scale=scale, present=present),
        grid=(nheads,),
        in_specs=[pl.BlockSpec(memory_space=pltpu.SMEM),
                  pl.BlockSpec((n, r, c), lambda h: (0, 0, 0))],
        out_specs=pl.BlockSpec((None, n, r, c), lambda h: (h, 0, 0, 0)),
        out_shape=jax.ShapeDtypeStruct((nheads, n, r, c), F32),
        compiler_params=_params(("arbitrary",)),
        name="bias_expand",
    )(table, buckets)


def _in_proj_kernel(x_ref, g_ref, w_ref, cos_ref, sin_ref, gq_ref, gkv_ref, wq_ref, wqrot_ref, wk_ref, wv_ref,
                    o32_ref, o16_ref, ob_ref, q_ref, k_ref, v_ref):
    x = x_ref[...]
    ms = jnp.mean(x * x, axis=-1, keepdims=True)
    xn = x * lax.rsqrt(ms + EPS) * g_ref[...]
    p = _dot(xn.astype(BF16), w_ref[...])
    o32_ref[...] = p[:, P_START:Q_START]
    o16_ref[...] = p[:, Q_START:B_START].astype(BF16)
    for s in range(B_SLABS):
        ob_ref[s] = p[:, B_START + s * LANES:B_START + (s + 1) * LANES]
    cos = cos_ref[...]
    sin = sin_ref[...]
    cos4 = jnp.concatenate([cos] * A_HEADS, axis=1)
    sin4 = jnp.concatenate([sin] * A_HEADS, axis=1)
    cq = p[:, L_CQ:L_CQ + 256]
    ms = jnp.sum(cq * cq, axis=-1, keepdims=True) * (1.0 / Q_LORA)
    nq = (cq * lax.rsqrt(ms + EPS) * gq_ref[...]).astype(BF16)
    q = _dot(nq, wq_ref[...]) * cos4 + _dot(nq, wqrot_ref[...]) * sin4
    q_ref[...] = (q * ((A_NOPE + A_ROPE) ** -0.5 * LOG2E)).astype(BF16)
    ckv = p[:, L_CKV:L_CKV + KV_LORA]
    ms = jnp.mean(ckv * ckv, axis=-1, keepdims=True)
    nkv = (ckv * lax.rsqrt(ms + EPS) * gkv_ref[...]).astype(BF16)
    kr = p[:, L_KR:L_KR + LANES] * cos + p[:, L_KRROT:L_KRROT + LANES] * sin
    k = _dot(nkv, wk_ref[...]) + jnp.concatenate([kr] * A_HEADS, axis=1)
    k_ref[...] = k.astype(BF16)
    v_ref[...] = _dot_nt(wv_ref[...], nkv).astype(BF16)


def _in_proj(h, g, w, layer, cos_t, sin_t, gq, gkv, wq, wqrot, wk, wv_t, tm=512):
    bsz, seq, _ = h.shape
    w4 = A_HEADS * LANES
    row = lambda b, i: (b, i, 0)
    const = lambda b, i: (0, 0)
    return pl.pallas_call(
        _in_proj_kernel,
        grid=(bsz, seq // tm),
        in_specs=[pl.BlockSpec((None, tm, D_MODEL), row),
                  pl.BlockSpec((1, D_MODEL), const),
                  pl.BlockSpec((None, D_MODEL, NCOL), lambda b, i: (layer, 0, 0)),
                  pl.BlockSpec((tm, LANES), lambda b, i: (i, 0)),
                  pl.BlockSpec((tm, LANES), lambda b, i: (i, 0)),
                  pl.BlockSpec((1, 256), const),
                  pl.BlockSpec((1, KV_LORA), const),
                  pl.BlockSpec((256, w4), const),
                  pl.BlockSpec((256, w4), const),
                  pl.BlockSpec((KV_LORA, w4), const),
                  pl.BlockSpec((A_HEADS * A_V, KV_LORA), const)],
        out_specs=[pl.BlockSpec((None, tm, NCOL32), row),
                   pl.BlockSpec((None, tm, NCOL16), row),
                   pl.BlockSpec((None, B_SLABS, tm, LANES), lambda b, i: (b, 0, i, 0)),
                   pl.BlockSpec((None, tm, w4), row), pl.BlockSpec((None, tm, w4), row),
                   pl.BlockSpec((None, A_HEADS * A_V, tm), lambda b, i: (b, 0, i))],
        out_shape=[jax.ShapeDtypeStruct((bsz, seq, NCOL32), F32),
                   jax.ShapeDtypeStruct((bsz, seq, NCOL16), BF16),
                   jax.ShapeDtypeStruct((bsz, B_SLABS, seq, LANES), F32),
                   jax.ShapeDtypeStruct((bsz, seq, w4), BF16), jax.ShapeDtypeStruct((bsz, seq, w4), BF16),
                   jax.ShapeDtypeStruct((bsz, A_HEADS * A_V, seq), BF16)],
        compiler_params=_params(("arbitrary", "arbitrary")),
        name="in_proj",
    )(h, g, w, cos_t, sin_t, gq, gkv, wq, wqrot, wk, wv_t)


def _silu(g):
    return g * (1.0 / (1.0 + jnp.exp(-g)))


def _for_step_blocks(i, seq, block_body, unroll_upto=0):
    per = KV_CHUNK // QB
    for wb in range(seq // KV_CHUNK):
        @pl.when(i == wb)
        def _(width=(wb + 1) * KV_CHUNK):
            if width <= unroll_upto:
                for u in range(per):
                    block_body(width, per * i + u, slice(u * QB, (u + 1) * QB))
                return

            def one(u, carry):
                block_body(width, per * i + u, pl.ds(pl.multiple_of(u * QB, QB), QB))
                return carry

            lax.fori_loop(0, per, one, 0)


def _mask_tail_t(mask_ref, i, width):
    first = (width - KV_CHUNK) // QB
    return jnp.concatenate([mask_ref[jnp.clip(i - j, -1, 1) + 1] for j in range(first, width // QB)], axis=0)


def _col_reduce(x, op):
    rows, lanes = x.shape
    part = op(x.reshape(rows // COL_ACC_ROWS, COL_ACC_ROWS, lanes), axis=0)
    return op(part, axis=0, keepdims=True)


def _bias_col(bias_ref, h, i, width):
    return jnp.concatenate([bias_ref[h, jnp.maximum(i - j, -1) + 1] for j in range(width // QB)], axis=0)


def _lane_mask(width, seg, dtype):
    lane = lax.broadcasted_iota(jnp.int32, (1, LANES), 1)
    return jnp.where((lane >= seg * width) & (lane < (seg + 1) * width), 1.0, 0.0).astype(dtype)


def _mla_attn_kernel(q_ref, k_ref, vt_ref, g_ref, mask_ref, o_ref):
    i = pl.program_id(1)
    seq = k_ref.shape[0]

    def body(width, ib, rows):
        head_w = width - KV_CHUNK
        zero = jnp.zeros((QB, LANES), BF16)
        scores = []
        for g in range(A_HEADS // 2):
            q0 = q_ref[rows, 2 * g * LANES:(2 * g + 1) * LANES]
            q1 = q_ref[rows, (2 * g + 1) * LANES:(2 * g + 2) * LANES]
            qbd = jnp.concatenate([jnp.concatenate([q0, zero], axis=1),
                                   jnp.concatenate([zero, q1], axis=1)], axis=0)
            scores.append(_dot_nt(k_ref[:width, 2 * g * LANES:(2 * g + 2) * LANES], qbd))
        tail = _mask_tail_t(mask_ref, ib, width)
        outs = []
        for g in range(A_HEADS // 2):
            es, ls = [], []
            for hh in range(2):
                s = scores[g][:, hh * QB:(hh + 1) * QB]
                s = jnp.concatenate([s[:head_w], s[head_w:] + tail], axis=0) if head_w else s + tail
                e = jnp.exp2(s - _col_reduce(s, jnp.max))
                ls.append(_col_reduce(e, jnp.sum))
                es.append(e.astype(BF16))
            ot = _dot(vt_ref[g * LANES:(g + 1) * LANES, :width], jnp.concatenate(es, axis=1))
            outs.append(ot[:A_V, :QB] / ls[0])
            outs.append(ot[A_V:, QB:] / ls[1])
        y_t = jnp.concatenate(outs, axis=0)
        y = jnp.concatenate([y_t[:LANES].T, y_t[LANES:].T], axis=1)
        o_ref[rows, :] = (y * _silu(g_ref[rows, :])).astype(BF16)

    _for_step_blocks(i, seq, body, unroll_upto=seq)


def _mla_attn(q, k, v_t, p32, mask_tiles):
    bsz, seq, w4 = q.shape
    return pl.pallas_call(
        _mla_attn_kernel,
        grid=(bsz, seq // KV_CHUNK),
        in_specs=[pl.BlockSpec((None, KV_CHUNK, w4), lambda b, i: (b, i, 0)),
                  pl.BlockSpec((None, seq, w4), lambda b, i: (b, 0, 0)),
                  pl.BlockSpec((None, A_HEADS * A_V, seq), lambda b, i: (b, 0, 0)),
                  pl.BlockSpec((None, KV_CHUNK, 256), lambda b, i: (b, i, P_AG // 256)),
                  pl.BlockSpec((3, QB, QB), lambda b, i: (0, 0, 0))],
        out_specs=pl.BlockSpec((None, KV_CHUNK, BRANCH_WIDTH), lambda b, i: (b, i, 0)),
        out_shape=jax.ShapeDtypeStruct((bsz, seq, BRANCH_WIDTH), BF16),
        compiler_params=_params(("arbitrary", "arbitrary")),
        name="mla_attn",
    )(q, k, v_t, p32, mask_tiles)


def _transpose_values(v_ref, vt_ref):
    for j in range(v_ref.shape[0] // QB):
        for g in range(v_ref.shape[1] // LANES):
            tile = v_ref[j * QB:(j + 1) * QB, g * LANES:(g + 1) * LANES]
            vt_ref[g * LANES:(g + 1) * LANES, j * QB:(j + 1) * QB] = tile.T.astype(BF16)


def _diff_attn_kernel(q_ref, k_ref, v_ref, g_ref, bias_ref, lam_ref, subln_ref, o_ref, vt_ref, *, lambda_init):
    i = pl.program_id(1)
    seq = k_ref.shape[0]

    @pl.when(i == 0)
    def _():
        _transpose_values(v_ref, vt_ref)

    def body(width, ib, rows):
        lp = lam_ref[...]
        lam = (jnp.exp(jnp.sum(lp[0:1] * lp[1:2], axis=-1, keepdims=True))
               - jnp.exp(jnp.sum(lp[2:3] * lp[3:4], axis=-1, keepdims=True)) + lambda_init)
        scores = []
        for h in range(4):
            sl = slice((h // 2) * LANES, (h // 2 + 1) * LANES)
            qg = q_ref[rows, sl] * (D_QK ** -0.5 * LOG2E)
            qcat = jnp.concatenate([(qg * _lane_mask(D_QK, 2 * (h % 2) + mm, F32)).astype(BF16)
                                    for mm in range(2)], axis=0)
            scores.append(_dot_nt(k_ref[:width, sl], qcat))
        outs = []
        for h in range(4):
            bias = _bias_col(bias_ref, h, ib, width)
            es, ls = [], []
            for mm in range(2):
                s = scores[h][:, mm * QB:(mm + 1) * QB] + bias
                e = jnp.exp2(s - _col_reduce(s, jnp.max))
                ls.append(_col_reduce(e, jnp.sum))
                es.append(e.astype(BF16))
            ot = _dot(vt_ref[h * HEAD_DIM:(h + 1) * HEAD_DIM, :width], jnp.concatenate(es, axis=1))
            a = ot[:, :QB] / ls[0] - lam * (ot[:, QB:] / ls[1])
            ms = jnp.sum(a * a, axis=0, keepdims=True) * (1.0 / HEAD_DIM)
            outs.append(a * lax.rsqrt(ms + EPS))
        y_t = jnp.concatenate(outs, axis=0)
        y = jnp.concatenate([y_t[:LANES].T, y_t[LANES:].T], axis=1)
        o_ref[rows, :] = (y * (subln_ref[...] * (1.0 - lambda_init)) * _silu(g_ref[rows, :])).astype(BF16)

    _for_step_blocks(i, seq, body, unroll_upto=3 * seq // 4)


def _diff_attn(p32, p16, bias, lam_params, subln, lambda_init):
    bsz, seq, _ = p32.shape
    nq = seq // QB
    return pl.pallas_call(
        functools.partial(_diff_attn_kernel, lambda_init=lambda_init),
        grid=(bsz, seq // KV_CHUNK),
        in_specs=[pl.BlockSpec((None, KV_CHUNK, 256), lambda b, i: (b, i, P_DQ // 256)),
                  pl.BlockSpec((None, seq, 256), lambda b, i: (b, 0, Q_DK // 256)),
                  pl.BlockSpec((None, seq, 256), lambda b, i: (b, 0, P_DV // 256)),
                  pl.BlockSpec((None, KV_CHUNK, 256), lambda b, i: (b, i, P_DG // 256)),
                  pl.BlockSpec((4, nq + 1, QB, QB), lambda b, i: (0, 0, 0, 0)),
                  pl.BlockSpec((4, D_QK), lambda b, i: (0, 0)),
                  pl.BlockSpec((1, BRANCH_WIDTH), lambda b, i: (0, 0))],
        out_specs=pl.BlockSpec((None, KV_CHUNK, BRANCH_WIDTH), lambda b, i: (b, i, 0)),
        out_shape=jax.ShapeDtypeStruct((bsz, seq, BRANCH_WIDTH), BF16),
        scratch_shapes=[pltpu.VMEM((BRANCH_WIDTH, seq), BF16)],
        compiler_params=_params(("arbitrary", "arbitrary")),
        name="diff_attn",
    )(p32, p16, p32, p32, bias, lam_params, subln)


def _sortable_to_float(key):
    return pltpu.bitcast(jnp.where(key < 0, key ^ jnp.int32(0x7FFFFFFF), key), F32)


def _kth_largest(score_ref, width, k_top):
    def count_ge(key):
        thr = _sortable_to_float(key)
        return _col_reduce(jnp.where(score_ref[:width, :] >= thr, 1.0, 0.0), jnp.sum)

    int_min = jnp.full((1, QB), -2 ** 31, jnp.int32)
    zero = jnp.zeros((1, QB), jnp.int32)
    t = jnp.where(count_ge(zero) >= k_top, zero, int_min)

    def step(it, t):
        cand = t + (jnp.int32(1) << (30 - it))
        return jnp.where(count_ge(cand) >= k_top, cand, t)

    return _sortable_to_float(lax.fori_loop(0, 31, step, t))


def _first_ties(eq, need, row):
    eqf = jnp.where(eq, 1.0, 0.0)
    nbits = int(eq.shape[0]).bit_length()

    def body(it, j):
        cand = j + (jnp.int32(1) << (nbits - 1 - it))
        cnt = _col_reduce(jnp.where(row < cand, eqf, 0.0), jnp.sum)
        return jnp.where(cnt <= need, cand, j)

    j = lax.fori_loop(0, nbits, body, jnp.zeros((1, eq.shape[1]), jnp.int32))
    return eq & (row < j)


def _sparse_attn_kernel(q_ref, k_ref, v_ref, qi_ref, ki_ref, wi_ref, g_ref, bias_ref, o_ref,
                        score_ref, neg_ref, vt_ref, *, k_top):
    i = pl.program_id(1)
    seq = k_ref.shape[0]

    @pl.when(i == 0)
    def _():
        _transpose_values(v_ref, vt_ref)

    def body(width, ib, rows):
        head_w = width - KV_CHUNK
        ki = ki_ref[:width, :]
        w_t = wi_ref[rows, :].T * (IDX_DIM ** -0.5 * IDX_HEADS ** -0.5)
        score = None
        for h in range(0, IDX_HEADS, 2):
            qg = qi_ref[rows, (h // 4) * LANES:(h // 4 + 1) * LANES]
            qcat = jnp.concatenate([qg * _lane_mask(IDX_DIM, h % 4, BF16),
                                    qg * _lane_mask(IDX_DIM, h % 4 + 1, BF16)], axis=0)
            logit = _dot_nt(ki, qcat)
            term = (jnp.maximum(logit[:, :QB], 0.0) * w_t[h:h + 1]
                    + jnp.maximum(logit[:, QB:], 0.0) * w_t[h + 1:h + 2])
            score = term if score is None else score + term
        s_idx = head_w + lax.broadcasted_iota(jnp.int32, (KV_CHUNK, QB), 0)
        tail_ok = s_idx <= ib * QB + lax.broadcasted_iota(jnp.int32, (KV_CHUNK, QB), 1)
        if head_w:
            score_ref[:head_w, :] = score[:head_w]
        score_ref[head_w:width, :] = jnp.where(tail_ok, score[head_w:], NEG_INF)
        s2s = []
        for g in range(2):
            sl = slice(g * LANES, (g + 1) * LANES)
            qg = q_ref[rows, sl] * (HEAD_DIM ** -0.5 * LOG2E)
            qcat = jnp.concatenate([(qg * _lane_mask(HEAD_DIM, hh, F32)).astype(BF16) for hh in range(2)], axis=0)
            s2 = _dot_nt(k_ref[:width, sl], qcat)
            s2s.append([s2[:, hh * QB:(hh + 1) * QB] + _bias_col(bias_ref, 2 * g + hh, ib, width) for hh in range(2)])
        thr = _kth_largest(score_ref, width, k_top)
        keep_tail = (score_ref[head_w:width, :] >= thr) & tail_ok
        cnt = _col_reduce(jnp.where(keep_tail, 1.0, 0.0), jnp.sum)
        neg_ref[head_w:width, :] = jnp.where(keep_tail, 0.0, NEG_INF)
        if head_w:
            keep_head = score_ref[:head_w, :] >= thr
            cnt = cnt + _col_reduce(jnp.where(keep_head, 1.0, 0.0), jnp.sum)
            neg_ref[:head_w, :] = jnp.where(keep_head, 0.0, NEG_INF)

        @pl.when(jnp.max(cnt) > k_top)
        def _():
            x = score_ref[:width, :]
            gt = x > thr
            need = k_top - _col_reduce(jnp.where(gt, 1.0, 0.0), jnp.sum)
            row = lax.broadcasted_iota(jnp.int32, (width, QB), 0)
            neg_ref[:width, :] = jnp.where(gt | _first_ties(x == thr, need, row), 0.0, NEG_INF)

        outs = []
        for g in range(2):
            sl = slice(g * LANES, (g + 1) * LANES)
            es, ls = [], []
            for hh in range(2):
                s = s2s[g][hh] + neg_ref[:width, :]
                e = jnp.exp2(s - _col_reduce(s, jnp.max))
                ls.append(_col_reduce(e, jnp.sum))
                es.append(e.astype(BF16))
            ot = _dot(vt_ref[sl, :width], jnp.concatenate(es, axis=1))
            outs.append(ot[:HEAD_DIM, :QB] / ls[0])
            outs.append(ot[HEAD_DIM:, QB:] / ls[1])
        y_t = jnp.concatenate(outs, axis=0)
        y = jnp.concatenate([y_t[:LANES].T, y_t[LANES:].T], axis=1)
        o_ref[rows, :] = (y * _silu(g_ref[rows, :])).astype(BF16)

    _for_step_blocks(i, seq, body, unroll_upto=seq // 4)


def _sparse_attn(p32, p16, bias, k_top):
    bsz, seq, _ = p32.shape
    nq = seq // QB
    return pl.pallas_call(
        functools.partial(_sparse_attn_kernel, k_top=k_top),
        grid=(bsz, seq // KV_CHUNK),
        in_specs=[pl.BlockSpec((None, KV_CHUNK, 256), lambda b, i: (b, i, P_CQ2 // 256)),
                  pl.BlockSpec((None, seq, 256), lambda b, i: (b, 0, Q_CK // 256)),
                  pl.BlockSpec((None, seq, 256), lambda b, i: (b, 0, P_CV // 256)),
                  pl.BlockSpec((None, KV_CHUNK, 256), lambda b, i: (b, i, Q_CQI // 256)),
                  pl.BlockSpec((None, seq, LANES), lambda b, i: (b, 0, Q_CKI // LANES)),
                  pl.BlockSpec((None, KV_CHUNK, LANES), lambda b, i: (b, i, P_CWI // LANES)),
                  pl.BlockSpec((None, KV_CHUNK, 256), lambda b, i: (b, i, P_CG // 256)),
                  pl.BlockSpec((4, nq + 1, QB, QB), lambda b, i: (0, 0, 0, 0))],
        out_specs=pl.BlockSpec((None, KV_CHUNK, BRANCH_WIDTH), lambda b, i: (b, i, 0)),
        out_shape=jax.ShapeDtypeStruct((bsz, seq, BRANCH_WIDTH), BF16),
        scratch_shapes=[pltpu.VMEM((seq, QB), F32), pltpu.VMEM((seq, QB), F32),
                        pltpu.VMEM((BRANCH_WIDTH, seq), BF16)],
        compiler_params=_params(("arbitrary", "arbitrary")),
        name="sparse_attn",
    )(p32, p16, p32, p16, p16, p32, p32, bias)


def _dilated_kernel(x_ref, g_ref, bias_ref, o_ref, m_ref, l_ref, acc_ref):
    first = lax.broadcasted_iota(jnp.int32, (QB, LANES), 1) < HEAD_DIM
    hmask = [_lane_mask(HEAD_DIM, hh, F32) for hh in range(2)]

    def run(p, tiles, sink):
        work = []
        for cur, prev, variant in tiles:
            for g in range(2):
                q = x_ref[g, cur, :]
                kcat = x_ref[2 + g, cur, :]
                vcat = x_ref[4 + g, cur, :]
                if prev is not None:
                    kcat = jnp.concatenate([x_ref[2 + g, prev, :], kcat], axis=0)
                    vcat = jnp.concatenate([x_ref[4 + g, prev, :], vcat], axis=0)
                kcat = kcat.astype(BF16)
                scores = []
                for hh in range(2):
                    if prev is not None:
                        bias = bias_ref[2 * g + hh, 2 * p + variant]
                    else:
                        bias = bias_ref[2 * g + hh, 2 * p, :, QB:]
                    scores.append(_dot_nt((q * hmask[hh]).astype(BF16), kcat) + bias)
                work.append((vcat.astype(BF16), scores))
        for n, (vcat, scores) in enumerate(work):
            ms, ls, accs = [], [], []
            for s in scores:
                m = jnp.max(s, axis=-1, keepdims=True)
                e = jnp.exp2(s - m)
                ms.append(m)
                ls.append(jnp.sum(e, axis=-1, keepdims=True))
                accs.append(_dot(e.astype(BF16), vcat))
            sink(n // 2, n % 2, jnp.where(first, ms[0], ms[1]), jnp.where(first, ls[0], ls[1]),
                 jnp.where(first, accs[0], accs[1]))

    def store_stats(slot, d, starts):
        def sink(t, g, m, l, a):
            rows = pl.ds(starts[t], QB, stride=d)
            m_ref[slot, g, rows, :] = m
            l_ref[slot, g, rows, :] = l
            acc_ref[slot, g, rows, :] = a
        return sink

    def step16(n0, carry):
        starts = [n0 * DIL_UNROLL + u for u in range(DIL_UNROLL)]
        run(2, [(pl.ds(r, QB, stride=16), None, 0) for r in starts], store_stats(1, 16, starts))
        return carry

    lax.fori_loop(0, 16 // DIL_UNROLL, step16, 0)

    def step4(j, carry):
        starts = [r + 4 * QB * j for r in range(4)]
        tiles = [(pl.ds(s, QB, stride=4), pl.ds(jnp.maximum(s - 4 * QB, r), QB, stride=4), jnp.minimum(j, 1))
                 for r, s in enumerate(starts)]
        run(1, tiles, store_stats(0, 4, starts))
        return carry

    lax.fori_loop(0, 4, step4, 0)

    def step1(n0, carry):
        blocks = [n0 * DIL_UNROLL + u for u in range(DIL_UNROLL)]

        def sink(t, g, m, l, a):
            rows = pl.ds(pl.multiple_of(blocks[t] * QB, QB), QB)
            lanes = slice(g * LANES, (g + 1) * LANES)
            m4, m16 = m_ref[0, g, rows, :], m_ref[1, g, rows, :]
            m_tot = jnp.maximum(m, jnp.maximum(m4, m16))
            w1, w4, w16 = jnp.exp2(m - m_tot), jnp.exp2(m4 - m_tot), jnp.exp2(m16 - m_tot)
            num = w1 * a + w4 * acc_ref[0, g, rows, :] + w16 * acc_ref[1, g, rows, :]
            den = w1 * l + w4 * l_ref[0, g, rows, :] + w16 * l_ref[1, g, rows, :]
            o_ref[rows, lanes] = (num / den * _silu(g_ref[rows, lanes])).astype(BF16)

        tiles = []
        for j in blocks:
            cur = pl.ds(pl.multiple_of(j * QB, QB), QB)
            prev = pl.ds(pl.multiple_of(jnp.maximum(j - 1, 0) * QB, QB), QB)
            tiles.append((cur, prev, jnp.minimum(j, 1)))
        run(0, tiles, sink)
        return carry

    lax.fori_loop(0, g_ref.shape[0] // QB // DIL_UNROLL, step1, 0)


def _dilated_attn(pb, p32, bias):
    bsz, _, seq, _ = pb.shape
    stats = pltpu.VMEM((2, 2, seq, LANES), F32)
    return pl.pallas_call(
        _dilated_kernel,
        grid=(bsz,),
        in_specs=[pl.BlockSpec((None, B_SLABS, seq, LANES), lambda b: (b, 0, 0, 0)),
                  pl.BlockSpec((None, seq, 256), lambda b: (b, 0, P_BG // 256)),
                  pl.BlockSpec((4, 2 * len(DILATED_PATTERNS), QB, 2 * QB), lambda b: (0, 0, 0, 0))],
        out_specs=pl.BlockSpec((None, seq, BRANCH_WIDTH), lambda b: (b, 0, 0)),
        out_shape=jax.ShapeDtypeStruct((bsz, seq, BRANCH_WIDTH), BF16),
        scratch_shapes=[stats, stats, stats],
        compiler_params=_params(("arbitrary",)),
        name="dilated_attn",
    )(pb, p32, bias)


def _out_proj_kernel(ya_ref, yb_ref, yc_ref, yd_ref, w_ref, h_ref, g_ref, o_ref):
    y = _dot(jnp.concatenate([ya_ref[...], yb_ref[...], yc_ref[...], yd_ref[...]], axis=1), w_ref[...])
    ms = jnp.mean(y * y, axis=-1, keepdims=True)
    o_ref[...] = h_ref[...] + y * lax.rsqrt(ms + EPS) * g_ref[...]


def _out_proj(ya, yb, yc, yd, w, layer, h, g, tm=1024):
    bsz, seq, _ = h.shape
    yspec = pl.BlockSpec((None, tm, BRANCH_WIDTH), lambda b, i: (b, i, 0))
    return pl.pallas_call(
        _out_proj_kernel,
        grid=(bsz, seq // tm),
        in_specs=[yspec, yspec, yspec, yspec,
                  pl.BlockSpec((None, 4 * BRANCH_WIDTH, D_MODEL), lambda b, i: (layer, 0, 0)),
                  pl.BlockSpec((None, tm, D_MODEL), lambda b, i: (b, i, 0)),
                  pl.BlockSpec((1, D_MODEL), lambda b, i: (0, 0))],
        out_specs=pl.BlockSpec((None, tm, D_MODEL), lambda b, i: (b, i, 0)),
        out_shape=jax.ShapeDtypeStruct(h.shape, F32),
        compiler_params=_params(("arbitrary", "arbitrary")),
        name="out_proj",
    )(ya, yb, yc, yd, w, h, g)


def kernel(x, w_in, w_out, norm_pre, norm_post, mla_q_norm, mla_kv_norm, mla_w_uq, mla_w_ukv,
           diff_lambda, diff_subln, rel_bias):
    bsz, seq, _ = x.shape
    depth = w_in.shape[0]
    nq = seq // QB
    k_top = min(IDX_TOPK_MAX, seq // 4)
    cos_t, sin_t = _rope_tables(seq)
    bias_b = _bias_expand(rel_bias, _bucket_tiles_dilated(), 0, 4, LOG2E)
    causal_buckets_t = np.ascontiguousarray(np.swapaxes(_bucket_tiles_causal(nq), 1, 2))
    bias_c = _bias_expand(rel_bias, causal_buckets_t, 4, 4, LOG2E)
    bias_d = _bias_expand(rel_bias, causal_buckets_t, 8, 4, LOG2E)
    mask_tiles = jnp.asarray(_mask_tiles())
    w_arr_all = _arrange_w_in(w_in)
    w_out16 = w_out.astype(BF16)
    h = x
    for layer in range(depth):
        wq, wqrot, wk, wv = _arrange_mla(mla_w_uq[layer], mla_w_ukv[layer])
        gq = jnp.concatenate([mla_q_norm[layer], jnp.ones((256 - Q_LORA,), F32)])[None, :]
        gkv = mla_kv_norm[layer][None, :]
        p32, p16, pb, qa, ka, va = _in_proj(h, norm_pre[layer][None, :], w_arr_all, layer,
                                            cos_t, sin_t, gq, gkv, wq, wqrot, wk, wv)
        y_a = _mla_attn(qa, ka, va, p32, mask_tiles)
        y_b = _dilated_attn(pb, p32, bias_b)
        y_c = _sparse_attn(p32, p16, bias_c, k_top)
        lambda_init = 0.8 - 0.6 * math.exp(-0.3 * layer)
        subln = jnp.tile(diff_subln[layer], BRANCH_WIDTH // HEAD_DIM)[None, :]
        y_d = _diff_attn(p32, p16, bias_d, diff_lambda[layer], subln, lambda_init)
        h = _out_proj(y_a, y_b, y_c, y_d, w_out16, layer, h, norm_post[layer][None, :])
    return h
```

```python
import functools
import math

import jax
import jax.numpy as jnp
import numpy as np
from jax import lax
from jax.experimental import pallas as pl
from jax.experimental.pallas import tpu as pltpu

F32 = jnp.float32
BF16 = jnp.bfloat16

D_MODEL = 1024
A_HEADS, A_NOPE, A_ROPE, A_V = 4, 64, 32, 64
Q_LORA, KV_LORA = 192, 128
ROPE_THETA = 10000.0
HEAD_DIM = 64
DILATED_PATTERNS = ((128, 1), (512, 4), (2048, 16))
IDX_HEADS, IDX_DIM, IDX_TOPK_MAX = 8, 32, 256
D_QK = 32
BRANCH_WIDTH = 256
NUM_BUCKETS, MAX_DISTANCE = 32, 2048
NEG_INF = -1e30
EPS = 1e-6
LOG2E = math.log2(math.e)
KV_CHUNK = 256
COL_ACC_ROWS = 64
DIL_UNROLL = 4
LANES = 128
QB = 128
VMEM_LIMIT = 56 * 1024 * 1024

_SPLIT = (Q_LORA, KV_LORA, A_ROPE, 256, 256, 256, 256, 256, 256, 256, 256, IDX_HEADS * IDX_DIM, IDX_DIM,
          IDX_HEADS, 256, 256, 256, 256, 256)
_OFF = np.concatenate([[0], np.cumsum(_SPLIT)]).tolist()
(_A_CQ, _A_CKV, _A_KR, _A_G, _B_Q, _B_K, _B_V, _B_G, _C_Q, _C_K, _C_V, _C_QI, _C_KI, _C_WI, _C_G,
 _D_Q, _D_K, _D_V, _D_G) = range(19)

L_CQ, L_CKV, L_KR, L_KRROT = 0, 256, 384, 512
P_START = 512
P_CWI, P_AG, P_BG, P_CQ2, P_CV, P_CG, P_DQ, P_DV, P_DG = 128, 256, 512, 768, 1024, 1280, 1536, 1792, 2048
NCOL32 = 2304
Q_START = P_START + NCOL32
Q_CK, Q_CQI, Q_DK, Q_CKI = 0, 256, 512, 768
NCOL16 = 896
B_START = Q_START + NCOL16
B_SLABS = 6
NCOL = B_START + B_SLABS * LANES


def _dot(a, b):
    return jnp.dot(a, b, preferred_element_type=F32)


def _dot_nt(a, b):
    return lax.dot_general(a, b, (((1,), (1,)), ((), ())), preferred_element_type=F32)


def _params(sem):
    return pltpu.CompilerParams(dimension_semantics=sem, vmem_limit_bytes=VMEM_LIMIT)


def _rot_cols(w):
    half = w.shape[-1] // 2
    return jnp.concatenate([-w[..., half:], w[..., :half]], axis=-1)


def _arrange_w_in(w):
    depth, rows, in_cols = w.shape
    tm = 256
    return pl.pallas_call(
        _arrange_w_in_kernel,
        grid=(depth, rows // tm),
        in_specs=[pl.BlockSpec((None, tm, in_cols), lambda l, i: (l, i, 0))],
        out_specs=pl.BlockSpec((None, tm, NCOL), lambda l, i: (l, i, 0)),
        out_shape=jax.ShapeDtypeStruct((depth, rows, NCOL), BF16),
        compiler_params=_params(("arbitrary", "arbitrary")),
        name="arrange_w_in",
    )(w)


def _w_in_pieces():
    pieces = []

    def put(dst, seg, lo=0, hi=None, scale=1.0):
        hi = _SPLIT[seg] if hi is None else hi
        pieces.append((dst, _OFF[seg] + lo, hi - lo, scale))
        return dst + hi - lo

    put(L_CQ, _A_CQ)
    put(L_CKV, _A_CKV)
    put(L_KR + A_NOPE, _A_KR)
    half = A_ROPE // 2
    put(L_KRROT + A_NOPE, _A_KR, half, A_ROPE, scale=-1.0)
    put(L_KRROT + A_NOPE + half, _A_KR, 0, half)
    for dst, seg in ((P_CWI, _C_WI), (P_AG, _A_G), (P_BG, _B_G), (P_CQ2, _C_Q), (P_CV, _C_V), (P_CG, _C_G),
                     (P_DQ, _D_Q), (P_DV, _D_V), (P_DG, _D_G)):
        put(P_START + dst, seg)
    for dst, seg in ((Q_CK, _C_K), (Q_CQI, _C_QI), (Q_DK, _D_K)):
        put(Q_START + dst, seg)
    for copy in range(LANES // IDX_DIM):
        put(Q_START + Q_CKI + copy * IDX_DIM, _C_KI)
    base = put(B_START, _B_Q, scale=HEAD_DIM ** -0.5 * LOG2E)
    base = put(base, _B_K)
    base = put(base, _B_V)
    assert base == NCOL
    return pieces


def _arrange_w_in_kernel(w_ref, o_ref):
    o_ref[...] = jnp.zeros(o_ref.shape, o_ref.dtype)
    for dst, src, n, scale in _w_in_pieces():
        lo = src // LANES * LANES
        hi = min(-(-(src + n) // LANES) * LANES, w_ref.shape[1])
        v = w_ref[:, lo:hi][:, src - lo:src - lo + n]
        o_ref[:, dst:dst + n] = (v * scale if scale != 1.0 else v).astype(o_ref.dtype)


def _arrange_mla(w_uq, w_ukv):
    wq = w_uq.reshape(Q_LORA, A_HEADS, A_NOPE + A_ROPE)
    nope, rope = wq[..., :A_NOPE], wq[..., A_NOPE:]
    zq = jnp.zeros((Q_LORA, A_HEADS, LANES - A_NOPE - A_ROPE), w_uq.dtype)
    wq_main = jnp.concatenate([nope, rope, zq], axis=-1).reshape(Q_LORA, A_HEADS * LANES)
    wq_rot = jnp.concatenate([jnp.zeros_like(nope), _rot_cols(rope), zq], axis=-1).reshape(Q_LORA, A_HEADS * LANES)
    pad = jnp.zeros((256 - Q_LORA, A_HEADS * LANES), w_uq.dtype)
    wq_main = jnp.concatenate([wq_main, pad], axis=0)
    wq_rot = jnp.concatenate([wq_rot, pad], axis=0)
    wkv = w_ukv.reshape(KV_LORA, A_HEADS, A_NOPE + A_V)
    knope, v = wkv[..., :A_NOPE], wkv[..., A_NOPE:]
    wk = jnp.concatenate([knope, jnp.zeros_like(knope)], axis=-1).reshape(KV_LORA, A_HEADS * LANES)
    wv_t = v.reshape(KV_LORA, A_HEADS * A_V).T
    return wq_main.astype(BF16), wq_rot.astype(BF16), wk.astype(BF16), wv_t.astype(BF16)


def _rope_tables(seq):
    inv = ROPE_THETA ** (-jnp.arange(0, A_ROPE, 2, dtype=F32) / A_ROPE)
    ang = jnp.arange(seq, dtype=F32)[:, None] * inv[None, :]
    cos, sin = jnp.cos(ang), jnp.sin(ang)
    one = jnp.ones((seq, A_NOPE), F32)
    zero = jnp.zeros((seq, LANES - A_NOPE - A_ROPE), F32)
    cos_t = jnp.concatenate([one, cos, cos, zero], axis=1)
    sin_t = jnp.concatenate([jnp.zeros_like(one), sin, sin, zero], axis=1)
    return cos_t, sin_t


def _t5_bucket_np(rel):
    n = np.maximum(rel, 0)
    max_exact = NUM_BUCKETS // 2
    nf = np.maximum(n, max_exact).astype(np.float64)
    large = max_exact + (np.log(nf / max_exact) / math.log(MAX_DISTANCE / max_exact)
                         * (NUM_BUCKETS - max_exact)).astype(np.int32)
    large = np.minimum(large, NUM_BUCKETS - 1)
    return np.where(n < max_exact, n, large).astype(np.int32)


MASKED_BUCKET = NUM_BUCKETS


def _bucket_tiles_causal(nq):
    q = np.arange(QB)[:, None]
    k = np.arange(QB)[None, :]
    tiles = [np.full((QB, QB), MASKED_BUCKET, np.int32)]
    for d in range(nq):
        rel = QB * d + q - k
        tiles.append(np.where(rel >= 0, _t5_bucket_np(rel), MASKED_BUCKET).astype(np.int32))
    return np.stack(tiles)


def _mask_tiles():
    k = np.arange(QB)[:, None]
    q = np.arange(QB)[None, :]
    diag = np.where(k <= q, 0.0, NEG_INF)
    return np.stack([np.full((QB, QB), NEG_INF), diag, np.zeros((QB, QB))]).astype(np.float32)


def _bucket_tiles_dilated():
    q = np.arange(QB)[:, None]
    k = np.arange(2 * QB)[None, :]
    rel = q + QB - k
    tiles = []
    for (window, d) in DILATED_PATTERNS:
        in_band = (rel >= 0) & (rel <= window // d)
        for has_prev in (False, True):
            ok = in_band & (has_prev | (k >= QB))
            tiles.append(np.where(ok, _t5_bucket_np(rel * d), MASKED_BUCKET).astype(np.int32))
    return np.stack(tiles)


def _bias_expand_kernel(table_ref, bucket_ref, out_ref, *, head0, scale, present):
    h = pl.program_id(0) + head0
    for n, buckets_in_tile in enumerate(present):
        bk = bucket_ref[n]
        acc = jnp.where(bk == MASKED_BUCKET, NEG_INF, 0.0)
        for b in buckets_in_tile:
            acc = jnp.where(bk == b, table_ref[b, h] * scale, acc)
        out_ref[n] = acc


def _bias_expand(table, buckets, head0, nheads, scale=1.0):
    n, r, c = buckets.shape
    present = tuple(tuple(int(b) for b in np.unique(tile) if b != MASKED_BUCKET) for tile in buckets)
    buckets = jnp.asarray(buckets)
    return pl.pallas_call(
        functools.partial(_bias_expand_kernel, head0=head0, scale=scale, present=present),
        grid=(nheads,),
        in_specs=[pl.BlockSpec(memory_space=pltpu.SMEM),
                  pl.BlockSpec((n, r, c), lambda h: (0, 0, 0))],
        out_specs=pl.BlockSpec((None, n, r, c), lambda h: (h, 0, 0, 0)),
        out_shape=jax.ShapeDtypeStruct((nheads, n, r, c), F32),
        compiler_params=_params(("arbitrary",)),
        name="bias_expand",
    )(table, buckets)


def _in_proj_kernel(x_ref, g_ref, w_ref, cos_ref, sin_ref, gq_ref, gkv_ref, wq_ref, wqrot_ref, wk_ref, wv_ref,
                    o32_ref, o16_ref, ob_ref, q_ref, k_ref, v_ref):
    x = x_ref[...]
    ms = jnp.mean(x * x, axis=-1, keepdims=True)
    xn = x * lax.rsqrt(ms + EPS) * g_ref[...]
    p = _dot(xn.astype(BF16), w_ref[...])
    o32_ref[...] = p[:, P_START:Q_START]
    o16_ref[...] = p[:, Q_START:B_START].astype(BF16)
    for s in range(B_SLABS):
        ob_ref[s] = p[:, B_START + s * LANES:B_START + (s + 1) * LANES]
    cos = cos_ref[...]
    sin = sin_ref[...]
    cos4 = jnp.concatenate([cos] * A_HEADS, axis=1)
    sin4 = jnp.concatenate([sin] * A_HEADS, axis=1)
    cq = p[:, L_CQ:L_CQ + 256]
    ms = jnp.sum(cq * cq, axis=-1, keepdims=True) * (1.0 / Q_LORA)
    nq = (cq * lax.rsqrt(ms + EPS) * gq_ref[...]).astype(BF16)
    q = _dot(nq, wq_ref[...]) * cos4 + _dot(nq, wqrot_ref[...]) * sin4
    q_ref[...] = (q * ((A_NOPE + A_ROPE) ** -0.5 * LOG2E)).astype(BF16)
    ckv = p[:, L_CKV:L_CKV + KV_LORA]
    ms = jnp.mean(ckv * ckv, axis=-1, keepdims=True)
    nkv = (ckv * lax.rsqrt(ms + EPS) * gkv_ref[...]).astype(BF16)
    kr = p[:, L_KR:L_KR + LANES] * cos + p[:, L_KRROT:L_KRROT + LANES] * sin
    k = _dot(nkv, wk_ref[...]) + jnp.concatenate([kr] * A_HEADS, axis=1)
    k_ref[...] = k.astype(BF16)
    v_ref[...] = _dot_nt(wv_ref[...], nkv).astype(BF16)


def _in_proj(h, g, w, layer, cos_t, sin_t, gq, gkv, wq, wqrot, wk, wv_t, tm=512):
    bsz, seq, _ = h.shape
    w4 = A_HEADS * LANES
    row = lambda b, i: (b, i, 0)
    const = lambda b, i: (0, 0)
    return pl.pallas_call(
        _in_proj_kernel,
        grid=(bsz, seq // tm),
        in_specs=[pl.BlockSpec((None, tm, D_MODEL), row),
                  pl.BlockSpec((1, D_MODEL), const),
                  pl.BlockSpec((None, D_MODEL, NCOL), lambda b, i: (layer, 0, 0)),
                  pl.BlockSpec((tm, LANES), lambda b, i: (i, 0)),
                  pl.BlockSpec((tm, LANES), lambda b, i: (i, 0)),
                  pl.BlockSpec((1, 256), const),
                  pl.BlockSpec((1, KV_LORA), const),
                  pl.BlockSpec((256, w4), const),
                  pl.BlockSpec((256, w4), const),
                  pl.BlockSpec((KV_LORA, w4), const),
                  pl.BlockSpec((A_HEADS * A_V, KV_LORA), const)],
        out_specs=[pl.BlockSpec((None, tm, NCOL32), row),
                   pl.BlockSpec((None, tm, NCOL16), row),
                   pl.BlockSpec((None, B_SLABS, tm, LANES), lambda b, i: (b, 0, i, 0)),
                   pl.BlockSpec((None, tm, w4), row), pl.BlockSpec((None, tm, w4), row),
                   pl.BlockSpec((None, A_HEADS * A_V, tm), lambda b, i: (b, 0, i))],
        out_shape=[jax.ShapeDtypeStruct((bsz, seq, NCOL32), F32),
                   jax.ShapeDtypeStruct((bsz, seq, NCOL16), BF16),
                   jax.ShapeDtypeStruct((bsz, B_SLABS, seq, LANES), F32),
                   jax.ShapeDtypeStruct((bsz, seq, w4), BF16), jax.ShapeDtypeStruct((bsz, seq, w4), BF16),
                   jax.ShapeDtypeStruct((bsz, A_HEADS * A_V, seq), BF16)],
        compiler_params=_params(("arbitrary", "arbitrary")),
        name="in_proj",
    )(h, g, w, cos_t, sin_t, gq, gkv, wq, wqrot, wk, wv_t)


def _silu(g):
    return g * (1.0 / (1.0 + jnp.exp(-g)))


def _for_step_blocks(i, seq, block_body, unroll_upto=0):
    per = KV_CHUNK // QB
    for wb in range(seq // KV_CHUNK):
        @pl.when(i == wb)
        def _(width=(wb + 1) * KV_CHUNK):
            if width <= unroll_upto:
                for u in range(per):
                    block_body(width, per * i + u, slice(u * QB, (u + 1) * QB))
                return

            def one(u, carry):
                block_body(width, per * i + u, pl.ds(pl.multiple_of(u * QB, QB), QB))
                return carry

            lax.fori_loop(0, per, one, 0)


def _mask_tail_t(mask_ref, i, width):
    first = (width - KV_CHUNK) // QB
    return jnp.concatenate([mask_ref[jnp.clip(i - j, -1, 1) + 1] for j in range(first, width // QB)], axis=0)


def _col_reduce(x, op):
    rows, lanes = x.shape
    part = op(x.reshape(rows // COL_ACC_ROWS, COL_ACC_ROWS, lanes), axis=0)
    return op(part, axis=0, keepdims=True)


def _bias_col(bias_ref, h, i, width):
    return jnp.concatenate([bias_ref[h, jnp.maximum(i - j, -1) + 1] for j in range(width // QB)], axis=0)


def _lane_mask(width, seg, dtype):
    lane = lax.broadcasted_iota(jnp.int32, (1, LANES), 1)
    return jnp.where((lane >= seg * width) & (lane < (seg + 1) * width), 1.0, 0.0).astype(dtype)


def _mla_attn_kernel(q_ref, k_ref, vt_ref, g_ref, mask_ref, o_ref):
    i = pl.program_id(1)
    seq = k_ref.shape[0]

    def body(width, ib, rows):
        head_w = width - KV_CHUNK
        zero = jnp.zeros((QB, LANES), BF16)
        scores = []
        for g in range(A_HEADS // 2):
            q0 = q_ref[rows, 2 * g * LANES:(2 * g + 1) * LANES]
            q1 = q_ref[rows, (2 * g + 1) * LANES:(2 * g + 2) * LANES]
            qbd = jnp.concatenate([jnp.concatenate([q0, zero], axis=1),
                                   jnp.concatenate([zero, q1], axis=1)], axis=0)
            scores.append(_dot_nt(k_ref[:width, 2 * g * LANES:(2 * g + 2) * LANES], qbd))
        tail = _mask_tail_t(mask_ref, ib, width)
        outs = []
        for g in range(A_HEADS // 2):
            es, ls = [], []
            for hh in range(2):
                s = scores[g][:, hh * QB:(hh + 1) * QB]
                s = jnp.concatenate([s[:head_w], s[head_w:] + tail], axis=0) if head_w else s + tail
                e = jnp.exp2(s - _col_reduce(s, jnp.max))
                ls.append(_col_reduce(e, jnp.sum))
                es.append(e.astype(BF16))
            ot = _dot(vt_ref[g * LANES:(g + 1) * LANES, :width], jnp.concatenate(es, axis=1))
            outs.append(ot[:A_V, :QB] / ls[0])
            outs.append(ot[A_V:, QB:] / ls[1])
        y_t = jnp.concatenate(outs, axis=0)
        y = jnp.concatenate([y_t[:LANES].T, y_t[LANES:].T], axis=1)
        o_ref[rows, :] = (y * _silu(g_ref[rows, :])).astype(BF16)

    _for_step_blocks(i, seq, body, unroll_upto=seq)


def _mla_attn(q, k, v_t, p32, mask_tiles):
    bsz, seq, w4 = q.shape
    return pl.pallas_call(
        _mla_attn_kernel,
        grid=(bsz, seq // KV_CHUNK),
        in_specs=[pl.BlockSpec((None, KV_CHUNK, w4), lambda b, i: (b, i, 0)),
                  pl.BlockSpec((None, seq, w4), lambda b, i: (b, 0, 0)),
                  pl.BlockSpec((None, A_HEADS * A_V, seq), lambda b, i: (b, 0, 0)),
                  pl.BlockSpec((None, KV_CHUNK, 256), lambda b, i: (b, i, P_AG // 256)),
                  pl.BlockSpec((3, QB, QB), lambda b, i: (0, 0, 0))],
        out_specs=pl.BlockSpec((None, KV_CHUNK, BRANCH_WIDTH), lambda b, i: (b, i, 0)),
        out_shape=jax.ShapeDtypeStruct((bsz, seq, BRANCH_WIDTH), BF16),
        compiler_params=_params(("arbitrary", "arbitrary")),
        name="mla_attn",
    )(q, k, v_t, p32, mask_tiles)


def _transpose_values(v_ref, vt_ref):
    for j in range(v_ref.shape[0] // QB):
        for g in range(v_ref.shape[1] // LANES):
            tile = v_ref[j * QB:(j + 1) * QB, g * LANES:(g + 1) * LANES]
            vt_ref[g * LANES:(g + 1) * LANES, j * QB:(j + 1) * QB] = tile.T.astype(BF16)


def _diff_attn_kernel(q_ref, k_ref, v_ref, g_ref, bias_ref, lam_ref, subln_ref, o_ref, vt_ref, *, lambda_init):
    i = pl.program_id(1)
    seq = k_ref.shape[0]

    @pl.when(i == 0)
    def _():
        _transpose_values(v_ref, vt_ref)

    def body(width, ib, rows):
        lp = lam_ref[...]
        lam = (jnp.exp(jnp.sum(lp[0:1] * lp[1:2], axis=-1, keepdims=True))
               - jnp.exp(jnp.sum(lp[2:3] * lp[3:4], axis=-1, keepdims=True)) + lambda_init)
        scores = []
        for h in range(4):
            sl = slice((h // 2) * LANES, (h // 2 + 1) * LANES)
            qg = q_ref[rows, sl] * (D_QK ** -0.5 * LOG2E)
            qcat = jnp.concatenate([(qg * _lane_mask(D_QK, 2 * (h % 2) + mm, F32)).astype(BF16)
                                    for mm in range(2)], axis=0)
            scores.append(_dot_nt(k_ref[:width, sl], qcat))
        outs = []
        for h in range(4):
            bias = _bias_col(bias_ref, h, ib, width)
            es, ls = [], []
            for mm in range(2):
                s = scores[h][:, mm * QB:(mm + 1) * QB] + bias
                e = jnp.exp2(s - _col_reduce(s, jnp.max))
                ls.append(_col_reduce(e, jnp.sum))
                es.append(e.astype(BF16))
            ot = _dot(vt_ref[h * HEAD_DIM:(h + 1) * HEAD_DIM, :width], jnp.concatenate(es, axis=1))
            a = ot[:, :QB] / ls[0] - lam * (ot[:, QB:] / ls[1])
            ms = jnp.sum(a * a, axis=0, keepdims=True) * (1.0 / HEAD_DIM)
            outs.append(a * lax.rsqrt(ms + EPS))
        y_t = jnp.concatenate(outs, axis=0)
        y = jnp.concatenate([y_t[:LANES].T, y_t[LANES:].T], axis=1)
        o_ref[rows, :] = (y * (subln_ref[...] * (1.0 - lambda_init)) * _silu(g_ref[rows, :])).astype(BF16)

    _for_step_blocks(i, seq, body, unroll_upto=seq // 2)


def _diff_attn(p32, p16, bias, lam_params, subln, lambda_init):
    bsz, seq, _ = p32.shape
    nq = seq // QB
    return pl.pallas_call(
        functools.partial(_diff_attn_kernel, lambda_init=lambda_init),
        grid=(bsz, seq // KV_CHUNK),
        in_specs=[pl.BlockSpec((None, KV_CHUNK, 256), lambda b, i: (b, i, P_DQ // 256)),
                  pl.BlockSpec((None, seq, 256), lambda b, i: (b, 0, Q_DK // 256)),
                  pl.BlockSpec((None, seq, 256), lambda b, i: (b, 0, P_DV // 256)),
                  pl.BlockSpec((None, KV_CHUNK, 256), lambda b, i: (b, i, P_DG // 256)),
                  pl.BlockSpec((4, nq + 1, QB, QB), lambda b, i: (0, 0, 0, 0)),
                  pl.BlockSpec((4, D_QK), lambda b, i: (0, 0)),
                  pl.BlockSpec((1, BRANCH_WIDTH), lambda b, i: (0, 0))],
        out_specs=pl.BlockSpec((None, KV_CHUNK, BRANCH_WIDTH), lambda b, i: (b, i, 0)),
        out_shape=jax.ShapeDtypeStruct((bsz, seq, BRANCH_WIDTH), BF16),
        scratch_shapes=[pltpu.VMEM((BRANCH_WIDTH, seq), BF16)],
        compiler_params=_params(("arbitrary", "arbitrary")),
        name="diff_attn",
    )(p32, p16, p32, p32, bias, lam_params, subln)


def _sortable_to_float(key):
    return pltpu.bitcast(jnp.where(key < 0, key ^ jnp.int32(0x7FFFFFFF), key), F32)


def _kth_largest(score_ref, width, k_top):
    def count_ge(key):
        thr = _sortable_to_float(key)
        return _col_reduce(jnp.where(score_ref[:width, :] >= thr, 1.0, 0.0), jnp.sum)

    int_min = jnp.full((1, QB), -2 ** 31, jnp.int32)
    zero = jnp.zeros((1, QB), jnp.int32)
    t = jnp.where(count_ge(zero) >= k_top, zero, int_min)

    def step(it, t):
        cand = t + (jnp.int32(1) << (30 - it))
        return jnp.where(count_ge(cand) >= k_top, cand, t)

    return _sortable_to_float(lax.fori_loop(0, 31, step, t))


def _first_ties(eq, need, row):
    eqf = jnp.where(eq, 1.0, 0.0)
    nbits = int(eq.shape[0]).bit_length()

    def body(it, j):
        cand = j + (jnp.int32(1) << (nbits - 1 - it))
        cnt = _col_reduce(jnp.where(row < cand, eqf, 0.0), jnp.sum)
        return jnp.where(cnt <= need, cand, j)

    j = lax.fori_loop(0, nbits, body, jnp.zeros((1, eq.shape[1]), jnp.int32))
    return eq & (row < j)


def _sparse_attn_kernel(q_ref, k_ref, v_ref, qi_ref, ki_ref, wi_ref, g_ref, bias_ref, o_ref,
                        score_ref, neg_ref, vt_ref, *, k_top):
    i = pl.program_id(1)
    seq = k_ref.shape[0]

    @pl.when(i == 0)
    def _():
        _transpose_values(v_ref, vt_ref)

    def body(width, ib, rows):
        head_w = width - KV_CHUNK
        ki = ki_ref[:width, :]
        w_t = wi_ref[rows, :].T * (IDX_DIM ** -0.5 * IDX_HEADS ** -0.5)
        score = None
        for h in range(0, IDX_HEADS, 2):
            qg = qi_ref[rows, (h // 4) * LANES:(h // 4 + 1) * LANES]
            qcat = jnp.concatenate([qg * _lane_mask(IDX_DIM, h % 4, BF16),
                                    qg * _lane_mask(IDX_DIM, h % 4 + 1, BF16)], axis=0)
            logit = _dot_nt(ki, qcat)
            term = (jnp.maximum(logit[:, :QB], 0.0) * w_t[h:h + 1]
                    + jnp.maximum(logit[:, QB:], 0.0) * w_t[h + 1:h + 2])
            score = term if score is None else score + term
        s_idx = head_w + lax.broadcasted_iota(jnp.int32, (KV_CHUNK, QB), 0)
        tail_ok = s_idx <= ib * QB + lax.broadcasted_iota(jnp.int32, (KV_CHUNK, QB), 1)
        if head_w:
            score_ref[:head_w, :] = score[:head_w]
        score_ref[head_w:width, :] = jnp.where(tail_ok, score[head_w:], NEG_INF)
        s2s = []
        for g in range(2):
            sl = slice(g * LANES, (g + 1) * LANES)
            qg = q_ref[rows, sl] * (HEAD_DIM ** -0.5 * LOG2E)
            qcat = jnp.concatenate([(qg * _lane_mask(HEAD_DIM, hh, F32)).astype(BF16) for hh in range(2)], axis=0)
            s2 = _dot_nt(k_ref[:width, sl], qcat)
            s2s.append([s2[:, hh * QB:(hh + 1) * QB] + _bias_col(bias_ref, 2 * g + hh, ib, width) for hh in range(2)])
        thr = _kth_largest(score_ref, width, k_top)
        keep_tail = (score_ref[head_w:width, :] >= thr) & tail_ok
        cnt = _col_reduce(jnp.where(keep_tail, 1.0, 0.0), jnp.sum)
        neg_ref[head_w:width, :] = jnp.where(keep_tail, 0.0, NEG_INF)
        if head_w:
            keep_head = score_ref[:head_w, :] >= thr
            cnt = cnt + _col_reduce(jnp.where(keep_head, 1.0, 0.0), jnp.sum)
            neg_ref[:head_w, :] = jnp.where(keep_head, 0.0, NEG_INF)

        @pl.when(jnp.max(cnt) > k_top)
        def _():
            x = score_ref[:width, :]
            gt = x > thr
            need = k_top - _col_reduce(jnp.where(gt, 1.0, 0.0), jnp.sum)
            row = lax.broadcasted_iota(jnp.int32, (width, QB), 0)
            neg_ref[:width, :] = jnp.where(gt | _first_ties(x == thr, need, row), 0.0, NEG_INF)

        outs = []
        for g in range(2):
            sl = slice(g * LANES, (g + 1) * LANES)
            es, ls = [], []
            for hh in range(2):
                s = s2s[g][hh] + neg_ref[:width, :]
                e = jnp.exp2(s - _col_reduce(s, jnp.max))
                ls.append(_col_reduce(e, jnp.sum))
                es.append(e.astype(BF16))
            ot = _dot(vt_ref[sl, :width], jnp.concatenate(es, axis=1))
            outs.append(ot[:HEAD_DIM, :QB] / ls[0])
            outs.append(ot[HEAD_DIM:, QB:] / ls[1])
        y_t = jnp.concatenate(outs, axis=0)
        y = jnp.concatenate([y_t[:LANES].T, y_t[LANES:].T], axis=1)
        o_ref[rows, :] = (y * _silu(g_ref[rows, :])).astype(BF16)

    _for_step_blocks(i, seq, body, unroll_upto=seq // 4)


def _sparse_attn(p32, p16, bias, k_top):
    bsz, seq, _ = p32.shape
    nq = seq // QB
    return pl.pallas_call(
        functools.partial(_sparse_attn_kernel, k_top=k_top),
        grid=(bsz, seq // KV_CHUNK),
        in_specs=[pl.BlockSpec((None, KV_CHUNK, 256), lambda b, i: (b, i, P_CQ2 // 256)),
                  pl.BlockSpec((None, seq, 256), lambda b, i: (b, 0, Q_CK // 256)),
                  pl.BlockSpec((None, seq, 256), lambda b, i: (b, 0, P_CV // 256)),
                  pl.BlockSpec((None, KV_CHUNK, 256), lambda b, i: (b, i, Q_CQI // 256)),
                  pl.BlockSpec((None, seq, LANES), lambda b, i: (b, 0, Q_CKI // LANES)),
                  pl.BlockSpec((None, KV_CHUNK, LANES), lambda b, i: (b, i, P_CWI // LANES)),
                  pl.BlockSpec((None, KV_CHUNK, 256), lambda b, i: (b, i, P_CG // 256)),
                  pl.BlockSpec((4, nq + 1, QB, QB), lambda b, i: (0, 0, 0, 0))],
        out_specs=pl.BlockSpec((None, KV_CHUNK, BRANCH_WIDTH), lambda b, i: (b, i, 0)),
        out_shape=jax.ShapeDtypeStruct((bsz, seq, BRANCH_WIDTH), BF16),
        scratch_shapes=[pltpu.VMEM((seq, QB), F32), pltpu.VMEM((seq, QB), F32),
                        pltpu.VMEM((BRANCH_WIDTH, seq), BF16)],
        compiler_params=_params(("arbitrary", "arbitrary")),
        name="sparse_attn",
    )(p32, p16, p32, p16, p16, p32, p32, bias)


def _dilated_kernel(x_ref, g_ref, bias_ref, o_ref, m_ref, l_ref, acc_ref):
    first = lax.broadcasted_iota(jnp.int32, (QB, LANES), 1) < HEAD_DIM
    hmask = [_lane_mask(HEAD_DIM, hh, F32) for hh in range(2)]

    def run(p, tiles, sink):
        work = []
        for cur, prev, variant in tiles:
            for g in range(2):
                q = x_ref[g, cur, :]
                kcat = x_ref[2 + g, cur, :]
                vcat = x_ref[4 + g, cur, :]
                if prev is not None:
                    kcat = jnp.concatenate([x_ref[2 + g, prev, :], kcat], axis=0)
                    vcat = jnp.concatenate([x_ref[4 + g, prev, :], vcat], axis=0)
                kcat = kcat.astype(BF16)
                scores = []
                for hh in range(2):
                    if prev is not None:
                        bias = bias_ref[2 * g + hh, 2 * p + variant]
                    else:
                        bias = bias_ref[2 * g + hh, 2 * p, :, QB:]
                    scores.append(_dot_nt((q * hmask[hh]).astype(BF16), kcat) + bias)
                work.append((vcat.astype(BF16), scores))
        for n, (vcat, scores) in enumerate(work):
            ms, ls, accs = [], [], []
            for s in scores:
                m = jnp.max(s, axis=-1, keepdims=True)
                e = jnp.exp2(s - m)
                ms.append(m)
                ls.append(jnp.sum(e, axis=-1, keepdims=True))
                accs.append(_dot(e.astype(BF16), vcat))
            sink(n // 2, n % 2, jnp.where(first, ms[0], ms[1]), jnp.where(first, ls[0], ls[1]),
                 jnp.where(first, accs[0], accs[1]))

    def store_stats(slot, d, starts):
        def sink(t, g, m, l, a):
            rows = pl.ds(starts[t], QB, stride=d)
            m_ref[slot, g, rows, :] = m
            l_ref[slot, g, rows, :] = l
            acc_ref[slot, g, rows, :] = a
        return sink

    def step16(n0, carry):
        starts = [n0 * DIL_UNROLL + u for u in range(DIL_UNROLL)]
        run(2, [(pl.ds(r, QB, stride=16), None, 0) for r in starts], store_stats(1, 16, starts))
        return carry

    lax.fori_loop(0, 16 // DIL_UNROLL, step16, 0)

    def step4(j, carry):
        starts = [r + 4 * QB * j for r in range(4)]
        tiles = [(pl.ds(s, QB, stride=4), pl.ds(jnp.maximum(s - 4 * QB, r), QB, stride=4), jnp.minimum(j, 1))
                 for r, s in enumerate(starts)]
        run(1, tiles, store_stats(0, 4, starts))
        return carry

    lax.fori_loop(0, 4, step4, 0)

    def step1(n0, carry):
        blocks = [n0 * DIL_UNROLL + u for u in range(DIL_UNROLL)]

        def sink(t, g, m, l, a):
            rows = pl.ds(pl.multiple_of(blocks[t] * QB, QB), QB)
            lanes = slice(g * LANES, (g + 1) * LANES)
            m4, m16 = m_ref[0, g, rows, :], m_ref[1, g, rows, :]
            m_tot = jnp.maximum(m, jnp.maximum(m4, m16))
            w1, w4, w16 = jnp.exp2(m - m_tot), jnp.exp2(m4 - m_tot), jnp.exp2(m16 - m_tot)
            num = w1 * a + w4 * acc_ref[0, g, rows, :] + w16 * acc_ref[1, g, rows, :]
            den = w1 * l + w4 * l_ref[0, g, rows, :] + w16 * l_ref[1, g, rows, :]
            o_ref[rows, lanes] = (num / den * _silu(g_ref[rows, lanes])).astype(BF16)

        tiles = []
        for j in blocks:
            cur = pl.ds(pl.multiple_of(j * QB, QB), QB)
            prev = pl.ds(pl.multiple_of(jnp.maximum(j - 1, 0) * QB, QB), QB)
            tiles.append((cur, prev, jnp.minimum(j, 1)))
        run(0, tiles, sink)
        return carry

    lax.fori_loop(0, g_ref.shape[0] // QB // DIL_UNROLL, step1, 0)


def _dilated_attn(pb, p32, bias):
    bsz, _, seq, _ = pb.shape
    stats = pltpu.VMEM((2, 2, seq, LANES), F32)
    return pl.pallas_call(
        _dilated_kernel,
        grid=(bsz,),
        in_specs=[pl.BlockSpec((None, B_SLABS, seq, LANES), lambda b: (b, 0, 0, 0)),
                  pl.BlockSpec((None, seq, 256), lambda b: (b, 0, P_BG // 256)),
                  pl.BlockSpec((4, 2 * len(DILATED_PATTERNS), QB, 2 * QB), lambda b: (0, 0, 0, 0))],
        out_specs=pl.BlockSpec((None, seq, BRANCH_WIDTH), lambda b: (b, 0, 0)),
        out_shape=jax.ShapeDtypeStruct((bsz, seq, BRANCH_WIDTH), BF16),
        scratch_shapes=[stats, stats, stats],
        compiler_params=_params(("arbitrary",)),
        name="dilated_attn",
    )(pb, p32, bias)


def _out_proj_kernel(ya_ref, yb_ref, yc_ref, yd_ref, w_ref, h_ref, g_ref, o_ref):
    y = _dot(jnp.concatenate([ya_ref[...], yb_ref[...], yc_ref[...], yd_ref[...]], axis=1), w_ref[...])
    ms = jnp.mean(y * y, axis=-1, keepdims=True)
    o_ref[...] = h_ref[...] + y * lax.rsqrt(ms + EPS) * g_ref[...]


def _out_proj(ya, yb, yc, yd, w, layer, h, g, tm=1024):
    bsz, seq, _ = h.shape
    yspec = pl.BlockSpec((None, tm, BRANCH_WIDTH), lambda b, i: (b, i, 0))
    return pl.pallas_call(
        _out_proj_kernel,
        grid=(bsz, seq // tm),
        in_specs=[yspec, yspec, yspec, yspec,
                  pl.BlockSpec((None, 4 * BRANCH_WIDTH, D_MODEL), lambda b, i: (layer, 0, 0)),
                  pl.BlockSpec((None, tm, D_MODEL), lambda b, i: (b, i, 0)),
                  pl.BlockSpec((1, D_MODEL), lambda b, i: (0, 0))],
        out_specs=pl.BlockSpec((None, tm, D_MODEL), lambda b, i: (b, i, 0)),
        out_shape=jax.ShapeDtypeStruct(h.shape, F32),
        compiler_params=_params(("arbitrary", "arbitrary")),
        name="out_proj",
    )(ya, yb, yc, yd, w, h, g)


def kernel(x, w_in, w_out, norm_pre, norm_post, mla_q_norm, mla_kv_norm, mla_w_uq, mla_w_ukv,
           diff_lambda, diff_subln, rel_bias):
    bsz, seq, _ = x.shape
    depth = w_in.shape[0]
    nq = seq // QB
    k_top = min(IDX_TOPK_MAX, seq // 4)
    cos_t, sin_t = _rope_tables(seq)
    bias_b = _bias_expand(rel_bias, _bucket_tiles_dilated(), 0, 4, LOG2E)
    causal_buckets_t = np.ascontiguousarray(np.swapaxes(_bucket_tiles_causal(nq), 1, 2))
    bias_c = _bias_expand(rel_bias, causal_buckets_t, 4, 4, LOG2E)
    bias_d = _bias_expand(rel_bias, causal_buckets_t, 8, 4, LOG2E)
    mask_tiles = jnp.asarray(_mask_tiles())
    w_arr_all = _arrange_w_in(w_in)
    w_out16 = w_out.astype(BF16)
    h = x
    for layer in range(depth):
        wq, wqrot, wk, wv = _arrange_mla(mla_w_uq[layer], mla_w_ukv[layer])
        gq = jnp.concatenate([mla_q_norm[layer], jnp.ones((256 - Q_LORA,), F32)])[None, :]
        gkv = mla_kv_norm[layer][None, :]
        p32, p16, pb, qa, ka, va = _in_proj(h, norm_pre[layer][None, :], w_arr_all, layer,
                                            cos_t, sin_t, gq, gkv, wq, wqrot, wk, wv)
        y_a = _mla_attn(qa, ka, va, p32, mask_tiles)
        y_b = _dilated_attn(pb, p32, bias_b)
        y_c = _sparse_attn(p32, p16, bias_c, k_top)
        lambda_init = 0.8 - 0.6 * math.exp(-0.3 * layer)
        subln = jnp.tile(diff_subln[layer], BRANCH_WIDTH // HEAD_DIM)[None, :]
        y_d = _diff_attn(p32, p16, bias_d, diff_lambda[layer], subln, lambda_init)
        h = _out_proj(y_a, y_b, y_c, y_d, w_out16, layer, h, norm_post[layer][None, :])
    return h
```

```python
import functools
import math

import jax
import jax.numpy as jnp
import numpy as np
from jax import lax
from jax.experimental import pallas as pl
from jax.experimental.pallas import tpu as pltpu

F32 = jnp.float32
BF16 = jnp.bfloat16

D_MODEL = 1024
A_HEADS, A_NOPE, A_ROPE, A_V = 4, 64, 32, 64
Q_LORA, KV_LORA = 192, 128
ROPE_THETA = 10000.0
HEAD_DIM = 64
DILATED_PATTERNS = ((128, 1), (512, 4), (2048, 16))
IDX_HEADS, IDX_DIM, IDX_TOPK_MAX = 8, 32, 256
D_QK = 32
BRANCH_WIDTH = 256
NUM_BUCKETS, MAX_DISTANCE = 32, 2048
NEG_INF = -1e30
EPS = 1e-6
LOG2E = math.log2(math.e)
KV_CHUNK = 256
COL_ACC_ROWS = 64
DIL_UNROLL = 4
LANES = 128
QB = 128
VMEM_LIMIT = 56 * 1024 * 1024

_SPLIT = (Q_LORA, KV_LORA, A_ROPE, 256, 256, 256, 256, 256, 256, 256, 256, IDX_HEADS * IDX_DIM, IDX_DIM,
          IDX_HEADS, 256, 256, 256, 256, 256)
_OFF = np.concatenate([[0], np.cumsum(_SPLIT)]).tolist()
(_A_CQ, _A_CKV, _A_KR, _A_G, _B_Q, _B_K, _B_V, _B_G, _C_Q, _C_K, _C_V, _C_QI, _C_KI, _C_WI, _C_G,
 _D_Q, _D_K, _D_V, _D_G) = range(19)

L_CQ, L_CKV, L_KR, L_KRROT = 0, 256, 384, 512
P_START = 512
P_CWI, P_AG, P_BG, P_CQ2, P_CV, P_CG, P_DQ, P_DV, P_DG = 128, 256, 512, 768, 1024, 1280, 1536, 1792, 2048
NCOL32 = 2304
Q_START = P_START + NCOL32
Q_CK, Q_CQI, Q_DK, Q_CKI = 0, 256, 512, 768
NCOL16 = 896
B_START = Q_START + NCOL16
B_SLABS = 6
NCOL = B_START + B_SLABS * LANES


def _dot(a, b):
    return jnp.dot(a, b, preferred_element_type=F32)


def _dot_nt(a, b):
    return lax.dot_general(a, b, (((1,), (1,)), ((), ())), preferred_element_type=F32)


def _params(sem):
    return pltpu.CompilerParams(dimension_semantics=sem, vmem_limit_bytes=VMEM_LIMIT)


def _rot_cols(w):
    half = w.shape[-1] // 2
    return jnp.concatenate([-w[..., half:], w[..., :half]], axis=-1)


def _arrange_w_in(w):
    depth, rows, in_cols = w.shape
    tm = 256
    return pl.pallas_call(
        _arrange_w_in_kernel,
        grid=(depth, rows // tm),
        in_specs=[pl.BlockSpec((None, tm, in_cols), lambda l, i: (l, i, 0))],
        out_specs=pl.BlockSpec((None, tm, NCOL), lambda l, i: (l, i, 0)),
        out_shape=jax.ShapeDtypeStruct((depth, rows, NCOL), BF16),
        compiler_params=_params(("arbitrary", "arbitrary")),
        name="arrange_w_in",
    )(w)


def _w_in_pieces():
    pieces = []

    def put(dst, seg, lo=0, hi=None, scale=1.0):
        hi = _SPLIT[seg] if hi is None else hi
        pieces.append((dst, _OFF[seg] + lo, hi - lo, scale))
        return dst + hi - lo

    put(L_CQ, _A_CQ)
    put(L_CKV, _A_CKV)
    put(L_KR + A_NOPE, _A_KR)
    half = A_ROPE // 2
    put(L_KRROT + A_NOPE, _A_KR, half, A_ROPE, scale=-1.0)
    put(L_KRROT + A_NOPE + half, _A_KR, 0, half)
    for dst, seg in ((P_CWI, _C_WI), (P_AG, _A_G), (P_BG, _B_G), (P_CQ2, _C_Q), (P_CV, _C_V), (P_CG, _C_G),
                     (P_DQ, _D_Q), (P_DV, _D_V), (P_DG, _D_G)):
        put(P_START + dst, seg)
    for dst, seg in ((Q_CK, _C_K), (Q_CQI, _C_QI), (Q_DK, _D_K)):
        put(Q_START + dst, seg)
    for copy in range(LANES // IDX_DIM):
        put(Q_START + Q_CKI + copy * IDX_DIM, _C_KI)
    base = put(B_START, _B_Q, scale=HEAD_DIM ** -0.5 * LOG2E)
    base = put(base, _B_K)
    base = put(base, _B_V)
    assert base == NCOL
    return pieces


def _arrange_w_in_kernel(w_ref, o_ref):
    o_ref[...] = jnp.zeros(o_ref.shape, o_ref.dtype)
    for dst, src, n, scale in _w_in_pieces():
        lo = src // LANES * LANES
        hi = min(-(-(src + n) // LANES) * LANES, w_ref.shape[1])
        v = w_ref[:, lo:hi][:, src - lo:src - lo + n]
        o_ref[:, dst:dst + n] = (v * scale if scale != 1.0 else v).astype(o_ref.dtype)


def _arrange_mla(w_uq, w_ukv):
    wq = w_uq.reshape(Q_LORA, A_HEADS, A_NOPE + A_ROPE)
    nope, rope = wq[..., :A_NOPE], wq[..., A_NOPE:]
    zq = jnp.zeros((Q_LORA, A_HEADS, LANES - A_NOPE - A_ROPE), w_uq.dtype)
    wq_main = jnp.concatenate([nope, rope, zq], axis=-1).reshape(Q_LORA, A_HEADS * LANES)
    wq_rot = jnp.concatenate([jnp.zeros_like(nope), _rot_cols(rope), zq], axis=-1).reshape(Q_LORA, A_HEADS * LANES)
    pad = jnp.zeros((256 - Q_LORA, A_HEADS * LANES), w_uq.dtype)
    wq_main = jnp.concatenate([wq_main, pad], axis=0)
    wq_rot = jnp.concatenate([wq_rot, pad], axis=0)
    wkv = w_ukv.reshape(KV_LORA, A_HEADS, A_NOPE + A_V)
    knope, v = wkv[..., :A_NOPE], wkv[..., A_NOPE:]
    wk = jnp.concatenate([knope, jnp.zeros_like(knope)], axis=-1).reshape(KV_LORA, A_HEADS * LANES)
    wv_t = v.reshape(KV_LORA, A_HEADS * A_V).T
    return wq_main.astype(BF16), wq_rot.astype(BF16), wk.astype(BF16), wv_t.astype(BF16)


def _rope_tables(seq):
    inv = ROPE_THETA ** (-jnp.arange(0, A_ROPE, 2, dtype=F32) / A_ROPE)
    ang = jnp.arange(seq, dtype=F32)[:, None] * inv[None, :]
    cos, sin = jnp.cos(ang), jnp.sin(ang)
    one = jnp.ones((seq, A_NOPE), F32)
    zero = jnp.zeros((seq, LANES - A_NOPE - A_ROPE), F32)
    cos_t = jnp.concatenate([one, cos, cos, zero], axis=1)
    sin_t = jnp.concatenate([jnp.zeros_like(one), sin, sin, zero], axis=1)
    return cos_t, sin_t


def _t5_bucket_np(rel):
    n = np.maximum(rel, 0)
    max_exact = NUM_BUCKETS // 2
    nf = np.maximum(n, max_exact).astype(np.float64)
    large = max_exact + (np.log(nf / max_exact) / math.log(MAX_DISTANCE / max_exact)
                         * (NUM_BUCKETS - max_exact)).astype(np.int32)
    large = np.minimum(large, NUM_BUCKETS - 1)
    return np.where(n < max_exact, n, large).astype(np.int32)


MASKED_BUCKET = NUM_BUCKETS


def _bucket_tiles_causal(nq):
    q = np.arange(QB)[:, None]
    k = np.arange(QB)[None, :]
    tiles = [np.full((QB, QB), MASKED_BUCKET, np.int32)]
    for d in range(nq):
        rel = QB * d + q - k
        tiles.append(np.where(rel >= 0, _t5_bucket_np(rel), MASKED_BUCKET).astype(np.int32))
    return np.stack(tiles)


def _mask_tiles():
    k = np.arange(QB)[:, None]
    q = np.arange(QB)[None, :]
    diag = np.where(k <= q, 0.0, NEG_INF)
    return np.stack([np.full((QB, QB), NEG_INF), diag, np.zeros((QB, QB))]).astype(np.float32)


def _bucket_tiles_dilated():
    q = np.arange(QB)[:, None]
    k = np.arange(2 * QB)[None, :]
    rel = q + QB - k
    tiles = []
    for (window, d) in DILATED_PATTERNS:
        in_band = (rel >= 0) & (rel <= window // d)
        for has_prev in (False, True):
            ok = in_band & (has_prev | (k >= QB))
            tiles.append(np.where(ok, _t5_bucket_np(rel * d), MASKED_BUCKET).astype(np.int32))
    return np.stack(tiles)


def _bias_expand_kernel(table_ref, bucket_ref, out_ref, *, head0, scale, present):
    h = pl.program_id(0) + head0
    for n, buckets_in_tile in enumerate(present):
        bk = bucket_ref[n]
        acc = jnp.where(bk == MASKED_BUCKET, NEG_INF, 0.0)
        for b in buckets_in_tile:
            acc = jnp.where(bk == b, table_ref[b, h] * scale, acc)
        out_ref[n] = acc


def _bias_expand(table, buckets, head0, nheads, scale=1.0):
    n, r, c = buckets.shape
    present = tuple(tuple(int(b) for b in np.unique(tile) if b != MASKED_BUCKET) for tile in buckets)
    buckets = jnp.asarray(buckets)
    return pl.pallas_call(
        functools.partial(_bias_expand_kernel, head0=head0, scale=scale, present=present),
        grid=(nheads,),
        in_specs=[pl.BlockSpec(memory_space=pltpu.SMEM),
                  pl.BlockSpec((n, r, c), lambda h: (0, 0, 0))],
        out_specs=pl.BlockSpec((None, n, r, c), lambda h: (h, 0, 0, 0)),
        out_shape=jax.ShapeDtypeStruct((nheads, n, r, c), F32),
        compiler_params=_params(("arbitrary",)),
        name="bias_expand",
    )(table, buckets)


def _in_proj_kernel(x_ref, g_ref, w_ref, cos_ref, sin_ref, gq_ref, gkv_ref, wq_ref, wqrot_ref, wk_ref, wv_ref,
                    o32_ref, o16_ref, ob_ref, q_ref, k_ref, v_ref):
    x = x_ref[...]
    ms = jnp.mean(x * x, axis=-1, keepdims=True)
    xn = x * lax.rsqrt(ms + EPS) * g_ref[...]
    p = _dot(xn.astype(BF16), w_ref[...])
    o32_ref[...] = p[:, P_START:Q_START]
    o16_ref[...] = p[:, Q_START:B_START].astype(BF16)
    for s in range(B_SLABS):
        ob_ref[s] = p[:, B_START + s * LANES:B_START + (s + 1) * LANES]
    cos = cos_ref[...]
    sin = sin_ref[...]
    cos4 = jnp.concatenate([cos] * A_HEADS, axis=1)
    sin4 = jnp.concatenate([sin] * A_HEADS, axis=1)
    cq = p[:, L_CQ:L_CQ + 256]
    ms = jnp.sum(cq * cq, axis=-1, keepdims=True) * (1.0 / Q_LORA)
    nq = (cq * lax.rsqrt(ms + EPS) * gq_ref[...]).astype(BF16)
    q = _dot(nq, wq_ref[...]) * cos4 + _dot(nq, wqrot_ref[...]) * sin4
    q_ref[...] = (q * ((A_NOPE + A_ROPE) ** -0.5 * LOG2E)).astype(BF16)
    ckv = p[:, L_CKV:L_CKV + KV_LORA]
    ms = jnp.mean(ckv * ckv, axis=-1, keepdims=True)
    nkv = (ckv * lax.rsqrt(ms + EPS) * gkv_ref[...]).astype(BF16)
    kr = p[:, L_KR:L_KR + LANES] * cos + p[:, L_KRROT:L_KRROT + LANES] * sin
    k = _dot(nkv, wk_ref[...]) + jnp.concatenate([kr] * A_HEADS, axis=1)
    k_ref[...] = k.astype(BF16)
    v_ref[...] = _dot_nt(wv_ref[...], nkv).astype(BF16)


def _in_proj(h, g, w, layer, cos_t, sin_t, gq, gkv, wq, wqrot, wk, wv_t, tm=512):
    bsz, seq, _ = h.shape
    w4 = A_HEADS * LANES
    row = lambda b, i: (b, i, 0)
    const = lambda b, i: (0, 0)
    return pl.pallas_call(
        _in_proj_kernel,
        grid=(bsz, seq // tm),
        in_specs=[pl.BlockSpec((None, tm, D_MODEL), row),
                  pl.BlockSpec((1, D_MODEL), const),
                  pl.BlockSpec((None, D_MODEL, NCOL), lambda b, i: (layer, 0, 0)),
                  pl.BlockSpec((tm, LANES), lambda b, i: (i, 0)),
                  pl.BlockSpec((tm, LANES), lambda b, i: (i, 0)),
                  pl.BlockSpec((1, 256), const),
                  pl.BlockSpec((1, KV_LORA), const),
                  pl.BlockSpec((256, w4), const),
                  pl.BlockSpec((256, w4), const),
                  pl.BlockSpec((KV_LORA, w4), const),
                  pl.BlockSpec((A_HEADS * A_V, KV_LORA), const)],
        out_specs=[pl.BlockSpec((None, tm, NCOL32), row),
                   pl.BlockSpec((None, tm, NCOL16), row),
                   pl.BlockSpec((None, B_SLABS, tm, LANES), lambda b, i: (b, 0, i, 0)),
                   pl.BlockSpec((None, tm, w4), row), pl.BlockSpec((None, tm, w4), row),
                   pl.BlockSpec((None, A_HEADS * A_V, tm), lambda b, i: (b, 0, i))],
        out_shape=[jax.ShapeDtypeStruct((bsz, seq, NCOL32), F32),
                   jax.ShapeDtypeStruct((bsz, seq, NCOL16), BF16),
                   jax.ShapeDtypeStruct((bsz, B_SLABS, seq, LANES), F32),
                   jax.ShapeDtypeStruct((bsz, seq, w4), BF16), jax.ShapeDtypeStruct((bsz, seq, w4), BF16),
                   jax.ShapeDtypeStruct((bsz, A_HEADS * A_V, seq), BF16)],
        compiler_params=_params(("arbitrary", "arbitrary")),
        name="in_proj",
    )(h, g, w, cos_t, sin_t, gq, gkv, wq, wqrot, wk, wv_t)


def _silu(g):
    return g * (1.0 / (1.0 + jnp.exp(-g)))


def _for_step_blocks(i, seq, block_body, unroll_upto=0):
    per = KV_CHUNK // QB
    for wb in range(seq // KV_CHUNK):
        @pl.when(i == wb)
        def _(width=(wb + 1) * KV_CHUNK):
            if width <= unroll_upto:
                for u in range(per):
                    block_body(width, per * i + u, slice(u * QB, (u + 1) * QB))
                return

            def one(u, carry):
                block_body(width, per * i + u, pl.ds(pl.multiple_of(u * QB, QB), QB))
                return carry

            lax.fori_loop(0, per, one, 0)


def _mask_tail_t(mask_ref, i, width):
    first = (width - KV_CHUNK) // QB
    return jnp.concatenate([mask_ref[jnp.clip(i - j, -1, 1) + 1] for j in range(first, width // QB)], axis=0)


def _col_reduce(x, op):
    rows, lanes = x.shape
    part = op(x.reshape(rows // COL_ACC_ROWS, COL_ACC_ROWS, lanes), axis=0)
    return op(part, axis=0, keepdims=True)


def _bias_col(bias_ref, h, i, width):
    return jnp.concatenate([bias_ref[h, jnp.maximum(i - j, -1) + 1] for j in range(width // QB)], axis=0)


def _lane_mask(width, seg, dtype):
    lane = lax.broadcasted_iota(jnp.int32, (1, LANES), 1)
    return jnp.where((lane >= seg * width) & (lane < (seg + 1) * width), 1.0, 0.0).astype(dtype)


def _mla_attn_kernel(q_ref, k_ref, vt_ref, g_ref, mask_ref, o_ref):
    i = pl.program_id(1)
    seq = k_ref.shape[0]

    def body(width, ib, rows):
        head_w = width - KV_CHUNK
        zero = jnp.zeros((QB, LANES), BF16)
        scores = []
        for g in range(A_HEADS // 2):
            q0 = q_ref[rows, 2 * g * LANES:(2 * g + 1) * LANES]
            q1 = q_ref[rows, (2 * g + 1) * LANES:(2 * g + 2) * LANES]
            qbd = jnp.concatenate([jnp.concatenate([q0, zero], axis=1),
                                   jnp.concatenate([zero, q1], axis=1)], axis=0)
            scores.append(_dot_nt(k_ref[:width, 2 * g * LANES:(2 * g + 2) * LANES], qbd))
        tail = _mask_tail_t(mask_ref, ib, width)
        outs = []
        for g in range(A_HEADS // 2):
            es, ls = [], []
            for hh in range(2):
                s = scores[g][:, hh * QB:(hh + 1) * QB]
                s = jnp.concatenate([s[:head_w], s[head_w:] + tail], axis=0) if head_w else s + tail
                e = jnp.exp2(s - _col_reduce(s, jnp.max))
                ls.append(_col_reduce(e, jnp.sum))
                es.append(e.astype(BF16))
            ot = _dot(vt_ref[g * LANES:(g + 1) * LANES, :width], jnp.concatenate(es, axis=1))
            outs.append(ot[:A_V, :QB] / ls[0])
            outs.append(ot[A_V:, QB:] / ls[1])
        y_t = jnp.concatenate(outs, axis=0)
        y = jnp.concatenate([y_t[:LANES].T, y_t[LANES:].T], axis=1)
        o_ref[rows, :] = (y * _silu(g_ref[rows, :])).astype(BF16)

    _for_step_blocks(i, seq, body, unroll_upto=seq)


def _mla_attn(q, k, v_t, p32, mask_tiles):
    bsz, seq, w4 = q.shape
    return pl.pallas_call(
        _mla_attn_kernel,
        grid=(bsz, seq // KV_CHUNK),
        in_specs=[pl.BlockSpec((None, KV_CHUNK, w4), lambda b, i: (b, i, 0)),
                  pl.BlockSpec((None, seq, w4), lambda b, i: (b, 0, 0)),
                  pl.BlockSpec((None, A_HEADS * A_V, seq), lambda b, i: (b, 0, 0)),
                  pl.BlockSpec((None, KV_CHUNK, 256), lambda b, i: (b, i, P_AG // 256)),
                  pl.BlockSpec((3, QB, QB), lambda b, i: (0, 0, 0))],
        out_specs=pl.BlockSpec((None, KV_CHUNK, BRANCH_WIDTH), lambda b, i: (b, i, 0)),
        out_shape=jax.ShapeDtypeStruct((bsz, seq, BRANCH_WIDTH), BF16),
        compiler_params=_params(("arbitrary", "arbitrary")),
        name="mla_attn",
    )(q, k, v_t, p32, mask_tiles)


def _transpose_values(v_ref, vt_ref):
    for j in range(v_ref.shape[0] // QB):
        for g in range(v_ref.shape[1] // LANES):
            tile = v_ref[j * QB:(j + 1) * QB, g * LANES:(g + 1) * LANES]
            vt_ref[g * LANES:(g + 1) * LANES, j * QB:(j + 1) * QB] = tile.T.astype(BF16)


def _diff_attn_kernel(q_ref, k_ref, v_ref, g_ref, bias_ref, lam_ref, subln_ref, o_ref, vt_ref, *, lambda_init):
    i = pl.program_id(1)
    seq = k_ref.shape[0]

    @pl.when(i == 0)
    def _():
        _transpose_values(v_ref, vt_ref)

    def body(width, ib, rows):
        lp = lam_ref[...]
        lam = (jnp.exp(jnp.sum(lp[0:1] * lp[1:2], axis=-1, keepdims=True))
               - jnp.exp(jnp.sum(lp[2:3] * lp[3:4], axis=-1, keepdims=True)) + lambda_init)
        scores = []
        for h in range(4):
            sl = slice((h // 2) * LANES, (h // 2 + 1) * LANES)
            qg = q_ref[rows, sl] * (D_QK ** -0.5 * LOG2E)
            qcat = jnp.concatenate([(qg * _lane_mask(D_QK, 2 * (h % 2) + mm, F32)).astype(BF16)
                                    for mm in range(2)], axis=0)
            scores.append(_dot_nt(k_ref[:width, sl], qcat))
        outs = []
        for h in range(4):
            bias = _bias_col(bias_ref, h, ib, width)
            es, ls = [], []
            for mm in range(2):
                s = scores[h][:, mm * QB:(mm + 1) * QB] + bias
                e = jnp.exp2(s - _col_reduce(s, jnp.max))
                ls.append(_col_reduce(e, jnp.sum))
                es.append(e.astype(BF16))
            ot = _dot(vt_ref[h * HEAD_DIM:(h + 1) * HEAD_DIM, :width], jnp.concatenate(es, axis=1))
            a = ot[:, :QB] / ls[0] - lam * (ot[:, QB:] / ls[1])
            ms = jnp.sum(a * a, axis=0, keepdims=True) * (1.0 / HEAD_DIM)
            outs.append(a * lax.rsqrt(ms + EPS))
        y_t = jnp.concatenate(outs, axis=0)
        y = jnp.concatenate([y_t[:LANES].T, y_t[LANES:].T], axis=1)
        o_ref[rows, :] = (y * (subln_ref[...] * (1.0 - lambda_init)) * _silu(g_ref[rows, :])).astype(BF16)

    _for_step_blocks(i, seq, body, unroll_upto=seq // 2)


def _diff_attn(p32, p16, bias, lam_params, subln, lambda_init):
    bsz, seq, _ = p32.shape
    nq = seq // QB
    return pl.pallas_call(
        functools.partial(_diff_attn_kernel, lambda_init=lambda_init),
        grid=(bsz, seq // KV_CHUNK),
        in_specs=[pl.BlockSpec((None, KV_CHUNK, 256), lambda b, i: (b, i, P_DQ // 256)),
                  pl.BlockSpec((None, seq, 256), lambda b, i: (b, 0, Q_DK // 256)),
                  pl.BlockSpec((None, seq, 256), lambda b, i: (b, 0, P_DV // 256)),
                  pl.BlockSpec((None, KV_CHUNK, 256), lambda b, i: (b, i, P_DG // 256)),
                  pl.BlockSpec((4, nq + 1, QB, QB), lambda b, i: (0, 0, 0, 0)),
                  pl.BlockSpec((4, D_QK), lambda b, i: (0, 0)),
                  pl.BlockSpec((1, BRANCH_WIDTH), lambda b, i: (0, 0))],
        out_specs=pl.BlockSpec((None, KV_CHUNK, BRANCH_WIDTH), lambda b, i: (b, i, 0)),
        out_shape=jax.ShapeDtypeStruct((bsz, seq, BRANCH_WIDTH), BF16),
        scratch_shapes=[pltpu.VMEM((BRANCH_WIDTH, seq), BF16)],
        compiler_params=_params(("arbitrary", "arbitrary")),
        name="diff_attn",
    )(p32, p16, p32, p32, bias, lam_params, subln)


def _sortable_to_float(key):
    return pltpu.bitcast(jnp.where(key < 0, key ^ jnp.int32(0x7FFFFFFF), key), F32)


def _kth_largest(score_ref, width, k_top):
    def count_ge(key):
        thr = _sortable_to_float(key)
        return _col_reduce(jnp.where(score_ref[:width, :] >= thr, 1.0, 0.0), jnp.sum)

    int_min = jnp.full((1, QB), -2 ** 31, jnp.int32)
    zero = jnp.zeros((1, QB), jnp.int32)
    t = jnp.where(count_ge(zero) >= k_top, zero, int_min)

    def step(it, t):
        cand = t + (jnp.int32(1) << (30 - it))
        return jnp.where(count_ge(cand) >= k_top, cand, t)

    return _sortable_to_float(lax.fori_loop(0, 31, step, t))


def _first_ties(eq, need, row):
    eqf = jnp.where(eq, 1.0, 0.0)
    nbits = int(eq.shape[0]).bit_length()

    def body(it, j):
        cand = j + (jnp.int32(1) << (nbits - 1 - it))
        cnt = _col_reduce(jnp.where(row < cand, eqf, 0.0), jnp.sum)
        return jnp.where(cnt <= need, cand, j)

    j = lax.fori_loop(0, nbits, body, jnp.zeros((1, eq.shape[1]), jnp.int32))
    return eq & (row < j)


def _sparse_attn_kernel(q_ref, k_ref, v_ref, qi_ref, ki_ref, wi_ref, g_ref, bias_ref, o_ref,
                        score_ref, neg_ref, vt_ref, *, k_top):
    i = pl.program_id(1)
    seq = k_ref.shape[0]

    @pl.when(i == 0)
    def _():
        _transpose_values(v_ref, vt_ref)

    def body(width, ib, rows):
        head_w = width - KV_CHUNK
        ki = ki_ref[:width, :]
        w_t = wi_ref[rows, :].T * (IDX_DIM ** -0.5 * IDX_HEADS ** -0.5)
        score = None
        for h in range(0, IDX_HEADS, 2):
            qg = qi_ref[rows, (h // 4) * LANES:(h // 4 + 1) * LANES]
            qcat = jnp.concatenate([qg * _lane_mask(IDX_DIM, h % 4, BF16),
                                    qg * _lane_mask(IDX_DIM, h % 4 + 1, BF16)], axis=0)
            logit = _dot_nt(ki, qcat)
            term = (jnp.maximum(logit[:, :QB], 0.0) * w_t[h:h + 1]
                    + jnp.maximum(logit[:, QB:], 0.0) * w_t[h + 1:h + 2])
            score = term if score is None else score + term
        s_idx = head_w + lax.broadcasted_iota(jnp.int32, (KV_CHUNK, QB), 0)
        tail_ok = s_idx <= ib * QB + lax.broadcasted_iota(jnp.int32, (KV_CHUNK, QB), 1)
        if head_w:
            score_ref[:head_w, :] = score[:head_w]
        score_ref[head_w:width, :] = jnp.where(tail_ok, score[head_w:], NEG_INF)
        s2s = []
        for g in range(2):
            sl = slice(g * LANES, (g + 1) * LANES)
            qg = q_ref[rows, sl] * (HEAD_DIM ** -0.5 * LOG2E)
            qcat = jnp.concatenate([(qg * _lane_mask(HEAD_DIM, hh, F32)).astype(BF16) for hh in range(2)], axis=0)
            s2 = _dot_nt(k_ref[:width, sl], qcat)
            s2s.append([s2[:, hh * QB:(hh + 1) * QB] + _bias_col(bias_ref, 2 * g + hh, ib, width) for hh in range(2)])
        thr = _kth_largest(score_ref, width, k_top)
        keep_tail = (score_ref[head_w:width, :] >= thr) & tail_ok
        cnt = _col_reduce(jnp.where(keep_tail, 1.0, 0.0), jnp.sum)
        neg_ref[head_w:width, :] = jnp.where(keep_tail, 0.0, NEG_INF)
        if head_w:
            keep_head = score_ref[:head_w, :] >= thr
            cnt = cnt + _col_reduce(jnp.where(keep_head, 1.0, 0.0), jnp.sum)
            neg_ref[:head_w, :] = jnp.where(keep_head, 0.0, NEG_INF)

        @pl.when(jnp.max(cnt) > k_top)
        def _():
            x = score_ref[:width, :]
            gt = x > thr
            need = k_top - _col_reduce(jnp.where(gt, 1.0, 0.0), jnp.sum)
            row = lax.broadcasted_iota(jnp.int32, (width, QB), 0)
            neg_ref[:width, :] = jnp.where(gt | _first_ties(x == thr, need, row), 0.0, NEG_INF)

        outs = []
        for g in range(2):
            sl = slice(g * LANES, (g + 1) * LANES)
            es, ls = [], []
            for hh in range(2):
                s = s2s[g][hh] + neg_ref[:width, :]
                e = jnp.exp2(s - _col_reduce(s, jnp.max))
                ls.append(_col_reduce(e, jnp.sum))
                es.append(e.astype(BF16))
            ot = _dot(vt_ref[sl, :width], jnp.concatenate(es, axis=1))
            outs.append(ot[:HEAD_DIM, :QB] / ls[0])
            outs.append(ot[HEAD_DIM:, QB:] / ls[1])
        y_t = jnp.concatenate(outs, axis=0)
        y = jnp.concatenate([y_t[:LANES].T, y_t[LANES:].T], axis=1)
        o_ref[rows, :] = (y * _silu(g_ref[rows, :])).astype(BF16)

    _for_step_blocks(i, seq, body, unroll_upto=seq // 4)


def _sparse_attn(p32, p16, bias, k_top):
    bsz, seq, _ = p32.shape
    nq = seq // QB
    return pl.pallas_call(
        functools.partial(_sparse_attn_kernel, k_top=k_top),
        grid=(bsz, seq // KV_CHUNK),
        in_specs=[pl.BlockSpec((None, KV_CHUNK, 256), lambda b, i: (b, i, P_CQ2 // 256)),
                  pl.BlockSpec((None, seq, 256), lambda b, i: (b, 0, Q_CK // 256)),
                  pl.BlockSpec((None, seq, 256), lambda b, i: (b, 0, P_CV // 256)),
                  pl.BlockSpec((None, KV_CHUNK, 256), lambda b, i: (b, i, Q_CQI // 256)),
                  pl.BlockSpec((None, seq, LANES), lambda b, i: (b, 0, Q_CKI // LANES)),
                  pl.BlockSpec((None, KV_CHUNK, LANES), lambda b, i: (b, i, P_CWI // LANES)),
                  pl.BlockSpec((None, KV_CHUNK, 256), lambda b, i: (b, i, P_CG // 256)),
                  pl.BlockSpec((4, nq + 1, QB, QB), lambda b, i: (0, 0, 0, 0))],
        out_specs=pl.BlockSpec((None, KV_CHUNK, BRANCH_WIDTH), lambda b, i: (b, i, 0)),
        out_shape=jax.ShapeDtypeStruct((bsz, seq, BRANCH_WIDTH), BF16),
        scratch_shapes=[pltpu.VMEM((seq, QB), F32), pltpu.VMEM((seq, QB), F32),
                        pltpu.VMEM((BRANCH_WIDTH, seq), BF16)],
        compiler_params=_params(("arbitrary", "arbitrary")),
        name="sparse_attn",
    )(p32, p16, p32, p16, p16, p32, p32, bias)


def _dilated_kernel(x_ref, g_ref, bias_ref, o_ref, m_ref, l_ref, acc_ref):
    first = lax.broadcasted_iota(jnp.int32, (QB, LANES), 1) < HEAD_DIM
    hmask = [_lane_mask(HEAD_DIM, hh, F32) for hh in range(2)]

    def run(p, tiles, sink):
        work = []
        for cur, prev, variant in tiles:
            for g in range(2):
                q = x_ref[g, cur, :]
                kcat = x_ref[2 + g, cur, :]
                vcat = x_ref[4 + g, cur, :]
                if prev is not None:
                    kcat = jnp.concatenate([x_ref[2 + g, prev, :], kcat], axis=0)
                    vcat = jnp.concatenate([x_ref[4 + g, prev, :], vcat], axis=0)
                kcat = kcat.astype(BF16)
                scores = []
                for hh in range(2):
                    if prev is not None:
                        bias = bias_ref[2 * g + hh, 2 * p + variant]
                    else:
                        bias = bias_ref[2 * g + hh, 2 * p, QB:, :]
                    scores.append(_dot_nt(kcat, (q * hmask[hh]).astype(BF16)) + bias)
                vcat = vcat.astype(BF16)
                work.append((jnp.concatenate([vcat, jnp.ones_like(vcat)], axis=1), scores))
        eye = (lax.broadcasted_iota(jnp.int32, (QB, QB), 0) == lax.broadcasted_iota(jnp.int32, (QB, QB), 1))
        ones = jnp.ones((QB, LANES), BF16)
        for n, (v1, scores) in enumerate(work):
            ms, ls, accs = [], [], []
            for s in scores:
                m = jnp.max(s, axis=0, keepdims=True).astype(BF16).astype(F32)
                e = jnp.exp2(s - m)
                r = lax.dot_general(e.astype(BF16), v1, (((0,), (0,)), ((), ())), preferred_element_type=F32)
                ms.append(_dot(jnp.where(eye, m, 0.0).astype(BF16), ones))
                ls.append(r[:, LANES:])
                accs.append(r[:, :LANES])
            sink(n // 2, n % 2, jnp.where(first, ms[0], ms[1]), jnp.where(first, ls[0], ls[1]),
                 jnp.where(first, accs[0], accs[1]))

    def store_stats(slot, d, starts):
        def sink(t, g, m, l, a):
            rows = pl.ds(starts[t], QB, stride=d)
            m_ref[slot, g, rows, :] = m
            l_ref[slot, g, rows, :] = l
            acc_ref[slot, g, rows, :] = a
        return sink

    def step16(n0, carry):
        starts = [n0 * DIL_UNROLL + u for u in range(DIL_UNROLL)]
        run(2, [(pl.ds(r, QB, stride=16), None, 0) for r in starts], store_stats(1, 16, starts))
        return carry

    lax.fori_loop(0, 16 // DIL_UNROLL, step16, 0)

    def step4(j, carry):
        starts = [r + 4 * QB * j for r in range(4)]
        tiles = [(pl.ds(s, QB, stride=4), pl.ds(jnp.maximum(s - 4 * QB, r), QB, stride=4), jnp.minimum(j, 1))
                 for r, s in enumerate(starts)]
        run(1, tiles, store_stats(0, 4, starts))
        return carry

    lax.fori_loop(0, 4, step4, 0)

    def step1(n0, carry):
        blocks = [n0 * DIL_UNROLL + u for u in range(DIL_UNROLL)]

        def sink(t, g, m, l, a):
            rows = pl.ds(pl.multiple_of(blocks[t] * QB, QB), QB)
            lanes = slice(g * LANES, (g + 1) * LANES)
            m4, m16 = m_ref[0, g, rows, :], m_ref[1, g, rows, :]
            m_tot = jnp.maximum(m, jnp.maximum(m4, m16))
            w1, w4, w16 = jnp.exp2(m - m_tot), jnp.exp2(m4 - m_tot), jnp.exp2(m16 - m_tot)
            num = w1 * a + w4 * acc_ref[0, g, rows, :] + w16 * acc_ref[1, g, rows, :]
            den = w1 * l + w4 * l_ref[0, g, rows, :] + w16 * l_ref[1, g, rows, :]
            o_ref[rows, lanes] = (num / den * _silu(g_ref[rows, lanes])).astype(BF16)

        tiles = []
        for j in blocks:
            cur = pl.ds(pl.multiple_of(j * QB, QB), QB)
            prev = pl.ds(pl.multiple_of(jnp.maximum(j - 1, 0) * QB, QB), QB)
            tiles.append((cur, prev, jnp.minimum(j, 1)))
        run(0, tiles, sink)
        return carry

    lax.fori_loop(0, g_ref.shape[0] // QB // DIL_UNROLL, step1, 0)


def _dilated_attn(pb, p32, bias):
    bsz, _, seq, _ = pb.shape
    stats = pltpu.VMEM((2, 2, seq, LANES), F32)
    return pl.pallas_call(
        _dilated_kernel,
        grid=(bsz,),
        in_specs=[pl.BlockSpec((None, B_SLABS, seq, LANES), lambda b: (b, 0, 0, 0)),
                  pl.BlockSpec((None, seq, 256), lambda b: (b, 0, P_BG // 256)),
                  pl.BlockSpec((4, 2 * len(DILATED_PATTERNS), 2 * QB, QB), lambda b: (0, 0, 0, 0))],
        out_specs=pl.BlockSpec((None, seq, BRANCH_WIDTH), lambda b: (b, 0, 0)),
        out_shape=jax.ShapeDtypeStruct((bsz, seq, BRANCH_WIDTH), BF16),
        scratch_shapes=[stats, stats, stats],
        compiler_params=_params(("arbitrary",)),
        name="dilated_attn",
    )(pb, p32, bias)


def _out_proj_kernel(ya_ref, yb_ref, yc_ref, yd_ref, w_ref, h_ref, g_ref, o_ref):
    y = _dot(jnp.concatenate([ya_ref[...], yb_ref[...], yc_ref[...], yd_ref[...]], axis=1), w_ref[...])
    ms = jnp.mean(y * y, axis=-1, keepdims=True)
    o_ref[...] = h_ref[...] + y * lax.rsqrt(ms + EPS) * g_ref[...]


def _out_proj(ya, yb, yc, yd, w, layer, h, g, tm=1024):
    bsz, seq, _ = h.shape
    yspec = pl.BlockSpec((None, tm, BRANCH_WIDTH), lambda b, i: (b, i, 0))
    return pl.pallas_call(
        _out_proj_kernel,
        grid=(bsz, seq // tm),
        in_specs=[yspec, yspec, yspec, yspec,
                  pl.BlockSpec((None, 4 * BRANCH_WIDTH, D_MODEL), lambda b, i: (layer, 0, 0)),
                  pl.BlockSpec((None, tm, D_MODEL), lambda b, i: (b, i, 0)),
                  pl.BlockSpec((1, D_MODEL), lambda b, i: (0, 0))],
        out_specs=pl.BlockSpec((None, tm, D_MODEL), lambda b, i: (b, i, 0)),
        out_shape=jax.ShapeDtypeStruct(h.shape, F32),
        compiler_params=_params(("arbitrary", "arbitrary")),
        name="out_proj",
    )(ya, yb, yc, yd, w, h, g)


def kernel(x, w_in, w_out, norm_pre, norm_post, mla_q_norm, mla_kv_norm, mla_w_uq, mla_w_ukv,
           diff_lambda, diff_subln, rel_bias):
    bsz, seq, _ = x.shape
    depth = w_in.shape[0]
    nq = seq // QB
    k_top = min(IDX_TOPK_MAX, seq // 4)
    cos_t, sin_t = _rope_tables(seq)
    dilated_buckets_t = np.ascontiguousarray(np.swapaxes(_bucket_tiles_dilated(), 1, 2))
    bias_b = _bias_expand(rel_bias, dilated_buckets_t, 0, 4, LOG2E)
    causal_buckets_t = np.ascontiguousarray(np.swapaxes(_bucket_tiles_causal(nq), 1, 2))
    bias_c = _bias_expand(rel_bias, causal_buckets_t, 4, 4, LOG2E)
    bias_d = _bias_expand(rel_bias, causal_buckets_t, 8, 4, LOG2E)
    mask_tiles = jnp.asarray(_mask_tiles())
    w_arr_all = _arrange_w_in(w_in)
    w_out16 = w_out.astype(BF16)
    h = x
    for layer in range(depth):
        wq, wqrot, wk, wv = _arrange_mla(mla_w_uq[layer], mla_w_ukv[layer])
        gq = jnp.concatenate([mla_q_norm[layer], jnp.ones((256 - Q_LORA,), F32)])[None, :]
        gkv = mla_kv_norm[layer][None, :]
        p32, p16, pb, qa, ka, va = _in_proj(h, norm_pre[layer][None, :], w_arr_all, layer,
                                            cos_t, sin_t, gq, gkv, wq, wqrot, wk, wv)
        y_a = _mla_attn(qa, ka, va, p32, mask_tiles)
        y_b = _dilated_attn(pb, p32, bias_b)
        y_c = _sparse_attn(p32, p16, bias_c, k_top)
        lambda_init = 0.8 - 0.6 * math.exp(-0.3 * layer)
        subln = jnp.tile(diff_subln[layer], BRANCH_WIDTH // HEAD_DIM)[None, :]
        y_d = _diff_attn(p32, p16, bias_d, diff_lambda[layer], subln, lambda_init)
        h = _out_proj(y_a, y_b, y_c, y_d, w_out16, layer, h, norm_post[layer][None, :])
    return h
```

```python
import functools
import math

import jax
import jax.numpy as jnp
import numpy as np
from jax import lax
from jax.experimental import pallas as pl
from jax.experimental.pallas import tpu as pltpu

F32 = jnp.float32
BF16 = jnp.bfloat16

D_MODEL = 1024
A_HEADS, A_NOPE, A_ROPE, A_V = 4, 64, 32, 64
Q_LORA, KV_LORA = 192, 128
ROPE_THETA = 10000.0
HEAD_DIM = 64
DILATED_PATTERNS = ((128, 1), (512, 4), (2048, 16))
IDX_HEADS, IDX_DIM, IDX_TOPK_MAX = 8, 32, 256
D_QK = 32
BRANCH_WIDTH = 256
NUM_BUCKETS, MAX_DISTANCE = 32, 2048
NEG_INF = -1e30
EPS = 1e-6
LOG2E = math.log2(math.e)
KV_CHUNK = 256
COL_ACC_ROWS = 64
DIL_UNROLL = 4
LANES = 128
QB = 128
VMEM_LIMIT = 56 * 1024 * 1024

_SPLIT = (Q_LORA, KV_LORA, A_ROPE, 256, 256, 256, 256, 256, 256, 256, 256, IDX_HEADS * IDX_DIM, IDX_DIM,
          IDX_HEADS, 256, 256, 256, 256, 256)
_OFF = np.concatenate([[0], np.cumsum(_SPLIT)]).tolist()
(_A_CQ, _A_CKV, _A_KR, _A_G, _B_Q, _B_K, _B_V, _B_G, _C_Q, _C_K, _C_V, _C_QI, _C_KI, _C_WI, _C_G,
 _D_Q, _D_K, _D_V, _D_G) = range(19)

L_CQ, L_CKV, L_KR, L_KRROT = 0, 256, 384, 512
P_START = 512
P_CWI, P_AG, P_BG, P_CQ2, P_CV, P_CG, P_DQ, P_DV, P_DG = 128, 256, 512, 768, 1024, 1280, 1536, 1792, 2048
NCOL32 = 2304
Q_START = P_START + NCOL32
Q_CK, Q_CQI, Q_DK, Q_CKI = 0, 256, 512, 768
NCOL16 = 896
B_START = Q_START + NCOL16
B_SLABS = 6
NCOL = B_START + B_SLABS * LANES


def _dot(a, b):
    return jnp.dot(a, b, preferred_element_type=F32)


def _dot_nt(a, b):
    return lax.dot_general(a, b, (((1,), (1,)), ((), ())), preferred_element_type=F32)


def _params(sem):
    return pltpu.CompilerParams(dimension_semantics=sem, vmem_limit_bytes=VMEM_LIMIT)


def _rot_cols(w):
    half = w.shape[-1] // 2
    return jnp.concatenate([-w[..., half:], w[..., :half]], axis=-1)


def _arrange_w_in(w):
    depth, rows, in_cols = w.shape
    tm = 256
    return pl.pallas_call(
        _arrange_w_in_kernel,
        grid=(depth, rows // tm),
        in_specs=[pl.BlockSpec((None, tm, in_cols), lambda l, i: (l, i, 0))],
        out_specs=pl.BlockSpec((None, tm, NCOL), lambda l, i: (l, i, 0)),
        out_shape=jax.ShapeDtypeStruct((depth, rows, NCOL), BF16),
        compiler_params=_params(("arbitrary", "arbitrary")),
        name="arrange_w_in",
    )(w)


def _w_in_pieces():
    pieces = []

    def put(dst, seg, lo=0, hi=None, scale=1.0):
        hi = _SPLIT[seg] if hi is None else hi
        pieces.append((dst, _OFF[seg] + lo, hi - lo, scale))
        return dst + hi - lo

    put(L_CQ, _A_CQ)
    put(L_CKV, _A_CKV)
    put(L_KR + A_NOPE, _A_KR)
    half = A_ROPE // 2
    put(L_KRROT + A_NOPE, _A_KR, half, A_ROPE, scale=-1.0)
    put(L_KRROT + A_NOPE + half, _A_KR, 0, half)
    for dst, seg in ((P_CWI, _C_WI), (P_AG, _A_G), (P_BG, _B_G), (P_CQ2, _C_Q), (P_CV, _C_V), (P_CG, _C_G),
                     (P_DQ, _D_Q), (P_DV, _D_V), (P_DG, _D_G)):
        put(P_START + dst, seg)
    for dst, seg in ((Q_CK, _C_K), (Q_CQI, _C_QI), (Q_DK, _D_K)):
        put(Q_START + dst, seg)
    for copy in range(LANES // IDX_DIM):
        put(Q_START + Q_CKI + copy * IDX_DIM, _C_KI)
    base = put(B_START, _B_Q, scale=HEAD_DIM ** -0.5 * LOG2E)
    base = put(base, _B_K)
    base = put(base, _B_V)
    assert base == NCOL
    return pieces


def _arrange_w_in_kernel(w_ref, o_ref):
    o_ref[...] = jnp.zeros(o_ref.shape, o_ref.dtype)
    for dst, src, n, scale in _w_in_pieces():
        lo = src // LANES * LANES
        hi = min(-(-(src + n) // LANES) * LANES, w_ref.shape[1])
        v = w_ref[:, lo:hi][:, src - lo:src - lo + n]
        o_ref[:, dst:dst + n] = (v * scale if scale != 1.0 else v).astype(o_ref.dtype)


def _arrange_mla(w_uq, w_ukv):
    wq = w_uq.reshape(Q_LORA, A_HEADS, A_NOPE + A_ROPE)
    nope, rope = wq[..., :A_NOPE], wq[..., A_NOPE:]
    zq = jnp.zeros((Q_LORA, A_HEADS, LANES - A_NOPE - A_ROPE), w_uq.dtype)
    wq_main = jnp.concatenate([nope, rope, zq], axis=-1).reshape(Q_LORA, A_HEADS * LANES)
    wq_rot = jnp.concatenate([jnp.zeros_like(nope), _rot_cols(rope), zq], axis=-1).reshape(Q_LORA, A_HEADS * LANES)
    pad = jnp.zeros((256 - Q_LORA, A_HEADS * LANES), w_uq.dtype)
    wq_main = jnp.concatenate([wq_main, pad], axis=0)
    wq_rot = jnp.concatenate([wq_rot, pad], axis=0)
    wkv = w_ukv.reshape(KV_LORA, A_HEADS, A_NOPE + A_V)
    knope, v = wkv[..., :A_NOPE], wkv[..., A_NOPE:]
    wk = jnp.concatenate([knope, jnp.zeros_like(knope)], axis=-1).reshape(KV_LORA, A_HEADS * LANES)
    wv_t = v.reshape(KV_LORA, A_HEADS * A_V).T
    return wq_main.astype(BF16), wq_rot.astype(BF16), wk.astype(BF16), wv_t.astype(BF16)


def _rope_tables(seq):
    inv = ROPE_THETA ** (-jnp.arange(0, A_ROPE, 2, dtype=F32) / A_ROPE)
    ang = jnp.arange(seq, dtype=F32)[:, None] * inv[None, :]
    cos, sin = jnp.cos(ang), jnp.sin(ang)
    one = jnp.ones((seq, A_NOPE), F32)
    zero = jnp.zeros((seq, LANES - A_NOPE - A_ROPE), F32)
    cos_t = jnp.concatenate([one, cos, cos, zero], axis=1)
    sin_t = jnp.concatenate([jnp.zeros_like(one), sin, sin, zero], axis=1)
    return cos_t, sin_t


def _t5_bucket_np(rel):
    n = np.maximum(rel, 0)
    max_exact = NUM_BUCKETS // 2
    nf = np.maximum(n, max_exact).astype(np.float64)
    large = max_exact + (np.log(nf / max_exact) / math.log(MAX_DISTANCE / max_exact)
                         * (NUM_BUCKETS - max_exact)).astype(np.int32)
    large = np.minimum(large, NUM_BUCKETS - 1)
    return np.where(n < max_exact, n, large).astype(np.int32)


MASKED_BUCKET = NUM_BUCKETS


def _bucket_tiles_causal(nq):
    q = np.arange(QB)[:, None]
    k = np.arange(QB)[None, :]
    tiles = [np.full((QB, QB), MASKED_BUCKET, np.int32)]
    for d in range(nq):
        rel = QB * d + q - k
        tiles.append(np.where(rel >= 0, _t5_bucket_np(rel), MASKED_BUCKET).astype(np.int32))
    return np.stack(tiles)


def _mask_tiles():
    k = np.arange(QB)[:, None]
    q = np.arange(QB)[None, :]
    diag = np.where(k <= q, 0.0, NEG_INF)
    return np.stack([np.full((QB, QB), NEG_INF), diag, np.zeros((QB, QB))]).astype(np.float32)


def _bucket_tiles_dilated():
    q = np.arange(QB)[:, None]
    k = np.arange(2 * QB)[None, :]
    rel = q + QB - k
    tiles = []
    for (window, d) in DILATED_PATTERNS:
        in_band = (rel >= 0) & (rel <= window // d)
        for has_prev in (False, True):
            ok = in_band & (has_prev | (k >= QB))
            tiles.append(np.where(ok, _t5_bucket_np(rel * d), MASKED_BUCKET).astype(np.int32))
    return np.stack(tiles)


def _bias_expand_kernel(table_ref, bucket_ref, out_ref, *, head0, scale, present):
    h = pl.program_id(0) + head0
    for n, buckets_in_tile in enumerate(present):
        bk = bucket_ref[n]
        acc = jnp.where(bk == MASKED_BUCKET, NEG_INF, 0.0)
        for b in buckets_in_tile:
            acc = jnp.where(bk == b, table_ref[b, h] * scale, acc)
        out_ref[n] = acc


def _bias_expand(table, buckets, head0, nheads, scale=1.0):
    n, r, c = buckets.shape
    present = tuple(tuple(int(b) for b in np.unique(tile) if b != MASKED_BUCKET) for tile in buckets)
    buckets = jnp.asarray(buckets)
    return pl.pallas_call(
        functools.partial(_bias_expand_kernel, head0=head0, scale=scale, present=present),
        grid=(nheads,),
        in_specs=[pl.BlockSpec(memory_space=pltpu.SMEM),
                  pl.BlockSpec((n, r, c), lambda h: (0, 0, 0))],
        out_specs=pl.BlockSpec((None, n, r, c), lambda h: (h, 0, 0, 0)),
        out_shape=jax.ShapeDtypeStruct((nheads, n, r, c), F32),
        compiler_params=_params(("arbitrary",)),
        name="bias_expand",
    )(table, buckets)


def _in_proj_kernel(x_ref, g_ref, w_ref, cos_ref, sin_ref, gq_ref, gkv_ref, wq_ref, wqrot_ref, wk_ref, wv_ref,
                    o32_ref, o16_ref, ob_ref, q_ref, k_ref, v_ref):
    x = x_ref[...]
    ms = jnp.mean(x * x, axis=-1, keepdims=True)
    xn = x * lax.rsqrt(ms + EPS) * g_ref[...]
    p = _dot(xn.astype(BF16), w_ref[...])
    o32_ref[...] = p[:, P_START:Q_START]
    o16_ref[...] = p[:, Q_START:B_START].astype(BF16)
    for s in range(B_SLABS):
        ob_ref[s] = p[:, B_START + s * LANES:B_START + (s + 1) * LANES]
    cos = cos_ref[...]
    sin = sin_ref[...]
    cos4 = jnp.concatenate([cos] * A_HEADS, axis=1)
    sin4 = jnp.concatenate([sin] * A_HEADS, axis=1)
    cq = p[:, L_CQ:L_CQ + 256]
    ms = jnp.sum(cq * cq, axis=-1, keepdims=True) * (1.0 / Q_LORA)
    nq = (cq * lax.rsqrt(ms + EPS) * gq_ref[...]).astype(BF16)
    q = _dot(nq, wq_ref[...]) * cos4 + _dot(nq, wqrot_ref[...]) * sin4
    q_ref[...] = (q * ((A_NOPE + A_ROPE) ** -0.5 * LOG2E)).astype(BF16)
    ckv = p[:, L_CKV:L_CKV + KV_LORA]
    ms = jnp.mean(ckv * ckv, axis=-1, keepdims=True)
    nkv = (ckv * lax.rsqrt(ms + EPS) * gkv_ref[...]).astype(BF16)
    kr = p[:, L_KR:L_KR + LANES] * cos + p[:, L_KRROT:L_KRROT + LANES] * sin
    k = _dot(nkv, wk_ref[...]) + jnp.concatenate([kr] * A_HEADS, axis=1)
    k_ref[...] = k.astype(BF16)
    v_ref[...] = _dot_nt(wv_ref[...], nkv).astype(BF16)


def _in_proj(h, g, w, layer, cos_t, sin_t, gq, gkv, wq, wqrot, wk, wv_t, tm=512):
    bsz, seq, _ = h.shape
    w4 = A_HEADS * LANES
    row = lambda b, i: (b, i, 0)
    const = lambda b, i: (0, 0)
    return pl.pallas_call(
        _in_proj_kernel,
        grid=(bsz, seq // tm),
        in_specs=[pl.BlockSpec((None, tm, D_MODEL), row),
                  pl.BlockSpec((1, D_MODEL), const),
                  pl.BlockSpec((None, D_MODEL, NCOL), lambda b, i: (layer, 0, 0)),
                  pl.BlockSpec((tm, LANES), lambda b, i: (i, 0)),
                  pl.BlockSpec((tm, LANES), lambda b, i: (i, 0)),
                  pl.BlockSpec((1, 256), const),
                  pl.BlockSpec((1, KV_LORA), const),
                  pl.BlockSpec((256, w4), const),
                  pl.BlockSpec((256, w4), const),
                  pl.BlockSpec((KV_LORA, w4), const),
                  pl.BlockSpec((A_HEADS * A_V, KV_LORA), const)],
        out_specs=[pl.BlockSpec((None, tm, NCOL32), row),
                   pl.BlockSpec((None, tm, NCOL16), row),
                   pl.BlockSpec((None, B_SLABS, tm, LANES), lambda b, i: (b, 0, i, 0)),
                   pl.BlockSpec((None, tm, w4), row), pl.BlockSpec((None, tm, w4), row),
                   pl.BlockSpec((None, A_HEADS * A_V, tm), lambda b, i: (b, 0, i))],
        out_shape=[jax.ShapeDtypeStruct((bsz, seq, NCOL32), F32),
                   jax.ShapeDtypeStruct((bsz, seq, NCOL16), BF16),
                   jax.ShapeDtypeStruct((bsz, B_SLABS, seq, LANES), F32),
                   jax.ShapeDtypeStruct((bsz, seq, w4), BF16), jax.ShapeDtypeStruct((bsz, seq, w4), BF16),
                   jax.ShapeDtypeStruct((bsz, A_HEADS * A_V, seq), BF16)],
        compiler_params=_params(("arbitrary", "arbitrary")),
        name="in_proj",
    )(h, g, w, cos_t, sin_t, gq, gkv, wq, wqrot, wk, wv_t)


def _silu(g):
    return g * (1.0 / (1.0 + jnp.exp(-g)))


def _for_step_blocks(i, seq, block_body, unroll_upto=0):
    per = KV_CHUNK // QB
    for wb in range(seq // KV_CHUNK):
        @pl.when(i == wb)
        def _(width=(wb + 1) * KV_CHUNK):
            if width <= unroll_upto:
                for u in range(per):
                    block_body(width, per * i + u, slice(u * QB, (u + 1) * QB))
                return

            def one(u, carry):
                block_body(width, per * i + u, pl.ds(pl.multiple_of(u * QB, QB), QB))
                return carry

            lax.fori_loop(0, per, one, 0)


def _mask_tail_t(mask_ref, i, width):
    first = (width - KV_CHUNK) // QB
    return jnp.concatenate([mask_ref[jnp.clip(i - j, -1, 1) + 1] for j in range(first, width // QB)], axis=0)


def _col_reduce(x, op):
    rows, lanes = x.shape
    part = op(x.reshape(rows // COL_ACC_ROWS, COL_ACC_ROWS, lanes), axis=0)
    return op(part, axis=0, keepdims=True)


def _bias_col(bias_ref, h, i, width):
    return jnp.concatenate([bias_ref[h, jnp.maximum(i - j, -1) + 1] for j in range(width // QB)], axis=0)


def _lane_mask(width, seg, dtype):
    lane = lax.broadcasted_iota(jnp.int32, (1, LANES), 1)
    return jnp.where((lane >= seg * width) & (lane < (seg + 1) * width), 1.0, 0.0).astype(dtype)


def _mla_attn_kernel(q_ref, k_ref, vt_ref, g_ref, mask_ref, o_ref):
    i = pl.program_id(1)
    seq = k_ref.shape[0]

    def body(width, ib, rows):
        head_w = width - KV_CHUNK
        zero = jnp.zeros((QB, LANES), BF16)
        scores = []
        for g in range(A_HEADS // 2):
            q0 = q_ref[rows, 2 * g * LANES:(2 * g + 1) * LANES]
            q1 = q_ref[rows, (2 * g + 1) * LANES:(2 * g + 2) * LANES]
            qbd = jnp.concatenate([jnp.concatenate([q0, zero], axis=1),
                                   jnp.concatenate([zero, q1], axis=1)], axis=0)
            scores.append(_dot_nt(k_ref[:width, 2 * g * LANES:(2 * g + 2) * LANES], qbd))
        tail = _mask_tail_t(mask_ref, ib, width)
        outs = []
        for g in range(A_HEADS // 2):
            es, ls = [], []
            for hh in range(2):
                s = scores[g][:, hh * QB:(hh + 1) * QB]
                s = jnp.concatenate([s[:head_w], s[head_w:] + tail], axis=0) if head_w else s + tail
                e = jnp.exp2(s - _col_reduce(s, jnp.max))
                ls.append(_col_reduce(e, jnp.sum))
                es.append(e.astype(BF16))
            ot = _dot(vt_ref[g * LANES:(g + 1) * LANES, :width], jnp.concatenate(es, axis=1))
            outs.append(ot[:A_V, :QB] / ls[0])
            outs.append(ot[A_V:, QB:] / ls[1])
        y_t = jnp.concatenate(outs, axis=0)
        y = jnp.concatenate([y_t[:LANES].T, y_t[LANES:].T], axis=1)
        o_ref[rows, :] = (y * _silu(g_ref[rows, :])).astype(BF16)

    _for_step_blocks(i, seq, body, unroll_upto=seq)


def _mla_attn(q, k, v_t, p32, mask_tiles):
    bsz, seq, w4 = q.shape
    return pl.pallas_call(
        _mla_attn_kernel,
        grid=(bsz, seq // KV_CHUNK),
        in_specs=[pl.BlockSpec((None, KV_CHUNK, w4), lambda b, i: (b, i, 0)),
                  pl.BlockSpec((None, seq, w4), lambda b, i: (b, 0, 0)),
                  pl.BlockSpec((None, A_HEADS * A_V, seq), lambda b, i: (b, 0, 0)),
                  pl.BlockSpec((None, KV_CHUNK, 256), lambda b, i: (b, i, P_AG // 256)),
                  pl.BlockSpec((3, QB, QB), lambda b, i: (0, 0, 0))],
        out_specs=pl.BlockSpec((None, KV_CHUNK, BRANCH_WIDTH), lambda b, i: (b, i, 0)),
        out_shape=jax.ShapeDtypeStruct((bsz, seq, BRANCH_WIDTH), BF16),
        compiler_params=_params(("arbitrary", "arbitrary")),
        name="mla_attn",
    )(q, k, v_t, p32, mask_tiles)


def _transpose_values(v_ref, vt_ref):
    for j in range(v_ref.shape[0] // QB):
        for g in range(v_ref.shape[1] // LANES):
            tile = v_ref[j * QB:(j + 1) * QB, g * LANES:(g + 1) * LANES]
            vt_ref[g * LANES:(g + 1) * LANES, j * QB:(j + 1) * QB] = tile.T.astype(BF16)


def _diff_attn_kernel(q_ref, k_ref, v_ref, g_ref, bias_ref, lam_ref, subln_ref, o_ref, vt_ref, *, lambda_init):
    i = pl.program_id(1)
    seq = k_ref.shape[0]

    @pl.when(i == 0)
    def _():
        _transpose_values(v_ref, vt_ref)

    def body(width, ib, rows):
        lp = lam_ref[...]
        lam = (jnp.exp(jnp.sum(lp[0:1] * lp[1:2], axis=-1, keepdims=True))
               - jnp.exp(jnp.sum(lp[2:3] * lp[3:4], axis=-1, keepdims=True)) + lambda_init)
        scores = []
        for h in range(4):
            sl = slice((h // 2) * LANES, (h // 2 + 1) * LANES)
            qg = q_ref[rows, sl] * (D_QK ** -0.5 * LOG2E)
            qcat = jnp.concatenate([(qg * _lane_mask(D_QK, 2 * (h % 2) + mm, F32)).astype(BF16)
                                    for mm in range(2)], axis=0)
            scores.append(_dot_nt(k_ref[:width, sl], qcat))
        outs = []
        for h in range(4):
            bias = _bias_col(bias_ref, h, ib, width)
            es, ls = [], []
            for mm in range(2):
                s = scores[h][:, mm * QB:(mm + 1) * QB] + bias
                e = jnp.exp2(s - _col_reduce(s, jnp.max))
                ls.append(_col_reduce(e, jnp.sum))
                es.append(e.astype(BF16))
            ot = _dot(vt_ref[h * HEAD_DIM:(h + 1) * HEAD_DIM, :width], jnp.concatenate(es, axis=1))
            a = ot[:, :QB] / ls[0] - lam * (ot[:, QB:] / ls[1])
            ms = jnp.sum(a * a, axis=0, keepdims=True) * (1.0 / HEAD_DIM)
            outs.append(a * lax.rsqrt(ms + EPS))
        y_t = jnp.concatenate(outs, axis=0)
        y = jnp.concatenate([y_t[:LANES].T, y_t[LANES:].T], axis=1)
        o_ref[rows, :] = (y * (subln_ref[...] * (1.0 - lambda_init)) * _silu(g_ref[rows, :])).astype(BF16)

    _for_step_blocks(i, seq, body, unroll_upto=seq // 2)


def _diff_attn(p32, p16, bias, lam_params, subln, lambda_init):
    bsz, seq, _ = p32.shape
    nq = seq // QB
    return pl.pallas_call(
        functools.partial(_diff_attn_kernel, lambda_init=lambda_init),
        grid=(bsz, seq // KV_CHUNK),
        in_specs=[pl.BlockSpec((None, KV_CHUNK, 256), lambda b, i: (b, i, P_DQ // 256)),
                  pl.BlockSpec((None, seq, 256), lambda b, i: (b, 0, Q_DK // 256)),
                  pl.BlockSpec((None, seq, 256), lambda b, i: (b, 0, P_DV // 256)),
                  pl.BlockSpec((None, KV_CHUNK, 256), lambda b, i: (b, i, P_DG // 256)),
                  pl.BlockSpec((4, nq + 1, QB, QB), lambda b, i: (0, 0, 0, 0)),
                  pl.BlockSpec((4, D_QK), lambda b, i: (0, 0)),
                  pl.BlockSpec((1, BRANCH_WIDTH), lambda b, i: (0, 0))],
        out_specs=pl.BlockSpec((None, KV_CHUNK, BRANCH_WIDTH), lambda b, i: (b, i, 0)),
        out_shape=jax.ShapeDtypeStruct((bsz, seq, BRANCH_WIDTH), BF16),
        scratch_shapes=[pltpu.VMEM((BRANCH_WIDTH, seq), BF16)],
        compiler_params=_params(("arbitrary", "arbitrary")),
        name="diff_attn",
    )(p32, p16, p32, p32, bias, lam_params, subln)


def _sortable_to_float(key):
    return pltpu.bitcast(jnp.where(key < 0, key ^ jnp.int32(0x7FFFFFFF), key), F32)


def _kth_largest(score_ref, width, k_top):
    def count_ge(key):
        thr = _sortable_to_float(key)
        return _col_reduce(jnp.where(score_ref[:width, :] >= thr, 1.0, 0.0), jnp.sum)

    int_min = jnp.full((1, QB), -2 ** 31, jnp.int32)
    zero = jnp.zeros((1, QB), jnp.int32)
    t = jnp.where(count_ge(zero) >= k_top, zero, int_min)

    def step(it, t):
        cand = t + (jnp.int32(1) << (30 - it))
        return jnp.where(count_ge(cand) >= k_top, cand, t)

    return _sortable_to_float(lax.fori_loop(0, 31, step, t))


def _first_ties(eq, need, row):
    eqf = jnp.where(eq, 1.0, 0.0)
    nbits = int(eq.shape[0]).bit_length()

    def body(it, j):
        cand = j + (jnp.int32(1) << (nbits - 1 - it))
        cnt = _col_reduce(jnp.where(row < cand, eqf, 0.0), jnp.sum)
        return jnp.where(cnt <= need, cand, j)

    j = lax.fori_loop(0, nbits, body, jnp.zeros((1, eq.shape[1]), jnp.int32))
    return eq & (row < j)


def _sparse_attn_kernel(q_ref, k_ref, v_ref, qi_ref, ki_ref, wi_ref, g_ref, bias_ref, o_ref,
                        score_ref, neg_ref, vt_ref, *, k_top):
    i = pl.program_id(1)
    seq = k_ref.shape[0]

    @pl.when(i == 0)
    def _():
        _transpose_values(v_ref, vt_ref)

    def body(width, ib, rows):
        head_w = width - KV_CHUNK
        ki = ki_ref[:width, :]
        w_t = wi_ref[rows, :].T * (IDX_DIM ** -0.5 * IDX_HEADS ** -0.5)
        score = None
        for h in range(0, IDX_HEADS, 2):
            qg = qi_ref[rows, (h // 4) * LANES:(h // 4 + 1) * LANES]
            qcat = jnp.concatenate([qg * _lane_mask(IDX_DIM, h % 4, BF16),
                                    qg * _lane_mask(IDX_DIM, h % 4 + 1, BF16)], axis=0)
            logit = _dot_nt(ki, qcat)
            term = (jnp.maximum(logit[:, :QB], 0.0) * w_t[h:h + 1]
                    + jnp.maximum(logit[:, QB:], 0.0) * w_t[h + 1:h + 2])
            score = term if score is None else score + term
        s_idx = head_w + lax.broadcasted_iota(jnp.int32, (KV_CHUNK, QB), 0)
        tail_ok = s_idx <= ib * QB + lax.broadcasted_iota(jnp.int32, (KV_CHUNK, QB), 1)
        if head_w:
            score_ref[:head_w, :] = score[:head_w]
        score_ref[head_w:width, :] = jnp.where(tail_ok, score[head_w:], NEG_INF)
        s2s = []
        for g in range(2):
            sl = slice(g * LANES, (g + 1) * LANES)
            qg = q_ref[rows, sl] * (HEAD_DIM ** -0.5 * LOG2E)
            qcat = jnp.concatenate([(qg * _lane_mask(HEAD_DIM, hh, F32)).astype(BF16) for hh in range(2)], axis=0)
            s2 = _dot_nt(k_ref[:width, sl], qcat)
            s2s.append([s2[:, hh * QB:(hh + 1) * QB] + _bias_col(bias_ref, 2 * g + hh, ib, width) for hh in range(2)])
        thr = _kth_largest(score_ref, width, k_top)
        keep_tail = (score_ref[head_w:width, :] >= thr) & tail_ok
        cnt = _col_reduce(jnp.where(keep_tail, 1.0, 0.0), jnp.sum)
        neg_ref[head_w:width, :] = jnp.where(keep_tail, 0.0, NEG_INF)
        if head_w:
            keep_head = score_ref[:head_w, :] >= thr
            cnt = cnt + _col_reduce(jnp.where(keep_head, 1.0, 0.0), jnp.sum)
            neg_ref[:head_w, :] = jnp.where(keep_head, 0.0, NEG_INF)

        @pl.when(jnp.max(cnt) > k_top)
        def _():
            x = score_ref[:width, :]
            gt = x > thr
            need = k_top - _col_reduce(jnp.where(gt, 1.0, 0.0), jnp.sum)
            row = lax.broadcasted_iota(jnp.int32, (width, QB), 0)
            neg_ref[:width, :] = jnp.where(gt | _first_ties(x == thr, need, row), 0.0, NEG_INF)

        outs = []
        for g in range(2):
            sl = slice(g * LANES, (g + 1) * LANES)
            es, ls = [], []
            for hh in range(2):
                s = s2s[g][hh] + neg_ref[:width, :]
                e = jnp.exp2(s - _col_reduce(s, jnp.max))
                ls.append(_col_reduce(e, jnp.sum))
                es.append(e.astype(BF16))
            ot = _dot(vt_ref[sl, :width], jnp.concatenate(es, axis=1))
            outs.append(ot[:HEAD_DIM, :QB] / ls[0])
            outs.append(ot[HEAD_DIM:, QB:] / ls[1])
        y_t = jnp.concatenate(outs, axis=0)
        y = jnp.concatenate([y_t[:LANES].T, y_t[LANES:].T], axis=1)
        o_ref[rows, :] = (y * _silu(g_ref[rows, :])).astype(BF16)

    _for_step_blocks(i, seq, body, unroll_upto=seq // 4)


def _sparse_attn(p32, p16, bias, k_top):
    bsz, seq, _ = p32.shape
    nq = seq // QB
    return pl.pallas_call(
        functools.partial(_sparse_attn_kernel, k_top=k_top),
        grid=(bsz, seq // KV_CHUNK),
        in_specs=[pl.BlockSpec((None, KV_CHUNK, 256), lambda b, i: (b, i, P_CQ2 // 256)),
                  pl.BlockSpec((None, seq, 256), lambda b, i: (b, 0, Q_CK // 256)),
                  pl.BlockSpec((None, seq, 256), lambda b, i: (b, 0, P_CV // 256)),
                  pl.BlockSpec((None, KV_CHUNK, 256), lambda b, i: (b, i, Q_CQI // 256)),
                  pl.BlockSpec((None, seq, LANES), lambda b, i: (b, 0, Q_CKI // LANES)),
                  pl.BlockSpec((None, KV_CHUNK, LANES), lambda b, i: (b, i, P_CWI // LANES)),
                  pl.BlockSpec((None, KV_CHUNK, 256), lambda b, i: (b, i, P_CG // 256)),
                  pl.BlockSpec((4, nq + 1, QB, QB), lambda b, i: (0, 0, 0, 0))],
        out_specs=pl.BlockSpec((None, KV_CHUNK, BRANCH_WIDTH), lambda b, i: (b, i, 0)),
        out_shape=jax.ShapeDtypeStruct((bsz, seq, BRANCH_WIDTH), BF16),
        scratch_shapes=[pltpu.VMEM((seq, QB), F32), pltpu.VMEM((seq, QB), F32),
                        pltpu.VMEM((BRANCH_WIDTH, seq), BF16)],
        compiler_params=_params(("arbitrary", "arbitrary")),
        name="sparse_attn",
    )(p32, p16, p32, p16, p16, p32, p32, bias)


def _dilated_kernel(x_ref, g_ref, bias_ref, o_ref, m_ref, l_ref, acc_ref):
    first = lax.broadcasted_iota(jnp.int32, (QB, LANES), 1) < HEAD_DIM
    hmask = [_lane_mask(HEAD_DIM, hh, F32) for hh in range(2)]

    def run(p, tiles, sink):
        work = []
        for cur, prev, variant in tiles:
            for g in range(2):
                q = x_ref[g, cur, :]
                kcat = x_ref[2 + g, cur, :]
                vcat = x_ref[4 + g, cur, :]
                if prev is not None:
                    kcat = jnp.concatenate([x_ref[2 + g, prev, :], kcat], axis=0)
                    vcat = jnp.concatenate([x_ref[4 + g, prev, :], vcat], axis=0)
                kcat = kcat.astype(BF16)
                scores = []
                for hh in range(2):
                    if prev is not None:
                        bias = bias_ref[2 * g + hh, 2 * p + variant]
                    else:
                        bias = bias_ref[2 * g + hh, 2 * p, :, QB:]
                    scores.append(_dot_nt((q * hmask[hh]).astype(BF16), kcat) + bias)
                vcat = vcat.astype(BF16)
                work.append((jnp.concatenate([vcat, jnp.ones_like(vcat)], axis=1), scores))
        for n, (v1, scores) in enumerate(work):
            ms, ls, accs = [], [], []
            for s in scores:
                m = jnp.max(s, axis=-1, keepdims=True)
                e = jnp.exp2(s - m)
                r = _dot(e.astype(BF16), v1)
                ms.append(m)
                ls.append(r[:, LANES:])
                accs.append(r[:, :LANES])
            sink(n // 2, n % 2, jnp.where(first, ms[0], ms[1]), jnp.where(first, ls[0], ls[1]),
                 jnp.where(first, accs[0], accs[1]))

    def store_stats(slot, d, starts):
        def sink(t, g, m, l, a):
            rows = pl.ds(starts[t], QB, stride=d)
            m_ref[slot, g, rows, :] = m
            l_ref[slot, g, rows, :] = l
            acc_ref[slot, g, rows, :] = a
        return sink

    def step16(n0, carry):
        starts = [n0 * DIL_UNROLL + u for u in range(DIL_UNROLL)]
        run(2, [(pl.ds(r, QB, stride=16), None, 0) for r in starts], store_stats(1, 16, starts))
        return carry

    lax.fori_loop(0, 16 // DIL_UNROLL, step16, 0)

    def step4(j, carry):
        starts = [r + 4 * QB * j for r in range(4)]
        tiles = [(pl.ds(s, QB, stride=4), pl.ds(jnp.maximum(s - 4 * QB, r), QB, stride=4), jnp.minimum(j, 1))
                 for r, s in enumerate(starts)]
        run(1, tiles, store_stats(0, 4, starts))
        return carry

    lax.fori_loop(0, 4, step4, 0)

    def step1(n0, carry):
        blocks = [n0 * DIL_UNROLL + u for u in range(DIL_UNROLL)]

        def sink(t, g, m, l, a):
            rows = pl.ds(pl.multiple_of(blocks[t] * QB, QB), QB)
            lanes = slice(g * LANES, (g + 1) * LANES)
            m4, m16 = m_ref[0, g, rows, :], m_ref[1, g, rows, :]
            m_tot = jnp.maximum(m, jnp.maximum(m4, m16))
            w1, w4, w16 = jnp.exp2(m - m_tot), jnp.exp2(m4 - m_tot), jnp.exp2(m16 - m_tot)
            num = w1 * a + w4 * acc_ref[0, g, rows, :] + w16 * acc_ref[1, g, rows, :]
            den = w1 * l + w4 * l_ref[0, g, rows, :] + w16 * l_ref[1, g, rows, :]
            o_ref[rows, lanes] = (num / den * _silu(g_ref[rows, lanes])).astype(BF16)

        tiles = []
        for j in blocks:
            cur = pl.ds(pl.multiple_of(j * QB, QB), QB)
            prev = pl.ds(pl.multiple_of(jnp.maximum(j - 1, 0) * QB, QB), QB)
            tiles.append((cur, prev, jnp.minimum(j, 1)))
        run(0, tiles, sink)
        return carry

    lax.fori_loop(0, g_ref.shape[0] // QB // DIL_UNROLL, step1, 0)


def _dilated_attn(pb, p32, bias):
    bsz, _, seq, _ = pb.shape
    stats = pltpu.VMEM((2, 2, seq, LANES), F32)
    return pl.pallas_call(
        _dilated_kernel,
        grid=(bsz,),
        in_specs=[pl.BlockSpec((None, B_SLABS, seq, LANES), lambda b: (b, 0, 0, 0)),
                  pl.BlockSpec((None, seq, 256), lambda b: (b, 0, P_BG // 256)),
                  pl.BlockSpec((4, 2 * len(DILATED_PATTERNS), QB, 2 * QB), lambda b: (0, 0, 0, 0))],
        out_specs=pl.BlockSpec((None, seq, BRANCH_WIDTH), lambda b: (b, 0, 0)),
        out_shape=jax.ShapeDtypeStruct((bsz, seq, BRANCH_WIDTH), BF16),
        scratch_shapes=[stats, stats, stats],
        compiler_params=_params(("arbitrary",)),
        name="dilated_attn",
    )(pb, p32, bias)


def _out_proj_kernel(ya_ref, yb_ref, yc_ref, yd_ref, w_ref, h_ref, g_ref, o_ref):
    y = _dot(jnp.concatenate([ya_ref[...], yb_ref[...], yc_ref[...], yd_ref[...]], axis=1), w_ref[...])
    ms = jnp.mean(y * y, axis=-1, keepdims=True)
    o_ref[...] = h_ref[...] + y * lax.rsqrt(ms + EPS) * g_ref[...]


def _out_proj(ya, yb, yc, yd, w, layer, h, g, tm=1024):
    bsz, seq, _ = h.shape
    yspec = pl.BlockSpec((None, tm, BRANCH_WIDTH), lambda b, i: (b, i, 0))
    return pl.pallas_call(
        _out_proj_kernel,
        grid=(bsz, seq // tm),
        in_specs=[yspec, yspec, yspec, yspec,
                  pl.BlockSpec((None, 4 * BRANCH_WIDTH, D_MODEL), lambda b, i: (layer, 0, 0)),
                  pl.BlockSpec((None, tm, D_MODEL), lambda b, i: (b, i, 0)),
                  pl.BlockSpec((1, D_MODEL), lambda b, i: (0, 0))],
        out_specs=pl.BlockSpec((None, tm, D_MODEL), lambda b, i: (b, i, 0)),
        out_shape=jax.ShapeDtypeStruct(h.shape, F32),
        compiler_params=_params(("arbitrary", "arbitrary")),
        name="out_proj",
    )(ya, yb, yc, yd, w, h, g)


def kernel(x, w_in, w_out, norm_pre, norm_post, mla_q_norm, mla_kv_norm, mla_w_uq, mla_w_ukv,
           diff_lambda, diff_subln, rel_bias):
    bsz, seq, _ = x.shape
    depth = w_in.shape[0]
    nq = seq // QB
    k_top = min(IDX_TOPK_MAX, seq // 4)
    cos_t, sin_t = _rope_tables(seq)
    bias_b = _bias_expand(rel_bias, _bucket_tiles_dilated(), 0, 4, LOG2E)
    causal_buckets_t = np.ascontiguousarray(np.swapaxes(_bucket_tiles_causal(nq), 1, 2))
    bias_c = _bias_expand(rel_bias, causal_buckets_t, 4, 4, LOG2E)
    bias_d = _bias_expand(rel_bias, causal_buckets_t, 8, 4, LOG2E)
    mask_tiles = jnp.asarray(_mask_tiles())
    w_arr_all = _arrange_w_in(w_in)
    w_out16 = w_out.astype(BF16)
    h = x
    for layer in range(depth):
        wq, wqrot, wk, wv = _arrange_mla(mla_w_uq[layer], mla_w_ukv[layer])
        gq = jnp.concatenate([mla_q_norm[layer], jnp.ones((256 - Q_LORA,), F32)])[None, :]
        gkv = mla_kv_norm[layer][None, :]
        p32, p16, pb, qa, ka, va = _in_proj(h, norm_pre[layer][None, :], w_arr_all, layer,
                                            cos_t, sin_t, gq, gkv, wq, wqrot, wk, wv)
        y_a = _mla_attn(qa, ka, va, p32, mask_tiles)
        y_b = _dilated_attn(pb, p32, bias_b)
        y_c = _sparse_attn(p32, p16, bias_c, k_top)
        lambda_init = 0.8 - 0.6 * math.exp(-0.3 * layer)
        subln = jnp.tile(diff_subln[layer], BRANCH_WIDTH // HEAD_DIM)[None, :]
        y_d = _diff_attn(p32, p16, bias_d, diff_lambda[layer], subln, lambda_init)
        h = _out_proj(y_a, y_b, y_c, y_d, w_out16, layer, h, norm_post[layer][None, :])
    return h
```
